```python
import math
import jax
import jax.numpy as jnp
from jax import lax
import numpy as np


D_MODEL = 1024
BATCH = 8
SEQ = 8192
DEPTH = 2

MEM_LEN = 256
N_BRANCH = 4
MIX_W = D_MODEL // N_BRANCH
N_IN_SPLITS = 11
IN_COLS = N_IN_SPLITS * MIX_W
HG_HEADS = 4
HG_DH = MIX_W // HG_HEADS
HG_CHUNK = 32
RET_HEADS = 4
RET_DH = MIX_W // RET_HEADS
RET_CHUNK = 128
ROPE_BASE = 10000.0
LRU_BLOCKS = 4
LRU_BS = MIX_W // LRU_BLOCKS
CONV_W = 4
LRU_C = 8.0
S5_GROUP = 16
S5_GROUPS = MIX_W // S5_GROUP
S5_STATE = 64
XA_HEADS = 4
XA_DH = D_MODEL // XA_HEADS
D_FF = -(-8 * D_MODEL // (3 * 256)) * 256
EPS = 1e-6

kernel_name = 'hybrid_gated_hgrn2_retention_rglru_s5_block'


def rms_norm(x, g):
    xf = x.astype(jnp.float32)
    y = xf * lax.rsqrt(jnp.mean(xf * xf, axis=-1, keepdims=True) + EPS)
    return (y * g.astype(jnp.float32)).astype(x.dtype)


def _head_rms(o, n_heads, g):
    B_, S_, W_ = o.shape
    oh = o.reshape(B_, S_, n_heads, W_ // n_heads)
    oh = oh * lax.rsqrt(jnp.mean(oh * oh, axis=-1, keepdims=True) + EPS)
    return oh.reshape(B_, S_, W_) * g.astype(jnp.float32)


def _to_chunks(t, n_heads, chunk):
    B_, S_, W_ = t.shape
    return t.reshape(B_, S_ // chunk, chunk, n_heads, W_ // n_heads).transpose(0, 3, 1, 2, 4)


def _from_chunks(o):
    B_, H_, N_, C_, d_ = o.shape
    return o.transpose(0, 2, 3, 1, 4).reshape(B_, N_ * C_, H_ * d_)


def _inter_chunk(q_dec, k_end, v, decay_end):
    def step(state, inp):
        qn, kn, vn, dn = inp
        o = jnp.einsum('bhck,bhkv->bhcv', qn, state)
        state = dn[..., None] * state + jnp.einsum('bhck,bhcv->bhkv', kn, vn)
        return state, o
    xs = (jnp.moveaxis(q_dec, 2, 0), jnp.moveaxis(k_end, 2, 0),
          jnp.moveaxis(v, 2, 0), jnp.moveaxis(decay_end, 2, 0))
    B_, H_, _, _, K_ = q_dec.shape
    s0 = jnp.zeros((B_, H_, K_, v.shape[-1]), jnp.float32)
    _, o = lax.scan(step, s0, xs)
    return jnp.moveaxis(o, 0, 2)


def hgrn2_mixer(q, f_logit, i, g, lb, norm_g):
    dt = q.dtype
    q, f_logit, i, g = (t.astype(jnp.float32) for t in (q, f_logit, i, g))
    lb = lb.astype(jnp.float32)
    f = lb + (1.0 - lb) * jax.nn.sigmoid(f_logit)
    log_f = jnp.log(f)
    k = 1.0 - f
    q = jax.nn.silu(q)
    qc, kc, vc, lfc = (_to_chunks(t, HG_HEADS, HG_CHUNK) for t in (q, k, i, log_f))
    b = jnp.cumsum(lfc, axis=3)
    b_end = b[..., -1:, :]
    q_dec = qc * jnp.exp(b)
    k_inv = kc * jnp.exp(-b)
    k_end = kc * jnp.exp(b_end - b)
    causal = jnp.tril(jnp.ones((HG_CHUNK, HG_CHUNK), dtype=bool))
    scores = jnp.where(causal, jnp.einsum('bhnck,bhnsk->bhncs', q_dec, k_inv), 0.0)
    o = jnp.einsum('bhncs,bhnsv->bhncv', scores, vc)
    o = o + _inter_chunk(q_dec, k_end, vc, jnp.exp(b_end[..., 0, :]))
    o = _head_rms(_from_chunks(o), HG_HEADS, norm_g) * jax.nn.silu(g)
    return o.astype(dt)


def _rotate(t, cos, sin):
    t1 = t[..., 0::2]
    t2 = t[..., 1::2]
    c = cos[None, :, None, :]
    s = sin[None, :, None, :]
    return jnp.stack([t1 * c - t2 * s, t1 * s + t2 * c], axis=-1).reshape(t.shape)


def retention_mixer(q, k, v, g, norm_g):
    dt = q.dtype
    q, k, v, g = (t.astype(jnp.float32) for t in (q, k, v, g))
    B_, S_, _ = q.shape
    pos = jnp.arange(S_, dtype=jnp.float32)
    inv_freq = ROPE_BASE ** (-jnp.arange(0, RET_DH, 2, dtype=jnp.float32) / RET_DH)
    ang = pos[:, None] * inv_freq[None, :]
    cos, sin = jnp.cos(ang), jnp.sin(ang)
    q = _rotate(q.reshape(B_, S_, RET_HEADS, RET_DH), cos, sin).reshape(B_, S_, MIX_W)
    k = _rotate(k.reshape(B_, S_, RET_HEADS, RET_DH), cos, sin).reshape(B_, S_, MIX_W) * (RET_DH ** -0.5)
    log_gamma = jnp.log1p(-jnp.power(2.0, -5.0 - jnp.arange(RET_HEADS, dtype=jnp.float32)))
    idx = jnp.arange(RET_CHUNK, dtype=jnp.float32)
    rel = idx[:, None] - idx[None, :]
    causal = rel >= 0
    decay = jnp.where(causal, jnp.exp(jnp.where(causal, rel, 0.0)[None] * log_gamma[:, None, None]), 0.0)
    qc, kc, vc = (_to_chunks(t, RET_HEADS, RET_CHUNK) for t in (q, k, v))
    scores = jnp.einsum('bhncd,bhnsd->bhncs', qc, kc) * decay[None, :, None]
    o = jnp.einsum('bhncs,bhnsv->bhncv', scores, vc)
    xi = jnp.exp((idx + 1.0)[None, :] * log_gamma[:, None])
    zeta = jnp.exp((RET_CHUNK - 1.0 - idx)[None, :] * log_gamma[:, None])
    g_end = jnp.exp(RET_CHUNK * log_gamma)
    q_x = qc * xi[None, :, None, :, None]
    k_z = kc * zeta[None, :, None, :, None]
    N_ = qc.shape[2]
    decay_end = jnp.broadcast_to(g_end[None, :, None, None], (B_, RET_HEADS, N_, RET_DH))
    o = o + _inter_chunk(q_x, k_z, vc, decay_end)
    o = _head_rms(_from_chunks(o), RET_HEADS, norm_g) * jax.nn.silu(g)
    return o.astype(dt)


def _linear_combine(e1, e2):
    a1, b1 = e1
    a2, b2 = e2
    return a1 * a2, a2 * b1 + b2


def rglru_mixer(x_gate, x_in, conv_w, conv_b, wa, ba, wx, bx, lam):
    dt = x_in.dtype
    xg = x_gate.astype(jnp.float32)
    xi = x_in.astype(jnp.float32)
    B_, S_, _ = xi.shape
    xc = lax.conv_general_dilated(
        xi, conv_w.astype(jnp.float32)[:, None, :], window_strides=(1,),
        padding=[(CONV_W - 1, 0)], dimension_numbers=('NWC', 'WIO', 'NWC'),
        feature_group_count=MIX_W) + conv_b.astype(jnp.float32)
    xb = xc.reshape(B_, S_, LRU_BLOCKS, LRU_BS)
    r = jax.nn.sigmoid(jnp.einsum('bsni,nij->bsnj', xb, wa.astype(jnp.float32)).reshape(B_, S_, MIX_W) + ba.astype(jnp.float32))
    ig = jax.nn.sigmoid(jnp.einsum('bsni,nij->bsnj', xb, wx.astype(jnp.float32)).reshape(B_, S_, MIX_W) + bx.astype(jnp.float32))
    log_a = -LRU_C * r * jax.nn.softplus(-lam.astype(jnp.float32))
    a = jnp.exp(log_a)
    u = jnp.sqrt(-jnp.expm1(2.0 * log_a)) * (ig * xc)
    _, h = lax.associative_scan(_linear_combine, (a, u), axis=1)
    return (h * jax.nn.gelu(xg)).astype(dt)


def _complex_combine(e1, e2):
    ar1, ai1, br1, bi1 = e1
    ar2, ai2, br2, bi2 = e2
    ar = ar1 * ar2 - ai1 * ai2
    ai = ar1 * ai2 + ai1 * ar2
    br = ar2 * br1 - ai2 * bi1 + br2
    bi = ar2 * bi1 + ai2 * br1 + bi2
    return ar, ai, br, bi


def s5_mixer(u, lam_re, lam_im, b_re, b_im, c_re, c_im, d, log_dt, glu_w, glu_b):
    dt_ = u.dtype
    uf = u.astype(jnp.float32)
    lam_re, lam_im, b_re, b_im, c_re, c_im, d, log_dt = (
        t.astype(jnp.float32) for t in (lam_re, lam_im, b_re, b_im, c_re, c_im, d, log_dt))
    B_, S_, _ = uf.shape
    step = jnp.exp(log_dt)[:, None]
    mag = jnp.exp(lam_re * step)
    lb_re = mag * jnp.cos(lam_im * step)
    lb_im = mag * jnp.sin(lam_im * step)
    den = lam_re * lam_re + lam_im * lam_im
    f_re = ((lb_re - 1.0) * lam_re + lb_im * lam_im) / den
    f_im = (lb_im * lam_re - (lb_re - 1.0) * lam_im) / den
    bb_re = f_re[..., None] * b_re - f_im[..., None] * b_im
    bb_im = f_re[..., None] * b_im + f_im[..., None] * b_re
    ug = uf.reshape(B_, S_, S5_GROUPS, S5_GROUP)
    bu_re = jnp.einsum('bsgp,gnp->bsgn', ug, bb_re)
    bu_im = jnp.einsum('bsgp,gnp->bsgn', ug, bb_im)
    a_re = jnp.broadcast_to(lb_re[None, None], (1, S_, S5_GROUPS, S5_STATE))
    a_im = jnp.broadcast_to(lb_im[None, None], (1, S_, S5_GROUPS, S5_STATE))
    _, _, h_re, h_im = lax.associative_scan(_complex_combine, (a_re, a_im, bu_re, bu_im), axis=1)
    y = jnp.einsum('bsgn,gpn->bsgp', h_re, c_re) - jnp.einsum('bsgn,gpn->bsgp', h_im, c_im)
    y = y.reshape(B_, S_, MIX_W) + d * uf
    act = jax.nn.gelu(y)
    out = act * jax.nn.sigmoid(act @ glu_w.astype(jnp.float32) + glu_b.astype(jnp.float32))
    return out.astype(dt_)


def mixer_block(h, lb, w_in, w_gate, b_gate, hg_norm, ret_norm, conv_w, conv_b, wa, ba, wx, bx, lam,
                lam_re, lam_im, b_re, b_im, c_re, c_im, s5_d, log_dt, glu_w, glu_b, w_up, w_out):
    B_, S_, _ = h.shape
    z = h @ w_in
    hq, hf, hi, hg, rq, rk, rv, rg, lgate, lx, su = jnp.split(z, N_IN_SPLITS, axis=-1)
    y_a = hgrn2_mixer(hq, hf, hi, hg, lb, hg_norm)
    y_b = retention_mixer(rq, rk, rv, rg, ret_norm)
    y_c = rglru_mixer(lgate, lx, conv_w, conv_b, wa, ba, wx, bx, lam)
    y_d = s5_mixer(su, lam_re, lam_im, b_re, b_im, c_re, c_im, s5_d, log_dt, glu_w, glu_b)
    branches = jnp.stack([y_a, y_b, y_c, y_d], axis=2)
    up = jnp.einsum('bsnc,ncd->bsnd', branches, w_up)
    gates = jax.nn.sigmoid(h @ w_gate + b_gate).reshape(B_, S_, N_BRANCH, D_MODEL)
    return jnp.sum(gates * up, axis=2) @ w_out


def cross_attention(h, m, w_q, w_kv, w_o):
    B_, S_, _ = h.shape
    M_ = m.shape[1]
    q = (h @ w_q).reshape(B_, S_, XA_HEADS, XA_DH)
    k, v = jnp.split(m @ w_kv, 2, axis=-1)
    k = k.reshape(B_, M_, XA_HEADS, XA_DH)
    v = v.reshape(B_, M_, XA_HEADS, XA_DH)
    s = jnp.einsum('bshd,bmhd->bhsm', q, k).astype(jnp.float32) * (XA_DH ** -0.5)
    p = jax.nn.softmax(s, axis=-1).astype(v.dtype)
    o = jnp.einsum('bhsm,bmhd->bshd', p, v).reshape(B_, S_, D_MODEL)
    return o @ w_o


def swiglu(h, w_gu, w_down):
    gate, up = jnp.split(h @ w_gu, 2, axis=-1)
    return (jax.nn.silu(gate) * up) @ w_down


def _fwd_setup_inputs(seed: int = 0) -> dict:
    key = jax.random.key(seed)
    ks = iter(jax.random.split(key, 64))
    f32 = jnp.float32
    L, W, G, N, P = DEPTH, MIX_W, S5_GROUPS, S5_STATE, S5_GROUP

    def nrm(shape, scale):
        return jax.random.normal(next(ks), shape, f32) * scale

    def gain(shape):
        return 1.0 + 0.02 * jax.random.normal(next(ks), shape, f32)

    x = nrm((BATCH, SEQ, D_MODEL), 1.0)
    mem = nrm((BATCH, MEM_LEN, D_MODEL), 1.0)
    hg_lower_bounds = nrm((L, W), 0.5)
    norm_mix_pre = gain((L, D_MODEL))
    norm_mix_post = gain((L, D_MODEL))
    w_in = nrm((L, D_MODEL, IN_COLS), D_MODEL ** -0.5)
    w_gate = nrm((L, D_MODEL, N_BRANCH * D_MODEL), D_MODEL ** -0.5)
    b_gate = nrm((L, N_BRANCH * D_MODEL), 0.02)
    hg_norm = gain((L, W))
    ret_norm = gain((L, W))
    lru_conv_w = nrm((L, CONV_W, W), CONV_W ** -0.5)
    lru_conv_b = nrm((L, W), 0.02)
    lru_wa = nrm((L, LRU_BLOCKS, LRU_BS, LRU_BS), LRU_BS ** -0.5)
    lru_ba = nrm((L, W), 0.02)
    lru_wx = nrm((L, LRU_BLOCKS, LRU_BS, LRU_BS), LRU_BS ** -0.5)
    lru_bx = nrm((L, W), 0.02)
    a_c = jax.random.uniform(next(ks), (L, W), f32, minval=0.9, maxval=0.999)
    a_base = a_c ** (1.0 / LRU_C)
    lru_lambda = jnp.log(a_base) - jnp.log1p(-a_base)
    n_idx = jnp.arange(N, dtype=f32)
    s5_lam_re = -0.5 + 0.01 * jax.random.normal(next(ks), (L, G, N), f32)
    s5_lam_im = math.pi * n_idx[None, None, :] + 0.01 * jax.random.normal(next(ks), (L, G, N), f32)
    s5_b_re = nrm((L, G, N, P), (2.0 * P) ** -0.5)
    s5_b_im = nrm((L, G, N, P), (2.0 * P) ** -0.5)
    s5_c_re = nrm((L, G, P, N), (2.0 * N) ** -0.5 * 4.0)
    s5_c_im = nrm((L, G, P, N), (2.0 * N) ** -0.5 * 4.0)
    s5_d = nrm((L, W), 1.0)
    s5_log_dt = jax.random.uniform(next(ks), (L, G), f32, minval=math.log(1e-3), maxval=math.log(1e-1))
    s5_glu_w = nrm((L, W, W), W ** -0.5)
    s5_glu_b = nrm((L, W), 0.02)
    w_up = nrm((L, N_BRANCH, W, D_MODEL), W ** -0.5)
    w_out = nrm((L, D_MODEL, D_MODEL), D_MODEL ** -0.5)
    norm_xa_pre = gain((L, D_MODEL))
    norm_xa_post = gain((L, D_MODEL))
    norm_mem = gain((L, D_MODEL))
    xa_w_q = nrm((L, D_MODEL, D_MODEL), D_MODEL ** -0.5)
    xa_w_kv = nrm((L, D_MODEL, 2 * D_MODEL), D_MODEL ** -0.5)
    xa_w_o = nrm((L, D_MODEL, D_MODEL), D_MODEL ** -0.5)
    norm_ffn_pre = gain((L, D_MODEL))
    norm_ffn_post = gain((L, D_MODEL))
    ffn_w_gu = nrm((L, D_MODEL, 2 * D_FF), D_MODEL ** -0.5)
    ffn_w_down = nrm((L, D_FF, D_MODEL), D_FF ** -0.5)
    return {
        'x': x, 'mem': mem, 'hg_lower_bounds': hg_lower_bounds,
        'norm_mix_pre': norm_mix_pre, 'norm_mix_post': norm_mix_post,
        'w_in': w_in, 'w_gate': w_gate, 'b_gate': b_gate,
        'hg_norm': hg_norm, 'ret_norm': ret_norm,
        'lru_conv_w': lru_conv_w, 'lru_conv_b': lru_conv_b, 'lru_wa': lru_wa, 'lru_ba': lru_ba,
        'lru_wx': lru_wx, 'lru_bx': lru_bx, 'lru_lambda': lru_lambda,
        's5_lam_re': s5_lam_re, 's5_lam_im': s5_lam_im, 's5_b_re': s5_b_re, 's5_b_im': s5_b_im,
        's5_c_re': s5_c_re, 's5_c_im': s5_c_im, 's5_d': s5_d, 's5_log_dt': s5_log_dt,
        's5_glu_w': s5_glu_w, 's5_glu_b': s5_glu_b,
        'w_up': w_up, 'w_out': w_out,
        'norm_xa_pre': norm_xa_pre, 'norm_xa_post': norm_xa_post, 'norm_mem': norm_mem,
        'xa_w_q': xa_w_q, 'xa_w_kv': xa_w_kv, 'xa_w_o': xa_w_o,
        'norm_ffn_pre': norm_ffn_pre, 'norm_ffn_post': norm_ffn_post,
        'ffn_w_gu': ffn_w_gu, 'ffn_w_down': ffn_w_down,
    }


def _fwd_reference(x, mem, hg_lower_bounds, norm_mix_pre, norm_mix_post, w_in, w_gate, b_gate,
              hg_norm, ret_norm, lru_conv_w, lru_conv_b, lru_wa, lru_ba, lru_wx, lru_bx, lru_lambda,
              s5_lam_re, s5_lam_im, s5_b_re, s5_b_im, s5_c_re, s5_c_im, s5_d, s5_log_dt,
              s5_glu_w, s5_glu_b, w_up, w_out, norm_xa_pre, norm_xa_post, norm_mem,
              xa_w_q, xa_w_kv, xa_w_o, norm_ffn_pre, norm_ffn_post, ffn_w_gu, ffn_w_down):
    p = jax.nn.softmax(hg_lower_bounds.astype(jnp.float32), axis=0)
    lower_bounds = jnp.cumsum(p, axis=0) - p[0:1]
    for l in range(DEPTH):
        h = rms_norm(x, norm_mix_pre[l])
        y = mixer_block(h, lower_bounds[l], w_in[l], w_gate[l], b_gate[l], hg_norm[l], ret_norm[l],
                        lru_conv_w[l], lru_conv_b[l], lru_wa[l], lru_ba[l], lru_wx[l], lru_bx[l], lru_lambda[l],
                        s5_lam_re[l], s5_lam_im[l], s5_b_re[l], s5_b_im[l], s5_c_re[l], s5_c_im[l],
                        s5_d[l], s5_log_dt[l], s5_glu_w[l], s5_glu_b[l], w_up[l], w_out[l])
        x = x + rms_norm(y, norm_mix_post[l])
        h = rms_norm(x, norm_xa_pre[l])
        m = rms_norm(mem, norm_mem[l])
        x = x + rms_norm(cross_attention(h, m, xa_w_q[l], xa_w_kv[l], xa_w_o[l]), norm_xa_post[l])
        h = rms_norm(x, norm_ffn_pre[l])
        x = x + rms_norm(swiglu(h, ffn_w_gu[l], ffn_w_down[l]), norm_ffn_post[l])
    return x


import jax as _jax
import jax.numpy as _jnp

TWIN_FORMAT = 'train_step'
FWD_PARAMS = ['x', 'mem', 'hg_lower_bounds', 'norm_mix_pre', 'norm_mix_post', 'w_in', 'w_gate', 'b_gate', 'hg_norm', 'ret_norm', 'lru_conv_w', 'lru_conv_b', 'lru_wa', 'lru_ba', 'lru_wx', 'lru_bx', 'lru_lambda', 's5_lam_re', 's5_lam_im', 's5_b_re', 's5_b_im', 's5_c_re', 's5_c_im', 's5_d', 's5_log_dt', 's5_glu_w', 's5_glu_b', 'w_up', 'w_out', 'norm_xa_pre', 'norm_xa_post', 'norm_mem', 'xa_w_q', 'xa_w_kv', 'xa_w_o', 'norm_ffn_pre', 'norm_ffn_post', 'ffn_w_gu', 'ffn_w_down']
TWIN_WEIGHTS = ['hg_lower_bounds', 'norm_mix_pre', 'norm_mix_post', 'w_in', 'w_gate', 'b_gate', 'hg_norm', 'ret_norm', 'lru_conv_w', 'lru_conv_b', 'lru_wa', 'lru_ba', 'lru_wx', 'lru_bx', 'lru_lambda', 's5_lam_re', 's5_lam_im', 's5_b_re', 's5_b_im', 's5_c_re', 's5_c_im', 's5_d', 's5_log_dt', 's5_glu_w', 's5_glu_b', 'w_up', 'w_out', 'norm_xa_pre', 'norm_xa_post', 'norm_mem', 'xa_w_q', 'xa_w_kv', 'xa_w_o', 'norm_ffn_pre', 'norm_ffn_post', 'ffn_w_gu', 'ffn_w_down']
TWIN_DIFF_INPUT = 'x'
TWIN_INPUTS = ['x', 'mem', 'hg_lower_bounds', 'norm_mix_pre', 'norm_mix_post', 'w_in', 'w_gate', 'b_gate', 'hg_norm', 'ret_norm', 'lru_conv_w', 'lru_conv_b', 'lru_wa', 'lru_ba', 'lru_wx', 'lru_bx', 'lru_lambda', 's5_lam_re', 's5_lam_im', 's5_b_re', 's5_b_im', 's5_c_re', 's5_c_im', 's5_d', 's5_log_dt', 's5_glu_w', 's5_glu_b', 'w_up', 'w_out', 'norm_xa_pre', 'norm_xa_post', 'norm_mem', 'xa_w_q', 'xa_w_kv', 'xa_w_o', 'norm_ffn_pre', 'norm_ffn_post', 'ffn_w_gu', 'ffn_w_down', 'loss_target', 'm_hg_lower_bounds', 'm_norm_mix_pre', 'm_norm_mix_post', 'm_w_in', 'm_w_gate', 'm_b_gate', 'm_hg_norm', 'm_ret_norm', 'm_lru_conv_w', 'm_lru_conv_b', 'm_lru_wa', 'm_lru_ba', 'm_lru_wx', 'm_lru_bx', 'm_lru_lambda', 'm_s5_lam_re', 'm_s5_lam_im', 'm_s5_b_re', 'm_s5_b_im', 'm_s5_c_re', 'm_s5_c_im', 'm_s5_d', 'm_s5_log_dt', 'm_s5_glu_w', 'm_s5_glu_b', 'm_w_up', 'm_w_out', 'm_norm_xa_pre', 'm_norm_xa_post', 'm_norm_mem', 'm_xa_w_q', 'm_xa_w_kv', 'm_xa_w_o', 'm_norm_ffn_pre', 'm_norm_ffn_post', 'm_ffn_w_gu', 'm_ffn_w_down', 'v_hg_lower_bounds', 'v_norm_mix_pre', 'v_norm_mix_post', 'v_w_in', 'v_w_gate', 'v_b_gate', 'v_hg_norm', 'v_ret_norm', 'v_lru_conv_w', 'v_lru_conv_b', 'v_lru_wa', 'v_lru_ba', 'v_lru_wx', 'v_lru_bx', 'v_lru_lambda', 'v_s5_lam_re', 'v_s5_lam_im', 'v_s5_b_re', 'v_s5_b_im', 'v_s5_c_re', 'v_s5_c_im', 'v_s5_d', 'v_s5_log_dt', 'v_s5_glu_w', 'v_s5_glu_b', 'v_w_up', 'v_w_out', 'v_norm_xa_pre', 'v_norm_xa_post', 'v_norm_mem', 'v_xa_w_q', 'v_xa_w_kv', 'v_xa_w_o', 'v_norm_ffn_pre', 'v_norm_ffn_post', 'v_ffn_w_gu', 'v_ffn_w_down']
TWIN_OUTPUTS = ['loss', 'grad_x', 'grad_hg_lower_bounds', 'grad_norm_mix_pre', 'grad_norm_mix_post', 'grad_w_in', 'grad_w_gate', 'grad_b_gate', 'grad_hg_norm', 'grad_ret_norm', 'grad_lru_conv_w', 'grad_lru_conv_b', 'grad_lru_wa', 'grad_lru_ba', 'grad_lru_wx', 'grad_lru_bx', 'grad_lru_lambda', 'grad_s5_lam_re', 'grad_s5_lam_im', 'grad_s5_b_re', 'grad_s5_b_im', 'grad_s5_c_re', 'grad_s5_c_im', 'grad_s5_d', 'grad_s5_log_dt', 'grad_s5_glu_w', 'grad_s5_glu_b', 'grad_w_up', 'grad_w_out', 'grad_norm_xa_pre', 'grad_norm_xa_post', 'grad_norm_mem', 'grad_xa_w_q', 'grad_xa_w_kv', 'grad_xa_w_o', 'grad_norm_ffn_pre', 'grad_norm_ffn_post', 'grad_ffn_w_gu', 'grad_ffn_w_down', 'delta_hg_lower_bounds', 'delta_norm_mix_pre', 'delta_norm_mix_post', 'delta_w_in', 'delta_w_gate', 'delta_b_gate', 'delta_hg_norm', 'delta_ret_norm', 'delta_lru_conv_w', 'delta_lru_conv_b', 'delta_lru_wa', 'delta_lru_ba', 'delta_lru_wx', 'delta_lru_bx', 'delta_lru_lambda', 'delta_s5_lam_re', 'delta_s5_lam_im', 'delta_s5_b_re', 'delta_s5_b_im', 'delta_s5_c_re', 'delta_s5_c_im', 'delta_s5_d', 'delta_s5_log_dt', 'delta_s5_glu_w', 'delta_s5_glu_b', 'delta_w_up', 'delta_w_out', 'delta_norm_xa_pre', 'delta_norm_xa_post', 'delta_norm_mem', 'delta_xa_w_q', 'delta_xa_w_kv', 'delta_xa_w_o', 'delta_norm_ffn_pre', 'delta_norm_ffn_post', 'delta_ffn_w_gu', 'delta_ffn_w_down', 'new_m_hg_lower_bounds', 'new_m_norm_mix_pre', 'new_m_norm_mix_post', 'new_m_w_in', 'new_m_w_gate', 'new_m_b_gate', 'new_m_hg_norm', 'new_m_ret_norm', 'new_m_lru_conv_w', 'new_m_lru_conv_b', 'new_m_lru_wa', 'new_m_lru_ba', 'new_m_lru_wx', 'new_m_lru_bx', 'new_m_lru_lambda', 'new_m_s5_lam_re', 'new_m_s5_lam_im', 'new_m_s5_b_re', 'new_m_s5_b_im', 'new_m_s5_c_re', 'new_m_s5_c_im', 'new_m_s5_d', 'new_m_s5_log_dt', 'new_m_s5_glu_w', 'new_m_s5_glu_b', 'new_m_w_up', 'new_m_w_out', 'new_m_norm_xa_pre', 'new_m_norm_xa_post', 'new_m_norm_mem', 'new_m_xa_w_q', 'new_m_xa_w_kv', 'new_m_xa_w_o', 'new_m_norm_ffn_pre', 'new_m_norm_ffn_post', 'new_m_ffn_w_gu', 'new_m_ffn_w_down', 'new_v_hg_lower_bounds', 'new_v_norm_mix_pre', 'new_v_norm_mix_post', 'new_v_w_in', 'new_v_w_gate', 'new_v_b_gate', 'new_v_hg_norm', 'new_v_ret_norm', 'new_v_lru_conv_w', 'new_v_lru_conv_b', 'new_v_lru_wa', 'new_v_lru_ba', 'new_v_lru_wx', 'new_v_lru_bx', 'new_v_lru_lambda', 'new_v_s5_lam_re', 'new_v_s5_lam_im', 'new_v_s5_b_re', 'new_v_s5_b_im', 'new_v_s5_c_re', 'new_v_s5_c_im', 'new_v_s5_d', 'new_v_s5_log_dt', 'new_v_s5_glu_w', 'new_v_s5_glu_b', 'new_v_w_up', 'new_v_w_out', 'new_v_norm_xa_pre', 'new_v_norm_xa_post', 'new_v_norm_mem', 'new_v_xa_w_q', 'new_v_xa_w_kv', 'new_v_xa_w_o', 'new_v_norm_ffn_pre', 'new_v_norm_ffn_post', 'new_v_ffn_w_gu', 'new_v_ffn_w_down']
TWIN_LEAF_KINDS = {'loss': 'loss', 'grad_x': 'grad_x', 'grad_hg_lower_bounds': 'grad_w', 'grad_norm_mix_pre': 'grad_w', 'grad_norm_mix_post': 'grad_w', 'grad_w_in': 'grad_w', 'grad_w_gate': 'grad_w', 'grad_b_gate': 'grad_w', 'grad_hg_norm': 'grad_w', 'grad_ret_norm': 'grad_w', 'grad_lru_conv_w': 'grad_w', 'grad_lru_conv_b': 'grad_w', 'grad_lru_wa': 'grad_w', 'grad_lru_ba': 'grad_w', 'grad_lru_wx': 'grad_w', 'grad_lru_bx': 'grad_w', 'grad_lru_lambda': 'grad_w', 'grad_s5_lam_re': 'grad_w', 'grad_s5_lam_im': 'grad_w', 'grad_s5_b_re': 'grad_w', 'grad_s5_b_im': 'grad_w', 'grad_s5_c_re': 'grad_w', 'grad_s5_c_im': 'grad_w', 'grad_s5_d': 'grad_w', 'grad_s5_log_dt': 'grad_w', 'grad_s5_glu_w': 'grad_w', 'grad_s5_glu_b': 'grad_w', 'grad_w_up': 'grad_w', 'grad_w_out': 'grad_w', 'grad_norm_xa_pre': 'grad_w', 'grad_norm_xa_post': 'grad_w', 'grad_norm_mem': 'grad_w', 'grad_xa_w_q': 'grad_w', 'grad_xa_w_kv': 'grad_w', 'grad_xa_w_o': 'grad_w', 'grad_norm_ffn_pre': 'grad_w', 'grad_norm_ffn_post': 'grad_w', 'grad_ffn_w_gu': 'grad_w', 'grad_ffn_w_down': 'grad_w', 'delta_hg_lower_bounds': 'delta_w', 'delta_norm_mix_pre': 'delta_w', 'delta_norm_mix_post': 'delta_w', 'delta_w_in': 'delta_w', 'delta_w_gate': 'delta_w', 'delta_b_gate': 'delta_w', 'delta_hg_norm': 'delta_w', 'delta_ret_norm': 'delta_w', 'delta_lru_conv_w': 'delta_w', 'delta_lru_conv_b': 'delta_w', 'delta_lru_wa': 'delta_w', 'delta_lru_ba': 'delta_w', 'delta_lru_wx': 'delta_w', 'delta_lru_bx': 'delta_w', 'delta_lru_lambda': 'delta_w', 'delta_s5_lam_re': 'delta_w', 'delta_s5_lam_im': 'delta_w', 'delta_s5_b_re': 'delta_w', 'delta_s5_b_im': 'delta_w', 'delta_s5_c_re': 'delta_w', 'delta_s5_c_im': 'delta_w', 'delta_s5_d': 'delta_w', 'delta_s5_log_dt': 'delta_w', 'delta_s5_glu_w': 'delta_w', 'delta_s5_glu_b': 'delta_w', 'delta_w_up': 'delta_w', 'delta_w_out': 'delta_w', 'delta_norm_xa_pre': 'delta_w', 'delta_norm_xa_post': 'delta_w', 'delta_norm_mem': 'delta_w', 'delta_xa_w_q': 'delta_w', 'delta_xa_w_kv': 'delta_w', 'delta_xa_w_o': 'delta_w', 'delta_norm_ffn_pre': 'delta_w', 'delta_norm_ffn_post': 'delta_w', 'delta_ffn_w_gu': 'delta_w', 'delta_ffn_w_down': 'delta_w', 'new_m_hg_lower_bounds': 'new_m', 'new_m_norm_mix_pre': 'new_m', 'new_m_norm_mix_post': 'new_m', 'new_m_w_in': 'new_m', 'new_m_w_gate': 'new_m', 'new_m_b_gate': 'new_m', 'new_m_hg_norm': 'new_m', 'new_m_ret_norm': 'new_m', 'new_m_lru_conv_w': 'new_m', 'new_m_lru_conv_b': 'new_m', 'new_m_lru_wa': 'new_m', 'new_m_lru_ba': 'new_m', 'new_m_lru_wx': 'new_m', 'new_m_lru_bx': 'new_m', 'new_m_lru_lambda': 'new_m', 'new_m_s5_lam_re': 'new_m', 'new_m_s5_lam_im': 'new_m', 'new_m_s5_b_re': 'new_m', 'new_m_s5_b_im': 'new_m', 'new_m_s5_c_re': 'new_m', 'new_m_s5_c_im': 'new_m', 'new_m_s5_d': 'new_m', 'new_m_s5_log_dt': 'new_m', 'new_m_s5_glu_w': 'new_m', 'new_m_s5_glu_b': 'new_m', 'new_m_w_up': 'new_m', 'new_m_w_out': 'new_m', 'new_m_norm_xa_pre': 'new_m', 'new_m_norm_xa_post': 'new_m', 'new_m_norm_mem': 'new_m', 'new_m_xa_w_q': 'new_m', 'new_m_xa_w_kv': 'new_m', 'new_m_xa_w_o': 'new_m', 'new_m_norm_ffn_pre': 'new_m', 'new_m_norm_ffn_post': 'new_m', 'new_m_ffn_w_gu': 'new_m', 'new_m_ffn_w_down': 'new_m', 'new_v_hg_lower_bounds': 'new_v', 'new_v_norm_mix_pre': 'new_v', 'new_v_norm_mix_post': 'new_v', 'new_v_w_in': 'new_v', 'new_v_w_gate': 'new_v', 'new_v_b_gate': 'new_v', 'new_v_hg_norm': 'new_v', 'new_v_ret_norm': 'new_v', 'new_v_lru_conv_w': 'new_v', 'new_v_lru_conv_b': 'new_v', 'new_v_lru_wa': 'new_v', 'new_v_lru_ba': 'new_v', 'new_v_lru_wx': 'new_v', 'new_v_lru_bx': 'new_v', 'new_v_lru_lambda': 'new_v', 'new_v_s5_lam_re': 'new_v', 'new_v_s5_lam_im': 'new_v', 'new_v_s5_b_re': 'new_v', 'new_v_s5_b_im': 'new_v', 'new_v_s5_c_re': 'new_v', 'new_v_s5_c_im': 'new_v', 'new_v_s5_d': 'new_v', 'new_v_s5_log_dt': 'new_v', 'new_v_s5_glu_w': 'new_v', 'new_v_s5_glu_b': 'new_v', 'new_v_w_up': 'new_v', 'new_v_w_out': 'new_v', 'new_v_norm_xa_pre': 'new_v', 'new_v_norm_xa_post': 'new_v', 'new_v_norm_mem': 'new_v', 'new_v_xa_w_q': 'new_v', 'new_v_xa_w_kv': 'new_v', 'new_v_xa_w_o': 'new_v', 'new_v_norm_ffn_pre': 'new_v', 'new_v_norm_ffn_post': 'new_v', 'new_v_ffn_w_gu': 'new_v', 'new_v_ffn_w_down': 'new_v'}


def _forward(args):
    return _fwd_reference(*[args[k] for k in FWD_PARAMS])


def _output_shape():
    out = _jax.eval_shape(lambda: _forward(_fwd_setup_inputs(0)))
    return out.shape, out.dtype

N_MICROBATCH = 1
ADAM_LR = 0.001
ADAM_B1 = 0.9
ADAM_B2 = 0.999
ADAM_EPS = 1e-08
ADAM_WD = 0.01
ADAM_STEP = 10
PER_EXAMPLE_BATCH_AXIS = {'x': 0, 'mem': 0, 'loss_target': 0}
SHARED_INPUTS = []
_WEIGHT_DTYPES = {'hg_lower_bounds': _jnp.float32, 'norm_mix_pre': _jnp.float32, 'norm_mix_post': _jnp.float32, 'w_in': _jnp.float32, 'w_gate': _jnp.float32, 'b_gate': _jnp.float32, 'hg_norm': _jnp.float32, 'ret_norm': _jnp.float32, 'lru_conv_w': _jnp.float32, 'lru_conv_b': _jnp.float32, 'lru_wa': _jnp.float32, 'lru_ba': _jnp.float32, 'lru_wx': _jnp.float32, 'lru_bx': _jnp.float32, 'lru_lambda': _jnp.float32, 's5_lam_re': _jnp.float32, 's5_lam_im': _jnp.float32, 's5_b_re': _jnp.float32, 's5_b_im': _jnp.float32, 's5_c_re': _jnp.float32, 's5_c_im': _jnp.float32, 's5_d': _jnp.float32, 's5_log_dt': _jnp.float32, 's5_glu_w': _jnp.float32, 's5_glu_b': _jnp.float32, 'w_up': _jnp.float32, 'w_out': _jnp.float32, 'norm_xa_pre': _jnp.float32, 'norm_xa_post': _jnp.float32, 'norm_mem': _jnp.float32, 'xa_w_q': _jnp.float32, 'xa_w_kv': _jnp.float32, 'xa_w_o': _jnp.float32, 'norm_ffn_pre': _jnp.float32, 'norm_ffn_post': _jnp.float32, 'ffn_w_gu': _jnp.float32, 'ffn_w_down': _jnp.float32}
MOMENT_SCALE = {'hg_lower_bounds': 1.048022e-01, 'norm_mix_pre': 5.603309e+00, 'norm_mix_post': 6.368353e+01, 'w_in': 3.360523e+00, 'w_gate': 3.589887e-01, 'b_gate': 1.497546e+00, 'hg_norm': 2.525709e+00, 'ret_norm': 1.642573e+00, 'lru_conv_w': 1.192685e+01, 'lru_conv_b': 7.979617e+01, 'lru_wa': 3.189902e+00, 'lru_ba': 2.540233e+00, 'lru_wx': 5.984114e+00, 'lru_bx': 5.867949e+00, 'lru_lambda': 5.192369e+00, 's5_lam_re': 9.986350e-01, 's5_lam_im': 7.775993e-01, 's5_b_re': 7.672502e-01, 's5_b_im': 3.770010e-01, 's5_c_re': 2.831445e-01, 's5_c_im': 3.744854e-01, 's5_d': 9.773909e+00, 's5_log_dt': 5.570293e+01, 's5_glu_w': 1.539312e+00, 's5_glu_b': 5.288734e+00, 'w_up': 3.729716e+00, 'w_out': 7.593000e+00, 'norm_xa_pre': 3.470500e+00, 'norm_xa_post': 6.940834e+01, 'norm_mem': 1.793172e+01, 'xa_w_q': 3.151679e+00, 'xa_w_kv': 1.227710e+01, 'xa_w_o': 1.830642e+01, 'norm_ffn_pre': 5.307376e+00, 'norm_ffn_post': 6.374557e+01, 'ffn_w_gu': 2.370291e+00, 'ffn_w_down': 4.849562e+00}


def _to_microbatches(a, axis):
    t = _jnp.moveaxis(a, axis, 0)
    t = t.reshape((N_MICROBATCH, t.shape[0] // N_MICROBATCH) + t.shape[1:])
    return _jnp.moveaxis(t, 1, axis + 1)


def setup_inputs(seed: int = 0) -> dict:
    inp = _fwd_setup_inputs(seed)
    key = _jax.random.fold_in(_jax.random.key(seed), 7919)
    shape, _ = _output_shape()
    out = dict(inp)
    out["loss_target"] = _jax.random.normal(_jax.random.fold_in(key, 0), shape, _jnp.float32)
    for i, name in enumerate(TWIN_WEIGHTS):
        w = inp[name].astype(_jnp.float32)
        if MOMENT_SCALE is None:
            s = _jnp.sqrt(_jnp.mean(_jnp.square(w)) + 1e-30)
        else:
            s = MOMENT_SCALE[name]
        km, kv = _jax.random.split(_jax.random.fold_in(key, i + 1))
        out[name] = w
        out["m_" + name] = s * _jax.random.normal(km, w.shape, _jnp.float32)
        out["v_" + name] = (s * s) * _jax.random.uniform(kv, w.shape, _jnp.float32, 0.5, 1.5)
    if N_MICROBATCH > 1:
        for name, axis in PER_EXAMPLE_BATCH_AXIS.items():
            out[name] = _to_microbatches(out[name], axis)
    return {'x': out['x'], 'mem': out['mem'], 'hg_lower_bounds': out['hg_lower_bounds'], 'norm_mix_pre': out['norm_mix_pre'], 'norm_mix_post': out['norm_mix_post'], 'w_in': out['w_in'], 'w_gate': out['w_gate'], 'b_gate': out['b_gate'], 'hg_norm': out['hg_norm'], 'ret_norm': out['ret_norm'], 'lru_conv_w': out['lru_conv_w'], 'lru_conv_b': out['lru_conv_b'], 'lru_wa': out['lru_wa'], 'lru_ba': out['lru_ba'], 'lru_wx': out['lru_wx'], 'lru_bx': out['lru_bx'], 'lru_lambda': out['lru_lambda'], 's5_lam_re': out['s5_lam_re'], 's5_lam_im': out['s5_lam_im'], 's5_b_re': out['s5_b_re'], 's5_b_im': out['s5_b_im'], 's5_c_re': out['s5_c_re'], 's5_c_im': out['s5_c_im'], 's5_d': out['s5_d'], 's5_log_dt': out['s5_log_dt'], 's5_glu_w': out['s5_glu_w'], 's5_glu_b': out['s5_glu_b'], 'w_up': out['w_up'], 'w_out': out['w_out'], 'norm_xa_pre': out['norm_xa_pre'], 'norm_xa_post': out['norm_xa_post'], 'norm_mem': out['norm_mem'], 'xa_w_q': out['xa_w_q'], 'xa_w_kv': out['xa_w_kv'], 'xa_w_o': out['xa_w_o'], 'norm_ffn_pre': out['norm_ffn_pre'], 'norm_ffn_post': out['norm_ffn_post'], 'ffn_w_gu': out['ffn_w_gu'], 'ffn_w_down': out['ffn_w_down'], 'loss_target': out['loss_target'], 'm_hg_lower_bounds': out['m_hg_lower_bounds'], 'm_norm_mix_pre': out['m_norm_mix_pre'], 'm_norm_mix_post': out['m_norm_mix_post'], 'm_w_in': out['m_w_in'], 'm_w_gate': out['m_w_gate'], 'm_b_gate': out['m_b_gate'], 'm_hg_norm': out['m_hg_norm'], 'm_ret_norm': out['m_ret_norm'], 'm_lru_conv_w': out['m_lru_conv_w'], 'm_lru_conv_b': out['m_lru_conv_b'], 'm_lru_wa': out['m_lru_wa'], 'm_lru_ba': out['m_lru_ba'], 'm_lru_wx': out['m_lru_wx'], 'm_lru_bx': out['m_lru_bx'], 'm_lru_lambda': out['m_lru_lambda'], 'm_s5_lam_re': out['m_s5_lam_re'], 'm_s5_lam_im': out['m_s5_lam_im'], 'm_s5_b_re': out['m_s5_b_re'], 'm_s5_b_im': out['m_s5_b_im'], 'm_s5_c_re': out['m_s5_c_re'], 'm_s5_c_im': out['m_s5_c_im'], 'm_s5_d': out['m_s5_d'], 'm_s5_log_dt': out['m_s5_log_dt'], 'm_s5_glu_w': out['m_s5_glu_w'], 'm_s5_glu_b': out['m_s5_glu_b'], 'm_w_up': out['m_w_up'], 'm_w_out': out['m_w_out'], 'm_norm_xa_pre': out['m_norm_xa_pre'], 'm_norm_xa_post': out['m_norm_xa_post'], 'm_norm_mem': out['m_norm_mem'], 'm_xa_w_q': out['m_xa_w_q'], 'm_xa_w_kv': out['m_xa_w_kv'], 'm_xa_w_o': out['m_xa_w_o'], 'm_norm_ffn_pre': out['m_norm_ffn_pre'], 'm_norm_ffn_post': out['m_norm_ffn_post'], 'm_ffn_w_gu': out['m_ffn_w_gu'], 'm_ffn_w_down': out['m_ffn_w_down'], 'v_hg_lower_bounds': out['v_hg_lower_bounds'], 'v_norm_mix_pre': out['v_norm_mix_pre'], 'v_norm_mix_post': out['v_norm_mix_post'], 'v_w_in': out['v_w_in'], 'v_w_gate': out['v_w_gate'], 'v_b_gate': out['v_b_gate'], 'v_hg_norm': out['v_hg_norm'], 'v_ret_norm': out['v_ret_norm'], 'v_lru_conv_w': out['v_lru_conv_w'], 'v_lru_conv_b': out['v_lru_conv_b'], 'v_lru_wa': out['v_lru_wa'], 'v_lru_ba': out['v_lru_ba'], 'v_lru_wx': out['v_lru_wx'], 'v_lru_bx': out['v_lru_bx'], 'v_lru_lambda': out['v_lru_lambda'], 'v_s5_lam_re': out['v_s5_lam_re'], 'v_s5_lam_im': out['v_s5_lam_im'], 'v_s5_b_re': out['v_s5_b_re'], 'v_s5_b_im': out['v_s5_b_im'], 'v_s5_c_re': out['v_s5_c_re'], 'v_s5_c_im': out['v_s5_c_im'], 'v_s5_d': out['v_s5_d'], 'v_s5_log_dt': out['v_s5_log_dt'], 'v_s5_glu_w': out['v_s5_glu_w'], 'v_s5_glu_b': out['v_s5_glu_b'], 'v_w_up': out['v_w_up'], 'v_w_out': out['v_w_out'], 'v_norm_xa_pre': out['v_norm_xa_pre'], 'v_norm_xa_post': out['v_norm_xa_post'], 'v_norm_mem': out['v_norm_mem'], 'v_xa_w_q': out['v_xa_w_q'], 'v_xa_w_kv': out['v_xa_w_kv'], 'v_xa_w_o': out['v_xa_w_o'], 'v_norm_ffn_pre': out['v_norm_ffn_pre'], 'v_norm_ffn_post': out['v_norm_ffn_post'], 'v_ffn_w_gu': out['v_ffn_w_gu'], 'v_ffn_w_down': out['v_ffn_w_down']}


def _loss(weights, diff, rest, loss_target):
    with _jax.named_scope("forward"):
        args = {**rest, TWIN_DIFF_INPUT: diff, **{k: w.astype(_WEIGHT_DTYPES[k]) for k, w in weights.items()}}
        y = _forward(args)
    with _jax.named_scope("loss_head"):
        err = _jnp.square(y.astype(_jnp.float32) - loss_target)
        return 0.5 * _jnp.sum(_jnp.mean(err, axis=-1)) if err.ndim else 0.5 * err


def _adamw(w, g, m, v):
    m = ADAM_B1 * m + (1.0 - ADAM_B1) * g
    v = ADAM_B2 * v + (1.0 - ADAM_B2) * _jnp.square(g)
    m_hat = m / (1.0 - ADAM_B1 ** ADAM_STEP)
    v_hat = v / (1.0 - ADAM_B2 ** ADAM_STEP)
    delta = -ADAM_LR * (m_hat / (_jnp.sqrt(v_hat) + ADAM_EPS) + ADAM_WD * w)
    return delta, m, v


def reference(x, mem, hg_lower_bounds, norm_mix_pre, norm_mix_post, w_in, w_gate, b_gate, hg_norm, ret_norm, lru_conv_w, lru_conv_b, lru_wa, lru_ba, lru_wx, lru_bx, lru_lambda, s5_lam_re, s5_lam_im, s5_b_re, s5_b_im, s5_c_re, s5_c_im, s5_d, s5_log_dt, s5_glu_w, s5_glu_b, w_up, w_out, norm_xa_pre, norm_xa_post, norm_mem, xa_w_q, xa_w_kv, xa_w_o, norm_ffn_pre, norm_ffn_post, ffn_w_gu, ffn_w_down, loss_target, m_hg_lower_bounds, m_norm_mix_pre, m_norm_mix_post, m_w_in, m_w_gate, m_b_gate, m_hg_norm, m_ret_norm, m_lru_conv_w, m_lru_conv_b, m_lru_wa, m_lru_ba, m_lru_wx, m_lru_bx, m_lru_lambda, m_s5_lam_re, m_s5_lam_im, m_s5_b_re, m_s5_b_im, m_s5_c_re, m_s5_c_im, m_s5_d, m_s5_log_dt, m_s5_glu_w, m_s5_glu_b, m_w_up, m_w_out, m_norm_xa_pre, m_norm_xa_post, m_norm_mem, m_xa_w_q, m_xa_w_kv, m_xa_w_o, m_norm_ffn_pre, m_norm_ffn_post, m_ffn_w_gu, m_ffn_w_down, v_hg_lower_bounds, v_norm_mix_pre, v_norm_mix_post, v_w_in, v_w_gate, v_b_gate, v_hg_norm, v_ret_norm, v_lru_conv_w, v_lru_conv_b, v_lru_wa, v_lru_ba, v_lru_wx, v_lru_bx, v_lru_lambda, v_s5_lam_re, v_s5_lam_im, v_s5_b_re, v_s5_b_im, v_s5_c_re, v_s5_c_im, v_s5_d, v_s5_log_dt, v_s5_glu_w, v_s5_glu_b, v_w_up, v_w_out, v_norm_xa_pre, v_norm_xa_post, v_norm_mem, v_xa_w_q, v_xa_w_kv, v_xa_w_o, v_norm_ffn_pre, v_norm_ffn_post, v_ffn_w_gu, v_ffn_w_down):
    given = dict(x=x, mem=mem, hg_lower_bounds=hg_lower_bounds, norm_mix_pre=norm_mix_pre, norm_mix_post=norm_mix_post, w_in=w_in, w_gate=w_gate, b_gate=b_gate, hg_norm=hg_norm, ret_norm=ret_norm, lru_conv_w=lru_conv_w, lru_conv_b=lru_conv_b, lru_wa=lru_wa, lru_ba=lru_ba, lru_wx=lru_wx, lru_bx=lru_bx, lru_lambda=lru_lambda, s5_lam_re=s5_lam_re, s5_lam_im=s5_lam_im, s5_b_re=s5_b_re, s5_b_im=s5_b_im, s5_c_re=s5_c_re, s5_c_im=s5_c_im, s5_d=s5_d, s5_log_dt=s5_log_dt, s5_glu_w=s5_glu_w, s5_glu_b=s5_glu_b, w_up=w_up, w_out=w_out, norm_xa_pre=norm_xa_pre, norm_xa_post=norm_xa_post, norm_mem=norm_mem, xa_w_q=xa_w_q, xa_w_kv=xa_w_kv, xa_w_o=xa_w_o, norm_ffn_pre=norm_ffn_pre, norm_ffn_post=norm_ffn_post, ffn_w_gu=ffn_w_gu, ffn_w_down=ffn_w_down, loss_target=loss_target, m_hg_lower_bounds=m_hg_lower_bounds, m_norm_mix_pre=m_norm_mix_pre, m_norm_mix_post=m_norm_mix_post, m_w_in=m_w_in, m_w_gate=m_w_gate, m_b_gate=m_b_gate, m_hg_norm=m_hg_norm, m_ret_norm=m_ret_norm, m_lru_conv_w=m_lru_conv_w, m_lru_conv_b=m_lru_conv_b, m_lru_wa=m_lru_wa, m_lru_ba=m_lru_ba, m_lru_wx=m_lru_wx, m_lru_bx=m_lru_bx, m_lru_lambda=m_lru_lambda, m_s5_lam_re=m_s5_lam_re, m_s5_lam_im=m_s5_lam_im, m_s5_b_re=m_s5_b_re, m_s5_b_im=m_s5_b_im, m_s5_c_re=m_s5_c_re, m_s5_c_im=m_s5_c_im, m_s5_d=m_s5_d, m_s5_log_dt=m_s5_log_dt, m_s5_glu_w=m_s5_glu_w, m_s5_glu_b=m_s5_glu_b, m_w_up=m_w_up, m_w_out=m_w_out, m_norm_xa_pre=m_norm_xa_pre, m_norm_xa_post=m_norm_xa_post, m_norm_mem=m_norm_mem, m_xa_w_q=m_xa_w_q, m_xa_w_kv=m_xa_w_kv, m_xa_w_o=m_xa_w_o, m_norm_ffn_pre=m_norm_ffn_pre, m_norm_ffn_post=m_norm_ffn_post, m_ffn_w_gu=m_ffn_w_gu, m_ffn_w_down=m_ffn_w_down, v_hg_lower_bounds=v_hg_lower_bounds, v_norm_mix_pre=v_norm_mix_pre, v_norm_mix_post=v_norm_mix_post, v_w_in=v_w_in, v_w_gate=v_w_gate, v_b_gate=v_b_gate, v_hg_norm=v_hg_norm, v_ret_norm=v_ret_norm, v_lru_conv_w=v_lru_conv_w, v_lru_conv_b=v_lru_conv_b, v_lru_wa=v_lru_wa, v_lru_ba=v_lru_ba, v_lru_wx=v_lru_wx, v_lru_bx=v_lru_bx, v_lru_lambda=v_lru_lambda, v_s5_lam_re=v_s5_lam_re, v_s5_lam_im=v_s5_lam_im, v_s5_b_re=v_s5_b_re, v_s5_b_im=v_s5_b_im, v_s5_c_re=v_s5_c_re, v_s5_c_im=v_s5_c_im, v_s5_d=v_s5_d, v_s5_log_dt=v_s5_log_dt, v_s5_glu_w=v_s5_glu_w, v_s5_glu_b=v_s5_glu_b, v_w_up=v_w_up, v_w_out=v_w_out, v_norm_xa_pre=v_norm_xa_pre, v_norm_xa_post=v_norm_xa_post, v_norm_mem=v_norm_mem, v_xa_w_q=v_xa_w_q, v_xa_w_kv=v_xa_w_kv, v_xa_w_o=v_xa_w_o, v_norm_ffn_pre=v_norm_ffn_pre, v_norm_ffn_post=v_norm_ffn_post, v_ffn_w_gu=v_ffn_w_gu, v_ffn_w_down=v_ffn_w_down)
    weights = {n: given[n] for n in TWIN_WEIGHTS}
    shared = {n: given[n] for n in SHARED_INPUTS}
    per_example = {n: given[n] for n in ['x', 'mem']}
    grad_fn = _jax.value_and_grad(_loss, argnums=(0, 1))

    def one_microbatch(ex, loss_target):
        ex = dict(ex)
        diff = ex.pop(TWIN_DIFF_INPUT)
        return grad_fn(weights, diff, {**shared, **ex}, loss_target)

    if N_MICROBATCH == 1:
        loss, (grad_w, grad_x) = one_microbatch(per_example, given["loss_target"])
    else:
        def body(carry, xs):
            loss_sum, grad_sum = carry
            l_k, (gw_k, gx_k) = one_microbatch(xs[0], xs[1])
            with _jax.named_scope("update"):
                return (loss_sum + l_k, _jax.tree.map(_jnp.add, grad_sum, gw_k)), gx_k

        init = (_jnp.zeros((), _jnp.float32), _jax.tree.map(_jnp.zeros_like, weights))
        (loss, grad_w), grad_x = _jax.lax.scan(body, init, (per_example, given["loss_target"]))
    with _jax.named_scope("update"):
        delta_w, new_m, new_v = {}, {}, {}
        for n in TWIN_WEIGHTS:
            delta_w[n], new_m[n], new_v[n] = _adamw(weights[n], grad_w[n], given["m_" + n], given["v_" + n])
    return (loss, grad_x, *[grad_w[n] for n in TWIN_WEIGHTS], *[delta_w[n] for n in TWIN_WEIGHTS],
            *[new_m[n] for n in TWIN_WEIGHTS], *[new_v[n] for n in TWIN_WEIGHTS])
```

```python
import functools

import numpy as np
import jax
import jax.numpy as jnp
from jax import lax
from jax.experimental import pallas as pl
from jax.experimental.pallas import tpu as pltpu

F32 = jnp.float32
BF16 = jnp.bfloat16
EPS = 1e-6
N_DEV = 8
LANES = 128
VMEM_LIMIT = 60 * 1024 * 1024

HEADS = 4
HEAD_DIM = 64
MIX_W = HEADS * HEAD_DIM
HG_CHUNK = 32
RET_CHUNK = 128
S5_GROUPS = 16
S5_GROUP = 16
S5_STATE = 64
LRU_C = 8.0
XA_HEADS = 4

ADAM_LR = 0.001
ADAM_B1 = 0.9
ADAM_B2 = 0.999
ADAM_EPS = 1e-08
ADAM_WD = 0.01
ADAM_STEP = 10

BIG = (("w_in", 2), ("w_gate", 2), ("w_up", 3), ("w_out", 1), ("xa_w_q", 1), ("xa_w_kv", 2), ("xa_w_o", 1),
       ("ffn_w_gu", 2), ("ffn_w_down", 1))
SMALL_SHARDED = (("lru_conv_w", 2), ("s5_glu_w", 1))
WEIGHTS = ("hg_lower_bounds", "norm_mix_pre", "norm_mix_post", "w_in", "w_gate", "b_gate", "hg_norm", "ret_norm",
           "lru_conv_w", "lru_conv_b", "lru_wa", "lru_ba", "lru_wx", "lru_bx", "lru_lambda", "s5_lam_re", "s5_lam_im",
           "s5_b_re", "s5_b_im", "s5_c_re", "s5_c_im", "s5_d", "s5_log_dt", "s5_glu_w", "s5_glu_b", "w_up", "w_out",
           "norm_xa_pre", "norm_xa_post", "norm_mem", "xa_w_q", "xa_w_kv", "xa_w_o", "norm_ffn_pre", "norm_ffn_post",
           "ffn_w_gu", "ffn_w_down")
_BIG_NAMES = tuple(n for n, _ in BIG)
_SMALL_SHARDED_NAMES = tuple(n for n, _ in SMALL_SHARDED)
REPLICATED = tuple(n for n in WEIGHTS if n not in _BIG_NAMES and n not in _SMALL_SHARDED_NAMES)


def _dot(a, b):
    return jnp.dot(a.astype(BF16), b.astype(BF16), preferred_element_type=F32)


def _dot_nt(a, b):
    return lax.dot_general(a.astype(BF16), b.astype(BF16), (((1,), (1,)), ((), ())), preferred_element_type=F32)


def _dot_tn(a, b):
    return lax.dot_general(a.astype(BF16), b.astype(BF16), (((0,), (0,)), ((), ())), preferred_element_type=F32)


def _dot_exact(a, b):
    return jnp.dot(a, b, precision=lax.Precision.HIGHEST, preferred_element_type=F32)


def _rms(x, g):
    return x * lax.rsqrt(jnp.mean(x * x, axis=-1, keepdims=True) + EPS) * g


def _shift_down(x, d, fill):
    return jnp.concatenate([jnp.full((d, x.shape[1]), fill, x.dtype), x[:-d]], axis=0)


def _shift_up(x, d, fill):
    return jnp.concatenate([x[d:], jnp.full((d, x.shape[1]), fill, x.dtype)], axis=0)


def _cumsum_rows(x):
    d = 1
    while d < x.shape[0]:
        x = x + _shift_down(x, d, 0.0)
        d *= 2
    return x


def _lane_head(shape, dim):
    return lax.shift_right_logical(lax.broadcasted_iota(jnp.int32, shape, dim), 6)


def _head_masks(width=MIX_W):
    head = _lane_head((1, width), 1)
    return [(head == h).astype(F32) for h in range(HEADS)]


def _block_diag_mask():
    return (_lane_head((MIX_W, MIX_W), 0) == _lane_head((MIX_W, MIX_W), 1)).astype(F32)


def _head_rms(o, g):
    ms = _dot_exact(o * o, _block_diag_mask()) * (1.0 / HEAD_DIM)
    return o * lax.rsqrt(ms + EPS) * g


def _swap_pairs(x):
    lane = lax.broadcasted_iota(jnp.int32, x.shape, 1)
    return jnp.where((lane & 1) == 0, jnp.roll(x, -1, axis=1), jnp.roll(x, 1, axis=1))


def _stack_heads(t, masks):
    return jnp.concatenate([t * m for m in masks], axis=0)


@jax.custom_vjp
def _real_scan(a, u, h0):
    return _real_scan_fwd(a, u, h0)[0]


def _real_scan_fwd(a, u, h0):
    t = a.shape[0]
    acc_a, acc_u = a, u
    d = 1
    while d < t:
        acc_u = acc_u + acc_a * _shift_down(acc_u, d, 0.0)
        acc_a = acc_a * _shift_down(acc_a, d, 1.0)
        d *= 2
    h = acc_u + acc_a * h0
    return h, (a, h, h0)


def _real_scan_bwd(res, dh):
    a, h, h0 = res
    t = a.shape[0]
    acc_a = _shift_up(a, 1, 0.0)
    g = dh
    d = 1
    while d < t:
        g = g + acc_a * _shift_up(g, d, 0.0)
        acc_a = acc_a * _shift_up(acc_a, d, 1.0)
        d *= 2
    h_prev = jnp.concatenate([h0, h[:-1]], axis=0)
    return g * h_prev, g, (a * g)[0:1]


_real_scan.defvjp(_real_scan_fwd, _real_scan_bwd)


def _cmul(ar, ai, br, bi):
    return ar * br - ai * bi, ar * bi + ai * br


@jax.custom_vjp
def _complex_scan(ar, ai, ur, ui, h0r, h0i):
    return _complex_scan_fwd(ar, ai, ur, ui, h0r, h0i)[0]


def _complex_scan_fwd(ar, ai, ur, ui, h0r, h0i):
    t = ar.shape[0]
    pr, pi, sr, si = ar, ai, ur, ui
    d = 1
    while d < t:
        mr, mi = _cmul(pr, pi, _shift_down(sr, d, 0.0), _shift_down(si, d, 0.0))
        sr, si = sr + mr, si + mi
        pr, pi = _cmul(pr, pi, _shift_down(pr, d, 1.0), _shift_down(pi, d, 0.0))
        d *= 2
    mr, mi = _cmul(pr, pi, h0r, h0i)
    hr, hi = sr + mr, si + mi
    return (hr, hi), (ar, ai, hr, hi, h0r, h0i)


def _complex_scan_bwd(res, dh):
    ar, ai, hr, hi, h0r, h0i = res
    dhr, dhi = dh
    t = ar.shape[0]
    pr, pi = _shift_up(ar, 1, 0.0), -_shift_up(ai, 1, 0.0)
    gr, gi = dhr, dhi
    d = 1
    while d < t:
        mr, mi = _cmul(pr, pi, _shift_up(gr, d, 0.0), _shift_up(gi, d, 0.0))
        gr, gi = gr + mr, gi + mi
        pr, pi = _cmul(pr, pi, _shift_up(pr, d, 1.0), _shift_up(pi, d, 0.0))
        d *= 2
    qr = jnp.concatenate([h0r, hr[:-1]], axis=0)
    qi = jnp.concatenate([h0i, hi[:-1]], axis=0)
    dar, dai = gr * qr + gi * qi, gi * qr - gr * qi
    d0r, d0i = _cmul(ar[0:1], -ai[0:1], gr[0:1], gi[0:1])
    return dar, dai, gr, gi, d0r, d0i


_complex_scan.defvjp(_complex_scan_fwd, _complex_scan_bwd)


def fn_norm(st, rows, params):
    (x,), (g,) = rows, params
    return (), (_rms(x, g),), ()


def fn_keep_norm(st, rows, params):
    (x,), (g,) = rows, params
    return (), (x, _rms(x, g)), ()


def fn_hgrn2(st, rows, params):
    (state,) = st
    (z,) = rows
    lb, norm_g = params
    q, f_logit, v_all, g = (z[:, k * MIX_W:(k + 1) * MIX_W] for k in range(4))
    f = lb + (1.0 - lb) * jax.nn.sigmoid(f_logit)
    log_f = jnp.log(f)
    k_all = 1.0 - f
    q_all = jax.nn.silu(q)
    masks = _head_masks()
    bd = _block_diag_mask()
    c = HG_CHUNK
    col = lax.broadcasted_iota(jnp.int32, (c, HEADS * c), 1) & (c - 1)
    causal = col <= lax.broadcasted_iota(jnp.int32, (c, HEADS * c), 0)
    outs = []
    for n in range(z.shape[0] // c):
        sl = slice(n * c, (n + 1) * c)
        lf = log_f[sl]
        b = _cumsum_rows(lf)
        b_end = jnp.sum(lf, axis=0, keepdims=True)
        q_dec = q_all[sl] * jnp.exp(b)
        k_inv = k_all[sl] * jnp.exp(-b)
        k_end = k_all[sl] * jnp.exp(b_end - b)
        v = v_all[sl]
        scores = jnp.where(causal, _dot_nt(q_dec, _stack_heads(k_inv, masks)), 0.0)
        outs.append(_dot(scores, _stack_heads(v, masks)) + _dot_nt(q_dec, state))
        state = state * jnp.exp(b_end) + _dot_tn(v, k_end) * bd
    o = jnp.concatenate(outs, axis=0) if len(outs) > 1 else outs[0]
    return (state,), (_head_rms(o, norm_g) * jax.nn.silu(g),), ()


def fn_retention(st, rows, params):
    (state,) = st
    z, cos_t, sin_t = rows
    norm_g, xi, zeta, decay, g_end = params
    q, k, v_all, g = (z[:, i * MIX_W:(i + 1) * MIX_W] for i in range(4))
    q_all = q * cos_t + _swap_pairs(q) * sin_t
    k_all = (k * cos_t + _swap_pairs(k) * sin_t) * (HEAD_DIM ** -0.5)
    masks = _head_masks()
    bd = _block_diag_mask()
    c = RET_CHUNK
    outs = []
    for n in range(z.shape[0] // c):
        sl = slice(n * c, (n + 1) * c)
        qc, kc, v = q_all[sl], k_all[sl], v_all[sl]
        scores = _dot_nt(qc, _stack_heads(kc, masks)) * decay
        outs.append(_dot(scores, _stack_heads(v, masks)) + _dot_nt(qc * xi, state))
        state = state * g_end + _dot_tn(v, kc * zeta) * bd
    o = jnp.concatenate(outs, axis=0) if len(outs) > 1 else outs[0]
    return (state,), (_head_rms(o, norm_g) * jax.nn.silu(g),), ()


def fn_rglru(st, rows, params):
    tail_x, tail_h = st
    (z,) = rows
    conv_w, conv_b, wa, ba, wx, bx, sp = params
    t = z.shape[0]
    xg, xi = z[:, :MIX_W], z[:, MIX_W:]
    full = jnp.concatenate([tail_x, xi], axis=0)
    xc = conv_b
    for k in range(4):
        xc = xc + conv_w[k:k + 1] * full[5 + k:5 + k + t]
    r = jax.nn.sigmoid(_dot(xc, wa) + ba)
    ig = jax.nn.sigmoid(_dot(xc, wx) + bx)
    log_a = -LRU_C * r * sp
    a = jnp.exp(log_a)
    one_minus_a2 = -jnp.tanh(log_a) * (a * a + 1.0)
    u = jnp.sqrt(one_minus_a2) * (ig * xc)
    h = _real_scan(a, u, tail_h[7:8])
    return (xi[t - 8:], h[t - 8:]), (h * jax.nn.gelu(xg),), ()


def fn_s5(st, rows, params):
    tail_r, tail_i = st
    (u,) = rows
    bt_re, bt_im, lb_re, lb_im, ct_re, ct_im, d, glu_w, glu_b = params
    t = u.shape[0]
    bu_re = _dot(u, bt_re)
    bu_im = _dot(u, bt_im)
    a_re = jnp.broadcast_to(lb_re, bu_re.shape)
    a_im = jnp.broadcast_to(lb_im, bu_im.shape)
    h_re, h_im = _complex_scan(a_re, a_im, bu_re, bu_im, tail_r[7:8], tail_i[7:8])
    y = _dot(h_re, ct_re) - _dot(h_im, ct_im) + d * u
    act = jax.nn.gelu(y)
    out = act * jax.nn.sigmoid(_dot(act, glu_w) + glu_b)
    return (h_re[t - 8:], h_im[t - 8:]), (out,), ()


def fn_merge(st, rows, params):
    ya, yb, yc, yd, gl, x = rows
    w0, w1, w2, w3, b_gate, w_out, g_post = params
    d = x.shape[1]
    mix = None
    for n, (y, w) in enumerate(((ya, w0), (yb, w1), (yc, w2), (yd, w3))):
        gate = jax.nn.sigmoid(gl[:, n * d:(n + 1) * d] + b_gate[:, n * d:(n + 1) * d])
        term = gate * _dot(y, w)
        mix = term if mix is None else mix + term
    return (), (x + _rms(_dot(mix, w_out), g_post),), ()


def fn_mem(st, rows, params):
    (mem,), (g, wk, wv) = rows, params
    m = _rms(mem, g)
    return (), (_dot(m, wk), _dot(m, wv)), ()


def fn_xattn(st, rows, params):
    (x,) = rows
    g_pre, wq, k, v, wo, g_post, g_next = params
    d = x.shape[1]
    dh = d // XA_HEADS
    q = _dot(_rms(x, g_pre), wq)
    heads = []
    for h in range(XA_HEADS):
        sl = slice(h * dh, (h + 1) * dh)
        s = _dot_nt(q[:, sl], k[:, sl]) * (dh ** -0.5)
        heads.append(_dot(jax.nn.softmax(s, axis=-1), v[:, sl]))
    x2 = x + _rms(_dot(jnp.concatenate(heads, axis=1), wo), g_post)
    return (), (x2, _rms(x2, g_next)), ()


def fn_glu(st, rows, params):
    (gu,) = rows
    f = gu.shape[1] // 2
    return (), (jax.nn.silu(gu[:, :f]) * gu[:, f:],), ()


def fn_res_norm(st, rows, params):
    (x, o), (g_post, g_next) = rows, params
    xn = x + _rms(o, g_post)
    return (), (xn, _rms(xn, g_next)), ()


def fn_res(st, rows, params):
    (x, o), (g_post,) = rows, params
    return (), (x + _rms(o, g_post),), ()


def fn_res_loss(st, rows, params):
    (x, o, target), (g_post,) = rows, params
    err = x + _rms(o, g_post) - target
    inv_d = 1.0 / x.shape[1]
    loss = 0.5 * inv_d * jnp.sum(err * err)
    return (), (err * inv_d,), (jnp.full((8, LANES), loss, F32),)


def _params():
    return pltpu.CompilerParams(dimension_semantics=("arbitrary",), vmem_limit_bytes=VMEM_LIMIT)


def _row_spec(tb, width, colblk, nb, reverse):
    if reverse:
        return pl.BlockSpec((tb, width), lambda i: (nb - 1 - i, colblk))
    return pl.BlockSpec((tb, width), lambda i: (i, colblk))


def _full_spec(shape):
    return pl.BlockSpec(shape, lambda i: (0,) * len(shape))


def _saved_spec(shape, nb, reverse):
    if reverse:
        return pl.BlockSpec((1,) + shape, lambda i: (nb - 1 - i, 0, 0))
    return pl.BlockSpec((1,) + shape, lambda i: (i, 0, 0))


def stage_fwd(name, fn, rows, params, *, tb, outs, states=(), accs=()):
    n_rows = rows[0][0].shape[0]
    nb = n_rows // tb
    nr, npar, no, na, ns = len(rows), len(params), len(outs), len(accs), len(states)

    def body(*refs):
        row_refs, par_refs = refs[:nr], refs[nr:nr + npar]
        o = nr + npar
        out_refs, acc_refs = refs[o:o + no], refs[o + no:o + no + na]
        saved_refs = refs[o + no + na:o + no + na + ns]
        st_refs = refs[o + no + na + ns:]
        i = pl.program_id(0)

        @pl.when(i == 0)
        def _():
            for r in st_refs + acc_refs:
                r[...] = jnp.zeros_like(r)

        st = tuple(r[...] for r in st_refs)
        for sv, s in zip(saved_refs, st):
            sv[0] = s
        new_st, out_vals, acc_vals = fn(st, tuple(r[...].astype(F32) for r in row_refs),
                                        tuple(r[...].astype(F32) for r in par_refs))
        for r, v in zip(out_refs, out_vals):
            r[...] = v.astype(r.dtype)
        for r, v in zip(acc_refs, acc_vals):
            r[...] += v
        for r, v in zip(st_refs, new_st):
            r[...] = v

    res = pl.pallas_call(
        body, name=name, grid=(nb,),
        in_specs=[_row_spec(tb, w, cb, nb, False) for _, w, cb in rows] + [_full_spec(p.shape) for p in params],
        out_specs=[_row_spec(tb, w, 0, nb, False) for w, _ in outs] + [_full_spec(s) for s in accs]
        + [_saved_spec(s, nb, False) for s in states],
        out_shape=[jax.ShapeDtypeStruct((n_rows, w), dt) for w, dt in outs]
        + [jax.ShapeDtypeStruct(s, F32) for s in accs] + [jax.ShapeDtypeStruct((nb,) + s, F32) for s in states],
        scratch_shapes=[pltpu.VMEM(s, F32) for s in states],
        compiler_params=_params(),
    )(*[a for a, _, _ in rows], *params)
    return res[:no], res[no:no + na], res[no + na:]


def stage_bwd(name, fn, rows, params, cts, *, tb, saved=(), diff_rows=(), diff_params=(), row_dtypes=None):
    n_rows = rows[0][0].shape[0]
    nb = n_rows // tb
    nr, npar, ns, nc = len(rows), len(params), len(saved), len(cts)
    ndr, ndp = len(diff_rows), len(diff_params)
    row_dtypes = row_dtypes or (F32,) * ndr
    state_shapes = [s.shape[1:] for s in saved]

    def body(*refs):
        row_refs, par_refs = refs[:nr], refs[nr:nr + npar]
        o = nr + npar
        saved_refs, ct_refs = refs[o:o + ns], refs[o + ns:o + ns + nc]
        o = o + ns + nc
        drow_refs, dpar_refs, dst_refs = refs[o:o + ndr], refs[o + ndr:o + ndr + ndp], refs[o + ndr + ndp:]
        i = pl.program_id(0)

        @pl.when(i == 0)
        def _():
            for r in dst_refs + dpar_refs:
                r[...] = jnp.zeros_like(r)

        st = tuple(r[0] for r in saved_refs)
        row_vals = [r[...].astype(F32) for r in row_refs]
        par_vals = [r[...].astype(F32) for r in par_refs]

        def f(st_, dr_, dp_):
            rv, pv = list(row_vals), list(par_vals)
            for k, v in zip(diff_rows, dr_):
                rv[k] = v
            for k, v in zip(diff_params, dp_):
                pv[k] = v
            new_st, out_vals, _ = fn(st_, tuple(rv), tuple(pv))
            return new_st, out_vals

        _, vjp = jax.vjp(f, st, tuple(row_vals[k] for k in diff_rows), tuple(par_vals[k] for k in diff_params))
        g_st, g_rows, g_par = vjp((tuple(r[...] for r in dst_refs), tuple(r[...].astype(F32) for r in ct_refs)))
        for r, v in zip(drow_refs, g_rows):
            r[...] = v.astype(r.dtype)
        for r, v in zip(dpar_refs, g_par):
            r[...] += v
        for r, v in zip(dst_refs, g_st):
            r[...] = v

    res = pl.pallas_call(
        body, name=name, grid=(nb,),
        in_specs=[_row_spec(tb, w, cb, nb, True) for _, w, cb in rows] + [_full_spec(p.shape) for p in params]
        + [_saved_spec(s, nb, True) for s in state_shapes] + [_row_spec(tb, c.shape[1], 0, nb, True) for c in cts],
        out_specs=[_row_spec(tb, rows[k][1], 0, nb, True) for k in diff_rows]
        + [_full_spec(params[k].shape) for k in diff_params],
        out_shape=[jax.ShapeDtypeStruct((n_rows, rows[k][1]), dt) for k, dt in zip(diff_rows, row_dtypes)]
        + [jax.ShapeDtypeStruct(params[k].shape, F32) for k in diff_params],
        scratch_shapes=[pltpu.VMEM(s, F32) for s in state_shapes],
        compiler_params=_params(),
    )(*[a for a, _, _ in rows], *params, *saved, *cts)
    return res[:ndr], res[ndr:]


def _pick(n, target):
    if n <= target:
        return n
    best = None
    for t in range(LANES, target + 1, LANES):
        if n % t == 0:
            best = t
    assert best is not None, n
    return best


def matmul(name, a, b, mode, out_dtype, add=None):
    if mode == "tn":
        k, m = a.shape
    else:
        m, k = a.shape
    n = b.shape[0] if mode == "nt" else b.shape[1]
    tm, tn, tk = _pick(m, 512), _pick(n, 512), _pick(k, 1024 if mode != "tn" else 512)
    nk = k // tk
    a_spec = pl.BlockSpec((tk, tm), lambda i, j, kk: (kk, i)) if mode == "tn" else pl.BlockSpec((tm, tk), lambda i, j, kk: (i, kk))
    b_spec = pl.BlockSpec((tn, tk), lambda i, j, kk: (j, kk)) if mode == "nt" else pl.BlockSpec((tk, tn), lambda i, j, kk: (kk, j))
    o_spec = pl.BlockSpec((tm, tn), lambda i, j, kk: (i, j))
    dims = {"nn": (((1,), (0,)), ((), ())), "nt": (((1,), (1,)), ((), ())), "tn": (((0,), (0,)), ((), ()))}[mode]
    has_add = add is not None

    def body(*refs):
        a_ref, b_ref = refs[0], refs[1]
        o_ref, acc_ref = refs[-2], refs[-1]
        kk = pl.program_id(2)

        @pl.when(kk == 0)
        def _():
            acc_ref[...] = refs[2][...].astype(F32) if has_add else jnp.zeros_like(acc_ref)

        acc_ref[...] += lax.dot_general(a_ref[...].astype(BF16), b_ref[...].astype(BF16), dims,
                                        preferred_element_type=F32)

        @pl.when(kk == nk - 1)
        def _():
            o_ref[...] = acc_ref[...].astype(o_ref.dtype)

    return pl.pallas_call(
        body, name=name, grid=(m // tm, n // tn, nk),
        in_specs=[a_spec, b_spec] + ([o_spec] if has_add else []),
        out_specs=o_spec, out_shape=jax.ShapeDtypeStruct((m, n), out_dtype),
        scratch_shapes=[pltpu.VMEM((tm, tn), F32)],
        compiler_params=pltpu.CompilerParams(dimension_semantics=("parallel", "parallel", "arbitrary"),
                                             vmem_limit_bytes=VMEM_LIMIT),
    )(a, b, *([add] if has_add else []))


def _mesh_position():
    return lax.axis_index("x"), lax.axis_index("y"), lax.axis_index("c")


def _peer(pos, k):
    x, y, c = pos
    px = 1 - x if k & 4 else x
    py = 1 - y if k & 2 else y
    pc = 1 - c if k & 1 else c
    return (px, py, pc), 4 * px + 2 * py + pc


def _exchange(name, x, gather):
    rows = x.shape[-2]

    def body(x_ref, o_ref, send_sems, recv_sems, local_sem):
        pos = _mesh_position()
        me = 4 * pos[0] + 2 * pos[1] + pos[2]
        mine = pltpu.make_async_copy(x_ref if gather else x_ref.at[me], o_ref.at[me], local_sem)
        mine.start()
        sends = []
        for k in range(1, N_DEV):
            peer, peer_idx = _peer(pos, k)
            cp = pltpu.make_async_remote_copy(
                src_ref=x_ref if gather else x_ref.at[peer_idx], dst_ref=o_ref.at[me],
                send_sem=send_sems.at[k - 1], recv_sem=recv_sems.at[k - 1],
                device_id=peer, device_id_type=pl.DeviceIdType.MESH)
            cp.start()
            sends.append(cp)
        for k in range(1, N_DEV):
            peer, peer_idx = _peer(pos, k)
            pltpu.make_async_remote_copy(
                src_ref=x_ref if gather else x_ref.at[me], dst_ref=o_ref.at[peer_idx],
                send_sem=send_sems.at[k - 1], recv_sem=recv_sems.at[k - 1],
                device_id=peer, device_id_type=pl.DeviceIdType.MESH).wait_recv()
        for cp in sends:
            cp.wait_send()
        mine.wait()

    return pl.pallas_call(
        body, name=name,
        in_specs=[pl.BlockSpec(memory_space=pl.ANY)], out_specs=pl.BlockSpec(memory_space=pl.ANY),
        out_shape=jax.ShapeDtypeStruct((N_DEV, rows, LANES), x.dtype),
        scratch_shapes=[pltpu.SemaphoreType.DMA((N_DEV - 1,)), pltpu.SemaphoreType.DMA((N_DEV - 1,)),
                        pltpu.SemaphoreType.DMA],
    )(x)


def adamw(name, w, m, v, gparts):
    rows = w.shape[0]
    parts = gparts.shape[0]
    tr = _pick_rows(rows)
    c1 = 1.0 - ADAM_B1 ** ADAM_STEP
    c2 = 1.0 - ADAM_B2 ** ADAM_STEP

    def body(w_ref, m_ref, v_ref, g_ref, go_ref, d_ref, mo_ref, vo_ref):
        g = g_ref[0]
        for p in range(1, parts):
            g = g + g_ref[p]
        m_new = ADAM_B1 * m_ref[...] + (1.0 - ADAM_B1) * g
        v_new = ADAM_B2 * v_ref[...] + (1.0 - ADAM_B2) * (g * g)
        m_hat = m_new / c1
        v_hat = v_new / c2
        go_ref[...] = g
        d_ref[...] = -ADAM_LR * (m_hat / (jnp.sqrt(v_hat) + ADAM_EPS) + ADAM_WD * w_ref[...])
        mo_ref[...] = m_new
        vo_ref[...] = v_new

    spec = pl.BlockSpec((tr, LANES), lambda i: (i, 0))
    return pl.pallas_call(
        body, name=name, grid=(rows // tr,),
        in_specs=[spec, spec, spec, pl.BlockSpec((parts, tr, LANES), lambda i: (0, i, 0))],
        out_specs=[spec] * 4, out_shape=[jax.ShapeDtypeStruct((rows, LANES), F32)] * 4,
        compiler_params=pltpu.CompilerParams(dimension_semantics=("parallel",), vmem_limit_bytes=VMEM_LIMIT),
    )(w, m, v, gparts)


def _pick_rows(rows):
    for t in (1024, 512, 256, 128, 64, 32, 16, 8):
        if rows % t == 0:
            return t
    raise ValueError(rows)


def _pack(arrays, dtype, row_multiple):
    flat = jnp.concatenate([a.astype(dtype).reshape(-1) for a in arrays])
    unit = row_multiple * LANES
    pad = (-flat.shape[0]) % unit
    if pad:
        flat = jnp.concatenate([flat, jnp.zeros((pad,), dtype)])
    return flat.reshape(-1, LANES)


def _unpack(flat2d, shapes):
    flat = flat2d.reshape(-1)
    out, off = [], 0
    for s in shapes:
        n = int(np.prod(s))
        out.append(flat[off:off + n].reshape(s))
        off += n
    return out


def _unpack_stacked(stacked, shapes):
    flat = stacked.reshape(N_DEV, -1)
    out, off = [], 0
    for s in shapes:
        n = int(np.prod(s))
        out.append(flat[:, off:off + n].reshape((N_DEV,) + tuple(s)))
        off += n
    return out


def _merge_shards(stacked, axis):
    t = jnp.moveaxis(stacked, 0, axis)
    s = t.shape
    return t.reshape(s[:axis] + (s[axis] * s[axis + 1],) + s[axis + 2:])


def _split_shards(full, axis):
    s = full.shape
    t = full.reshape(s[:axis] + (N_DEV, s[axis] // N_DEV) + s[axis + 1:])
    return jnp.moveaxis(t, axis, 0)


def _lower_bounds(hg_lower_bounds):
    p = jax.nn.softmax(hg_lower_bounds, axis=0)
    return jnp.cumsum(p, axis=0) - p[0:1]


def _prep_layer(p):
    def row(v):
        return v.reshape(1, -1)

    eye_b = jnp.eye(HEADS, dtype=F32)
    eye_g = jnp.eye(S5_GROUPS, dtype=F32)
    step = jnp.exp(p["s5_log_dt"])[:, None]
    lam_re, lam_im = p["s5_lam_re"], p["s5_lam_im"]
    mag = jnp.exp(lam_re * step)
    lb_re = mag * jnp.cos(lam_im * step)
    lb_im = mag * jnp.sin(lam_im * step)
    den = lam_re * lam_re + lam_im * lam_im
    f_re = ((lb_re - 1.0) * lam_re + lb_im * lam_im) / den
    f_im = (lb_im * lam_re - (lb_re - 1.0) * lam_im) / den
    bb_re = f_re[..., None] * p["s5_b_re"] - f_im[..., None] * p["s5_b_im"]
    bb_im = f_re[..., None] * p["s5_b_im"] + f_im[..., None] * p["s5_b_re"]
    width = S5_GROUPS * S5_GROUP
    n_state = S5_GROUPS * S5_STATE
    return dict(
        lb=row(p["lb"]), hg_norm=row(p["hg_norm"]), ret_norm=row(p["ret_norm"]),
        conv_w=p["lru_conv_w"], conv_b=row(p["lru_conv_b"]),
        wa=jnp.einsum("nij,nm->nimj", p["lru_wa"], eye_b).reshape(MIX_W, MIX_W), ba=row(p["lru_ba"]),
        wx=jnp.einsum("nij,nm->nimj", p["lru_wx"], eye_b).reshape(MIX_W, MIX_W), bx=row(p["lru_bx"]),
        sp=row(jax.nn.softplus(-p["lru_lambda"])),
        bt_re=jnp.einsum("gnp,gh->gphn", bb_re, eye_g).reshape(width, n_state),
        bt_im=jnp.einsum("gnp,gh->gphn", bb_im, eye_g).reshape(width, n_state),
        lb_re=row(lb_re), lb_im=row(lb_im),
        ct_re=jnp.einsum("gpn,gh->gnhp", p["s5_c_re"], eye_g).reshape(n_state, width),
        ct_im=jnp.einsum("gpn,gh->gnhp", p["s5_c_im"], eye_g).reshape(n_state, width),
        s5_d=row(p["s5_d"]), glu_w=p["s5_glu_w"], glu_b=row(p["s5_glu_b"]),
        b_gate=row(p["b_gate"]),
        norm_mix_pre=row(p["norm_mix_pre"]), norm_mix_post=row(p["norm_mix_post"]),
        norm_xa_pre=row(p["norm_xa_pre"]), norm_xa_post=row(p["norm_xa_post"]), norm_mem=row(p["norm_mem"]),
        norm_ffn_pre=row(p["norm_ffn_pre"]), norm_ffn_post=row(p["norm_ffn_post"]),
    )


_PREP_INPUTS = ("hg_norm", "ret_norm", "lru_conv_w", "lru_conv_b", "lru_wa", "lru_ba", "lru_wx", "lru_bx", "lru_lambda",
                "s5_lam_re", "s5_lam_im", "s5_b_re", "s5_b_im", "s5_c_re", "s5_c_im", "s5_d", "s5_log_dt", "s5_glu_w",
                "s5_glu_b", "b_gate", "norm_mix_pre", "norm_mix_post", "norm_xa_pre", "norm_xa_post", "norm_mem",
                "norm_ffn_pre", "norm_ffn_post")


def _retention_constants():
    lg = np.log1p(-np.power(2.0, -5.0 - np.arange(HEADS)))
    idx = np.arange(RET_CHUNK)

    def lanes(per_head_rows):
        return np.repeat(per_head_rows.T[:, :, None], HEAD_DIM, axis=2).reshape(RET_CHUNK, MIX_W)

    xi = lanes(np.exp((idx + 1.0)[None, :] * lg[:, None]))
    zeta = lanes(np.exp((RET_CHUNK - 1.0 - idx)[None, :] * lg[:, None]))
    rel = idx[:, None] - idx[None, :]
    decay = np.where(rel[None] >= 0, np.exp(np.maximum(rel, 0)[None] * lg[:, None, None]), 0.0)
    decay = np.transpose(decay, (1, 0, 2)).reshape(RET_CHUNK, HEADS * RET_CHUNK)
    g_end = np.repeat(np.exp(RET_CHUNK * lg), HEAD_DIM)[None, :]
    return tuple(jnp.asarray(a, F32) for a in (xi, zeta, decay, g_end))


def _rotary_tables(seq):
    pos = jnp.arange(seq, dtype=F32)
    inv_freq = 10000.0 ** (-jnp.arange(0, HEAD_DIM, 2, dtype=F32) / HEAD_DIM)
    ang = pos[:, None] * inv_freq[None, :]
    cos, sin = jnp.cos(ang), jnp.sin(ang)
    cos_t = jnp.tile(jnp.repeat(cos, 2, axis=1), (1, HEADS))
    sin_t = jnp.tile(jnp.stack([-sin, sin], axis=-1).reshape(seq, HEAD_DIM), (1, HEADS))
    return cos_t, sin_t


TB_HG = 128
TB_RET = 128
TB_LRU = 256
TB_S5 = 128
TB_ROW = 256

_STATE = (MIX_W, MIX_W)
_TAIL = (8, MIX_W)
_S5_TAIL = (8, S5_GROUPS * S5_STATE)


def _mixer_operands(l, z, kp, rot, ret_c):
    xi, zeta, decay, g_end = ret_c
    return dict(
        hg=dict(name=f"hgrn2_{l}", fn=fn_hgrn2, rows=[(z, 4 * MIX_W, 0)], params=[kp["lb"], kp["hg_norm"]],
                tb=TB_HG, states=(_STATE,), diff_params=(0, 1)),
        ret=dict(name=f"retention_{l}", fn=fn_retention, rows=[(z, 4 * MIX_W, 1), (rot[0], MIX_W, 0), (rot[1], MIX_W, 0)],
                 params=[kp["ret_norm"], xi, zeta, decay, g_end], tb=TB_RET, states=(_STATE,), diff_params=(0,)),
        lru=dict(name=f"rglru_{l}", fn=fn_rglru, rows=[(z, 2 * MIX_W, 4)],
                 params=[kp["conv_w"], kp["conv_b"], kp["wa"], kp["ba"], kp["wx"], kp["bx"], kp["sp"]],
                 tb=TB_LRU, states=(_TAIL, _TAIL), diff_params=(0, 1, 2, 3, 4, 5, 6)),
        s5=dict(name=f"s5_{l}", fn=fn_s5, rows=[(z, MIX_W, 10)],
                params=[kp["bt_re"], kp["bt_im"], kp["lb_re"], kp["lb_im"], kp["ct_re"], kp["ct_im"], kp["s5_d"],
                        kp["glu_w"], kp["glu_b"]],
                tb=TB_S5, states=(_S5_TAIL, _S5_TAIL), diff_params=tuple(range(9))),
    )


def _layer_forward(l, x, h, mem, bw, kp, rot, ret_c, g_next, target):
    d = x.shape[1]
    sv = dict(x=x, h=h)
    z = matmul(f"in_proj_{l}", h, bw["w_in"], "nn", F32)
    gl = matmul(f"gate_proj_{l}", h, bw["w_gate"], "nn", F32)
    sv.update(z=z, gl=gl)
    ops = _mixer_operands(l, z, kp, rot, ret_c)
    ys = []
    for key in ("hg", "ret", "lru", "s5"):
        o = ops[key]
        (y,), _, saved = stage_fwd(o["name"] + "_fwd", o["fn"], o["rows"], o["params"], tb=o["tb"],
                                   outs=[(MIX_W, F32)], states=o["states"])
        ys.append(y)
        sv[key + "_states"] = saved
    sv["ys"] = ys
    merge_params = [bw["w_up"][n] for n in range(4)] + [kp["b_gate"], bw["w_out"], kp["norm_mix_post"]]
    merge_rows = [(y, MIX_W, 0) for y in ys] + [(gl, 4 * d, 0), (x, d, 0)]
    (x1,), _, _ = stage_fwd(f"merge_{l}_fwd", fn_merge, merge_rows, merge_params, tb=TB_ROW, outs=[(d, F32)])
    wk, wv = bw["xa_w_kv"][:, :d], bw["xa_w_kv"][:, d:]
    mem_params = [kp["norm_mem"], wk, wv]
    (k, v), _, _ = stage_fwd(f"mem_{l}_fwd", fn_mem, [(mem, d, 0)], mem_params, tb=mem.shape[0],
                             outs=[(d, F32), (d, F32)])
    xa_params = [kp["norm_xa_pre"], bw["xa_w_q"], k, v, bw["xa_w_o"], kp["norm_xa_post"], kp["norm_ffn_pre"]]
    (x2, h3), _, _ = stage_fwd(f"xattn_{l}_fwd", fn_xattn, [(x1, d, 0)], xa_params, tb=TB_ROW,
                               outs=[(d, F32), (d, BF16)])
    gu = matmul(f"ffn_gu_{l}", h3, bw["ffn_w_gu"], "nn", F32)
    (a,), _, _ = stage_fwd(f"glu_{l}_fwd", fn_glu, [(gu, gu.shape[1], 0)], [], tb=TB_ROW, outs=[(gu.shape[1] // 2, BF16)])
    o3 = matmul(f"ffn_down_{l}", a, bw["ffn_w_down"], "nn", F32)
    sv.update(x1=x1, k=k, v=v, x2=x2, h3=h3, gu=gu, a=a, o3=o3, merge_params=merge_params, merge_rows=merge_rows,
              mem_params=mem_params, xa_params=xa_params)
    if g_next is not None:
        (x3, hn), _, _ = stage_fwd(f"res_{l}_fwd", fn_res_norm, [(x2, d, 0), (o3, d, 0)], [kp["norm_ffn_post"], g_next],
                                   tb=TB_ROW, outs=[(d, F32), (d, BF16)])
        return x3, hn, sv
    (dy,), (loss,), _ = stage_fwd(f"loss_{l}_fwd", fn_res_loss, [(x2, d, 0), (o3, d, 0), (target, d, 0)],
                                  [kp["norm_ffn_post"]], tb=TB_ROW, outs=[(d, F32)], accs=[(8, LANES)])
    return dy, loss[0, 0], sv


def _layer_backward(l, sv, mem, bw, kp, rot, ret_c, g_next, dx3, dhn):
    d = sv["x"].shape[1]
    gk, gb = {}, {}
    res_rows = [(sv["x2"], d, 0), (sv["o3"], d, 0)]
    if g_next is not None:
        (dx2, do3), (gk["norm_ffn_post"], gk["g_next"]) = stage_bwd(
            f"res_{l}_bwd", fn_res_norm, res_rows, [kp["norm_ffn_post"], g_next], [dx3, dhn], tb=TB_ROW,
            diff_rows=(0, 1), diff_params=(0, 1))
    else:
        (dx2, do3), (gk["norm_ffn_post"],) = stage_bwd(
            f"res_{l}_bwd", fn_res, res_rows, [kp["norm_ffn_post"]], [dx3], tb=TB_ROW, diff_rows=(0, 1), diff_params=(0,))
    da = matmul(f"ffn_down_dx_{l}", do3, bw["ffn_w_down"], "nt", F32)
    gb["ffn_w_down"] = matmul(f"ffn_down_dw_{l}", sv["a"], do3, "tn", F32)
    (dgu,), _ = stage_bwd(f"glu_{l}_bwd", fn_glu, [(sv["gu"], sv["gu"].shape[1], 0)], [], [da], tb=TB_ROW,
                          diff_rows=(0,), row_dtypes=(BF16,))
    dh3 = matmul(f"ffn_gu_dx_{l}", dgu, bw["ffn_w_gu"], "nt", F32)
    gb["ffn_w_gu"] = matmul(f"ffn_gu_dw_{l}", sv["h3"], dgu, "tn", F32)
    (dx1,), xa_g = stage_bwd(f"xattn_{l}_bwd", fn_xattn, [(sv["x1"], d, 0)], sv["xa_params"], [dx2, dh3], tb=TB_ROW,
                             diff_rows=(0,), diff_params=tuple(range(7)))
    gk["norm_xa_pre"], gb["xa_w_q"], dk, dv, gb["xa_w_o"], gk["norm_xa_post"], gk["norm_ffn_pre"] = xa_g
    _, (gk["norm_mem"], dwk, dwv) = stage_bwd(f"mem_{l}_bwd", fn_mem, [(mem, d, 0)], sv["mem_params"], [dk, dv],
                                              tb=mem.shape[0], diff_params=(0, 1, 2))
    gb["xa_w_kv"] = jnp.concatenate([dwk, dwv], axis=1)
    merge_d, merge_g = stage_bwd(f"merge_{l}_bwd", fn_merge, sv["merge_rows"], sv["merge_params"], [dx1], tb=TB_ROW,
                                 diff_rows=tuple(range(6)), diff_params=tuple(range(7)),
                                 row_dtypes=(F32, F32, F32, F32, BF16, F32))
    dys, dgl, dx0 = merge_d[:4], merge_d[4], merge_d[5]
    gb["w_up"] = jnp.stack(merge_g[:4])
    gk["b_gate"], gb["w_out"], gk["norm_mix_post"] = merge_g[4:]
    ops = _mixer_operands(l, sv["z"], kp, rot, ret_c)
    dz = {}
    for key, dy in zip(("hg", "ret", "lru", "s5"), dys):
        o = ops[key]
        (dz[key],), pg = stage_bwd(o["name"] + "_bwd", o["fn"], o["rows"], o["params"], [dy], tb=o["tb"],
                                   saved=sv[key + "_states"], diff_rows=(0,), diff_params=o["diff_params"],
                                   row_dtypes=(BF16,))
        gk[key] = pg
    dh = matmul(f"gate_proj_dx_{l}", dgl, bw["w_gate"], "nt", F32)
    gb["w_gate"] = matmul(f"gate_proj_dw_{l}", sv["h"], dgl, "tn", F32)
    dw_in, col = [], 0
    for key in ("hg", "ret", "lru", "s5"):
        width = dz[key].shape[1]
        dh = matmul(f"in_proj_dx_{key}_{l}", dz[key], bw["w_in"][:, col:col + width], "nt", F32, add=dh)
        dw_in.append(matmul(f"in_proj_dw_{key}_{l}", sv["h"], dz[key], "tn", F32))
        col += width
    gb["w_in"] = jnp.concatenate(dw_in, axis=1)
    return dx0, dh, gk, gb


def _kernel_grads_to_prep(gk):
    hg, ret, lru, s5 = gk["hg"], gk["ret"], gk["lru"], gk["s5"]
    return dict(
        lb=hg[0], hg_norm=hg[1], ret_norm=ret[0],
        conv_w=lru[0], conv_b=lru[1], wa=lru[2], ba=lru[3], wx=lru[4], bx=lru[5], sp=lru[6],
        bt_re=s5[0], bt_im=s5[1], lb_re=s5[2], lb_im=s5[3], ct_re=s5[4], ct_im=s5[5], s5_d=s5[6], glu_w=s5[7],
        glu_b=s5[8], b_gate=gk["b_gate"], norm_mix_pre=gk["norm_mix_pre"], norm_mix_post=gk["norm_mix_post"],
        norm_xa_pre=gk["norm_xa_pre"], norm_xa_post=gk["norm_xa_post"], norm_mem=gk["norm_mem"],
        norm_ffn_pre=gk["norm_ffn_pre"], norm_ffn_post=gk["norm_ffn_post"],
    )


def _step(inp):
    x, mem, target = inp["x"][0], inp["mem"][0], inp["loss_target"][0]
    seq = x.shape[0]
    depth = inp["w_in"].shape[0]
    me = 4 * lax.axis_index("x") + 2 * lax.axis_index("y") + lax.axis_index("c")

    big_shapes = [inp[n].shape for n in _BIG_NAMES]
    big_all = _unpack_stacked(_exchange("gather_big", _pack([inp[n] for n in _BIG_NAMES], BF16, 16), True), big_shapes)
    full_big = {n: _merge_shards(s, ax) for (n, ax), s in zip(BIG, big_all)}
    small_shapes = [inp[n].shape for n in _SMALL_SHARDED_NAMES]
    small_all = _unpack_stacked(_exchange("gather_small", _pack([inp[n] for n in _SMALL_SHARDED_NAMES], F32, 8), True),
                                small_shapes)
    full_small = {n: _merge_shards(s, ax) for (n, ax), s in zip(SMALL_SHARDED, small_all)}

    lbs, lbs_vjp = jax.vjp(_lower_bounds, inp["hg_lower_bounds"])
    kps, prep_vjps = [], []
    for l in range(depth):
        p = {n: (full_small[n][l] if n in full_small else inp[n][l]) for n in _PREP_INPUTS}
        p["lb"] = lbs[l]
        kp, vj = jax.vjp(_prep_layer, p)
        kps.append(kp)
        prep_vjps.append(vj)
    bws = [{n: full_big[n][l] for n in _BIG_NAMES} for l in range(depth)]
    rot = _rotary_tables(seq)
    ret_c = _retention_constants()

    (h,), _, _ = stage_fwd("norm_in_fwd", fn_norm, [(x, x.shape[1], 0)], [kps[0]["norm_mix_pre"]], tb=TB_ROW,
                           outs=[(x.shape[1], BF16)])
    saved = []
    xs = x
    for l in range(depth):
        g_next = kps[l + 1]["norm_mix_pre"] if l + 1 < depth else None
        xs, h, sv = _layer_forward(l, xs, h, mem, bws[l], kps[l], rot, ret_c, g_next, target)
        saved.append(sv)
    dy, loss_local = xs, h

    grads_big = {n: [None] * depth for n in _BIG_NAMES}
    gks = [None] * depth
    dx, dh = dy, None
    for l in reversed(range(depth)):
        g_next = kps[l + 1]["norm_mix_pre"] if l + 1 < depth else None
        dx, dh, gk, gb = _layer_backward(l, saved[l], mem, bws[l], kps[l], rot, ret_c, g_next, dx, dh)
        gks[l] = gk
        for n in _BIG_NAMES:
            grads_big[n][l] = gb[n]
    (grad_x,), (g_pre0,) = stage_bwd("norm_in_bwd", fn_keep_norm, [(x, x.shape[1], 0)], [kps[0]["norm_mix_pre"]],
                                     [dx, dh], tb=TB_ROW, diff_rows=(0,), diff_params=(0,))
    for l in range(depth):
        gks[l]["norm_mix_pre"] = g_pre0 if l == 0 else gks[l - 1]["g_next"]

    small_grads = {n: [None] * depth for n in _PREP_INPUTS}
    d_lbs = []
    for l in range(depth):
        (gp,) = prep_vjps[l](_kernel_grads_to_prep(gks[l]))
        d_lbs.append(gp["lb"])
        for n in _PREP_INPUTS:
            small_grads[n][l] = gp[n]
    small_local = {n: jnp.stack(v) for n, v in small_grads.items()}
    (small_local["hg_lower_bounds"],) = lbs_vjp(jnp.stack(d_lbs))
    small_names = REPLICATED + _SMALL_SHARDED_NAMES
    full_shapes = [small_local[n].shape for n in small_names]
    small_parts = _exchange("gather_small_grads", _pack([small_local[n] for n in small_names], F32, 8), True)

    send = jnp.concatenate([_split_shards(jnp.stack(grads_big[n]), ax).reshape(N_DEV, -1) for n, ax in BIG], axis=1)
    pad = (-send.shape[1]) % (512 * LANES)
    send = jnp.pad(send, ((0, 0), (0, pad))).reshape(N_DEV, -1, LANES)
    big_parts = _exchange("scatter_big_grads", send, False)

    out = {}
    res = adamw("adamw_big", *[_pack([inp[pre + n] for n in _BIG_NAMES], F32, 512) for pre in ("", "m_", "v_")], big_parts)
    for kind, flat in zip(("grad_", "delta_", "new_m_", "new_v_"), res):
        for n, a in zip(_BIG_NAMES, _unpack(flat, big_shapes)):
            out[kind + n] = a
    n_small = small_parts.shape[1]
    zeros = [jnp.zeros(s, F32) for s in full_shapes[len(REPLICATED):]]
    res = adamw("adamw_small", *[_pack([inp[pre + n] for n in REPLICATED] + zeros, F32, 8) for pre in ("", "m_", "v_")],
                small_parts)
    assert res[0].shape[0] == n_small
    summed = _unpack(res[0], full_shapes)
    for kind, flat in zip(("grad_", "delta_", "new_m_", "new_v_"), res):
        for n, a in zip(REPLICATED, _unpack(flat, full_shapes[:len(REPLICATED)])):
            out[kind + n] = a
    shard_g = []
    for (n, ax), g_full in zip(SMALL_SHARDED, summed[len(REPLICATED):]):
        width = inp[n].shape[ax]
        shard_g.append(lax.dynamic_slice_in_dim(g_full, me * width, width, axis=ax))
    res = adamw("adamw_small_sharded",
                *[_pack([inp[pre + n] for n in _SMALL_SHARDED_NAMES], F32, 8) for pre in ("", "m_", "v_")],
                _pack(shard_g, F32, 8)[None])
    for kind, flat in zip(("grad_", "delta_", "new_m_", "new_v_"), res):
        for n, a in zip(_SMALL_SHARDED_NAMES, _unpack(flat, small_shapes)):
            out[kind + n] = a

    out["loss"] = lax.psum(loss_local, ("x", "y", "c"))
    out["grad_x"] = grad_x[None]
    return out


def kernel(x, mem, hg_lower_bounds, norm_mix_pre, norm_mix_post, w_in, w_gate, b_gate, hg_norm, ret_norm, lru_conv_w, lru_conv_b, lru_wa, lru_ba, lru_wx, lru_bx, lru_lambda, s5_lam_re, s5_lam_im, s5_b_re, s5_b_im, s5_c_re, s5_c_im, s5_d, s5_log_dt, s5_glu_w, s5_glu_b, w_up, w_out, norm_xa_pre, norm_xa_post, norm_mem, xa_w_q, xa_w_kv, xa_w_o, norm_ffn_pre, norm_ffn_post, ffn_w_gu, ffn_w_down, loss_target, m_hg_lower_bounds, m_norm_mix_pre, m_norm_mix_post, m_w_in, m_w_gate, m_b_gate, m_hg_norm, m_ret_norm, m_lru_conv_w, m_lru_conv_b, m_lru_wa, m_lru_ba, m_lru_wx, m_lru_bx, m_lru_lambda, m_s5_lam_re, m_s5_lam_im, m_s5_b_re, m_s5_b_im, m_s5_c_re, m_s5_c_im, m_s5_d, m_s5_log_dt, m_s5_glu_w, m_s5_glu_b, m_w_up, m_w_out, m_norm_xa_pre, m_norm_xa_post, m_norm_mem, m_xa_w_q, m_xa_w_kv, m_xa_w_o, m_norm_ffn_pre, m_norm_ffn_post, m_ffn_w_gu, m_ffn_w_down, v_hg_lower_bounds, v_norm_mix_pre, v_norm_mix_post, v_w_in, v_w_gate, v_b_gate, v_hg_norm, v_ret_norm, v_lru_conv_w, v_lru_conv_b, v_lru_wa, v_lru_ba, v_lru_wx, v_lru_bx, v_lru_lambda, v_s5_lam_re, v_s5_lam_im, v_s5_b_re, v_s5_b_im, v_s5_c_re, v_s5_c_im, v_s5_d, v_s5_log_dt, v_s5_glu_w, v_s5_glu_b, v_w_up, v_w_out, v_norm_xa_pre, v_norm_xa_post, v_norm_mem, v_xa_w_q, v_xa_w_kv, v_xa_w_o, v_norm_ffn_pre, v_norm_ffn_post, v_ffn_w_gu, v_ffn_w_down):
    values = (x, mem, hg_lower_bounds, norm_mix_pre, norm_mix_post, w_in, w_gate, b_gate, hg_norm, ret_norm, lru_conv_w, lru_conv_b, lru_wa, lru_ba, lru_wx, lru_bx, lru_lambda, s5_lam_re, s5_lam_im, s5_b_re, s5_b_im, s5_c_re, s5_c_im, s5_d, s5_log_dt, s5_glu_w, s5_glu_b, w_up, w_out, norm_xa_pre, norm_xa_post, norm_mem, xa_w_q, xa_w_kv, xa_w_o, norm_ffn_pre, norm_ffn_post, ffn_w_gu, ffn_w_down, loss_target, m_hg_lower_bounds, m_norm_mix_pre, m_norm_mix_post, m_w_in, m_w_gate, m_b_gate, m_hg_norm, m_ret_norm, m_lru_conv_w, m_lru_conv_b, m_lru_wa, m_lru_ba, m_lru_wx, m_lru_bx, m_lru_lambda, m_s5_lam_re, m_s5_lam_im, m_s5_b_re, m_s5_b_im, m_s5_c_re, m_s5_c_im, m_s5_d, m_s5_log_dt, m_s5_glu_w, m_s5_glu_b, m_w_up, m_w_out, m_norm_xa_pre, m_norm_xa_post, m_norm_mem, m_xa_w_q, m_xa_w_kv, m_xa_w_o, m_norm_ffn_pre, m_norm_ffn_post, m_ffn_w_gu, m_ffn_w_down, v_hg_lower_bounds, v_norm_mix_pre, v_norm_mix_post, v_w_in, v_w_gate, v_b_gate, v_hg_norm, v_ret_norm, v_lru_conv_w, v_lru_conv_b, v_lru_wa, v_lru_ba, v_lru_wx, v_lru_bx, v_lru_lambda, v_s5_lam_re, v_s5_lam_im, v_s5_b_re, v_s5_b_im, v_s5_c_re, v_s5_c_im, v_s5_d, v_s5_log_dt, v_s5_glu_w, v_s5_glu_b, v_w_up, v_w_out, v_norm_xa_pre, v_norm_xa_post, v_norm_mem, v_xa_w_q, v_xa_w_kv, v_xa_w_o, v_norm_ffn_pre, v_norm_ffn_post, v_ffn_w_gu, v_ffn_w_down)
    names = ("x", "mem") + WEIGHTS + ("loss_target",) + tuple("m_" + n for n in WEIGHTS) + tuple("v_" + n for n in WEIGHTS)
    out = _step(dict(zip(names, values)))
    order = ["loss", "grad_x"] + [k + n for k in ("grad_", "delta_", "new_m_", "new_v_") for n in WEIGHTS]
    return tuple(out[k] for k in order)
```

```python
import functools

import numpy as np
import jax
import jax.numpy as jnp
from jax import lax
from jax.experimental import pallas as pl
from jax.experimental.pallas import tpu as pltpu

F32 = jnp.float32
BF16 = jnp.bfloat16
EPS = 1e-6
N_DEV = 8
LANES = 128
VMEM_LIMIT = 60 * 1024 * 1024

HEADS = 4
HEAD_DIM = 64
MIX_W = HEADS * HEAD_DIM
HG_CHUNK = 32
RET_CHUNK = 128
S5_GROUPS = 16
S5_GROUP = 16
S5_STATE = 64
LRU_C = 8.0
XA_HEADS = 4

ADAM_LR = 0.001
ADAM_B1 = 0.9
ADAM_B2 = 0.999
ADAM_EPS = 1e-08
ADAM_WD = 0.01
ADAM_STEP = 10

BIG = (("w_in", 2), ("w_gate", 2), ("w_up", 3), ("w_out", 1), ("xa_w_q", 1), ("xa_w_kv", 2), ("xa_w_o", 1),
       ("ffn_w_gu", 2), ("ffn_w_down", 1))
SMALL_SHARDED = (("lru_conv_w", 2), ("s5_glu_w", 1))
WEIGHTS = ("hg_lower_bounds", "norm_mix_pre", "norm_mix_post", "w_in", "w_gate", "b_gate", "hg_norm", "ret_norm",
           "lru_conv_w", "lru_conv_b", "lru_wa", "lru_ba", "lru_wx", "lru_bx", "lru_lambda", "s5_lam_re", "s5_lam_im",
           "s5_b_re", "s5_b_im", "s5_c_re", "s5_c_im", "s5_d", "s5_log_dt", "s5_glu_w", "s5_glu_b", "w_up", "w_out",
           "norm_xa_pre", "norm_xa_post", "norm_mem", "xa_w_q", "xa_w_kv", "xa_w_o", "norm_ffn_pre", "norm_ffn_post",
           "ffn_w_gu", "ffn_w_down")
_BIG_NAMES = tuple(n for n, _ in BIG)
_SMALL_SHARDED_NAMES = tuple(n for n, _ in SMALL_SHARDED)
REPLICATED = tuple(n for n in WEIGHTS if n not in _BIG_NAMES and n not in _SMALL_SHARDED_NAMES)


def _dot(a, b):
    return jnp.dot(a.astype(BF16), b.astype(BF16), preferred_element_type=F32)


def _dot_nt(a, b):
    return lax.dot_general(a.astype(BF16), b.astype(BF16), (((1,), (1,)), ((), ())), preferred_element_type=F32)


def _dot_tn(a, b):
    return lax.dot_general(a.astype(BF16), b.astype(BF16), (((0,), (0,)), ((), ())), preferred_element_type=F32)


def _dot_exact(a, b):
    return jnp.dot(a, b, precision=lax.Precision.HIGHEST, preferred_element_type=F32)


def _rms(x, g):
    return x * lax.rsqrt(jnp.mean(x * x, axis=-1, keepdims=True) + EPS) * g


def _shift_down(x, d, fill):
    return jnp.concatenate([jnp.full((d, x.shape[1]), fill, x.dtype), x[:-d]], axis=0)


def _shift_up(x, d, fill):
    return jnp.concatenate([x[d:], jnp.full((d, x.shape[1]), fill, x.dtype)], axis=0)


def _cumsum_rows(x):
    d = 1
    while d < x.shape[0]:
        x = x + _shift_down(x, d, 0.0)
        d *= 2
    return x


def _lane_head(shape, dim):
    return lax.shift_right_logical(lax.broadcasted_iota(jnp.int32, shape, dim), 6)


def _head_masks(width=MIX_W):
    head = _lane_head((1, width), 1)
    return [(head == h).astype(F32) for h in range(HEADS)]


def _block_diag_mask():
    return (_lane_head((MIX_W, MIX_W), 0) == _lane_head((MIX_W, MIX_W), 1)).astype(F32)


def _head_rms(o, g):
    ms = _dot_exact(o * o, _block_diag_mask()) * (1.0 / HEAD_DIM)
    return o * lax.rsqrt(ms + EPS) * g


def _swap_pairs(x):
    lane = lax.broadcasted_iota(jnp.int32, x.shape, 1)
    return jnp.where((lane & 1) == 0, jnp.roll(x, -1, axis=1), jnp.roll(x, 1, axis=1))


def _stack_heads(t, masks):
    return jnp.concatenate([t * m for m in masks], axis=0)


@jax.custom_vjp
def _real_scan(a, u, h0):
    return _real_scan_fwd(a, u, h0)[0]


def _real_scan_fwd(a, u, h0):
    t = a.shape[0]
    acc_a, acc_u = a, u
    d = 1
    while d < t:
        acc_u = acc_u + acc_a * _shift_down(acc_u, d, 0.0)
        acc_a = acc_a * _shift_down(acc_a, d, 1.0)
        d *= 2
    h = acc_u + acc_a * h0
    return h, (a, h, h0)


def _real_scan_bwd(res, dh):
    a, h, h0 = res
    t = a.shape[0]
    acc_a = _shift_up(a, 1, 0.0)
    g = dh
    d = 1
    while d < t:
        g = g + acc_a * _shift_up(g, d, 0.0)
        acc_a = acc_a * _shift_up(acc_a, d, 1.0)
        d *= 2
    h_prev = jnp.concatenate([h0, h[:-1]], axis=0)
    return g * h_prev, g, (a * g)[0:1]


_real_scan.defvjp(_real_scan_fwd, _real_scan_bwd)


def _cmul(ar, ai, br, bi):
    return ar * br - ai * bi, ar * bi + ai * br


@jax.custom_vjp
def _complex_scan(ar, ai, ur, ui, h0r, h0i):
    return _complex_scan_fwd(ar, ai, ur, ui, h0r, h0i)[0]


def _complex_scan_fwd(ar, ai, ur, ui, h0r, h0i):
    t = ar.shape[0]
    pr, pi, sr, si = ar, ai, ur, ui
    d = 1
    while d < t:
        mr, mi = _cmul(pr, pi, _shift_down(sr, d, 0.0), _shift_down(si, d, 0.0))
        sr, si = sr + mr, si + mi
        pr, pi = _cmul(pr, pi, _shift_down(pr, d, 1.0), _shift_down(pi, d, 0.0))
        d *= 2
    mr, mi = _cmul(pr, pi, h0r, h0i)
    hr, hi = sr + mr, si + mi
    return (hr, hi), (ar, ai, hr, hi, h0r, h0i)


def _complex_scan_bwd(res, dh):
    ar, ai, hr, hi, h0r, h0i = res
    dhr, dhi = dh
    t = ar.shape[0]
    pr, pi = _shift_up(ar, 1, 0.0), -_shift_up(ai, 1, 0.0)
    gr, gi = dhr, dhi
    d = 1
    while d < t:
        mr, mi = _cmul(pr, pi, _shift_up(gr, d, 0.0), _shift_up(gi, d, 0.0))
        gr, gi = gr + mr, gi + mi
        pr, pi = _cmul(pr, pi, _shift_up(pr, d, 1.0), _shift_up(pi, d, 0.0))
        d *= 2
    qr = jnp.concatenate([h0r, hr[:-1]], axis=0)
    qi = jnp.concatenate([h0i, hi[:-1]], axis=0)
    dar, dai = gr * qr + gi * qi, gi * qr - gr * qi
    d0r, d0i = _cmul(ar[0:1], -ai[0:1], gr[0:1], gi[0:1])
    return dar, dai, gr, gi, d0r, d0i


_complex_scan.defvjp(_complex_scan_fwd, _complex_scan_bwd)


def fn_norm(st, rows, params):
    (x,), (g,) = rows, params
    return (), (_rms(x, g),), ()


def fn_keep_norm(st, rows, params):
    (x,), (g,) = rows, params
    return (), (x, _rms(x, g)), ()


def fn_hgrn2(st, rows, params):
    (state,) = st
    (z,) = rows
    lb, norm_g = params
    q, f_logit, v_all, g = (z[:, k * MIX_W:(k + 1) * MIX_W] for k in range(4))
    f = lb + (1.0 - lb) * jax.nn.sigmoid(f_logit)
    log_f = jnp.log(f)
    k_all = 1.0 - f
    q_all = jax.nn.silu(q)
    masks = _head_masks()
    bd = _block_diag_mask()
    c = HG_CHUNK
    col = lax.broadcasted_iota(jnp.int32, (c, HEADS * c), 1) & (c - 1)
    causal = col <= lax.broadcasted_iota(jnp.int32, (c, HEADS * c), 0)
    outs = []
    for n in range(z.shape[0] // c):
        sl = slice(n * c, (n + 1) * c)
        lf = log_f[sl]
        b = _cumsum_rows(lf)
        b_end = jnp.sum(lf, axis=0, keepdims=True)
        q_dec = q_all[sl] * jnp.exp(b)
        k_inv = k_all[sl] * jnp.exp(-b)
        k_end = k_all[sl] * jnp.exp(b_end - b)
        v = v_all[sl]
        scores = jnp.where(causal, _dot_nt(q_dec, _stack_heads(k_inv, masks)), 0.0)
        outs.append(_dot(scores, _stack_heads(v, masks)) + _dot_nt(q_dec, state))
        state = state * jnp.exp(b_end) + _dot_tn(v, k_end) * bd
    o = jnp.concatenate(outs, axis=0) if len(outs) > 1 else outs[0]
    return (state,), (_head_rms(o, norm_g) * jax.nn.silu(g),), ()


def fn_retention(st, rows, params):
    (state,) = st
    z, cos_t, sin_t = rows
    norm_g, xi, zeta, decay, g_end = params
    q, k, v_all, g = (z[:, i * MIX_W:(i + 1) * MIX_W] for i in range(4))
    q_all = q * cos_t + _swap_pairs(q) * sin_t
    k_all = (k * cos_t + _swap_pairs(k) * sin_t) * (HEAD_DIM ** -0.5)
    masks = _head_masks()
    bd = _block_diag_mask()
    c = RET_CHUNK
    outs = []
    for n in range(z.shape[0] // c):
        sl = slice(n * c, (n + 1) * c)
        qc, kc, v = q_all[sl], k_all[sl], v_all[sl]
        scores = _dot_nt(qc, _stack_heads(kc, masks)) * decay
        outs.append(_dot(scores, _stack_heads(v, masks)) + _dot_nt(qc * xi, state))
        state = state * g_end + _dot_tn(v, kc * zeta) * bd
    o = jnp.concatenate(outs, axis=0) if len(outs) > 1 else outs[0]
    return (state,), (_head_rms(o, norm_g) * jax.nn.silu(g),), ()


def fn_rglru(st, rows, params):
    tail_x, tail_h = st
    (z,) = rows
    conv_w, conv_b, wa, ba, wx, bx, sp = params
    t = z.shape[0]
    xg, xi = z[:, :MIX_W], z[:, MIX_W:]
    full = jnp.concatenate([tail_x, xi], axis=0)
    xc = conv_b
    for k in range(4):
        xc = xc + conv_w[k:k + 1] * full[5 + k:5 + k + t]
    r = jax.nn.sigmoid(_dot(xc, wa) + ba)
    ig = jax.nn.sigmoid(_dot(xc, wx) + bx)
    log_a = -LRU_C * r * sp
    a = jnp.exp(log_a)
    one_minus_a2 = -jnp.tanh(log_a) * (a * a + 1.0)
    u = jnp.sqrt(one_minus_a2) * (ig * xc)
    h = _real_scan(a, u, tail_h[7:8])
    return (xi[t - 8:], h[t - 8:]), (h * jax.nn.gelu(xg),), ()


def fn_s5(st, rows, params):
    tail_r, tail_i = st
    (u,) = rows
    bt_re, bt_im, lb_re, lb_im, ct_re, ct_im, d, glu_w, glu_b = params
    t = u.shape[0]
    bu_re = _dot(u, bt_re)
    bu_im = _dot(u, bt_im)
    a_re = jnp.broadcast_to(lb_re, bu_re.shape)
    a_im = jnp.broadcast_to(lb_im, bu_im.shape)
    h_re, h_im = _complex_scan(a_re, a_im, bu_re, bu_im, tail_r[7:8], tail_i[7:8])
    y = _dot(h_re, ct_re) - _dot(h_im, ct_im) + d * u
    act = jax.nn.gelu(y)
    out = act * jax.nn.sigmoid(_dot(act, glu_w) + glu_b)
    return (h_re[t - 8:], h_im[t - 8:]), (out,), ()


def fn_merge(st, rows, params):
    ya, yb, yc, yd, gl, x = rows
    w0, w1, w2, w3, b_gate, w_out, g_post = params
    d = x.shape[1]
    mix = None
    for n, (y, w) in enumerate(((ya, w0), (yb, w1), (yc, w2), (yd, w3))):
        gate = jax.nn.sigmoid(gl[:, n * d:(n + 1) * d] + b_gate[:, n * d:(n + 1) * d])
        term = gate * _dot(y, w)
        mix = term if mix is None else mix + term
    return (), (x + _rms(_dot(mix, w_out), g_post),), ()


def fn_mem(st, rows, params):
    (mem,), (g, wk, wv) = rows, params
    m = _rms(mem, g)
    return (), (_dot(m, wk), _dot(m, wv)), ()


def fn_xattn(st, rows, params):
    (x,) = rows
    g_pre, wq, k, v, wo, g_post, g_next = params
    d = x.shape[1]
    dh = d // XA_HEADS
    q = _dot(_rms(x, g_pre), wq)
    heads = []
    for h in range(XA_HEADS):
        sl = slice(h * dh, (h + 1) * dh)
        s = _dot_nt(q[:, sl], k[:, sl]) * (dh ** -0.5)
        heads.append(_dot(jax.nn.softmax(s, axis=-1), v[:, sl]))
    x2 = x + _rms(_dot(jnp.concatenate(heads, axis=1), wo), g_post)
    return (), (x2, _rms(x2, g_next)), ()


def fn_glu(st, rows, params):
    (gu,) = rows
    f = gu.shape[1] // 2
    return (), (jax.nn.silu(gu[:, :f]) * gu[:, f:],), ()


def fn_res_norm(st, rows, params):
    (x, o), (g_post, g_next) = rows, params
    xn = x + _rms(o, g_post)
    return (), (xn, _rms(xn, g_next)), ()


def fn_res(st, rows, params):
    (x, o), (g_post,) = rows, params
    return (), (x + _rms(o, g_post),), ()


def fn_res_loss(st, rows, params):
    (x, o, target), (g_post,) = rows, params
    err = x + _rms(o, g_post) - target
    inv_d = 1.0 / x.shape[1]
    loss = 0.5 * inv_d * jnp.sum(err * err)
    return (), (err * inv_d,), (jnp.full((8, LANES), loss, F32),)


def _params():
    return pltpu.CompilerParams(dimension_semantics=("arbitrary",), vmem_limit_bytes=VMEM_LIMIT)


def _row_spec(tb, width, colblk, nb, reverse):
    if reverse:
        return pl.BlockSpec((tb, width), lambda i: (nb - 1 - i, colblk))
    return pl.BlockSpec((tb, width), lambda i: (i, colblk))


def _full_spec(shape):
    return pl.BlockSpec(shape, lambda i: (0,) * len(shape))


def _saved_spec(shape, nb, reverse):
    if reverse:
        return pl.BlockSpec((1,) + shape, lambda i: (nb - 1 - i, 0, 0))
    return pl.BlockSpec((1,) + shape, lambda i: (i, 0, 0))


def stage_fwd(name, fn, rows, params, *, tb, outs, states=(), accs=()):
    n_rows = rows[0][0].shape[0]
    nb = n_rows // tb
    nr, npar, no, na, ns = len(rows), len(params), len(outs), len(accs), len(states)

    def body(*refs):
        row_refs, par_refs = refs[:nr], refs[nr:nr + npar]
        o = nr + npar
        out_refs, acc_refs = refs[o:o + no], refs[o + no:o + no + na]
        saved_refs = refs[o + no + na:o + no + na + ns]
        st_refs = refs[o + no + na + ns:]
        i = pl.program_id(0)

        @pl.when(i == 0)
        def _():
            for r in st_refs + acc_refs:
                r[...] = jnp.zeros_like(r)

        st = tuple(r[...] for r in st_refs)
        for sv, s in zip(saved_refs, st):
            sv[0] = s
        new_st, out_vals, acc_vals = fn(st, tuple(r[...].astype(F32) for r in row_refs),
                                        tuple(r[...].astype(F32) for r in par_refs))
        for r, v in zip(out_refs, out_vals):
            r[...] = v.astype(r.dtype)
        for r, v in zip(acc_refs, acc_vals):
            r[...] += v
        for r, v in zip(st_refs, new_st):
            r[...] = v

    res = pl.pallas_call(
        body, name=name, grid=(nb,),
        in_specs=[_row_spec(tb, w, cb, nb, False) for _, w, cb in rows] + [_full_spec(p.shape) for p in params],
        out_specs=[_row_spec(tb, w, 0, nb, False) for w, _ in outs] + [_full_spec(s) for s in accs]
        + [_saved_spec(s, nb, False) for s in states],
        out_shape=[jax.ShapeDtypeStruct((n_rows, w), dt) for w, dt in outs]
        + [jax.ShapeDtypeStruct(s, F32) for s in accs] + [jax.ShapeDtypeStruct((nb,) + s, F32) for s in states],
        scratch_shapes=[pltpu.VMEM(s, F32) for s in states],
        compiler_params=_params(),
    )(*[a for a, _, _ in rows], *params)
    return res[:no], res[no:no + na], res[no + na:]


def stage_bwd(name, fn, rows, params, cts, *, tb, saved=(), diff_rows=(), diff_params=(), row_dtypes=None):
    n_rows = rows[0][0].shape[0]
    nb = n_rows // tb
    nr, npar, ns, nc = len(rows), len(params), len(saved), len(cts)
    ndr, ndp = len(diff_rows), len(diff_params)
    row_dtypes = row_dtypes or (F32,) * ndr
    state_shapes = [s.shape[1:] for s in saved]

    def body(*refs):
        row_refs, par_refs = refs[:nr], refs[nr:nr + npar]
        o = nr + npar
        saved_refs, ct_refs = refs[o:o + ns], refs[o + ns:o + ns + nc]
        o = o + ns + nc
        drow_refs, dpar_refs, dst_refs = refs[o:o + ndr], refs[o + ndr:o + ndr + ndp], refs[o + ndr + ndp:]
        i = pl.program_id(0)

        @pl.when(i == 0)
        def _():
            for r in dst_refs + dpar_refs:
                r[...] = jnp.zeros_like(r)

        st = tuple(r[0] for r in saved_refs)
        row_vals = [r[...].astype(F32) for r in row_refs]
        par_vals = [r[...].astype(F32) for r in par_refs]

        def f(st_, dr_, dp_):
            rv, pv = list(row_vals), list(par_vals)
            for k, v in zip(diff_rows, dr_):
                rv[k] = v
            for k, v in zip(diff_params, dp_):
                pv[k] = v
            new_st, out_vals, _ = fn(st_, tuple(rv), tuple(pv))
            return new_st, out_vals

        _, vjp = jax.vjp(f, st, tuple(row_vals[k] for k in diff_rows), tuple(par_vals[k] for k in diff_params))
        g_st, g_rows, g_par = vjp((tuple(r[...] for r in dst_refs), tuple(r[...].astype(F32) for r in ct_refs)))
        for r, v in zip(drow_refs, g_rows):
            r[...] = v.astype(r.dtype)
        for r, v in zip(dpar_refs, g_par):
            r[...] += v
        for r, v in zip(dst_refs, g_st):
            r[...] = v

    res = pl.pallas_call(
        body, name=name, grid=(nb,),
        in_specs=[_row_spec(tb, w, cb, nb, True) for _, w, cb in rows] + [_full_spec(p.shape) for p in params]
        + [_saved_spec(s, nb, True) for s in state_shapes] + [_row_spec(tb, c.shape[1], 0, nb, True) for c in cts],
        out_specs=[_row_spec(tb, rows[k][1], 0, nb, True) for k in diff_rows]
        + [_full_spec(params[k].shape) for k in diff_params],
        out_shape=[jax.ShapeDtypeStruct((n_rows, rows[k][1]), dt) for k, dt in zip(diff_rows, row_dtypes)]
        + [jax.ShapeDtypeStruct(params[k].shape, F32) for k in diff_params],
        scratch_shapes=[pltpu.VMEM(s, F32) for s in state_shapes],
        compiler_params=_params(),
    )(*[a for a, _, _ in rows], *params, *saved, *cts)
    return res[:ndr], res[ndr:]


def _pick(n, target):
    if n <= target:
        return n
    best = None
    for t in range(LANES, target + 1, LANES):
        if n % t == 0:
            best = t
    assert best is not None, n
    return best


def matmul(name, a, b, mode, out_dtype, add=None):
    if mode == "tn":
        k, m = a.shape
    else:
        m, k = a.shape
    n = b.shape[0] if mode == "nt" else b.shape[1]
    if mode == "tn":
        tm, tn, tk = _pick(m, 512), _pick(n, 512), _pick(k, 512)
    else:
        tm, tn, tk = _pick(m, 1024), _pick(n, 512), _pick(k, 2816)
    nk = k // tk
    a_spec = pl.BlockSpec((tk, tm), lambda i, j, kk: (kk, i)) if mode == "tn" else pl.BlockSpec((tm, tk), lambda i, j, kk: (i, kk))
    b_spec = pl.BlockSpec((tn, tk), lambda i, j, kk: (j, kk)) if mode == "nt" else pl.BlockSpec((tk, tn), lambda i, j, kk: (kk, j))
    o_spec = pl.BlockSpec((tm, tn), lambda i, j, kk: (i, j))
    dims = {"nn": (((1,), (0,)), ((), ())), "nt": (((1,), (1,)), ((), ())), "tn": (((0,), (0,)), ((), ()))}[mode]
    has_add = add is not None

    def body(*refs):
        a_ref, b_ref = refs[0], refs[1]
        part = lax.dot_general(a_ref[...].astype(BF16), b_ref[...].astype(BF16), dims, preferred_element_type=F32)
        if nk == 1:
            o_ref = refs[-1]
            o_ref[...] = (part + refs[2][...].astype(F32) if has_add else part).astype(o_ref.dtype)
            return
        o_ref, acc_ref = refs[-2], refs[-1]
        kk = pl.program_id(2)

        @pl.when(kk == 0)
        def _():
            acc_ref[...] = part + refs[2][...].astype(F32) if has_add else part

        @pl.when(kk > 0)
        def _():
            acc_ref[...] += part

        @pl.when(kk == nk - 1)
        def _():
            o_ref[...] = acc_ref[...].astype(o_ref.dtype)

    return pl.pallas_call(
        body, name=name, grid=(m // tm, n // tn, nk),
        in_specs=[a_spec, b_spec] + ([o_spec] if has_add else []),
        out_specs=o_spec, out_shape=jax.ShapeDtypeStruct((m, n), out_dtype),
        scratch_shapes=[pltpu.VMEM((tm, tn), F32)] if nk > 1 else [],
        compiler_params=pltpu.CompilerParams(dimension_semantics=("parallel", "parallel", "arbitrary"),
                                             vmem_limit_bytes=VMEM_LIMIT),
    )(a, b, *([add] if has_add else []))


def _mesh_position():
    return lax.axis_index("x"), lax.axis_index("y"), lax.axis_index("c")


def _peer(pos, k):
    x, y, c = pos
    px = 1 - x if k & 4 else x
    py = 1 - y if k & 2 else y
    pc = 1 - c if k & 1 else c
    return (px, py, pc), 4 * px + 2 * py + pc


def _exchange(name, arrays, gather):
    n = len(arrays)

    def body(*refs):
        x_refs, o_refs = refs[:n], refs[n:2 * n]
        send_sems, recv_sems, local_sems = refs[2 * n:]
        pos = _mesh_position()
        me = 4 * pos[0] + 2 * pos[1] + pos[2]
        local = [pltpu.make_async_copy(x if gather else x.at[me], o.at[me], local_sems.at[a])
                 for a, (x, o) in enumerate(zip(x_refs, o_refs))]
        for cp in local:
            cp.start()
        sends = []
        for k in range(1, N_DEV):
            peer, peer_idx = _peer(pos, k)
            for a, (x, o) in enumerate(zip(x_refs, o_refs)):
                cp = pltpu.make_async_remote_copy(
                    src_ref=x if gather else x.at[peer_idx], dst_ref=o.at[me],
                    send_sem=send_sems.at[k - 1, a], recv_sem=recv_sems.at[k - 1, a],
                    device_id=peer, device_id_type=pl.DeviceIdType.MESH)
                cp.start()
                sends.append(cp)
        for k in range(1, N_DEV):
            peer, peer_idx = _peer(pos, k)
            for a, (x, o) in enumerate(zip(x_refs, o_refs)):
                pltpu.make_async_remote_copy(
                    src_ref=x if gather else x.at[me], dst_ref=o.at[peer_idx],
                    send_sem=send_sems.at[k - 1, a], recv_sem=recv_sems.at[k - 1, a],
                    device_id=peer, device_id_type=pl.DeviceIdType.MESH).wait_recv()
        for cp in sends:
            cp.wait_send()
        for cp in local:
            cp.wait()

    return pl.pallas_call(
        body, name=name,
        in_specs=[pl.BlockSpec(memory_space=pl.ANY)] * n, out_specs=[pl.BlockSpec(memory_space=pl.ANY)] * n,
        out_shape=[jax.ShapeDtypeStruct(((N_DEV,) + x.shape) if gather else x.shape, x.dtype) for x in arrays],
        scratch_shapes=[pltpu.SemaphoreType.DMA((N_DEV - 1, n)), pltpu.SemaphoreType.DMA((N_DEV - 1, n)),
                        pltpu.SemaphoreType.DMA((n,))],
    )(*arrays)


def adamw(name, w, m, v, gparts):
    layers, rows, cols = w.shape
    parts = gparts[0].shape[0]
    tr = 8
    while tr * 2 * cols <= 65536 and rows % (tr * 2) == 0:
        tr *= 2
    nblk = rows // tr
    c1 = 1.0 - ADAM_B1 ** ADAM_STEP
    c2 = 1.0 - ADAM_B2 ** ADAM_STEP

    def body(*refs):
        w_ref, m_ref, v_ref = refs[:3]
        g_refs = refs[3:3 + layers]
        go_ref, d_ref, mo_ref, vo_ref = refs[3 + layers:]
        layer = pl.program_id(0)
        g = None
        for ll, g_ref in enumerate(g_refs):
            s = g_ref[0]
            for p in range(1, parts):
                s = s + g_ref[p]
            g = s if g is None else jnp.where(layer == ll, s, g)
        m_new = ADAM_B1 * m_ref[...] + (1.0 - ADAM_B1) * g
        v_new = ADAM_B2 * v_ref[...] + (1.0 - ADAM_B2) * (g * g)
        m_hat = m_new / c1
        v_hat = v_new / c2
        go_ref[...] = g
        d_ref[...] = -ADAM_LR * (m_hat / (jnp.sqrt(v_hat) + ADAM_EPS) + ADAM_WD * w_ref[...])
        mo_ref[...] = m_new
        vo_ref[...] = v_new

    def part_spec(ll):
        return pl.BlockSpec((parts, tr, cols),
                            lambda l, i: (0, jnp.where(l == ll, i, jnp.where(l < ll, 0, nblk - 1)), 0))

    spec = pl.BlockSpec((None, tr, cols), lambda l, i: (l, i, 0))
    return pl.pallas_call(
        body, name=name, grid=(layers, nblk),
        in_specs=[spec, spec, spec] + [part_spec(ll) for ll in range(layers)],
        out_specs=[spec] * 4, out_shape=[jax.ShapeDtypeStruct(w.shape, F32)] * 4,
        compiler_params=pltpu.CompilerParams(dimension_semantics=("arbitrary", "arbitrary"),
                                             vmem_limit_bytes=VMEM_LIMIT),
    )(w, m, v, *gparts)


def _pack(arrays, dtype, row_multiple):
    flat = jnp.concatenate([a.astype(dtype).reshape(-1) for a in arrays])
    unit = row_multiple * LANES
    pad = (-flat.shape[0]) % unit
    if pad:
        flat = jnp.concatenate([flat, jnp.zeros((pad,), dtype)])
    return flat.reshape(-1, LANES)


def _unpack(flat2d, shapes):
    flat = flat2d.reshape(-1)
    out, off = [], 0
    for s in shapes:
        n = int(np.prod(s))
        out.append(flat[off:off + n].reshape(s))
        off += n
    return out


def _unpack_stacked(stacked, shapes):
    flat = stacked.reshape(N_DEV, -1)
    out, off = [], 0
    for s in shapes:
        n = int(np.prod(s))
        out.append(flat[:, off:off + n].reshape((N_DEV,) + tuple(s)))
        off += n
    return out


def _merge_shards(stacked, axis):
    t = jnp.moveaxis(stacked, 0, axis)
    s = t.shape
    return t.reshape(s[:axis] + (s[axis] * s[axis + 1],) + s[axis + 2:])


def _split_shards(full, axis):
    s = full.shape
    t = full.reshape(s[:axis] + (N_DEV, s[axis] // N_DEV) + s[axis + 1:])
    return jnp.moveaxis(t, axis, 0)


def _lower_bounds(hg_lower_bounds):
    p = jax.nn.softmax(hg_lower_bounds, axis=0)
    return jnp.cumsum(p, axis=0) - p[0:1]


def _prep_layer(p):
    def row(v):
        return v.reshape(1, -1)

    eye_b = jnp.eye(HEADS, dtype=F32)
    eye_g = jnp.eye(S5_GROUPS, dtype=F32)
    step = jnp.exp(p["s5_log_dt"])[:, None]
    lam_re, lam_im = p["s5_lam_re"], p["s5_lam_im"]
    mag = jnp.exp(lam_re * step)
    lb_re = mag * jnp.cos(lam_im * step)
    lb_im = mag * jnp.sin(lam_im * step)
    den = lam_re * lam_re + lam_im * lam_im
    f_re = ((lb_re - 1.0) * lam_re + lb_im * lam_im) / den
    f_im = (lb_im * lam_re - (lb_re - 1.0) * lam_im) / den
    bb_re = f_re[..., None] * p["s5_b_re"] - f_im[..., None] * p["s5_b_im"]
    bb_im = f_re[..., None] * p["s5_b_im"] + f_im[..., None] * p["s5_b_re"]
    width = S5_GROUPS * S5_GROUP
    n_state = S5_GROUPS * S5_STATE
    return dict(
        lb=row(p["lb"]), hg_norm=row(p["hg_norm"]), ret_norm=row(p["ret_norm"]),
        conv_w=p["lru_conv_w"], conv_b=row(p["lru_conv_b"]),
        wa=jnp.einsum("nij,nm->nimj", p["lru_wa"], eye_b).reshape(MIX_W, MIX_W), ba=row(p["lru_ba"]),
        wx=jnp.einsum("nij,nm->nimj", p["lru_wx"], eye_b).reshape(MIX_W, MIX_W), bx=row(p["lru_bx"]),
        sp=row(jax.nn.softplus(-p["lru_lambda"])),
        bt_re=jnp.einsum("gnp,gh->gphn", bb_re, eye_g).reshape(width, n_state),
        bt_im=jnp.einsum("gnp,gh->gphn", bb_im, eye_g).reshape(width, n_state),
        lb_re=row(lb_re), lb_im=row(lb_im),
        ct_re=jnp.einsum("gpn,gh->gnhp", p["s5_c_re"], eye_g).reshape(n_state, width),
        ct_im=jnp.einsum("gpn,gh->gnhp", p["s5_c_im"], eye_g).reshape(n_state, width),
        s5_d=row(p["s5_d"]), glu_w=p["s5_glu_w"], glu_b=row(p["s5_glu_b"]),
        b_gate=row(p["b_gate"]),
        norm_mix_pre=row(p["norm_mix_pre"]), norm_mix_post=row(p["norm_mix_post"]),
        norm_xa_pre=row(p["norm_xa_pre"]), norm_xa_post=row(p["norm_xa_post"]), norm_mem=row(p["norm_mem"]),
        norm_ffn_pre=row(p["norm_ffn_pre"]), norm_ffn_post=row(p["norm_ffn_post"]),
    )


_PREP_INPUTS = ("hg_norm", "ret_norm", "lru_conv_w", "lru_conv_b", "lru_wa", "lru_ba", "lru_wx", "lru_bx", "lru_lambda",
                "s5_lam_re", "s5_lam_im", "s5_b_re", "s5_b_im", "s5_c_re", "s5_c_im", "s5_d", "s5_log_dt", "s5_glu_w",
                "s5_glu_b", "b_gate", "norm_mix_pre", "norm_mix_post", "norm_xa_pre", "norm_xa_post", "norm_mem",
                "norm_ffn_pre", "norm_ffn_post")


def _retention_constants():
    lg = np.log1p(-np.power(2.0, -5.0 - np.arange(HEADS)))
    idx = np.arange(RET_CHUNK)

    def lanes(per_head_rows):
        return np.repeat(per_head_rows.T[:, :, None], HEAD_DIM, axis=2).reshape(RET_CHUNK, MIX_W)

    xi = lanes(np.exp((idx + 1.0)[None, :] * lg[:, None]))
    zeta = lanes(np.exp((RET_CHUNK - 1.0 - idx)[None, :] * lg[:, None]))
    rel = idx[:, None] - idx[None, :]
    decay = np.where(rel[None] >= 0, np.exp(np.maximum(rel, 0)[None] * lg[:, None, None]), 0.0)
    decay = np.transpose(decay, (1, 0, 2)).reshape(RET_CHUNK, HEADS * RET_CHUNK)
    g_end = np.repeat(np.exp(RET_CHUNK * lg), HEAD_DIM)[None, :]
    return tuple(jnp.asarray(a, F32) for a in (xi, zeta, decay, g_end))


def _rotary_tables(seq):
    pos = jnp.arange(seq, dtype=F32)
    inv_freq = 10000.0 ** (-jnp.arange(0, HEAD_DIM, 2, dtype=F32) / HEAD_DIM)
    ang = pos[:, None] * inv_freq[None, :]
    cos, sin = jnp.cos(ang), jnp.sin(ang)
    cos_t = jnp.tile(jnp.repeat(cos, 2, axis=1), (1, HEADS))
    sin_t = jnp.tile(jnp.stack([-sin, sin], axis=-1).reshape(seq, HEAD_DIM), (1, HEADS))
    return cos_t, sin_t


TB_HG = 128
TB_RET = 128
TB_LRU = 256
TB_S5 = 128
TB_ROW = 256

_STATE = (MIX_W, MIX_W)
_TAIL = (8, MIX_W)
_S5_TAIL = (8, S5_GROUPS * S5_STATE)


def _mixer_operands(l, z, kp, rot, ret_c):
    xi, zeta, decay, g_end = ret_c
    return dict(
        hg=dict(name=f"hgrn2_{l}", fn=fn_hgrn2, rows=[(z, 4 * MIX_W, 0)], params=[kp["lb"], kp["hg_norm"]],
                tb=TB_HG, states=(_STATE,), diff_params=(0, 1)),
        ret=dict(name=f"retention_{l}", fn=fn_retention, rows=[(z, 4 * MIX_W, 1), (rot[0], MIX_W, 0), (rot[1], MIX_W, 0)],
                 params=[kp["ret_norm"], xi, zeta, decay, g_end], tb=TB_RET, states=(_STATE,), diff_params=(0,)),
        lru=dict(name=f"rglru_{l}", fn=fn_rglru, rows=[(z, 2 * MIX_W, 4)],
                 params=[kp["conv_w"], kp["conv_b"], kp["wa"], kp["ba"], kp["wx"], kp["bx"], kp["sp"]],
                 tb=TB_LRU, states=(_TAIL, _TAIL), diff_params=(0, 1, 2, 3, 4, 5, 6)),
        s5=dict(name=f"s5_{l}", fn=fn_s5, rows=[(z, MIX_W, 10)],
                params=[kp["bt_re"], kp["bt_im"], kp["lb_re"], kp["lb_im"], kp["ct_re"], kp["ct_im"], kp["s5_d"],
                        kp["glu_w"], kp["glu_b"]],
                tb=TB_S5, states=(_S5_TAIL, _S5_TAIL), diff_params=tuple(range(9))),
    )


def _layer_forward(l, x, h, mem, bw, kp, rot, ret_c, g_next, target):
    d = x.shape[1]
    sv = dict(x=x, h=h)
    z = matmul(f"in_proj_{l}", h, bw["w_in"], "nn", F32)
    gl = matmul(f"gate_proj_{l}", h, bw["w_gate"], "nn", F32)
    sv.update(z=z, gl=gl)
    ops = _mixer_operands(l, z, kp, rot, ret_c)
    ys = []
    for key in ("hg", "ret", "lru", "s5"):
        o = ops[key]
        (y,), _, saved = stage_fwd(o["name"] + "_fwd", o["fn"], o["rows"], o["params"], tb=o["tb"],
                                   outs=[(MIX_W, F32)], states=o["states"])
        ys.append(y)
        sv[key + "_states"] = saved
    sv["ys"] = ys
    merge_params = [bw["w_up"][n] for n in range(4)] + [kp["b_gate"], bw["w_out"], kp["norm_mix_post"]]
    merge_rows = [(y, MIX_W, 0) for y in ys] + [(gl, 4 * d, 0), (x, d, 0)]
    (x1,), _, _ = stage_fwd(f"merge_{l}_fwd", fn_merge, merge_rows, merge_params, tb=TB_ROW, outs=[(d, F32)])
    wk, wv = bw["xa_w_kv"][:, :d], bw["xa_w_kv"][:, d:]
    mem_params = [kp["norm_mem"], wk, wv]
    (k, v), _, _ = stage_fwd(f"mem_{l}_fwd", fn_mem, [(mem, d, 0)], mem_params, tb=mem.shape[0],
                             outs=[(d, F32), (d, F32)])
    xa_params = [kp["norm_xa_pre"], bw["xa_w_q"], k, v, bw["xa_w_o"], kp["norm_xa_post"], kp["norm_ffn_pre"]]
    (x2, h3), _, _ = stage_fwd(f"xattn_{l}_fwd", fn_xattn, [(x1, d, 0)], xa_params, tb=TB_ROW,
                               outs=[(d, F32), (d, BF16)])
    gu = matmul(f"ffn_gu_{l}", h3, bw["ffn_w_gu"], "nn", F32)
    (a,), _, _ = stage_fwd(f"glu_{l}_fwd", fn_glu, [(gu, gu.shape[1], 0)], [], tb=TB_ROW, outs=[(gu.shape[1] // 2, BF16)])
    o3 = matmul(f"ffn_down_{l}", a, bw["ffn_w_down"], "nn", F32)
    sv.update(x1=x1, k=k, v=v, x2=x2, h3=h3, gu=gu, a=a, o3=o3, merge_params=merge_params, merge_rows=merge_rows,
              mem_params=mem_params, xa_params=xa_params)
    if g_next is not None:
        (x3, hn), _, _ = stage_fwd(f"res_{l}_fwd", fn_res_norm, [(x2, d, 0), (o3, d, 0)], [kp["norm_ffn_post"], g_next],
                                   tb=TB_ROW, outs=[(d, F32), (d, BF16)])
        return x3, hn, sv
    (dy,), (loss,), _ = stage_fwd(f"loss_{l}_fwd", fn_res_loss, [(x2, d, 0), (o3, d, 0), (target, d, 0)],
                                  [kp["norm_ffn_post"]], tb=TB_ROW, outs=[(d, F32)], accs=[(8, LANES)])
    return dy, loss[0, 0], sv


def _layer_backward(l, sv, mem, bw, kp, rot, ret_c, g_next, dx3, dhn):
    d = sv["x"].shape[1]
    gk, gb = {}, {}
    res_rows = [(sv["x2"], d, 0), (sv["o3"], d, 0)]
    if g_next is not None:
        (dx2, do3), (gk["norm_ffn_post"], gk["g_next"]) = stage_bwd(
            f"res_{l}_bwd", fn_res_norm, res_rows, [kp["norm_ffn_post"], g_next], [dx3, dhn], tb=TB_ROW,
            diff_rows=(0, 1), diff_params=(0, 1))
    else:
        (dx2, do3), (gk["norm_ffn_post"],) = stage_bwd(
            f"res_{l}_bwd", fn_res, res_rows, [kp["norm_ffn_post"]], [dx3], tb=TB_ROW, diff_rows=(0, 1), diff_params=(0,))
    da = matmul(f"ffn_down_dx_{l}", do3, bw["ffn_w_down"], "nt", F32)
    gb["ffn_w_down"] = matmul(f"ffn_down_dw_{l}", sv["a"], do3, "tn", F32)
    (dgu,), _ = stage_bwd(f"glu_{l}_bwd", fn_glu, [(sv["gu"], sv["gu"].shape[1], 0)], [], [da], tb=TB_ROW,
                          diff_rows=(0,), row_dtypes=(BF16,))
    dh3 = matmul(f"ffn_gu_dx_{l}", dgu, bw["ffn_w_gu"], "nt", F32)
    gb["ffn_w_gu"] = matmul(f"ffn_gu_dw_{l}", sv["h3"], dgu, "tn", F32)
    (dx1,), xa_g = stage_bwd(f"xattn_{l}_bwd", fn_xattn, [(sv["x1"], d, 0)], sv["xa_params"], [dx2, dh3], tb=TB_ROW,
                             diff_rows=(0,), diff_params=tuple(range(7)))
    gk["norm_xa_pre"], gb["xa_w_q"], dk, dv, gb["xa_w_o"], gk["norm_xa_post"], gk["norm_ffn_pre"] = xa_g
    _, (gk["norm_mem"], dwk, dwv) = stage_bwd(f"mem_{l}_bwd", fn_mem, [(mem, d, 0)], sv["mem_params"], [dk, dv],
                                              tb=mem.shape[0], diff_params=(0, 1, 2))
    gb["xa_w_kv"] = jnp.concatenate([dwk, dwv], axis=1)
    merge_d, merge_g = stage_bwd(f"merge_{l}_bwd", fn_merge, sv["merge_rows"], sv["merge_params"], [dx1], tb=TB_ROW,
                                 diff_rows=tuple(range(6)), diff_params=tuple(range(7)),
                                 row_dtypes=(F32, F32, F32, F32, BF16, F32))
    dys, dgl, dx0 = merge_d[:4], merge_d[4], merge_d[5]
    gb["w_up"] = jnp.stack(merge_g[:4])
    gk["b_gate"], gb["w_out"], gk["norm_mix_post"] = merge_g[4:]
    ops = _mixer_operands(l, sv["z"], kp, rot, ret_c)
    dz = {}
    for key, dy in zip(("hg", "ret", "lru", "s5"), dys):
        o = ops[key]
        (dz[key],), pg = stage_bwd(o["name"] + "_bwd", o["fn"], o["rows"], o["params"], [dy], tb=o["tb"],
                                   saved=sv[key + "_states"], diff_rows=(0,), diff_params=o["diff_params"],
                                   row_dtypes=(BF16,))
        gk[key] = pg
    dh = matmul(f"gate_proj_dx_{l}", dgl, bw["w_gate"], "nt", F32)
    gb["w_gate"] = matmul(f"gate_proj_dw_{l}", sv["h"], dgl, "tn", F32)
    dw_in, col = [], 0
    for key in ("hg", "ret", "lru", "s5"):
        width = dz[key].shape[1]
        dh = matmul(f"in_proj_dx_{key}_{l}", dz[key], bw["w_in"][:, col:col + width], "nt", F32, add=dh)
        dw_in.append(matmul(f"in_proj_dw_{key}_{l}", sv["h"], dz[key], "tn", F32))
        col += width
    gb["w_in"] = jnp.concatenate(dw_in, axis=1)
    return dx0, dh, gk, gb


def _kernel_grads_to_prep(gk):
    hg, ret, lru, s5 = gk["hg"], gk["ret"], gk["lru"], gk["s5"]
    return dict(
        lb=hg[0], hg_norm=hg[1], ret_norm=ret[0],
        conv_w=lru[0], conv_b=lru[1], wa=lru[2], ba=lru[3], wx=lru[4], bx=lru[5], sp=lru[6],
        bt_re=s5[0], bt_im=s5[1], lb_re=s5[2], lb_im=s5[3], ct_re=s5[4], ct_im=s5[5], s5_d=s5[6], glu_w=s5[7],
        glu_b=s5[8], b_gate=gk["b_gate"], norm_mix_pre=gk["norm_mix_pre"], norm_mix_post=gk["norm_mix_post"],
        norm_xa_pre=gk["norm_xa_pre"], norm_xa_post=gk["norm_xa_post"], norm_mem=gk["norm_mem"],
        norm_ffn_pre=gk["norm_ffn_pre"], norm_ffn_post=gk["norm_ffn_post"],
    )


def _step(inp):
    x, mem, target = inp["x"][0], inp["mem"][0], inp["loss_target"][0]
    seq = x.shape[0]
    depth = inp["w_in"].shape[0]
    me = 4 * lax.axis_index("x") + 2 * lax.axis_index("y") + lax.axis_index("c")

    bws = []
    for l in range(depth):
        stacked = _exchange(f"gather_big_{l}", [inp[n][l].astype(BF16) for n in _BIG_NAMES], True)
        bws.append({n: _merge_shards(s, ax - 1) for (n, ax), s in zip(BIG, stacked)})
    small_shapes = [inp[n].shape for n in _SMALL_SHARDED_NAMES]
    (small_stacked,) = _exchange("gather_small", [_pack([inp[n] for n in _SMALL_SHARDED_NAMES], F32, 8)], True)
    small_all = _unpack_stacked(small_stacked, small_shapes)
    full_small = {n: _merge_shards(s, ax) for (n, ax), s in zip(SMALL_SHARDED, small_all)}

    lbs, lbs_vjp = jax.vjp(_lower_bounds, inp["hg_lower_bounds"])
    kps, prep_vjps = [], []
    for l in range(depth):
        p = {n: (full_small[n][l] if n in full_small else inp[n][l]) for n in _PREP_INPUTS}
        p["lb"] = lbs[l]
        kp, vj = jax.vjp(_prep_layer, p)
        kps.append(kp)
        prep_vjps.append(vj)
    rot = _rotary_tables(seq)
    ret_c = _retention_constants()

    (h,), _, _ = stage_fwd("norm_in_fwd", fn_norm, [(x, x.shape[1], 0)], [kps[0]["norm_mix_pre"]], tb=TB_ROW,
                           outs=[(x.shape[1], BF16)])
    saved = []
    xs = x
    for l in range(depth):
        g_next = kps[l + 1]["norm_mix_pre"] if l + 1 < depth else None
        xs, h, sv = _layer_forward(l, xs, h, mem, bws[l], kps[l], rot, ret_c, g_next, target)
        saved.append(sv)
    dy, loss_local = xs, h

    big_parts = [None] * depth
    gks = [None] * depth
    dx, dh = dy, None
    for l in reversed(range(depth)):
        g_next = kps[l + 1]["norm_mix_pre"] if l + 1 < depth else None
        dx, dh, gk, gb = _layer_backward(l, saved[l], mem, bws[l], kps[l], rot, ret_c, g_next, dx, dh)
        gks[l] = gk
        big_parts[l] = _exchange(f"scatter_big_grads_{l}", [_split_shards(gb[n], ax - 1) for n, ax in BIG], False)
    (grad_x,), (g_pre0,) = stage_bwd("norm_in_bwd", fn_keep_norm, [(x, x.shape[1], 0)], [kps[0]["norm_mix_pre"]],
                                     [dx, dh], tb=TB_ROW, diff_rows=(0,), diff_params=(0,))
    for l in range(depth):
        gks[l]["norm_mix_pre"] = g_pre0 if l == 0 else gks[l - 1]["g_next"]

    small_grads = {n: [None] * depth for n in _PREP_INPUTS}
    d_lbs = []
    for l in range(depth):
        (gp,) = prep_vjps[l](_kernel_grads_to_prep(gks[l]))
        d_lbs.append(gp["lb"])
        for n in _PREP_INPUTS:
            small_grads[n][l] = gp[n]
    small_local = {n: jnp.stack(v) for n, v in small_grads.items()}
    (small_local["hg_lower_bounds"],) = lbs_vjp(jnp.stack(d_lbs))
    small_names = REPLICATED + _SMALL_SHARDED_NAMES
    full_shapes = [small_local[n].shape for n in small_names]
    (small_parts,) = _exchange("gather_small_grads", [_pack([small_local[n] for n in small_names], F32, 64)], True)

    out = {}
    kinds = ("grad_", "delta_", "new_m_", "new_v_")
    for i, n in enumerate(_BIG_NAMES):
        shape = inp[n].shape
        three = (shape[0], int(np.prod(shape[1:-1])), shape[-1])
        res = adamw("adamw_" + n, *[inp[pre + n].reshape(three) for pre in ("", "m_", "v_")],
                    [big_parts[l][i].reshape((N_DEV,) + three[1:]) for l in range(depth)])
        for kind, a in zip(kinds, res):
            out[kind + n] = a.reshape(shape)
    zeros = [jnp.zeros(s, F32) for s in full_shapes[len(REPLICATED):]]
    res = adamw("adamw_small", *[_pack([inp[pre + n] for n in REPLICATED] + zeros, F32, 64)[None] for pre in ("", "m_", "v_")],
                [small_parts])
    res = [r[0] for r in res]
    summed = _unpack(res[0], full_shapes)
    for kind, flat in zip(("grad_", "delta_", "new_m_", "new_v_"), res):
        for n, a in zip(REPLICATED, _unpack(flat, full_shapes[:len(REPLICATED)])):
            out[kind + n] = a
    shard_g = []
    for (n, ax), g_full in zip(SMALL_SHARDED, summed[len(REPLICATED):]):
        width = inp[n].shape[ax]
        shard_g.append(lax.dynamic_slice_in_dim(g_full, me * width, width, axis=ax))
    res = adamw("adamw_small_sharded",
                *[_pack([inp[pre + n] for n in _SMALL_SHARDED_NAMES], F32, 8)[None] for pre in ("", "m_", "v_")],
                [_pack(shard_g, F32, 8)[None]])
    res = [r[0] for r in res]
    for kind, flat in zip(("grad_", "delta_", "new_m_", "new_v_"), res):
        for n, a in zip(_SMALL_SHARDED_NAMES, _unpack(flat, small_shapes)):
            out[kind + n] = a

    out["loss"] = lax.psum(loss_local, ("x", "y", "c"))
    out["grad_x"] = grad_x[None]
    return out


def kernel(x, mem, hg_lower_bounds, norm_mix_pre, norm_mix_post, w_in, w_gate, b_gate, hg_norm, ret_norm, lru_conv_w, lru_conv_b, lru_wa, lru_ba, lru_wx, lru_bx, lru_lambda, s5_lam_re, s5_lam_im, s5_b_re, s5_b_im, s5_c_re, s5_c_im, s5_d, s5_log_dt, s5_glu_w, s5_glu_b, w_up, w_out, norm_xa_pre, norm_xa_post, norm_mem, xa_w_q, xa_w_kv, xa_w_o, norm_ffn_pre, norm_ffn_post, ffn_w_gu, ffn_w_down, loss_target, m_hg_lower_bounds, m_norm_mix_pre, m_norm_mix_post, m_w_in, m_w_gate, m_b_gate, m_hg_norm, m_ret_norm, m_lru_conv_w, m_lru_conv_b, m_lru_wa, m_lru_ba, m_lru_wx, m_lru_bx, m_lru_lambda, m_s5_lam_re, m_s5_lam_im, m_s5_b_re, m_s5_b_im, m_s5_c_re, m_s5_c_im, m_s5_d, m_s5_log_dt, m_s5_glu_w, m_s5_glu_b, m_w_up, m_w_out, m_norm_xa_pre, m_norm_xa_post, m_norm_mem, m_xa_w_q, m_xa_w_kv, m_xa_w_o, m_norm_ffn_pre, m_norm_ffn_post, m_ffn_w_gu, m_ffn_w_down, v_hg_lower_bounds, v_norm_mix_pre, v_norm_mix_post, v_w_in, v_w_gate, v_b_gate, v_hg_norm, v_ret_norm, v_lru_conv_w, v_lru_conv_b, v_lru_wa, v_lru_ba, v_lru_wx, v_lru_bx, v_lru_lambda, v_s5_lam_re, v_s5_lam_im, v_s5_b_re, v_s5_b_im, v_s5_c_re, v_s5_c_im, v_s5_d, v_s5_log_dt, v_s5_glu_w, v_s5_glu_b, v_w_up, v_w_out, v_norm_xa_pre, v_norm_xa_post, v_norm_mem, v_xa_w_q, v_xa_w_kv, v_xa_w_o, v_norm_ffn_pre, v_norm_ffn_post, v_ffn_w_gu, v_ffn_w_down):
    values = (x, mem, hg_lower_bounds, norm_mix_pre, norm_mix_post, w_in, w_gate, b_gate, hg_norm, ret_norm, lru_conv_w, lru_conv_b, lru_wa, lru_ba, lru_wx, lru_bx, lru_lambda, s5_lam_re, s5_lam_im, s5_b_re, s5_b_im, s5_c_re, s5_c_im, s5_d, s5_log_dt, s5_glu_w, s5_glu_b, w_up, w_out, norm_xa_pre, norm_xa_post, norm_mem, xa_w_q, xa_w_kv, xa_w_o, norm_ffn_pre, norm_ffn_post, ffn_w_gu, ffn_w_down, loss_target, m_hg_lower_bounds, m_norm_mix_pre, m_norm_mix_post, m_w_in, m_w_gate, m_b_gate, m_hg_norm, m_ret_norm, m_lru_conv_w, m_lru_conv_b, m_lru_wa, m_lru_ba, m_lru_wx, m_lru_bx, m_lru_lambda, m_s5_lam_re, m_s5_lam_im, m_s5_b_re, m_s5_b_im, m_s5_c_re, m_s5_c_im, m_s5_d, m_s5_log_dt, m_s5_glu_w, m_s5_glu_b, m_w_up, m_w_out, m_norm_xa_pre, m_norm_xa_post, m_norm_mem, m_xa_w_q, m_xa_w_kv, m_xa_w_o, m_norm_ffn_pre, m_norm_ffn_post, m_ffn_w_gu, m_ffn_w_down, v_hg_lower_bounds, v_norm_mix_pre, v_norm_mix_post, v_w_in, v_w_gate, v_b_gate, v_hg_norm, v_ret_norm, v_lru_conv_w, v_lru_conv_b, v_lru_wa, v_lru_ba, v_lru_wx, v_lru_bx, v_lru_lambda, v_s5_lam_re, v_s5_lam_im, v_s5_b_re, v_s5_b_im, v_s5_c_re, v_s5_c_im, v_s5_d, v_s5_log_dt, v_s5_glu_w, v_s5_glu_b, v_w_up, v_w_out, v_norm_xa_pre, v_norm_xa_post, v_norm_mem, v_xa_w_q, v_xa_w_kv, v_xa_w_o, v_norm_ffn_pre, v_norm_ffn_post, v_ffn_w_gu, v_ffn_w_down)
    names = ("x", "mem") + WEIGHTS + ("loss_target",) + tuple("m_" + n for n in WEIGHTS) + tuple("v_" + n for n in WEIGHTS)
    out = _step(dict(zip(names, values)))
    order = ["loss", "grad_x"] + [k + n for k in ("grad_", "delta_", "new_m_", "new_v_") for n in WEIGHTS]
    return tuple(out[k] for k in order)
```

```python
import functools

import numpy as np
import jax
import jax.numpy as jnp
from jax import lax
from jax.experimental import pallas as pl
from jax.experimental.pallas import tpu as pltpu

F32 = jnp.float32
BF16 = jnp.bfloat16
EPS = 1e-6
N_DEV = 8
LANES = 128
VMEM_LIMIT = 60 * 1024 * 1024

HEADS = 4
HEAD_DIM = 64
MIX_W = HEADS * HEAD_DIM
HG_CHUNK = 32
RET_CHUNK = 128
S5_GROUPS = 16
S5_GROUP = 16
S5_STATE = 64
LRU_C = 8.0
XA_HEADS = 4

ADAM_LR = 0.001
ADAM_B1 = 0.9
ADAM_B2 = 0.999
ADAM_EPS = 1e-08
ADAM_WD = 0.01
ADAM_STEP = 10

BIG = (("w_in", 2), ("w_gate", 2), ("w_up", 3), ("w_out", 1), ("xa_w_q", 1), ("xa_w_kv", 2), ("xa_w_o", 1),
       ("ffn_w_gu", 2), ("ffn_w_down", 1))
SMALL_SHARDED = (("lru_conv_w", 2), ("s5_glu_w", 1))
WEIGHTS = ("hg_lower_bounds", "norm_mix_pre", "norm_mix_post", "w_in", "w_gate", "b_gate", "hg_norm", "ret_norm",
           "lru_conv_w", "lru_conv_b", "lru_wa", "lru_ba", "lru_wx", "lru_bx", "lru_lambda", "s5_lam_re", "s5_lam_im",
           "s5_b_re", "s5_b_im", "s5_c_re", "s5_c_im", "s5_d", "s5_log_dt", "s5_glu_w", "s5_glu_b", "w_up", "w_out",
           "norm_xa_pre", "norm_xa_post", "norm_mem", "xa_w_q", "xa_w_kv", "xa_w_o", "norm_ffn_pre", "norm_ffn_post",
           "ffn_w_gu", "ffn_w_down")
GATHER_HOSTS = {"in_proj": (0, 2, 3), "gate_proj": (1, 4), "ffn_gu": (7, 6), "ffn_down": (8, 5)}
SCATTER_HOSTS = {"ffn_down_dx": (8, 2, 3), "ffn_down_dw": (7, 4, 6), "ffn_gu_dx": (1,), "ffn_gu_dw": (0, 5)}
_BIG_NAMES = tuple(n for n, _ in BIG)
_SMALL_SHARDED_NAMES = tuple(n for n, _ in SMALL_SHARDED)
REPLICATED = tuple(n for n in WEIGHTS if n not in _BIG_NAMES and n not in _SMALL_SHARDED_NAMES)


def _dot(a, b):
    return jnp.dot(a.astype(BF16), b.astype(BF16), preferred_element_type=F32)


def _dot_nt(a, b):
    return lax.dot_general(a.astype(BF16), b.astype(BF16), (((1,), (1,)), ((), ())), preferred_element_type=F32)


def _dot_tn(a, b):
    return lax.dot_general(a.astype(BF16), b.astype(BF16), (((0,), (0,)), ((), ())), preferred_element_type=F32)


def _dot_exact(a, b):
    return jnp.dot(a, b, precision=lax.Precision.HIGHEST, preferred_element_type=F32)


def _rms(x, g):
    return x * lax.rsqrt(jnp.mean(x * x, axis=-1, keepdims=True) + EPS) * g


def _shift_down(x, d, fill):
    return jnp.concatenate([jnp.full((d, x.shape[1]), fill, x.dtype), x[:-d]], axis=0)


def _shift_up(x, d, fill):
    return jnp.concatenate([x[d:], jnp.full((d, x.shape[1]), fill, x.dtype)], axis=0)


def _cumsum_rows(x):
    d = 1
    while d < x.shape[0]:
        x = x + _shift_down(x, d, 0.0)
        d *= 2
    return x


def _lane_head(shape, dim):
    return lax.shift_right_logical(lax.broadcasted_iota(jnp.int32, shape, dim), 6)


def _head_masks(width=MIX_W):
    head = _lane_head((1, width), 1)
    return [(head == h).astype(F32) for h in range(HEADS)]


def _block_diag_mask():
    return (_lane_head((MIX_W, MIX_W), 0) == _lane_head((MIX_W, MIX_W), 1)).astype(F32)


def _head_rms(o, g):
    ms = _dot_exact(o * o, _block_diag_mask()) * (1.0 / HEAD_DIM)
    return o * lax.rsqrt(ms + EPS) * g


def _swap_pairs(x):
    lane = lax.broadcasted_iota(jnp.int32, x.shape, 1)
    return jnp.where((lane & 1) == 0, jnp.roll(x, -1, axis=1), jnp.roll(x, 1, axis=1))


def _stack_heads(t, masks):
    return jnp.concatenate([t * m for m in masks], axis=0)


@jax.custom_vjp
def _real_scan(a, u, h0):
    return _real_scan_fwd(a, u, h0)[0]


def _real_scan_fwd(a, u, h0):
    t = a.shape[0]
    acc_a, acc_u = a, u
    d = 1
    while d < t:
        acc_u = acc_u + acc_a * _shift_down(acc_u, d, 0.0)
        acc_a = acc_a * _shift_down(acc_a, d, 1.0)
        d *= 2
    h = acc_u + acc_a * h0
    return h, (a, h, h0)


def _real_scan_bwd(res, dh):
    a, h, h0 = res
    t = a.shape[0]
    acc_a = _shift_up(a, 1, 0.0)
    g = dh
    d = 1
    while d < t:
        g = g + acc_a * _shift_up(g, d, 0.0)
        acc_a = acc_a * _shift_up(acc_a, d, 1.0)
        d *= 2
    h_prev = jnp.concatenate([h0, h[:-1]], axis=0)
    return g * h_prev, g, (a * g)[0:1]


_real_scan.defvjp(_real_scan_fwd, _real_scan_bwd)


def _cmul(ar, ai, br, bi):
    return ar * br - ai * bi, ar * bi + ai * br


@jax.custom_vjp
def _complex_scan(ar, ai, ur, ui, h0r, h0i):
    return _complex_scan_fwd(ar, ai, ur, ui, h0r, h0i)[0]


def _complex_scan_fwd(ar, ai, ur, ui, h0r, h0i):
    t = ar.shape[0]
    pr, pi, sr, si = ar, ai, ur, ui
    d = 1
    while d < t:
        mr, mi = _cmul(pr, pi, _shift_down(sr, d, 0.0), _shift_down(si, d, 0.0))
        sr, si = sr + mr, si + mi
        pr, pi = _cmul(pr, pi, _shift_down(pr, d, 1.0), _shift_down(pi, d, 0.0))
        d *= 2
    mr, mi = _cmul(pr, pi, h0r, h0i)
    hr, hi = sr + mr, si + mi
    return (hr, hi), (ar, ai, hr, hi, h0r, h0i)


def _complex_scan_bwd(res, dh):
    ar, ai, hr, hi, h0r, h0i = res
    dhr, dhi = dh
    t = ar.shape[0]
    pr, pi = _shift_up(ar, 1, 0.0), -_shift_up(ai, 1, 0.0)
    gr, gi = dhr, dhi
    d = 1
    while d < t:
        mr, mi = _cmul(pr, pi, _shift_up(gr, d, 0.0), _shift_up(gi, d, 0.0))
        gr, gi = gr + mr, gi + mi
        pr, pi = _cmul(pr, pi, _shift_up(pr, d, 1.0), _shift_up(pi, d, 0.0))
        d *= 2
    qr = jnp.concatenate([h0r, hr[:-1]], axis=0)
    qi = jnp.concatenate([h0i, hi[:-1]], axis=0)
    dar, dai = gr * qr + gi * qi, gi * qr - gr * qi
    d0r, d0i = _cmul(ar[0:1], -ai[0:1], gr[0:1], gi[0:1])
    return dar, dai, gr, gi, d0r, d0i


_complex_scan.defvjp(_complex_scan_fwd, _complex_scan_bwd)


def fn_norm(st, rows, params):
    (x,), (g,) = rows, params
    return (), (_rms(x, g),), ()


def fn_keep_norm(st, rows, params):
    (x,), (g,) = rows, params
    return (), (x, _rms(x, g)), ()


def fn_hgrn2(st, rows, params):
    (state,) = st
    (z,) = rows
    lb, norm_g = params
    q, f_logit, v_all, g = (z[:, k * MIX_W:(k + 1) * MIX_W] for k in range(4))
    f = lb + (1.0 - lb) * jax.nn.sigmoid(f_logit)
    log_f = jnp.log(f)
    k_all = 1.0 - f
    q_all = jax.nn.silu(q)
    masks = _head_masks()
    bd = _block_diag_mask()
    c = HG_CHUNK
    col = lax.broadcasted_iota(jnp.int32, (c, HEADS * c), 1) & (c - 1)
    causal = col <= lax.broadcasted_iota(jnp.int32, (c, HEADS * c), 0)
    outs = []
    for n in range(z.shape[0] // c):
        sl = slice(n * c, (n + 1) * c)
        lf = log_f[sl]
        b = _cumsum_rows(lf)
        b_end = jnp.sum(lf, axis=0, keepdims=True)
        q_dec = q_all[sl] * jnp.exp(b)
        k_inv = k_all[sl] * jnp.exp(-b)
        k_end = k_all[sl] * jnp.exp(b_end - b)
        v = v_all[sl]
        scores = jnp.where(causal, _dot_nt(q_dec, _stack_heads(k_inv, masks)), 0.0)
        outs.append(_dot(scores, _stack_heads(v, masks)) + _dot_nt(q_dec, state))
        state = state * jnp.exp(b_end) + _dot_tn(v, k_end) * bd
    o = jnp.concatenate(outs, axis=0) if len(outs) > 1 else outs[0]
    return (state,), (_head_rms(o, norm_g) * jax.nn.silu(g),), ()


def fn_retention(st, rows, params):
    (state,) = st
    z, cos_t, sin_t = rows
    norm_g, xi, zeta, decay, g_end = params
    q, k, v_all, g = (z[:, i * MIX_W:(i + 1) * MIX_W] for i in range(4))
    q_all = q * cos_t + _swap_pairs(q) * sin_t
    k_all = (k * cos_t + _swap_pairs(k) * sin_t) * (HEAD_DIM ** -0.5)
    masks = _head_masks()
    bd = _block_diag_mask()
    c = RET_CHUNK
    outs = []
    for n in range(z.shape[0] // c):
        sl = slice(n * c, (n + 1) * c)
        qc, kc, v = q_all[sl], k_all[sl], v_all[sl]
        scores = _dot_nt(qc, _stack_heads(kc, masks)) * decay
        outs.append(_dot(scores, _stack_heads(v, masks)) + _dot_nt(qc * xi, state))
        state = state * g_end + _dot_tn(v, kc * zeta) * bd
    o = jnp.concatenate(outs, axis=0) if len(outs) > 1 else outs[0]
    return (state,), (_head_rms(o, norm_g) * jax.nn.silu(g),), ()


def fn_rglru(st, rows, params):
    tail_x, tail_h = st
    (z,) = rows
    conv_w, conv_b, wa, ba, wx, bx, sp = params
    t = z.shape[0]
    xg, xi = z[:, :MIX_W], z[:, MIX_W:]
    full = jnp.concatenate([tail_x, xi], axis=0)
    xc = conv_b
    for k in range(4):
        xc = xc + conv_w[k:k + 1] * full[5 + k:5 + k + t]
    r = jax.nn.sigmoid(_dot(xc, wa) + ba)
    ig = jax.nn.sigmoid(_dot(xc, wx) + bx)
    log_a = -LRU_C * r * sp
    a = jnp.exp(log_a)
    one_minus_a2 = -jnp.tanh(log_a) * (a * a + 1.0)
    u = jnp.sqrt(one_minus_a2) * (ig * xc)
    h = _real_scan(a, u, tail_h[7:8])
    return (xi[t - 8:], h[t - 8:]), (h * jax.nn.gelu(xg),), ()


def fn_s5(st, rows, params):
    tail_r, tail_i = st
    (u,) = rows
    bt_re, bt_im, lb_re, lb_im, ct_re, ct_im, d, glu_w, glu_b = params
    t = u.shape[0]
    bu_re = _dot(u, bt_re)
    bu_im = _dot(u, bt_im)
    a_re = jnp.broadcast_to(lb_re, bu_re.shape)
    a_im = jnp.broadcast_to(lb_im, bu_im.shape)
    h_re, h_im = _complex_scan(a_re, a_im, bu_re, bu_im, tail_r[7:8], tail_i[7:8])
    y = _dot(h_re, ct_re) - _dot(h_im, ct_im) + d * u
    act = jax.nn.gelu(y)
    out = act * jax.nn.sigmoid(_dot(act, glu_w) + glu_b)
    return (h_re[t - 8:], h_im[t - 8:]), (out,), ()


def fn_merge(st, rows, params):
    ya, yb, yc, yd, gl, x = rows
    w0, w1, w2, w3, b_gate, w_out, g_post = params
    d = x.shape[1]
    mix = None
    for n, (y, w) in enumerate(((ya, w0), (yb, w1), (yc, w2), (yd, w3))):
        gate = jax.nn.sigmoid(gl[:, n * d:(n + 1) * d] + b_gate[:, n * d:(n + 1) * d])
        term = gate * _dot(y, w)
        mix = term if mix is None else mix + term
    return (), (x + _rms(_dot(mix, w_out), g_post),), ()


def fn_mem(st, rows, params):
    (mem,), (g, wk, wv) = rows, params
    m = _rms(mem, g)
    return (), (_dot(m, wk), _dot(m, wv)), ()


def fn_xattn(st, rows, params):
    (x,) = rows
    g_pre, wq, k, v, wo, g_post, g_next = params
    d = x.shape[1]
    dh = d // XA_HEADS
    q = _dot(_rms(x, g_pre), wq)
    heads = []
    for h in range(XA_HEADS):
        sl = slice(h * dh, (h + 1) * dh)
        s = _dot_nt(q[:, sl], k[:, sl]) * (dh ** -0.5)
        heads.append(_dot(jax.nn.softmax(s, axis=-1), v[:, sl]))
    x2 = x + _rms(_dot(jnp.concatenate(heads, axis=1), wo), g_post)
    return (), (x2, _rms(x2, g_next)), ()


def fn_glu(st, rows, params):
    (gu,) = rows
    f = gu.shape[1] // 2
    return (), (jax.nn.silu(gu[:, :f]) * gu[:, f:],), ()


def fn_res_norm(st, rows, params):
    (x, o), (g_post, g_next) = rows, params
    xn = x + _rms(o, g_post)
    return (), (xn, _rms(xn, g_next)), ()


def fn_res(st, rows, params):
    (x, o), (g_post,) = rows, params
    return (), (x + _rms(o, g_post),), ()


def fn_res_loss(st, rows, params):
    (x, o, target), (g_post,) = rows, params
    err = x + _rms(o, g_post) - target
    inv_d = 1.0 / x.shape[1]
    loss = 0.5 * inv_d * jnp.sum(err * err)
    return (), (err * inv_d,), (jnp.full((8, LANES), loss, F32),)


def _params():
    return pltpu.CompilerParams(dimension_semantics=("arbitrary",), vmem_limit_bytes=VMEM_LIMIT)


def _row_spec(tb, width, colblk, nb, reverse):
    if reverse:
        return pl.BlockSpec((tb, width), lambda i: (nb - 1 - i, colblk))
    return pl.BlockSpec((tb, width), lambda i: (i, colblk))


def _full_spec(shape):
    return pl.BlockSpec(shape, lambda i: (0,) * len(shape))


def _saved_spec(shape, nb, reverse):
    if reverse:
        return pl.BlockSpec((1,) + shape, lambda i: (nb - 1 - i, 0, 0))
    return pl.BlockSpec((1,) + shape, lambda i: (i, 0, 0))


def stage_fwd(name, fn, rows, params, *, tb, outs, states=(), accs=()):
    n_rows = rows[0][0].shape[0]
    nb = n_rows // tb
    nr, npar, no, na, ns = len(rows), len(params), len(outs), len(accs), len(states)

    def body(*refs):
        row_refs, par_refs = refs[:nr], refs[nr:nr + npar]
        o = nr + npar
        out_refs, acc_refs = refs[o:o + no], refs[o + no:o + no + na]
        saved_refs = refs[o + no + na:o + no + na + ns]
        st_refs = refs[o + no + na + ns:]
        i = pl.program_id(0)

        @pl.when(i == 0)
        def _():
            for r in st_refs + acc_refs:
                r[...] = jnp.zeros_like(r)

        st = tuple(r[...] for r in st_refs)
        for sv, s in zip(saved_refs, st):
            sv[0] = s
        new_st, out_vals, acc_vals = fn(st, tuple(r[...].astype(F32) for r in row_refs),
                                        tuple(r[...].astype(F32) for r in par_refs))
        for r, v in zip(out_refs, out_vals):
            r[...] = v.astype(r.dtype)
        for r, v in zip(acc_refs, acc_vals):
            r[...] += v
        for r, v in zip(st_refs, new_st):
            r[...] = v

    res = pl.pallas_call(
        body, name=name, grid=(nb,),
        in_specs=[_row_spec(tb, w, cb, nb, False) for _, w, cb in rows] + [_full_spec(p.shape) for p in params],
        out_specs=[_row_spec(tb, w, 0, nb, False) for w, _ in outs] + [_full_spec(s) for s in accs]
        + [_saved_spec(s, nb, False) for s in states],
        out_shape=[jax.ShapeDtypeStruct((n_rows, w), dt) for w, dt in outs]
        + [jax.ShapeDtypeStruct(s, F32) for s in accs] + [jax.ShapeDtypeStruct((nb,) + s, F32) for s in states],
        scratch_shapes=[pltpu.VMEM(s, F32) for s in states],
        compiler_params=_params(),
    )(*[a for a, _, _ in rows], *params)
    return res[:no], res[no:no + na], res[no + na:]


def stage_bwd(name, fn, rows, params, cts, *, tb, saved=(), diff_rows=(), diff_params=(), row_dtypes=None):
    n_rows = rows[0][0].shape[0]
    nb = n_rows // tb
    nr, npar, ns, nc = len(rows), len(params), len(saved), len(cts)
    ndr, ndp = len(diff_rows), len(diff_params)
    row_dtypes = row_dtypes or (F32,) * ndr
    state_shapes = [s.shape[1:] for s in saved]

    def body(*refs):
        row_refs, par_refs = refs[:nr], refs[nr:nr + npar]
        o = nr + npar
        saved_refs, ct_refs = refs[o:o + ns], refs[o + ns:o + ns + nc]
        o = o + ns + nc
        drow_refs, dpar_refs, dst_refs = refs[o:o + ndr], refs[o + ndr:o + ndr + ndp], refs[o + ndr + ndp:]
        i = pl.program_id(0)

        @pl.when(i == 0)
        def _():
            for r in dst_refs + dpar_refs:
                r[...] = jnp.zeros_like(r)

        st = tuple(r[0] for r in saved_refs)
        row_vals = [r[...].astype(F32) for r in row_refs]
        par_vals = [r[...].astype(F32) for r in par_refs]

        def f(st_, dr_, dp_):
            rv, pv = list(row_vals), list(par_vals)
            for k, v in zip(diff_rows, dr_):
                rv[k] = v
            for k, v in zip(diff_params, dp_):
                pv[k] = v
            new_st, out_vals, _ = fn(st_, tuple(rv), tuple(pv))
            return new_st, out_vals

        _, vjp = jax.vjp(f, st, tuple(row_vals[k] for k in diff_rows), tuple(par_vals[k] for k in diff_params))
        g_st, g_rows, g_par = vjp((tuple(r[...] for r in dst_refs), tuple(r[...].astype(F32) for r in ct_refs)))
        for r, v in zip(drow_refs, g_rows):
            r[...] = v.astype(r.dtype)
        for r, v in zip(dpar_refs, g_par):
            r[...] += v
        for r, v in zip(dst_refs, g_st):
            r[...] = v

    res = pl.pallas_call(
        body, name=name, grid=(nb,),
        in_specs=[_row_spec(tb, w, cb, nb, True) for _, w, cb in rows] + [_full_spec(p.shape) for p in params]
        + [_saved_spec(s, nb, True) for s in state_shapes] + [_row_spec(tb, c.shape[1], 0, nb, True) for c in cts],
        out_specs=[_row_spec(tb, rows[k][1], 0, nb, True) for k in diff_rows]
        + [_full_spec(params[k].shape) for k in diff_params],
        out_shape=[jax.ShapeDtypeStruct((n_rows, rows[k][1]), dt) for k, dt in zip(diff_rows, row_dtypes)]
        + [jax.ShapeDtypeStruct(params[k].shape, F32) for k in diff_params],
        scratch_shapes=[pltpu.VMEM(s, F32) for s in state_shapes],
        compiler_params=_params(),
    )(*[a for a, _, _ in rows], *params, *saved, *cts)
    return res[:ndr], res[ndr:]


def _pick(n, target):
    if n <= target:
        return n
    best = None
    for t in range(LANES, target + 1, LANES):
        if n % t == 0:
            best = t
    assert best is not None, n
    return best


def _mesh_position():
    return lax.axis_index("x"), lax.axis_index("y"), lax.axis_index("c")


def _peer(pos, k):
    x, y, c = pos
    px = 1 - x if k & 4 else x
    py = 1 - y if k & 2 else y
    pc = 1 - c if k & 1 else c
    return (px, py, pc), 4 * px + 2 * py + pc


def _exchange_copies(x_refs, o_refs, send_sems, recv_sems, local_sems, gather):
    pos = _mesh_position()
    me = 4 * pos[0] + 2 * pos[1] + pos[2]
    pairs = list(enumerate(zip(x_refs, o_refs)))

    def remote(k, a, src, dst):
        peer, _ = _peer(pos, k)
        return pltpu.make_async_remote_copy(src_ref=src, dst_ref=dst, send_sem=send_sems.at[k - 1, a],
                                            recv_sem=recv_sems.at[k - 1, a], device_id=peer,
                                            device_id_type=pl.DeviceIdType.MESH)

    def local(a, x, o):
        return pltpu.make_async_copy(x if gather else x.at[me], o.at[me], local_sems.at[a])

    def start():
        for a, (x, o) in pairs:
            local(a, x, o).start()
        for k in range(1, N_DEV):
            peer_idx = _peer(pos, k)[1]
            for a, (x, o) in pairs:
                remote(k, a, x if gather else x.at[peer_idx], o.at[me]).start()

    def wait():
        for k in range(1, N_DEV):
            peer_idx = _peer(pos, k)[1]
            for a, (x, o) in pairs:
                arrival = remote(k, a, x if gather else x.at[me], o.at[peer_idx])
                arrival.wait_recv()
                arrival.wait_send()
        for a, (x, o) in pairs:
            local(a, x, o).wait()

    return start, wait


def _exchange_shapes(arrays, gather):
    return [jax.ShapeDtypeStruct(((N_DEV,) + x.shape) if gather else x.shape, x.dtype) for x in arrays]


def _exchange_sems(n):
    return [pltpu.SemaphoreType.DMA((N_DEV - 1, n)), pltpu.SemaphoreType.DMA((N_DEV - 1, n)),
            pltpu.SemaphoreType.DMA((n,))]


def _exchange(name, arrays, gather):
    n = len(arrays)

    def body(*refs):
        start, wait = _exchange_copies(refs[:n], refs[n:2 * n], *refs[2 * n:], gather)
        start()
        wait()

    return pl.pallas_call(
        body, name=name,
        in_specs=[pl.BlockSpec(memory_space=pl.ANY)] * n, out_specs=[pl.BlockSpec(memory_space=pl.ANY)] * n,
        out_shape=_exchange_shapes(arrays, gather), scratch_shapes=_exchange_sems(n),
    )(*arrays)


def matmul(name, a, b, mode, out_dtype, add=None, side=None):
    if mode == "tn":
        k, m = a.shape
    else:
        m, k = a.shape
    n = b.shape[0] if mode == "nt" else b.shape[1]
    if mode == "tn":
        tm, tn, tk = _pick(m, 1024), _pick(n, 512), _pick(k, 1024)
    else:
        tm, tn, tk = _pick(m, 1024), _pick(n, 512), _pick(k, 2816)
    nk = k // tk
    grid = (m // tm, n // tn, nk)
    a_spec = pl.BlockSpec((tk, tm), lambda i, j, kk: (kk, i)) if mode == "tn" else pl.BlockSpec((tm, tk), lambda i, j, kk: (i, kk))
    b_spec = pl.BlockSpec((tn, tk), lambda i, j, kk: (j, kk)) if mode == "nt" else pl.BlockSpec((tk, tn), lambda i, j, kk: (kk, j))
    o_spec = pl.BlockSpec((tm, tn), lambda i, j, kk: (i, j))
    dims = {"nn": (((1,), (0,)), ((), ())), "nt": (((1,), (1,)), ((), ())), "tn": (((0,), (0,)), ((), ()))}[mode]
    has_add = add is not None
    side_arrays, gather = side if side is not None else ((), False)
    ns = len(side_arrays)
    n_in = 2 + has_add

    def body(*refs):
        a_ref, b_ref = refs[0], refs[1]
        side_in, o_ref, side_out = refs[n_in:n_in + ns], refs[n_in + ns], refs[n_in + ns + 1:n_in + 2 * ns + 1]
        scratch = refs[n_in + 2 * ns + 1:]
        ids = [pl.program_id(d) for d in range(3)]
        if ns:
            start, wait = _exchange_copies(side_in, side_out, *scratch[-3:], gather)
            pl.when((ids[0] == 0) & (ids[1] == 0) & (ids[2] == 0))(start)
        part = lax.dot_general(a_ref[...].astype(BF16), b_ref[...].astype(BF16), dims, preferred_element_type=F32)
        if nk == 1:
            o_ref[...] = (part + refs[2][...].astype(F32) if has_add else part).astype(o_ref.dtype)
        else:
            acc_ref = scratch[0]

            @pl.when(ids[2] == 0)
            def _():
                acc_ref[...] = part + refs[2][...].astype(F32) if has_add else part

            @pl.when(ids[2] > 0)
            def _():
                acc_ref[...] += part

            @pl.when(ids[2] == nk - 1)
            def _():
                o_ref[...] = acc_ref[...].astype(o_ref.dtype)
        if ns:
            pl.when((ids[0] == grid[0] - 1) & (ids[1] == grid[1] - 1) & (ids[2] == nk - 1))(wait)

    any_spec = pl.BlockSpec(memory_space=pl.ANY)
    res = pl.pallas_call(
        body, name=name, grid=grid,
        in_specs=[a_spec, b_spec] + ([o_spec] if has_add else []) + [any_spec] * ns,
        out_specs=[o_spec] + [any_spec] * ns,
        out_shape=[jax.ShapeDtypeStruct((m, n), out_dtype)] + _exchange_shapes(side_arrays, gather),
        scratch_shapes=([pltpu.VMEM((tm, tn), F32)] if nk > 1 else []) + (_exchange_sems(ns) if ns else []),
        compiler_params=pltpu.CompilerParams(
            dimension_semantics=("arbitrary",) * 3 if ns else ("parallel", "parallel", "arbitrary"),
            vmem_limit_bytes=VMEM_LIMIT),
    )(a, b, *([add] if has_add else []), *side_arrays)
    return (res[0], res[1:]) if ns else res[0]


def adamw(name, w, m, v, gparts):
    layers, rows, cols = w.shape
    parts = gparts[0].shape[0]
    tr = 8
    while tr * 2 * cols <= 65536 and rows % (tr * 2) == 0:
        tr *= 2
    nblk = rows // tr
    c1 = 1.0 - ADAM_B1 ** ADAM_STEP
    c2 = 1.0 - ADAM_B2 ** ADAM_STEP

    def body(*refs):
        w_ref, m_ref, v_ref = refs[:3]
        g_refs = refs[3:3 + layers]
        go_ref, d_ref, mo_ref, vo_ref = refs[3 + layers:]
        layer = pl.program_id(0)
        g = None
        for ll, g_ref in enumerate(g_refs):
            s = g_ref[0].astype(F32)
            for p in range(1, parts):
                s = s + g_ref[p].astype(F32)
            g = s if g is None else jnp.where(layer == ll, s, g)
        m_new = ADAM_B1 * m_ref[...] + (1.0 - ADAM_B1) * g
        v_new = ADAM_B2 * v_ref[...] + (1.0 - ADAM_B2) * (g * g)
        m_hat = m_new / c1
        v_hat = v_new / c2
        go_ref[...] = g
        d_ref[...] = -ADAM_LR * (m_hat / (jnp.sqrt(v_hat) + ADAM_EPS) + ADAM_WD * w_ref[...])
        mo_ref[...] = m_new
        vo_ref[...] = v_new

    def part_spec(ll):
        return pl.BlockSpec((parts, tr, cols),
                            lambda l, i: (0, jnp.where(l == ll, i, jnp.where(l < ll, 0, nblk - 1)), 0))

    spec = pl.BlockSpec((None, tr, cols), lambda l, i: (l, i, 0))
    return pl.pallas_call(
        body, name=name, grid=(layers, nblk),
        in_specs=[spec, spec, spec] + [part_spec(ll) for ll in range(layers)],
        out_specs=[spec] * 4, out_shape=[jax.ShapeDtypeStruct(w.shape, F32)] * 4,
        compiler_params=pltpu.CompilerParams(dimension_semantics=("arbitrary", "arbitrary"),
                                             vmem_limit_bytes=VMEM_LIMIT),
    )(w, m, v, *gparts)


def _pack(arrays, dtype, row_multiple):
    flat = jnp.concatenate([a.astype(dtype).reshape(-1) for a in arrays])
    unit = row_multiple * LANES
    pad = (-flat.shape[0]) % unit
    if pad:
        flat = jnp.concatenate([flat, jnp.zeros((pad,), dtype)])
    return flat.reshape(-1, LANES)


def _unpack(flat2d, shapes):
    flat = flat2d.reshape(-1)
    out, off = [], 0
    for s in shapes:
        n = int(np.prod(s))
        out.append(flat[off:off + n].reshape(s))
        off += n
    return out


def _unpack_stacked(stacked, shapes):
    flat = stacked.reshape(N_DEV, -1)
    out, off = [], 0
    for s in shapes:
        n = int(np.prod(s))
        out.append(flat[:, off:off + n].reshape((N_DEV,) + tuple(s)))
        off += n
    return out


def _merge_shards(stacked, axis):
    t = jnp.moveaxis(stacked, 0, axis)
    s = t.shape
    return t.reshape(s[:axis] + (s[axis] * s[axis + 1],) + s[axis + 2:])


def _split_shards(full, axis):
    s = full.shape
    t = full.reshape(s[:axis] + (N_DEV, s[axis] // N_DEV) + s[axis + 1:])
    return jnp.moveaxis(t, axis, 0)


def _lower_bounds(hg_lower_bounds):
    p = jax.nn.softmax(hg_lower_bounds, axis=0)
    return jnp.cumsum(p, axis=0) - p[0:1]


def _prep_layer(p):
    def row(v):
        return v.reshape(1, -1)

    eye_b = jnp.eye(HEADS, dtype=F32)
    eye_g = jnp.eye(S5_GROUPS, dtype=F32)
    step = jnp.exp(p["s5_log_dt"])[:, None]
    lam_re, lam_im = p["s5_lam_re"], p["s5_lam_im"]
    mag = jnp.exp(lam_re * step)
    lb_re = mag * jnp.cos(lam_im * step)
    lb_im = mag * jnp.sin(lam_im * step)
    den = lam_re * lam_re + lam_im * lam_im
    f_re = ((lb_re - 1.0) * lam_re + lb_im * lam_im) / den
    f_im = (lb_im * lam_re - (lb_re - 1.0) * lam_im) / den
    bb_re = f_re[..., None] * p["s5_b_re"] - f_im[..., None] * p["s5_b_im"]
    bb_im = f_re[..., None] * p["s5_b_im"] + f_im[..., None] * p["s5_b_re"]
    width = S5_GROUPS * S5_GROUP
    n_state = S5_GROUPS * S5_STATE
    return dict(
        lb=row(p["lb"]), hg_norm=row(p["hg_norm"]), ret_norm=row(p["ret_norm"]),
        conv_w=p["lru_conv_w"], conv_b=row(p["lru_conv_b"]),
        wa=jnp.einsum("nij,nm->nimj", p["lru_wa"], eye_b).reshape(MIX_W, MIX_W), ba=row(p["lru_ba"]),
        wx=jnp.einsum("nij,nm->nimj", p["lru_wx"], eye_b).reshape(MIX_W, MIX_W), bx=row(p["lru_bx"]),
        sp=row(jax.nn.softplus(-p["lru_lambda"])),
        bt_re=jnp.einsum("gnp,gh->gphn", bb_re, eye_g).reshape(width, n_state),
        bt_im=jnp.einsum("gnp,gh->gphn", bb_im, eye_g).reshape(width, n_state),
        lb_re=row(lb_re), lb_im=row(lb_im),
        ct_re=jnp.einsum("gpn,gh->gnhp", p["s5_c_re"], eye_g).reshape(n_state, width),
        ct_im=jnp.einsum("gpn,gh->gnhp", p["s5_c_im"], eye_g).reshape(n_state, width),
        s5_d=row(p["s5_d"]), glu_w=p["s5_glu_w"], glu_b=row(p["s5_glu_b"]),
        b_gate=row(p["b_gate"]),
        norm_mix_pre=row(p["norm_mix_pre"]), norm_mix_post=row(p["norm_mix_post"]),
        norm_xa_pre=row(p["norm_xa_pre"]), norm_xa_post=row(p["norm_xa_post"]), norm_mem=row(p["norm_mem"]),
        norm_ffn_pre=row(p["norm_ffn_pre"]), norm_ffn_post=row(p["norm_ffn_post"]),
    )


_PREP_INPUTS = ("hg_norm", "ret_norm", "lru_conv_w", "lru_conv_b", "lru_wa", "lru_ba", "lru_wx", "lru_bx", "lru_lambda",
                "s5_lam_re", "s5_lam_im", "s5_b_re", "s5_b_im", "s5_c_re", "s5_c_im", "s5_d", "s5_log_dt", "s5_glu_w",
                "s5_glu_b", "b_gate", "norm_mix_pre", "norm_mix_post", "norm_xa_pre", "norm_xa_post", "norm_mem",
                "norm_ffn_pre", "norm_ffn_post")


def _retention_constants():
    lg = np.log1p(-np.power(2.0, -5.0 - np.arange(HEADS)))
    idx = np.arange(RET_CHUNK)

    def lanes(per_head_rows):
        return np.repeat(per_head_rows.T[:, :, None], HEAD_DIM, axis=2).reshape(RET_CHUNK, MIX_W)

    xi = lanes(np.exp((idx + 1.0)[None, :] * lg[:, None]))
    zeta = lanes(np.exp((RET_CHUNK - 1.0 - idx)[None, :] * lg[:, None]))
    rel = idx[:, None] - idx[None, :]
    decay = np.where(rel[None] >= 0, np.exp(np.maximum(rel, 0)[None] * lg[:, None, None]), 0.0)
    decay = np.transpose(decay, (1, 0, 2)).reshape(RET_CHUNK, HEADS * RET_CHUNK)
    g_end = np.repeat(np.exp(RET_CHUNK * lg), HEAD_DIM)[None, :]
    return tuple(jnp.asarray(a, F32) for a in (xi, zeta, decay, g_end))


def _rotary_tables(seq):
    pos = jnp.arange(seq, dtype=F32)
    inv_freq = 10000.0 ** (-jnp.arange(0, HEAD_DIM, 2, dtype=F32) / HEAD_DIM)
    ang = pos[:, None] * inv_freq[None, :]
    cos, sin = jnp.cos(ang), jnp.sin(ang)
    cos_t = jnp.tile(jnp.repeat(cos, 2, axis=1), (1, HEADS))
    sin_t = jnp.tile(jnp.stack([-sin, sin], axis=-1).reshape(seq, HEAD_DIM), (1, HEADS))
    return cos_t, sin_t


TB_HG = 128
TB_RET = 128
TB_LRU = 256
TB_S5 = 128
TB_ROW = 256

_STATE = (MIX_W, MIX_W)
_TAIL = (8, MIX_W)
_S5_TAIL = (8, S5_GROUPS * S5_STATE)


def _mixer_operands(l, z, kp, rot, ret_c):
    xi, zeta, decay, g_end = ret_c
    return dict(
        hg=dict(name=f"hgrn2_{l}", fn=fn_hgrn2, rows=[(z, 4 * MIX_W, 0)], params=[kp["lb"], kp["hg_norm"]],
                tb=TB_HG, states=(_STATE,), diff_params=(0, 1)),
        ret=dict(name=f"retention_{l}", fn=fn_retention, rows=[(z, 4 * MIX_W, 1), (rot[0], MIX_W, 0), (rot[1], MIX_W, 0)],
                 params=[kp["ret_norm"], xi, zeta, decay, g_end], tb=TB_RET, states=(_STATE,), diff_params=(0,)),
        lru=dict(name=f"rglru_{l}", fn=fn_rglru, rows=[(z, 2 * MIX_W, 4)],
                 params=[kp["conv_w"], kp["conv_b"], kp["wa"], kp["ba"], kp["wx"], kp["bx"], kp["sp"]],
                 tb=TB_LRU, states=(_TAIL, _TAIL), diff_params=(0, 1, 2, 3, 4, 5, 6)),
        s5=dict(name=f"s5_{l}", fn=fn_s5, rows=[(z, MIX_W, 10)],
                params=[kp["bt_re"], kp["bt_im"], kp["lb_re"], kp["lb_im"], kp["ct_re"], kp["ct_im"], kp["s5_d"],
                        kp["glu_w"], kp["glu_b"]],
                tb=TB_S5, states=(_S5_TAIL, _S5_TAIL), diff_params=tuple(range(9))),
    )


class _Hosted:
    def __init__(self, sides=None):
        self.sides = sides or {}
        self.got = {}

    def matmul(self, key, name, *args, **kw):
        side = self.sides.get(key)
        if side is None:
            return matmul(name, *args, **kw)
        out, self.got[key] = matmul(name, *args, side=side, **kw)
        return out


def _layer_forward(l, x, h, mem, bw, kp, rot, ret_c, g_next, target, hosted):
    d = x.shape[1]
    sv = dict(x=x, h=h)
    z = hosted.matmul("in_proj", f"in_proj_{l}", h, bw["w_in"], "nn", F32)
    gl = hosted.matmul("gate_proj", f"gate_proj_{l}", h, bw["w_gate"], "nn", F32)
    sv.update(z=z, gl=gl)
    ops = _mixer_operands(l, z, kp, rot, ret_c)
    ys = []
    for key in ("hg", "ret", "lru", "s5"):
        o = ops[key]
        (y,), _, saved = stage_fwd(o["name"] + "_fwd", o["fn"], o["rows"], o["params"], tb=o["tb"],
                                   outs=[(MIX_W, F32)], states=o["states"])
        ys.append(y)
        sv[key + "_states"] = saved
    sv["ys"] = ys
    merge_params = [bw["w_up"][n] for n in range(4)] + [kp["b_gate"], bw["w_out"], kp["norm_mix_post"]]
    merge_rows = [(y, MIX_W, 0) for y in ys] + [(gl, 4 * d, 0), (x, d, 0)]
    (x1,), _, _ = stage_fwd(f"merge_{l}_fwd", fn_merge, merge_rows, merge_params, tb=TB_ROW, outs=[(d, F32)])
    wk, wv = bw["xa_w_kv"][:, :d], bw["xa_w_kv"][:, d:]
    mem_params = [kp["norm_mem"], wk, wv]
    (k, v), _, _ = stage_fwd(f"mem_{l}_fwd", fn_mem, [(mem, d, 0)], mem_params, tb=mem.shape[0],
                             outs=[(d, F32), (d, F32)])
    xa_params = [kp["norm_xa_pre"], bw["xa_w_q"], k, v, bw["xa_w_o"], kp["norm_xa_post"], kp["norm_ffn_pre"]]
    (x2, h3), _, _ = stage_fwd(f"xattn_{l}_fwd", fn_xattn, [(x1, d, 0)], xa_params, tb=TB_ROW,
                               outs=[(d, F32), (d, BF16)])
    gu = hosted.matmul("ffn_gu", f"ffn_gu_{l}", h3, bw["ffn_w_gu"], "nn", F32)
    (a,), _, _ = stage_fwd(f"glu_{l}_fwd", fn_glu, [(gu, gu.shape[1], 0)], [], tb=TB_ROW, outs=[(gu.shape[1] // 2, BF16)])
    o3 = hosted.matmul("ffn_down", f"ffn_down_{l}", a, bw["ffn_w_down"], "nn", F32)
    sv.update(x1=x1, k=k, v=v, x2=x2, h3=h3, gu=gu, a=a, o3=o3, merge_params=merge_params, merge_rows=merge_rows,
              mem_params=mem_params, xa_params=xa_params)
    if g_next is not None:
        (x3, hn), _, _ = stage_fwd(f"res_{l}_fwd", fn_res_norm, [(x2, d, 0), (o3, d, 0)], [kp["norm_ffn_post"], g_next],
                                   tb=TB_ROW, outs=[(d, F32), (d, BF16)])
        return x3, hn, sv
    (dy,), (loss,), _ = stage_fwd(f"loss_{l}_fwd", fn_res_loss, [(x2, d, 0), (o3, d, 0), (target, d, 0)],
                                  [kp["norm_ffn_post"]], tb=TB_ROW, outs=[(d, F32)], accs=[(8, LANES)])
    return dy, loss[0, 0], sv


def _layer_backward(l, sv, mem, bw, kp, rot, ret_c, g_next, dx3, dhn, hosted):
    d = sv["x"].shape[1]
    gk, gb = {}, {}
    res_rows = [(sv["x2"], d, 0), (sv["o3"], d, 0)]
    if g_next is not None:
        (dx2, do3), (gk["norm_ffn_post"], gk["g_next"]) = stage_bwd(
            f"res_{l}_bwd", fn_res_norm, res_rows, [kp["norm_ffn_post"], g_next], [dx3, dhn], tb=TB_ROW,
            diff_rows=(0, 1), diff_params=(0, 1))
    else:
        (dx2, do3), (gk["norm_ffn_post"],) = stage_bwd(
            f"res_{l}_bwd", fn_res, res_rows, [kp["norm_ffn_post"]], [dx3], tb=TB_ROW, diff_rows=(0, 1), diff_params=(0,))
    da = hosted.matmul("ffn_down_dx", f"ffn_down_dx_{l}", do3, bw["ffn_w_down"], "nt", F32)
    gb["ffn_w_down"] = hosted.matmul("ffn_down_dw", f"ffn_down_dw_{l}", sv["a"], do3, "tn", BF16)
    (dgu,), _ = stage_bwd(f"glu_{l}_bwd", fn_glu, [(sv["gu"], sv["gu"].shape[1], 0)], [], [da], tb=TB_ROW,
                          diff_rows=(0,), row_dtypes=(BF16,))
    dh3 = hosted.matmul("ffn_gu_dx", f"ffn_gu_dx_{l}", dgu, bw["ffn_w_gu"], "nt", F32)
    gb["ffn_w_gu"] = hosted.matmul("ffn_gu_dw", f"ffn_gu_dw_{l}", sv["h3"], dgu, "tn", BF16)
    (dx1,), xa_g = stage_bwd(f"xattn_{l}_bwd", fn_xattn, [(sv["x1"], d, 0)], sv["xa_params"], [dx2, dh3], tb=TB_ROW,
                             diff_rows=(0,), diff_params=tuple(range(7)))
    gk["norm_xa_pre"], gb["xa_w_q"], dk, dv, gb["xa_w_o"], gk["norm_xa_post"], gk["norm_ffn_pre"] = xa_g
    _, (gk["norm_mem"], dwk, dwv) = stage_bwd(f"mem_{l}_bwd", fn_mem, [(mem, d, 0)], sv["mem_params"], [dk, dv],
                                              tb=mem.shape[0], diff_params=(0, 1, 2))
    gb["xa_w_kv"] = jnp.concatenate([dwk, dwv], axis=1)
    merge_d, merge_g = stage_bwd(f"merge_{l}_bwd", fn_merge, sv["merge_rows"], sv["merge_params"], [dx1], tb=TB_ROW,
                                 diff_rows=tuple(range(6)), diff_params=tuple(range(7)),
                                 row_dtypes=(F32, F32, F32, F32, BF16, F32))
    dys, dgl, dx0 = merge_d[:4], merge_d[4], merge_d[5]
    gb["w_up"] = jnp.stack(merge_g[:4])
    gk["b_gate"], gb["w_out"], gk["norm_mix_post"] = merge_g[4:]
    ops = _mixer_operands(l, sv["z"], kp, rot, ret_c)
    dz = {}
    for key, dy in zip(("hg", "ret", "lru", "s5"), dys):
        o = ops[key]
        (dz[key],), pg = stage_bwd(o["name"] + "_bwd", o["fn"], o["rows"], o["params"], [dy], tb=o["tb"],
                                   saved=sv[key + "_states"], diff_rows=(0,), diff_params=o["diff_params"],
                                   row_dtypes=(BF16,))
        gk[key] = pg
    dh = matmul(f"gate_proj_dx_{l}", dgl, bw["w_gate"], "nt", F32)
    gb["w_gate"] = matmul(f"gate_proj_dw_{l}", sv["h"], dgl, "tn", BF16)
    dw_in, col = [], 0
    for key in ("hg", "ret", "lru", "s5"):
        width = dz[key].shape[1]
        dh = matmul(f"in_proj_dx_{key}_{l}", dz[key], bw["w_in"][:, col:col + width], "nt", F32, add=dh)
        dw_in.append(matmul(f"in_proj_dw_{key}_{l}", sv["h"], dz[key], "tn", BF16))
        col += width
    gb["w_in"] = jnp.concatenate(dw_in, axis=1)
    return dx0, dh, gk, gb


def _kernel_grads_to_prep(gk):
    hg, ret, lru, s5 = gk["hg"], gk["ret"], gk["lru"], gk["s5"]
    return dict(
        lb=hg[0], hg_norm=hg[1], ret_norm=ret[0],
        conv_w=lru[0], conv_b=lru[1], wa=lru[2], ba=lru[3], wx=lru[4], bx=lru[5], sp=lru[6],
        bt_re=s5[0], bt_im=s5[1], lb_re=s5[2], lb_im=s5[3], ct_re=s5[4], ct_im=s5[5], s5_d=s5[6], glu_w=s5[7],
        glu_b=s5[8], b_gate=gk["b_gate"], norm_mix_pre=gk["norm_mix_pre"], norm_mix_post=gk["norm_mix_post"],
        norm_xa_pre=gk["norm_xa_pre"], norm_xa_post=gk["norm_xa_post"], norm_mem=gk["norm_mem"],
        norm_ffn_pre=gk["norm_ffn_pre"], norm_ffn_post=gk["norm_ffn_post"],
    )


def _step(inp):
    x, mem, target = inp["x"][0], inp["mem"][0], inp["loss_target"][0]
    seq = x.shape[0]
    depth = inp["w_in"].shape[0]
    me = 4 * lax.axis_index("x") + 2 * lax.axis_index("y") + lax.axis_index("c")

    small_shapes = [inp[n].shape for n in _SMALL_SHARDED_NAMES]
    (small_stacked,) = _exchange("gather_small", [_pack([inp[n] for n in _SMALL_SHARDED_NAMES], F32, 8)], True)
    small_all = _unpack_stacked(small_stacked, small_shapes)
    full_small = {n: _merge_shards(s, ax) for (n, ax), s in zip(SMALL_SHARDED, small_all)}

    lbs, lbs_vjp = jax.vjp(_lower_bounds, inp["hg_lower_bounds"])
    kps, prep_vjps = [], []
    for l in range(depth):
        p = {n: (full_small[n][l] if n in full_small else inp[n][l]) for n in _PREP_INPUTS}
        p["lb"] = lbs[l]
        kp, vj = jax.vjp(_prep_layer, p)
        kps.append(kp)
        prep_vjps.append(vj)
    rot = _rotary_tables(seq)
    ret_c = _retention_constants()

    (h,), _, _ = stage_fwd("norm_in_fwd", fn_norm, [(x, x.shape[1], 0)], [kps[0]["norm_mix_pre"]], tb=TB_ROW,
                           outs=[(x.shape[1], BF16)])
    def shards(l):
        return [inp[n][l].astype(BF16) for n in _BIG_NAMES]

    def hosted_for(groups, arrays, gather):
        if arrays is None:
            return _Hosted()
        return _Hosted({key: ([arrays[i] for i in idx], gather) for key, idx in groups.items()})

    def collect(groups, hosted):
        got = [None] * len(BIG)
        for key, idx in groups.items():
            for i, arr in zip(idx, hosted.got[key]):
                got[i] = arr
        return got

    saved, bws = [], []
    xs = x
    stacked = _exchange("gather_big_0", shards(0), True)
    for l in range(depth):
        bws.append({n: _merge_shards(s, ax - 1) for (n, ax), s in zip(BIG, stacked)})
        last = l + 1 == depth
        g_next = None if last else kps[l + 1]["norm_mix_pre"]
        hosted = hosted_for(GATHER_HOSTS, None if last else shards(l + 1), True)
        xs, h, sv = _layer_forward(l, xs, h, mem, bws[l], kps[l], rot, ret_c, g_next, target, hosted)
        saved.append(sv)
        if not last:
            stacked = collect(GATHER_HOSTS, hosted)
    dy, loss_local = xs, h

    big_parts = [None] * depth
    gks = [None] * depth
    dx, dh = dy, None
    pending = None
    for l in reversed(range(depth)):
        g_next = kps[l + 1]["norm_mix_pre"] if l + 1 < depth else None
        hosted = hosted_for(SCATTER_HOSTS, pending, False)
        dx, dh, gk, gb = _layer_backward(l, saved[l], mem, bws[l], kps[l], rot, ret_c, g_next, dx, dh, hosted)
        gks[l] = gk
        if pending is not None:
            big_parts[l + 1] = collect(SCATTER_HOSTS, hosted)
        pending = [_split_shards(gb[n].astype(BF16), ax - 1) for n, ax in BIG]
    big_parts[0] = _exchange("scatter_big_grads_0", pending, False)
    (grad_x,), (g_pre0,) = stage_bwd("norm_in_bwd", fn_keep_norm, [(x, x.shape[1], 0)], [kps[0]["norm_mix_pre"]],
                                     [dx, dh], tb=TB_ROW, diff_rows=(0,), diff_params=(0,))
    for l in range(depth):
        gks[l]["norm_mix_pre"] = g_pre0 if l == 0 else gks[l - 1]["g_next"]

    small_grads = {n: [None] * depth for n in _PREP_INPUTS}
    d_lbs = []
    for l in range(depth):
        (gp,) = prep_vjps[l](_kernel_grads_to_prep(gks[l]))
        d_lbs.append(gp["lb"])
        for n in _PREP_INPUTS:
            small_grads[n][l] = gp[n]
    small_local = {n: jnp.stack(v) for n, v in small_grads.items()}
    (small_local["hg_lower_bounds"],) = lbs_vjp(jnp.stack(d_lbs))
    small_names = REPLICATED + _SMALL_SHARDED_NAMES
    full_shapes = [small_local[n].shape for n in small_names]
    (small_parts,) = _exchange("gather_small_grads", [_pack([small_local[n] for n in small_names], F32, 64)], True)

    out = {}
    kinds = ("grad_", "delta_", "new_m_", "new_v_")
    for i, n in enumerate(_BIG_NAMES):
        shape = inp[n].shape
        three = (shape[0], int(np.prod(shape[1:-1])), shape[-1])
        res = adamw("adamw_" + n, *[inp[pre + n].reshape(three) for pre in ("", "m_", "v_")],
                    [big_parts[l][i].reshape((N_DEV,) + three[1:]) for l in range(depth)])
        for kind, a in zip(kinds, res):
            out[kind + n] = a.reshape(shape)
    zeros = [jnp.zeros(s, F32) for s in full_shapes[len(REPLICATED):]]
    res = adamw("adamw_small", *[_pack([inp[pre + n] for n in REPLICATED] + zeros, F32, 64)[None] for pre in ("", "m_", "v_")],
                [small_parts])
    res = [r[0] for r in res]
    summed = _unpack(res[0], full_shapes)
    for kind, flat in zip(("grad_", "delta_", "new_m_", "new_v_"), res):
        for n, a in zip(REPLICATED, _unpack(flat, full_shapes[:len(REPLICATED)])):
            out[kind + n] = a
    shard_g = []
    for (n, ax), g_full in zip(SMALL_SHARDED, summed[len(REPLICATED):]):
        width = inp[n].shape[ax]
        shard_g.append(lax.dynamic_slice_in_dim(g_full, me * width, width, axis=ax))
    res = adamw("adamw_small_sharded",
                *[_pack([inp[pre + n] for n in _SMALL_SHARDED_NAMES], F32, 8)[None] for pre in ("", "m_", "v_")],
                [_pack(shard_g, F32, 8)[None]])
    res = [r[0] for r in res]
    for kind, flat in zip(("grad_", "delta_", "new_m_", "new_v_"), res):
        for n, a in zip(_SMALL_SHARDED_NAMES, _unpack(flat, small_shapes)):
            out[kind + n] = a

    out["loss"] = lax.psum(loss_local, ("x", "y", "c"))
    out["grad_x"] = grad_x[None]
    return out


def kernel(x, mem, hg_lower_bounds, norm_mix_pre, norm_mix_post, w_in, w_gate, b_gate, hg_norm, ret_norm, lru_conv_w, lru_conv_b, lru_wa, lru_ba, lru_wx, lru_bx, lru_lambda, s5_lam_re, s5_lam_im, s5_b_re, s5_b_im, s5_c_re, s5_c_im, s5_d, s5_log_dt, s5_glu_w, s5_glu_b, w_up, w_out, norm_xa_pre, norm_xa_post, norm_mem, xa_w_q, xa_w_kv, xa_w_o, norm_ffn_pre, norm_ffn_post, ffn_w_gu, ffn_w_down, loss_target, m_hg_lower_bounds, m_norm_mix_pre, m_norm_mix_post, m_w_in, m_w_gate, m_b_gate, m_hg_norm, m_ret_norm, m_lru_conv_w, m_lru_conv_b, m_lru_wa, m_lru_ba, m_lru_wx, m_lru_bx, m_lru_lambda, m_s5_lam_re, m_s5_lam_im, m_s5_b_re, m_s5_b_im, m_s5_c_re, m_s5_c_im, m_s5_d, m_s5_log_dt, m_s5_glu_w, m_s5_glu_b, m_w_up, m_w_out, m_norm_xa_pre, m_norm_xa_post, m_norm_mem, m_xa_w_q, m_xa_w_kv, m_xa_w_o, m_norm_ffn_pre, m_norm_ffn_post, m_ffn_w_gu, m_ffn_w_down, v_hg_lower_bounds, v_norm_mix_pre, v_norm_mix_post, v_w_in, v_w_gate, v_b_gate, v_hg_norm, v_ret_norm, v_lru_conv_w, v_lru_conv_b, v_lru_wa, v_lru_ba, v_lru_wx, v_lru_bx, v_lru_lambda, v_s5_lam_re, v_s5_lam_im, v_s5_b_re, v_s5_b_im, v_s5_c_re, v_s5_c_im, v_s5_d, v_s5_log_dt, v_s5_glu_w, v_s5_glu_b, v_w_up, v_w_out, v_norm_xa_pre, v_norm_xa_post, v_norm_mem, v_xa_w_q, v_xa_w_kv, v_xa_w_o, v_norm_ffn_pre, v_norm_ffn_post, v_ffn_w_gu, v_ffn_w_down):
    values = (x, mem, hg_lower_bounds, norm_mix_pre, norm_mix_post, w_in, w_gate, b_gate, hg_norm, ret_norm, lru_conv_w, lru_conv_b, lru_wa, lru_ba, lru_wx, lru_bx, lru_lambda, s5_lam_re, s5_lam_im, s5_b_re, s5_b_im, s5_c_re, s5_c_im, s5_d, s5_log_dt, s5_glu_w, s5_glu_b, w_up, w_out, norm_xa_pre, norm_xa_post, norm_mem, xa_w_q, xa_w_kv, xa_w_o, norm_ffn_pre, norm_ffn_post, ffn_w_gu, ffn_w_down, loss_target, m_hg_lower_bounds, m_norm_mix_pre, m_norm_mix_post, m_w_in, m_w_gate, m_b_gate, m_hg_norm, m_ret_norm, m_lru_conv_w, m_lru_conv_b, m_lru_wa, m_lru_ba, m_lru_wx, m_lru_bx, m_lru_lambda, m_s5_lam_re, m_s5_lam_im, m_s5_b_re, m_s5_b_im, m_s5_c_re, m_s5_c_im, m_s5_d, m_s5_log_dt, m_s5_glu_w, m_s5_glu_b, m_w_up, m_w_out, m_norm_xa_pre, m_norm_xa_post, m_norm_mem, m_xa_w_q, m_xa_w_kv, m_xa_w_o, m_norm_ffn_pre, m_norm_ffn_post, m_ffn_w_gu, m_ffn_w_down, v_hg_lower_bounds, v_norm_mix_pre, v_norm_mix_post, v_w_in, v_w_gate, v_b_gate, v_hg_norm, v_ret_norm, v_lru_conv_w, v_lru_conv_b, v_lru_wa, v_lru_ba, v_lru_wx, v_lru_bx, v_lru_lambda, v_s5_lam_re, v_s5_lam_im, v_s5_b_re, v_s5_b_im, v_s5_c_re, v_s5_c_im, v_s5_d, v_s5_log_dt, v_s5_glu_w, v_s5_glu_b, v_w_up, v_w_out, v_norm_xa_pre, v_norm_xa_post, v_norm_mem, v_xa_w_q, v_xa_w_kv, v_xa_w_o, v_norm_ffn_pre, v_norm_ffn_post, v_ffn_w_gu, v_ffn_w_down)
    names = ("x", "mem") + WEIGHTS + ("loss_target",) + tuple("m_" + n for n in WEIGHTS) + tuple("v_" + n for n in WEIGHTS)
    out = _step(dict(zip(names, values)))
    order = ["loss", "grad_x"] + [k + n for k in ("grad_", "delta_", "new_m_", "new_v_") for n in WEIGHTS]
    return tuple(out[k] for k in order)
```

```python
import functools

import numpy as np
import jax
import jax.numpy as jnp
from jax import lax
from jax.experimental import pallas as pl
from jax.experimental.pallas import tpu as pltpu

F32 = jnp.float32
BF16 = jnp.bfloat16
EPS = 1e-6
N_DEV = 8
LANES = 128
VMEM_LIMIT = 60 * 1024 * 1024

HEADS = 4
HEAD_DIM = 64
MIX_W = HEADS * HEAD_DIM
HG_CHUNK = 32
RET_CHUNK = 128
S5_GROUPS = 16
S5_GROUP = 16
S5_STATE = 64
LRU_C = 8.0
XA_HEADS = 4

ADAM_LR = 0.001
ADAM_B1 = 0.9
ADAM_B2 = 0.999
ADAM_EPS = 1e-08
ADAM_WD = 0.01
ADAM_STEP = 10

BIG = (("w_in", 2), ("w_gate", 2), ("w_up", 3), ("w_out", 1), ("xa_w_q", 1), ("xa_w_kv", 2), ("xa_w_o", 1),
       ("ffn_w_gu", 2), ("ffn_w_down", 1))
SMALL_SHARDED = (("lru_conv_w", 2), ("s5_glu_w", 1))
WEIGHTS = ("hg_lower_bounds", "norm_mix_pre", "norm_mix_post", "w_in", "w_gate", "b_gate", "hg_norm", "ret_norm",
           "lru_conv_w", "lru_conv_b", "lru_wa", "lru_ba", "lru_wx", "lru_bx", "lru_lambda", "s5_lam_re", "s5_lam_im",
           "s5_b_re", "s5_b_im", "s5_c_re", "s5_c_im", "s5_d", "s5_log_dt", "s5_glu_w", "s5_glu_b", "w_up", "w_out",
           "norm_xa_pre", "norm_xa_post", "norm_mem", "xa_w_q", "xa_w_kv", "xa_w_o", "norm_ffn_pre", "norm_ffn_post",
           "ffn_w_gu", "ffn_w_down")
GATHER_IN = {"in_proj": (1,), "gate_proj": (2, 3, 4, 6), "hg": (5,), "s5": (7,), "merge": (8,)}
_BIG_NAMES = tuple(n for n, _ in BIG)
_SMALL_SHARDED_NAMES = tuple(n for n, _ in SMALL_SHARDED)
REPLICATED = tuple(n for n in WEIGHTS if n not in _BIG_NAMES and n not in _SMALL_SHARDED_NAMES)


def _dot(a, b):
    return jnp.dot(a.astype(BF16), b.astype(BF16), preferred_element_type=F32)


def _dot_nt(a, b):
    return lax.dot_general(a.astype(BF16), b.astype(BF16), (((1,), (1,)), ((), ())), preferred_element_type=F32)


def _dot_tn(a, b):
    return lax.dot_general(a.astype(BF16), b.astype(BF16), (((0,), (0,)), ((), ())), preferred_element_type=F32)


def _dot_exact(a, b):
    return jnp.dot(a, b, precision=lax.Precision.HIGHEST, preferred_element_type=F32)


def _rms(x, g):
    return x * lax.rsqrt(jnp.mean(x * x, axis=-1, keepdims=True) + EPS) * g


def _shift_down(x, d, fill):
    return jnp.concatenate([jnp.full((d, x.shape[1]), fill, x.dtype), x[:-d]], axis=0)


def _shift_up(x, d, fill):
    return jnp.concatenate([x[d:], jnp.full((d, x.shape[1]), fill, x.dtype)], axis=0)


def _cumsum_rows(x):
    d = 1
    while d < x.shape[0]:
        x = x + _shift_down(x, d, 0.0)
        d *= 2
    return x


def _lane_head(shape, dim):
    return lax.shift_right_logical(lax.broadcasted_iota(jnp.int32, shape, dim), 6)


def _head_masks(width=MIX_W):
    head = _lane_head((1, width), 1)
    return [(head == h).astype(F32) for h in range(HEADS)]


def _block_diag_mask():
    return (_lane_head((MIX_W, MIX_W), 0) == _lane_head((MIX_W, MIX_W), 1)).astype(F32)


def _head_rms(o, g):
    ms = _dot_exact(o * o, _block_diag_mask()) * (1.0 / HEAD_DIM)
    return o * lax.rsqrt(ms + EPS) * g


def _swap_pairs(x):
    lane = lax.broadcasted_iota(jnp.int32, x.shape, 1)
    return jnp.where((lane & 1) == 0, jnp.roll(x, -1, axis=1), jnp.roll(x, 1, axis=1))


def _stack_heads(t, masks):
    return jnp.concatenate([t * m for m in masks], axis=0)


@jax.custom_vjp
def _real_scan(a, u, h0):
    return _real_scan_fwd(a, u, h0)[0]


def _real_scan_fwd(a, u, h0):
    t = a.shape[0]
    acc_a, acc_u = a, u
    d = 1
    while d < t:
        acc_u = acc_u + acc_a * _shift_down(acc_u, d, 0.0)
        acc_a = acc_a * _shift_down(acc_a, d, 1.0)
        d *= 2
    h = acc_u + acc_a * h0
    return h, (a, h, h0)


def _real_scan_bwd(res, dh):
    a, h, h0 = res
    t = a.shape[0]
    acc_a = _shift_up(a, 1, 0.0)
    g = dh
    d = 1
    while d < t:
        g = g + acc_a * _shift_up(g, d, 0.0)
        acc_a = acc_a * _shift_up(acc_a, d, 1.0)
        d *= 2
    h_prev = jnp.concatenate([h0, h[:-1]], axis=0)
    return g * h_prev, g, (a * g)[0:1]


_real_scan.defvjp(_real_scan_fwd, _real_scan_bwd)


def _cmul(ar, ai, br, bi):
    return ar * br - ai * bi, ar * bi + ai * br


def _geometric_sums(ar, ai, ur, ui, forward):
    shift = _shift_down if forward else _shift_up
    t = ur.shape[0]
    pr, pi, sr, si = ar, ai, ur, ui
    d = 1
    while d < t:
        mr, mi = _cmul(pr, pi, shift(sr, d, 0.0), shift(si, d, 0.0))
        sr, si = sr + mr, si + mi
        pr, pi = _cmul(pr, pi, pr, pi)
        d *= 2
    return sr, si


@jax.custom_vjp
def _complex_scan(ar, ai, ur, ui, h0r, h0i):
    return _complex_scan_fwd(ar, ai, ur, ui, h0r, h0i)[0]


def _complex_scan_fwd(ar, ai, ur, ui, h0r, h0i):
    first = lax.broadcasted_iota(jnp.int32, ur.shape, 0) == 0
    cr, ci = _cmul(ar, ai, h0r, h0i)
    hr, hi = _geometric_sums(ar, ai, ur + jnp.where(first, cr, 0.0), ui + jnp.where(first, ci, 0.0), True)
    return (hr, hi), (ar, ai, hr, hi, h0r, h0i)


def _complex_scan_bwd(res, dh):
    ar, ai, hr, hi, h0r, h0i = res
    gr, gi = _geometric_sums(ar, -ai, dh[0], dh[1], False)
    qr = jnp.concatenate([h0r, hr[:-1]], axis=0)
    qi = jnp.concatenate([h0i, hi[:-1]], axis=0)
    dar = jnp.sum(gr * qr + gi * qi, axis=0, keepdims=True)
    dai = jnp.sum(gi * qr - gr * qi, axis=0, keepdims=True)
    d0r, d0i = _cmul(ar, -ai, gr[0:1], gi[0:1])
    return dar, dai, gr, gi, d0r, d0i


_complex_scan.defvjp(_complex_scan_fwd, _complex_scan_bwd)


def fn_norm(st, rows, params):
    (x,), (g,) = rows, params
    return (), (_rms(x, g),), ()


def fn_keep_norm(st, rows, params):
    (x,), (g,) = rows, params
    return (), (x, _rms(x, g)), ()


def fn_hgrn2(st, rows, params):
    (state,) = st
    (z,) = rows
    lb, norm_g = params
    q, f_logit, v_all, g = (z[:, k * MIX_W:(k + 1) * MIX_W] for k in range(4))
    f = lb + (1.0 - lb) * jax.nn.sigmoid(f_logit)
    log_f = jnp.log(f)
    k_all = 1.0 - f
    q_all = jax.nn.silu(q)
    masks = _head_masks()
    bd = _block_diag_mask()
    c = HG_CHUNK
    col = lax.broadcasted_iota(jnp.int32, (c, HEADS * c), 1) & (c - 1)
    causal = col <= lax.broadcasted_iota(jnp.int32, (c, HEADS * c), 0)
    outs = []
    for n in range(z.shape[0] // c):
        sl = slice(n * c, (n + 1) * c)
        lf = log_f[sl]
        b = _cumsum_rows(lf)
        b_end = jnp.sum(lf, axis=0, keepdims=True)
        q_dec = q_all[sl] * jnp.exp(b)
        k_inv = k_all[sl] * jnp.exp(-b)
        k_end = k_all[sl] * jnp.exp(b_end - b)
        v = v_all[sl]
        scores = jnp.where(causal, _dot_nt(q_dec, _stack_heads(k_inv, masks)), 0.0)
        outs.append(_dot(scores, _stack_heads(v, masks)) + _dot_nt(q_dec, state))
        state = state * jnp.exp(b_end) + _dot_tn(v, k_end) * bd
    o = jnp.concatenate(outs, axis=0) if len(outs) > 1 else outs[0]
    return (state,), (_head_rms(o, norm_g) * jax.nn.silu(g),), ()


def fn_retention(st, rows, params):
    (state,) = st
    z, cos_t, sin_t = rows
    norm_g, xi, zeta, decay, g_end = params
    q, k, v_all, g = (z[:, i * MIX_W:(i + 1) * MIX_W] for i in range(4))
    q_all = q * cos_t + _swap_pairs(q) * sin_t
    k_all = (k * cos_t + _swap_pairs(k) * sin_t) * (HEAD_DIM ** -0.5)
    masks = _head_masks()
    bd = _block_diag_mask()
    c = RET_CHUNK
    outs = []
    for n in range(z.shape[0] // c):
        sl = slice(n * c, (n + 1) * c)
        qc, kc, v = q_all[sl], k_all[sl], v_all[sl]
        scores = _dot_nt(qc, _stack_heads(kc, masks)) * decay
        outs.append(_dot(scores, _stack_heads(v, masks)) + _dot_nt(qc * xi, state))
        state = state * g_end + _dot_tn(v, kc * zeta) * bd
    o = jnp.concatenate(outs, axis=0) if len(outs) > 1 else outs[0]
    return (state,), (_head_rms(o, norm_g) * jax.nn.silu(g),), ()


def fn_rglru(st, rows, params):
    tail_x, tail_h = st
    (z,) = rows
    conv_w, conv_b, wa, ba, wx, bx, sp = params
    t = z.shape[0]
    xg, xi = z[:, :MIX_W], z[:, MIX_W:]
    full = jnp.concatenate([tail_x, xi], axis=0)
    xc = conv_b
    for k in range(4):
        xc = xc + conv_w[k:k + 1] * full[5 + k:5 + k + t]
    r = jax.nn.sigmoid(_dot(xc, wa) + ba)
    ig = jax.nn.sigmoid(_dot(xc, wx) + bx)
    log_a = -LRU_C * r * sp
    a = jnp.exp(log_a)
    one_minus_a2 = -jnp.tanh(log_a) * (a * a + 1.0)
    u = jnp.sqrt(one_minus_a2) * (ig * xc)
    h = _real_scan(a, u, tail_h[7:8])
    return (xi[t - 8:], h[t - 8:]), (h * jax.nn.gelu(xg),), ()


def fn_s5(st, rows, params):
    tail_r, tail_i = st
    (u,) = rows
    bt_re, bt_im, lb_re, lb_im, ct_re, ct_im, d, glu_w, glu_b = params
    t = u.shape[0]
    bu_re = _dot(u, bt_re)
    bu_im = _dot(u, bt_im)
    h_re, h_im = _complex_scan(lb_re, lb_im, bu_re, bu_im, tail_r[7:8], tail_i[7:8])
    y = _dot(h_re, ct_re) - _dot(h_im, ct_im) + d * u
    act = jax.nn.gelu(y)
    out = act * jax.nn.sigmoid(_dot(act, glu_w) + glu_b)
    return (h_re[t - 8:], h_im[t - 8:]), (out,), ()


def fn_merge(st, rows, params):
    ya, yb, yc, yd, gl, x = rows
    w0, w1, w2, w3, b_gate, w_out, g_post = params
    d = x.shape[1]
    mix = None
    for n, (y, w) in enumerate(((ya, w0), (yb, w1), (yc, w2), (yd, w3))):
        gate = jax.nn.sigmoid(gl[:, n * d:(n + 1) * d] + b_gate[:, n * d:(n + 1) * d])
        term = gate * _dot(y, w)
        mix = term if mix is None else mix + term
    return (), (x + _rms(_dot(mix, w_out), g_post),), ()


def fn_mem(st, rows, params):
    (mem,), (g, wk, wv) = rows, params
    m = _rms(mem, g)
    return (), (_dot(m, wk), _dot(m, wv)), ()


def fn_xattn(st, rows, params):
    (x,) = rows
    g_pre, wq, k, v, wo, g_post, g_next = params
    d = x.shape[1]
    dh = d // XA_HEADS
    q = _dot(_rms(x, g_pre), wq)
    heads = []
    for h in range(XA_HEADS):
        sl = slice(h * dh, (h + 1) * dh)
        s = _dot_nt(q[:, sl], k[:, sl]) * (dh ** -0.5)
        heads.append(_dot(jax.nn.softmax(s, axis=-1), v[:, sl]))
    x2 = x + _rms(_dot(jnp.concatenate(heads, axis=1), wo), g_post)
    return (), (x2, _rms(x2, g_next)), ()


def fn_glu(st, rows, params):
    (gu,) = rows
    f = gu.shape[1] // 2
    return (), (jax.nn.silu(gu[:, :f]) * gu[:, f:],), ()


def fn_res_norm(st, rows, params):
    (x, o), (g_post, g_next) = rows, params
    xn = x + _rms(o, g_post)
    return (), (xn, _rms(xn, g_next)), ()


def fn_res(st, rows, params):
    (x, o), (g_post,) = rows, params
    return (), (x + _rms(o, g_post),), ()


def fn_res_loss(st, rows, params):
    (x, o, target), (g_post,) = rows, params
    err = x + _rms(o, g_post) - target
    inv_d = 1.0 / x.shape[1]
    loss = 0.5 * inv_d * jnp.sum(err * err)
    return (), (err * inv_d,), (jnp.full((8, LANES), loss, F32),)


def _params():
    return pltpu.CompilerParams(dimension_semantics=("arbitrary",), vmem_limit_bytes=VMEM_LIMIT)


def _row_spec(tb, width, colblk, nb, reverse):
    if reverse:
        return pl.BlockSpec((tb, width), lambda i: (nb - 1 - i, colblk))
    return pl.BlockSpec((tb, width), lambda i: (i, colblk))


def _full_spec(shape):
    return pl.BlockSpec(shape, lambda i: (0,) * len(shape))


def _saved_spec(shape, nb, reverse):
    if reverse:
        return pl.BlockSpec((1,) + shape, lambda i: (nb - 1 - i, 0, 0))
    return pl.BlockSpec((1,) + shape, lambda i: (i, 0, 0))


def _side(side):
    arrays, gather = side if side is not None else ((), False)
    return list(arrays), gather, len(arrays)


_ANY = pl.BlockSpec(memory_space=pl.ANY)


def stage_fwd(name, fn, rows, params, *, tb, outs, states=(), accs=(), side=None):
    n_rows = rows[0][0].shape[0]
    nb = n_rows // tb
    nr, npar, no, na, ns = len(rows), len(params), len(outs), len(accs), len(states)
    side_arrays, gather, nx = _side(side)

    def body(*refs):
        row_refs, par_refs, side_in = refs[:nr], refs[nr:nr + npar], refs[nr + npar:nr + npar + nx]
        o = nr + npar + nx
        out_refs, acc_refs = refs[o:o + no], refs[o + no:o + no + na]
        saved_refs = refs[o + no + na:o + no + na + ns]
        o = o + no + na + ns
        side_out, st_refs, sems = refs[o:o + nx], refs[o + nx:o + nx + ns], refs[o + nx + ns:]
        i = pl.program_id(0)
        if nx:
            start, wait = _exchange_copies(side_in, side_out, *sems, gather)
            pl.when(i == 0)(start)

        @pl.when(i == 0)
        def _():
            for r in st_refs + acc_refs:
                r[...] = jnp.zeros_like(r)

        st = tuple(r[...] for r in st_refs)
        for sv, s in zip(saved_refs, st):
            sv[0] = s
        new_st, out_vals, acc_vals = fn(st, tuple(r[...].astype(F32) for r in row_refs),
                                        tuple(r[...].astype(F32) for r in par_refs))
        for r, v in zip(out_refs, out_vals):
            r[...] = v.astype(r.dtype)
        for r, v in zip(acc_refs, acc_vals):
            r[...] += v
        for r, v in zip(st_refs, new_st):
            r[...] = v
        if nx:
            pl.when(i == nb - 1)(wait)

    res = pl.pallas_call(
        body, name=name, grid=(nb,),
        in_specs=[_row_spec(tb, w, cb, nb, False) for _, w, cb in rows] + [_full_spec(p.shape) for p in params]
        + [_ANY] * nx,
        out_specs=[_row_spec(tb, w, 0, nb, False) for w, _ in outs] + [_full_spec(s) for s in accs]
        + [_saved_spec(s, nb, False) for s in states] + [_ANY] * nx,
        out_shape=[jax.ShapeDtypeStruct((n_rows, w), dt) for w, dt in outs]
        + [jax.ShapeDtypeStruct(s, F32) for s in accs] + [jax.ShapeDtypeStruct((nb,) + s, F32) for s in states]
        + _exchange_shapes(side_arrays, gather),
        scratch_shapes=[pltpu.VMEM(s, F32) for s in states] + (_exchange_sems(nx) if nx else []),
        compiler_params=_params(),
    )(*[a for a, _, _ in rows], *params, *side_arrays)
    base = (res[:no], res[no:no + na], res[no + na:no + na + ns])
    return base + (res[no + na + ns:],) if nx else base


def stage_bwd(name, fn, rows, params, cts, *, tb, saved=(), diff_rows=(), diff_params=(), row_dtypes=None, side=None):
    n_rows = rows[0][0].shape[0]
    nb = n_rows // tb
    nr, npar, ns, nc = len(rows), len(params), len(saved), len(cts)
    ndr, ndp = len(diff_rows), len(diff_params)
    row_dtypes = row_dtypes or (F32,) * ndr
    state_shapes = [s.shape[1:] for s in saved]
    side_arrays, gather, nx = _side(side)

    def body(*refs):
        row_refs, par_refs = refs[:nr], refs[nr:nr + npar]
        o = nr + npar
        saved_refs, ct_refs, side_in = refs[o:o + ns], refs[o + ns:o + ns + nc], refs[o + ns + nc:o + ns + nc + nx]
        o = o + ns + nc + nx
        drow_refs, dpar_refs, side_out = refs[o:o + ndr], refs[o + ndr:o + ndr + ndp], refs[o + ndr + ndp:o + ndr + ndp + nx]
        o = o + ndr + ndp + nx
        dst_refs, sems = refs[o:o + ns], refs[o + ns:]
        i = pl.program_id(0)
        if nx:
            start, wait = _exchange_copies(side_in, side_out, *sems, gather)
            pl.when(i == 0)(start)

        @pl.when(i == 0)
        def _():
            for r in dst_refs + dpar_refs:
                r[...] = jnp.zeros_like(r)

        st = tuple(r[0] for r in saved_refs)
        row_vals = [r[...].astype(F32) for r in row_refs]
        par_vals = [r[...].astype(F32) for r in par_refs]

        def f(st_, dr_, dp_):
            rv, pv = list(row_vals), list(par_vals)
            for k, v in zip(diff_rows, dr_):
                rv[k] = v
            for k, v in zip(diff_params, dp_):
                pv[k] = v
            new_st, out_vals, _ = fn(st_, tuple(rv), tuple(pv))
            return new_st, out_vals

        _, vjp = jax.vjp(f, st, tuple(row_vals[k] for k in diff_rows), tuple(par_vals[k] for k in diff_params))
        g_st, g_rows, g_par = vjp((tuple(r[...] for r in dst_refs), tuple(r[...].astype(F32) for r in ct_refs)))
        for r, v in zip(drow_refs, g_rows):
            r[...] = v.astype(r.dtype)
        for r, v in zip(dpar_refs, g_par):
            r[...] += v
        for r, v in zip(dst_refs, g_st):
            r[...] = v
        if nx:
            pl.when(i == nb - 1)(wait)

    res = pl.pallas_call(
        body, name=name, grid=(nb,),
        in_specs=[_row_spec(tb, w, cb, nb, True) for _, w, cb in rows] + [_full_spec(p.shape) for p in params]
        + [_saved_spec(s, nb, True) for s in state_shapes] + [_row_spec(tb, c.shape[1], 0, nb, True) for c in cts]
        + [_ANY] * nx,
        out_specs=[_row_spec(tb, rows[k][1], 0, nb, True) for k in diff_rows]
        + [_full_spec(params[k].shape) for k in diff_params] + [_ANY] * nx,
        out_shape=[jax.ShapeDtypeStruct((n_rows, rows[k][1]), dt) for k, dt in zip(diff_rows, row_dtypes)]
        + [jax.ShapeDtypeStruct(params[k].shape, F32) for k in diff_params] + _exchange_shapes(side_arrays, gather),
        scratch_shapes=[pltpu.VMEM(s, F32) for s in state_shapes] + (_exchange_sems(nx) if nx else []),
        compiler_params=_params(),
    )(*[a for a, _, _ in rows], *params, *saved, *cts, *side_arrays)
    base = (res[:ndr], res[ndr:ndr + ndp])
    return base + (res[ndr + ndp:],) if nx else base


def _pick(n, target):
    if n <= target:
        return n
    best = None
    for t in range(LANES, target + 1, LANES):
        if n % t == 0:
            best = t
    assert best is not None, n
    return best


def _mesh_position():
    return lax.axis_index("x"), lax.axis_index("y"), lax.axis_index("c")


def _peer(pos, k):
    x, y, c = pos
    px = 1 - x if k & 4 else x
    py = 1 - y if k & 2 else y
    pc = 1 - c if k & 1 else c
    return (px, py, pc), 4 * px + 2 * py + pc


def _exchange_copies(x_refs, o_refs, send_sems, recv_sems, local_sems, gather):
    pos = _mesh_position()
    me = 4 * pos[0] + 2 * pos[1] + pos[2]
    pairs = list(enumerate(zip(x_refs, o_refs)))

    def remote(k, a, src, dst):
        peer, _ = _peer(pos, k)
        return pltpu.make_async_remote_copy(src_ref=src, dst_ref=dst, send_sem=send_sems.at[k - 1, a],
                                            recv_sem=recv_sems.at[k - 1, a], device_id=peer,
                                            device_id_type=pl.DeviceIdType.MESH)

    def local(a, x, o):
        return pltpu.make_async_copy(x if gather else x.at[me], o.at[me], local_sems.at[a])

    def start():
        for a, (x, o) in pairs:
            local(a, x, o).start()
        for k in range(1, N_DEV):
            peer_idx = _peer(pos, k)[1]
            for a, (x, o) in pairs:
                remote(k, a, x if gather else x.at[peer_idx], o.at[me]).start()

    def wait():
        for k in range(1, N_DEV):
            peer_idx = _peer(pos, k)[1]
            for a, (x, o) in pairs:
                arrival = remote(k, a, x if gather else x.at[me], o.at[peer_idx])
                arrival.wait_recv()
                arrival.wait_send()
        for a, (x, o) in pairs:
            local(a, x, o).wait()

    return start, wait


def _exchange_shapes(arrays, gather):
    return [jax.ShapeDtypeStruct(((N_DEV,) + x.shape) if gather else x.shape, x.dtype) for x in arrays]


def _exchange_sems(n):
    return [pltpu.SemaphoreType.DMA((N_DEV - 1, n)), pltpu.SemaphoreType.DMA((N_DEV - 1, n)),
            pltpu.SemaphoreType.DMA((n,))]


def _exchange(name, arrays, gather):
    n = len(arrays)

    def body(*refs):
        start, wait = _exchange_copies(refs[:n], refs[n:2 * n], *refs[2 * n:], gather)
        start()
        wait()

    return pl.pallas_call(
        body, name=name,
        in_specs=[pl.BlockSpec(memory_space=pl.ANY)] * n, out_specs=[pl.BlockSpec(memory_space=pl.ANY)] * n,
        out_shape=_exchange_shapes(arrays, gather), scratch_shapes=_exchange_sems(n),
    )(*arrays)


def matmul(name, a, b, mode, out_dtype, add=None, side=None):
    if mode == "tn":
        k, m = a.shape
    else:
        m, k = a.shape
    n = b.shape[0] if mode == "nt" else b.shape[1]
    if mode == "tn":
        tm, tn, tk = _pick(m, 1024), _pick(n, 512), _pick(k, 1024)
    else:
        tm, tn, tk = _pick(m, 1024), _pick(n, 512), _pick(k, 2816)
    nk = k // tk
    grid = (m // tm, n // tn, nk)
    a_spec = pl.BlockSpec((tk, tm), lambda i, j, kk: (kk, i)) if mode == "tn" else pl.BlockSpec((tm, tk), lambda i, j, kk: (i, kk))
    b_spec = pl.BlockSpec((tn, tk), lambda i, j, kk: (j, kk)) if mode == "nt" else pl.BlockSpec((tk, tn), lambda i, j, kk: (kk, j))
    o_spec = pl.BlockSpec((tm, tn), lambda i, j, kk: (i, j))
    dims = {"nn": (((1,), (0,)), ((), ())), "nt": (((1,), (1,)), ((), ())), "tn": (((0,), (0,)), ((), ()))}[mode]
    has_add = add is not None
    side_arrays, gather = side if side is not None else ((), False)
    ns = len(side_arrays)
    n_in = 2 + has_add

    def body(*refs):
        a_ref, b_ref = refs[0], refs[1]
        side_in, o_ref, side_out = refs[n_in:n_in + ns], refs[n_in + ns], refs[n_in + ns + 1:n_in + 2 * ns + 1]
        scratch = refs[n_in + 2 * ns + 1:]
        ids = [pl.program_id(d) for d in range(3)]
        if ns:
            start, wait = _exchange_copies(side_in, side_out, *scratch[-3:], gather)
            pl.when((ids[0] == 0) & (ids[1] == 0) & (ids[2] == 0))(start)
        part = lax.dot_general(a_ref[...].astype(BF16), b_ref[...].astype(BF16), dims, preferred_element_type=F32)
        if nk == 1:
            o_ref[...] = (part + refs[2][...].astype(F32) if has_add else part).astype(o_ref.dtype)
        else:
            acc_ref = scratch[0]

            @pl.when(ids[2] == 0)
            def _():
                acc_ref[...] = part + refs[2][...].astype(F32) if has_add else part

            @pl.when(ids[2] > 0)
            def _():
                acc_ref[...] += part

            @pl.when(ids[2] == nk - 1)
            def _():
                o_ref[...] = acc_ref[...].astype(o_ref.dtype)
        if ns:
            pl.when((ids[0] == grid[0] - 1) & (ids[1] == grid[1] - 1) & (ids[2] == nk - 1))(wait)

    any_spec = pl.BlockSpec(memory_space=pl.ANY)
    res = pl.pallas_call(
        body, name=name, grid=grid,
        in_specs=[a_spec, b_spec] + ([o_spec] if has_add else []) + [any_spec] * ns,
        out_specs=[o_spec] + [any_spec] * ns,
        out_shape=[jax.ShapeDtypeStruct((m, n), out_dtype)] + _exchange_shapes(side_arrays, gather),
        scratch_shapes=([pltpu.VMEM((tm, tn), F32)] if nk > 1 else []) + (_exchange_sems(ns) if ns else []),
        compiler_params=pltpu.CompilerParams(
            dimension_semantics=("arbitrary",) * 3 if ns else ("parallel", "parallel", "arbitrary"),
            vmem_limit_bytes=VMEM_LIMIT),
    )(a, b, *([add] if has_add else []), *side_arrays)
    return (res[0], res[1:]) if ns else res[0]


def adamw(name, w, m, v, gparts):
    layers, rows, cols = w.shape
    parts = gparts[0].shape[0]
    tr = 8
    while tr * 2 * cols <= 65536 and rows % (tr * 2) == 0:
        tr *= 2
    nblk = rows // tr
    c1 = 1.0 - ADAM_B1 ** ADAM_STEP
    c2 = 1.0 - ADAM_B2 ** ADAM_STEP

    def body(*refs):
        w_ref, m_ref, v_ref = refs[:3]
        g_refs = refs[3:3 + layers]
        go_ref, d_ref, mo_ref, vo_ref = refs[3 + layers:]
        layer = pl.program_id(0)
        g = None
        for ll, g_ref in enumerate(g_refs):
            s = g_ref[0].astype(F32)
            for p in range(1, parts):
                s = s + g_ref[p].astype(F32)
            g = s if g is None else jnp.where(layer == ll, s, g)
        m_new = ADAM_B1 * m_ref[...] + (1.0 - ADAM_B1) * g
        v_new = ADAM_B2 * v_ref[...] + (1.0 - ADAM_B2) * (g * g)
        m_hat = m_new / c1
        v_hat = v_new / c2
        go_ref[...] = g
        d_ref[...] = -ADAM_LR * (m_hat / (jnp.sqrt(v_hat) + ADAM_EPS) + ADAM_WD * w_ref[...])
        mo_ref[...] = m_new
        vo_ref[...] = v_new

    def part_spec(ll):
        return pl.BlockSpec((parts, tr, cols),
                            lambda l, i: (0, jnp.where(l == ll, i, jnp.where(l < ll, 0, nblk - 1)), 0))

    spec = pl.BlockSpec((None, tr, cols), lambda l, i: (l, i, 0))
    return pl.pallas_call(
        body, name=name, grid=(layers, nblk),
        in_specs=[spec, spec, spec] + [part_spec(ll) for ll in range(layers)],
        out_specs=[spec] * 4, out_shape=[jax.ShapeDtypeStruct(w.shape, F32)] * 4,
        compiler_params=pltpu.CompilerParams(dimension_semantics=("arbitrary", "arbitrary"),
                                             vmem_limit_bytes=VMEM_LIMIT),
    )(w, m, v, *gparts)


def _pack(arrays, dtype, row_multiple):
    flat = jnp.concatenate([a.astype(dtype).reshape(-1) for a in arrays])
    unit = row_multiple * LANES
    pad = (-flat.shape[0]) % unit
    if pad:
        flat = jnp.concatenate([flat, jnp.zeros((pad,), dtype)])
    return flat.reshape(-1, LANES)


def _unpack(flat2d, shapes):
    flat = flat2d.reshape(-1)
    out, off = [], 0
    for s in shapes:
        n = int(np.prod(s))
        out.append(flat[off:off + n].reshape(s))
        off += n
    return out


def _unpack_stacked(stacked, shapes):
    flat = stacked.reshape(N_DEV, -1)
    out, off = [], 0
    for s in shapes:
        n = int(np.prod(s))
        out.append(flat[:, off:off + n].reshape((N_DEV,) + tuple(s)))
        off += n
    return out


def _merge_shards(stacked, axis):
    t = jnp.moveaxis(stacked, 0, axis)
    s = t.shape
    return t.reshape(s[:axis] + (s[axis] * s[axis + 1],) + s[axis + 2:])


def _split_shards(full, axis):
    s = full.shape
    t = full.reshape(s[:axis] + (N_DEV, s[axis] // N_DEV) + s[axis + 1:])
    return jnp.moveaxis(t, axis, 0)


def _lower_bounds(hg_lower_bounds):
    p = jax.nn.softmax(hg_lower_bounds, axis=0)
    return jnp.cumsum(p, axis=0) - p[0:1]


def _prep_layer(p):
    def row(v):
        return v.reshape(1, -1)

    eye_b = jnp.eye(HEADS, dtype=F32)
    eye_g = jnp.eye(S5_GROUPS, dtype=F32)
    step = jnp.exp(p["s5_log_dt"])[:, None]
    lam_re, lam_im = p["s5_lam_re"], p["s5_lam_im"]
    mag = jnp.exp(lam_re * step)
    lb_re = mag * jnp.cos(lam_im * step)
    lb_im = mag * jnp.sin(lam_im * step)
    den = lam_re * lam_re + lam_im * lam_im
    f_re = ((lb_re - 1.0) * lam_re + lb_im * lam_im) / den
    f_im = (lb_im * lam_re - (lb_re - 1.0) * lam_im) / den
    bb_re = f_re[..., None] * p["s5_b_re"] - f_im[..., None] * p["s5_b_im"]
    bb_im = f_re[..., None] * p["s5_b_im"] + f_im[..., None] * p["s5_b_re"]
    width = S5_GROUPS * S5_GROUP
    n_state = S5_GROUPS * S5_STATE
    return dict(
        lb=row(p["lb"]), hg_norm=row(p["hg_norm"]), ret_norm=row(p["ret_norm"]),
        conv_w=p["lru_conv_w"], conv_b=row(p["lru_conv_b"]),
        wa=jnp.einsum("nij,nm->nimj", p["lru_wa"], eye_b).reshape(MIX_W, MIX_W), ba=row(p["lru_ba"]),
        wx=jnp.einsum("nij,nm->nimj", p["lru_wx"], eye_b).reshape(MIX_W, MIX_W), bx=row(p["lru_bx"]),
        sp=row(jax.nn.softplus(-p["lru_lambda"])),
        bt_re=jnp.einsum("gnp,gh->gphn", bb_re, eye_g).reshape(width, n_state),
        bt_im=jnp.einsum("gnp,gh->gphn", bb_im, eye_g).reshape(width, n_state),
        lb_re=row(lb_re), lb_im=row(lb_im),
        ct_re=jnp.einsum("gpn,gh->gnhp", p["s5_c_re"], eye_g).reshape(n_state, width),
        ct_im=jnp.einsum("gpn,gh->gnhp", p["s5_c_im"], eye_g).reshape(n_state, width),
        s5_d=row(p["s5_d"]), glu_w=p["s5_glu_w"], glu_b=row(p["s5_glu_b"]),
        b_gate=row(p["b_gate"]),
        norm_mix_pre=row(p["norm_mix_pre"]), norm_mix_post=row(p["norm_mix_post"]),
        norm_xa_pre=row(p["norm_xa_pre"]), norm_xa_post=row(p["norm_xa_post"]), norm_mem=row(p["norm_mem"]),
        norm_ffn_pre=row(p["norm_ffn_pre"]), norm_ffn_post=row(p["norm_ffn_post"]),
    )


_PREP_INPUTS = ("hg_norm", "ret_norm", "lru_conv_w", "lru_conv_b", "lru_wa", "lru_ba", "lru_wx", "lru_bx", "lru_lambda",
                "s5_lam_re", "s5_lam_im", "s5_b_re", "s5_b_im", "s5_c_re", "s5_c_im", "s5_d", "s5_log_dt", "s5_glu_w",
                "s5_glu_b", "b_gate", "norm_mix_pre", "norm_mix_post", "norm_xa_pre", "norm_xa_post", "norm_mem",
                "norm_ffn_pre", "norm_ffn_post")


def _retention_constants():
    lg = np.log1p(-np.power(2.0, -5.0 - np.arange(HEADS)))
    idx = np.arange(RET_CHUNK)

    def lanes(per_head_rows):
        return np.repeat(per_head_rows.T[:, :, None], HEAD_DIM, axis=2).reshape(RET_CHUNK, MIX_W)

    xi = lanes(np.exp((idx + 1.0)[None, :] * lg[:, None]))
    zeta = lanes(np.exp((RET_CHUNK - 1.0 - idx)[None, :] * lg[:, None]))
    rel = idx[:, None] - idx[None, :]
    decay = np.where(rel[None] >= 0, np.exp(np.maximum(rel, 0)[None] * lg[:, None, None]), 0.0)
    decay = np.transpose(decay, (1, 0, 2)).reshape(RET_CHUNK, HEADS * RET_CHUNK)
    g_end = np.repeat(np.exp(RET_CHUNK * lg), HEAD_DIM)[None, :]
    return tuple(jnp.asarray(a, F32) for a in (xi, zeta, decay, g_end))


def _rotary_tables(seq):
    pos = jnp.arange(seq, dtype=F32)
    inv_freq = 10000.0 ** (-jnp.arange(0, HEAD_DIM, 2, dtype=F32) / HEAD_DIM)
    ang = pos[:, None] * inv_freq[None, :]
    cos, sin = jnp.cos(ang), jnp.sin(ang)
    cos_t = jnp.tile(jnp.repeat(cos, 2, axis=1), (1, HEADS))
    sin_t = jnp.tile(jnp.stack([-sin, sin], axis=-1).reshape(seq, HEAD_DIM), (1, HEADS))
    return cos_t, sin_t


TB_HG = 128
TB_RET = 128
TB_LRU = 256
TB_S5 = 128
TB_ROW = 256

_STATE = (MIX_W, MIX_W)
_TAIL = (8, MIX_W)
_S5_TAIL = (8, S5_GROUPS * S5_STATE)


def _mixer_operands(l, z, kp, rot, ret_c):
    xi, zeta, decay, g_end = ret_c
    return dict(
        hg=dict(name=f"hgrn2_{l}", fn=fn_hgrn2, rows=[(z, 4 * MIX_W, 0)], params=[kp["lb"], kp["hg_norm"]],
                tb=TB_HG, states=(_STATE,), diff_params=(0, 1)),
        ret=dict(name=f"retention_{l}", fn=fn_retention, rows=[(z, 4 * MIX_W, 1), (rot[0], MIX_W, 0), (rot[1], MIX_W, 0)],
                 params=[kp["ret_norm"], xi, zeta, decay, g_end], tb=TB_RET, states=(_STATE,), diff_params=(0,)),
        lru=dict(name=f"rglru_{l}", fn=fn_rglru, rows=[(z, 2 * MIX_W, 4)],
                 params=[kp["conv_w"], kp["conv_b"], kp["wa"], kp["ba"], kp["wx"], kp["bx"], kp["sp"]],
                 tb=TB_LRU, states=(_TAIL, _TAIL), diff_params=(0, 1, 2, 3, 4, 5, 6)),
        s5=dict(name=f"s5_{l}", fn=fn_s5, rows=[(z, MIX_W, 10)],
                params=[kp["bt_re"], kp["bt_im"], kp["lb_re"], kp["lb_im"], kp["ct_re"], kp["ct_im"], kp["s5_d"],
                        kp["glu_w"], kp["glu_b"]],
                tb=TB_S5, states=(_S5_TAIL, _S5_TAIL), diff_params=tuple(range(9))),
    )


def _layer_forward(l, x, h, mem, w_in, shards, next_w_in_shard, kp, rot, ret_c, g_next, target):
    d = x.shape[1]
    sv = dict(x=x, h=h)
    bw = {"w_in": w_in}

    def gather(idx):
        return [shards[i] for i in idx], True

    def take(idx, stacked):
        for i, s in zip(idx, stacked):
            bw[_BIG_NAMES[i]] = _merge_shards(s, BIG[i][1] - 1)

    z, got = matmul(f"in_proj_{l}", h, w_in, "nn", F32, side=gather(GATHER_IN["in_proj"]))
    take(GATHER_IN["in_proj"], got)
    gl, got = matmul(f"gate_proj_{l}", h, bw["w_gate"], "nn", F32, side=gather(GATHER_IN["gate_proj"]))
    take(GATHER_IN["gate_proj"], got)
    sv.update(z=z, gl=gl)
    ops = _mixer_operands(l, z, kp, rot, ret_c)
    ys = []
    for key in ("hg", "ret", "lru", "s5"):
        o = ops[key]
        idx = GATHER_IN.get(key)
        res = stage_fwd(o["name"] + "_fwd", o["fn"], o["rows"], o["params"], tb=o["tb"], outs=[(MIX_W, F32)],
                        states=o["states"], side=gather(idx) if idx else None)
        if idx:
            take(idx, res[3])
        ys.append(res[0][0])
        sv[key + "_states"] = res[2]
    sv["ys"] = ys
    merge_params = [bw["w_up"][n] for n in range(4)] + [kp["b_gate"], bw["w_out"], kp["norm_mix_post"]]
    merge_rows = [(y, MIX_W, 0) for y in ys] + [(gl, 4 * d, 0), (x, d, 0)]
    (x1,), _, _, got = stage_fwd(f"merge_{l}_fwd", fn_merge, merge_rows, merge_params, tb=TB_ROW, outs=[(d, F32)],
                                 side=gather(GATHER_IN["merge"]))
    take(GATHER_IN["merge"], got)
    wk, wv = bw["xa_w_kv"][:, :d], bw["xa_w_kv"][:, d:]
    mem_params = [kp["norm_mem"], wk, wv]
    (k, v), _, _ = stage_fwd(f"mem_{l}_fwd", fn_mem, [(mem, d, 0)], mem_params, tb=mem.shape[0],
                             outs=[(d, F32), (d, F32)])
    xa_params = [kp["norm_xa_pre"], bw["xa_w_q"], k, v, bw["xa_w_o"], kp["norm_xa_post"], kp["norm_ffn_pre"]]
    res = stage_fwd(f"xattn_{l}_fwd", fn_xattn, [(x1, d, 0)], xa_params, tb=TB_ROW, outs=[(d, F32), (d, BF16)],
                    side=([next_w_in_shard], True) if next_w_in_shard is not None else None)
    x2, h3 = res[0]
    next_w_in = _merge_shards(res[3][0], BIG[0][1] - 1) if next_w_in_shard is not None else None
    gu = matmul(f"ffn_gu_{l}", h3, bw["ffn_w_gu"], "nn", F32)
    (a,), _, _ = stage_fwd(f"glu_{l}_fwd", fn_glu, [(gu, gu.shape[1], 0)], [], tb=TB_ROW, outs=[(gu.shape[1] // 2, BF16)])
    o3 = matmul(f"ffn_down_{l}", a, bw["ffn_w_down"], "nn", F32)
    sv.update(x1=x1, k=k, v=v, x2=x2, h3=h3, gu=gu, a=a, o3=o3, merge_params=merge_params, merge_rows=merge_rows,
              mem_params=mem_params, xa_params=xa_params)
    if g_next is not None:
        (x3, hn), _, _ = stage_fwd(f"res_{l}_fwd", fn_res_norm, [(x2, d, 0), (o3, d, 0)], [kp["norm_ffn_post"], g_next],
                                   tb=TB_ROW, outs=[(d, F32), (d, BF16)])
        return x3, hn, sv, bw, next_w_in
    (dy,), (loss,), _ = stage_fwd(f"loss_{l}_fwd", fn_res_loss, [(x2, d, 0), (o3, d, 0), (target, d, 0)],
                                  [kp["norm_ffn_post"]], tb=TB_ROW, outs=[(d, F32)], accs=[(8, LANES)])
    return dy, loss[0, 0], sv, bw, next_w_in


def _layer_backward(l, sv, mem, bw, kp, rot, ret_c, g_next, dx3, dhn):
    d = sv["x"].shape[1]
    gk = {}
    parts = [None] * len(BIG)

    def scatter(*items):
        return [_split_shards(g.astype(BF16), BIG[i][1] - 1) for i, g in items], False

    def took(got, *idx):
        for i, p in zip(idx, got):
            parts[i] = p

    res_rows = [(sv["x2"], d, 0), (sv["o3"], d, 0)]
    if g_next is not None:
        (dx2, do3), (gk["norm_ffn_post"], gk["g_next"]) = stage_bwd(
            f"res_{l}_bwd", fn_res_norm, res_rows, [kp["norm_ffn_post"], g_next], [dx3, dhn], tb=TB_ROW,
            diff_rows=(0, 1), diff_params=(0, 1))
    else:
        (dx2, do3), (gk["norm_ffn_post"],) = stage_bwd(
            f"res_{l}_bwd", fn_res, res_rows, [kp["norm_ffn_post"]], [dx3], tb=TB_ROW, diff_rows=(0, 1), diff_params=(0,))
    da = matmul(f"ffn_down_dx_{l}", do3, bw["ffn_w_down"], "nt", F32)
    g_down = matmul(f"ffn_down_dw_{l}", sv["a"], do3, "tn", BF16)
    (dgu,), _, got = stage_bwd(f"glu_{l}_bwd", fn_glu, [(sv["gu"], sv["gu"].shape[1], 0)], [], [da], tb=TB_ROW,
                               diff_rows=(0,), row_dtypes=(BF16,), side=scatter((8, g_down)))
    took(got, 8)
    dh3 = matmul(f"ffn_gu_dx_{l}", dgu, bw["ffn_w_gu"], "nt", F32)
    g_gu = matmul(f"ffn_gu_dw_{l}", sv["h3"], dgu, "tn", BF16)
    (dx1,), xa_g, got = stage_bwd(f"xattn_{l}_bwd", fn_xattn, [(sv["x1"], d, 0)], sv["xa_params"], [dx2, dh3],
                                  tb=TB_ROW, diff_rows=(0,), diff_params=tuple(range(7)), side=scatter((7, g_gu)))
    took(got, 7)
    gk["norm_xa_pre"], g_q, dk, dv, g_o, gk["norm_xa_post"], gk["norm_ffn_pre"] = xa_g
    _, (gk["norm_mem"], dwk, dwv) = stage_bwd(f"mem_{l}_bwd", fn_mem, [(mem, d, 0)], sv["mem_params"], [dk, dv],
                                              tb=mem.shape[0], diff_params=(0, 1, 2))
    g_kv = jnp.concatenate([dwk, dwv], axis=1)
    merge_d, merge_g, got = stage_bwd(f"merge_{l}_bwd", fn_merge, sv["merge_rows"], sv["merge_params"], [dx1],
                                      tb=TB_ROW, diff_rows=tuple(range(6)), diff_params=tuple(range(7)),
                                      row_dtypes=(F32, F32, F32, F32, BF16, F32),
                                      side=scatter((4, g_q), (6, g_o), (5, g_kv)))
    took(got, 4, 6, 5)
    dys, dgl, dx0 = merge_d[:4], merge_d[4], merge_d[5]
    g_up = jnp.stack(merge_g[:4])
    gk["b_gate"], g_out, gk["norm_mix_post"] = merge_g[4:]
    ops = _mixer_operands(l, sv["z"], kp, rot, ret_c)
    dz = {}
    for key, dy in zip(("hg", "ret", "lru", "s5"), dys):
        o = ops[key]
        res = stage_bwd(o["name"] + "_bwd", o["fn"], o["rows"], o["params"], [dy], tb=o["tb"],
                        saved=sv[key + "_states"], diff_rows=(0,), diff_params=o["diff_params"], row_dtypes=(BF16,),
                        side=scatter((2, g_up), (3, g_out)) if key == "hg" else None)
        if key == "hg":
            took(res[2], 2, 3)
        dz[key] = res[0][0]
        gk[key] = res[1]
    g_gate = matmul(f"gate_proj_dw_{l}", sv["h"], dgl, "tn", BF16)
    keys = ("hg", "ret", "lru", "s5")
    g_in = jnp.concatenate([matmul(f"in_proj_dw_{key}_{l}", sv["h"], dz[key], "tn", BF16) for key in keys], axis=1)
    dh, got = matmul(f"gate_proj_dx_{l}", dgl, bw["w_gate"], "nt", F32, side=scatter((1, g_gate)))
    took(got, 1)
    col = 0
    for key in keys:
        width = dz[key].shape[1]
        w_cols = bw["w_in"][:, col:col + width]
        if key == "hg":
            dh, got = matmul(f"in_proj_dx_{key}_{l}", dz[key], w_cols, "nt", F32, add=dh, side=scatter((0, g_in)))
            took(got, 0)
        else:
            dh = matmul(f"in_proj_dx_{key}_{l}", dz[key], w_cols, "nt", F32, add=dh)
        col += width
    return dx0, dh, gk, parts


def _kernel_grads_to_prep(gk):
    hg, ret, lru, s5 = gk["hg"], gk["ret"], gk["lru"], gk["s5"]
    return dict(
        lb=hg[0], hg_norm=hg[1], ret_norm=ret[0],
        conv_w=lru[0], conv_b=lru[1], wa=lru[2], ba=lru[3], wx=lru[4], bx=lru[5], sp=lru[6],
        bt_re=s5[0], bt_im=s5[1], lb_re=s5[2], lb_im=s5[3], ct_re=s5[4], ct_im=s5[5], s5_d=s5[6], glu_w=s5[7],
        glu_b=s5[8], b_gate=gk["b_gate"], norm_mix_pre=gk["norm_mix_pre"], norm_mix_post=gk["norm_mix_post"],
        norm_xa_pre=gk["norm_xa_pre"], norm_xa_post=gk["norm_xa_post"], norm_mem=gk["norm_mem"],
        norm_ffn_pre=gk["norm_ffn_pre"], norm_ffn_post=gk["norm_ffn_post"],
    )


def _step(inp):
    x, mem, target = inp["x"][0], inp["mem"][0], inp["loss_target"][0]
    seq = x.shape[0]
    depth = inp["w_in"].shape[0]
    me = 4 * lax.axis_index("x") + 2 * lax.axis_index("y") + lax.axis_index("c")

    small_shapes = [inp[n].shape for n in _SMALL_SHARDED_NAMES]
    (small_stacked,) = _exchange("gather_small", [_pack([inp[n] for n in _SMALL_SHARDED_NAMES], F32, 8)], True)
    small_all = _unpack_stacked(small_stacked, small_shapes)
    full_small = {n: _merge_shards(s, ax) for (n, ax), s in zip(SMALL_SHARDED, small_all)}

    lbs, lbs_vjp = jax.vjp(_lower_bounds, inp["hg_lower_bounds"])
    kps, prep_vjps = [], []
    for l in range(depth):
        p = {n: (full_small[n][l] if n in full_small else inp[n][l]) for n in _PREP_INPUTS}
        p["lb"] = lbs[l]
        kp, vj = jax.vjp(_prep_layer, p)
        kps.append(kp)
        prep_vjps.append(vj)
    rot = _rotary_tables(seq)
    ret_c = _retention_constants()

    (h,), _, _ = stage_fwd("norm_in_fwd", fn_norm, [(x, x.shape[1], 0)], [kps[0]["norm_mix_pre"]], tb=TB_ROW,
                           outs=[(x.shape[1], BF16)])
    def shards(l):
        return [inp[n][l].astype(BF16) for n in _BIG_NAMES]

    saved, bws = [], []
    xs = x
    (stacked,) = _exchange("gather_w_in_0", [shards(0)[0]], True)
    w_in = _merge_shards(stacked, BIG[0][1] - 1)
    for l in range(depth):
        last = l + 1 == depth
        g_next = None if last else kps[l + 1]["norm_mix_pre"]
        xs, h, sv, bw, w_in = _layer_forward(l, xs, h, mem, w_in, shards(l), None if last else shards(l + 1)[0],
                                             kps[l], rot, ret_c, g_next, target)
        saved.append(sv)
        bws.append(bw)
    dy, loss_local = xs, h

    big_parts = [None] * depth
    gks = [None] * depth
    dx, dh = dy, None
    for l in reversed(range(depth)):
        g_next = kps[l + 1]["norm_mix_pre"] if l + 1 < depth else None
        dx, dh, gks[l], big_parts[l] = _layer_backward(l, saved[l], mem, bws[l], kps[l], rot, ret_c, g_next, dx, dh)
    (grad_x,), (g_pre0,) = stage_bwd("norm_in_bwd", fn_keep_norm, [(x, x.shape[1], 0)], [kps[0]["norm_mix_pre"]],
                                     [dx, dh], tb=TB_ROW, diff_rows=(0,), diff_params=(0,))
    for l in range(depth):
        gks[l]["norm_mix_pre"] = g_pre0 if l == 0 else gks[l - 1]["g_next"]

    small_grads = {n: [None] * depth for n in _PREP_INPUTS}
    d_lbs = []
    for l in range(depth):
        (gp,) = prep_vjps[l](_kernel_grads_to_prep(gks[l]))
        d_lbs.append(gp["lb"])
        for n in _PREP_INPUTS:
            small_grads[n][l] = gp[n]
    small_local = {n: jnp.stack(v) for n, v in small_grads.items()}
    (small_local["hg_lower_bounds"],) = lbs_vjp(jnp.stack(d_lbs))
    small_names = REPLICATED + _SMALL_SHARDED_NAMES
    full_shapes = [small_local[n].shape for n in small_names]
    (small_parts,) = _exchange("gather_small_grads", [_pack([small_local[n] for n in small_names], F32, 64)], True)

    out = {}
    kinds = ("grad_", "delta_", "new_m_", "new_v_")
    for i, n in enumerate(_BIG_NAMES):
        shape = inp[n].shape
        three = (shape[0], int(np.prod(shape[1:-1])), shape[-1])
        res = adamw("adamw_" + n, *[inp[pre + n].reshape(three) for pre in ("", "m_", "v_")],
                    [big_parts[l][i].reshape((N_DEV,) + three[1:]) for l in range(depth)])
        for kind, a in zip(kinds, res):
            out[kind + n] = a.reshape(shape)
    zeros = [jnp.zeros(s, F32) for s in full_shapes[len(REPLICATED):]]
    res = adamw("adamw_small", *[_pack([inp[pre + n] for n in REPLICATED] + zeros, F32, 64)[None] for pre in ("", "m_", "v_")],
                [small_parts])
    res = [r[0] for r in res]
    summed = _unpack(res[0], full_shapes)
    for kind, flat in zip(("grad_", "delta_", "new_m_", "new_v_"), res):
        for n, a in zip(REPLICATED, _unpack(flat, full_shapes[:len(REPLICATED)])):
            out[kind + n] = a
    shard_g = []
    for (n, ax), g_full in zip(SMALL_SHARDED, summed[len(REPLICATED):]):
        width = inp[n].shape[ax]
        shard_g.append(lax.dynamic_slice_in_dim(g_full, me * width, width, axis=ax))
    res = adamw("adamw_small_sharded",
                *[_pack([inp[pre + n] for n in _SMALL_SHARDED_NAMES], F32, 8)[None] for pre in ("", "m_", "v_")],
                [_pack(shard_g, F32, 8)[None]])
    res = [r[0] for r in res]
    for kind, flat in zip(("grad_", "delta_", "new_m_", "new_v_"), res):
        for n, a in zip(_SMALL_SHARDED_NAMES, _unpack(flat, small_shapes)):
            out[kind + n] = a

    out["loss"] = lax.psum(loss_local, ("x", "y", "c"))
    out["grad_x"] = grad_x[None]
    return out


def kernel(x, mem, hg_lower_bounds, norm_mix_pre, norm_mix_post, w_in, w_gate, b_gate, hg_norm, ret_norm, lru_conv_w, lru_conv_b, lru_wa, lru_ba, lru_wx, lru_bx, lru_lambda, s5_lam_re, s5_lam_im, s5_b_re, s5_b_im, s5_c_re, s5_c_im, s5_d, s5_log_dt, s5_glu_w, s5_glu_b, w_up, w_out, norm_xa_pre, norm_xa_post, norm_mem, xa_w_q, xa_w_kv, xa_w_o, norm_ffn_pre, norm_ffn_post, ffn_w_gu, ffn_w_down, loss_target, m_hg_lower_bounds, m_norm_mix_pre, m_norm_mix_post, m_w_in, m_w_gate, m_b_gate, m_hg_norm, m_ret_norm, m_lru_conv_w, m_lru_conv_b, m_lru_wa, m_lru_ba, m_lru_wx, m_lru_bx, m_lru_lambda, m_s5_lam_re, m_s5_lam_im, m_s5_b_re, m_s5_b_im, m_s5_c_re, m_s5_c_im, m_s5_d, m_s5_log_dt, m_s5_glu_w, m_s5_glu_b, m_w_up, m_w_out, m_norm_xa_pre, m_norm_xa_post, m_norm_mem, m_xa_w_q, m_xa_w_kv, m_xa_w_o, m_norm_ffn_pre, m_norm_ffn_post, m_ffn_w_gu, m_ffn_w_down, v_hg_lower_bounds, v_norm_mix_pre, v_norm_mix_post, v_w_in, v_w_gate, v_b_gate, v_hg_norm, v_ret_norm, v_lru_conv_w, v_lru_conv_b, v_lru_wa, v_lru_ba, v_lru_wx, v_lru_bx, v_lru_lambda, v_s5_lam_re, v_s5_lam_im, v_s5_b_re, v_s5_b_im, v_s5_c_re, v_s5_c_im, v_s5_d, v_s5_log_dt, v_s5_glu_w, v_s5_glu_b, v_w_up, v_w_out, v_norm_xa_pre, v_norm_xa_post, v_norm_mem, v_xa_w_q, v_xa_w_kv, v_xa_w_o, v_norm_ffn_pre, v_norm_ffn_post, v_ffn_w_gu, v_ffn_w_down):
    values = (x, mem, hg_lower_bounds, norm_mix_pre, norm_mix_post, w_in, w_gate, b_gate, hg_norm, ret_norm, lru_conv_w, lru_conv_b, lru_wa, lru_ba, lru_wx, lru_bx, lru_lambda, s5_lam_re, s5_lam_im, s5_b_re, s5_b_im, s5_c_re, s5_c_im, s5_d, s5_log_dt, s5_glu_w, s5_glu_b, w_up, w_out, norm_xa_pre, norm_xa_post, norm_mem, xa_w_q, xa_w_kv, xa_w_o, norm_ffn_pre, norm_ffn_post, ffn_w_gu, ffn_w_down, loss_target, m_hg_lower_bounds, m_norm_mix_pre, m_norm_mix_post, m_w_in, m_w_gate, m_b_gate, m_hg_norm, m_ret_norm, m_lru_conv_w, m_lru_conv_b, m_lru_wa, m_lru_ba, m_lru_wx, m_lru_bx, m_lru_lambda, m_s5_lam_re, m_s5_lam_im, m_s5_b_re, m_s5_b_im, m_s5_c_re, m_s5_c_im, m_s5_d, m_s5_log_dt, m_s5_glu_w, m_s5_glu_b, m_w_up, m_w_out, m_norm_xa_pre, m_norm_xa_post, m_norm_mem, m_xa_w_q, m_xa_w_kv, m_xa_w_o, m_norm_ffn_pre, m_norm_ffn_post, m_ffn_w_gu, m_ffn_w_down, v_hg_lower_bounds, v_norm_mix_pre, v_norm_mix_post, v_w_in, v_w_gate, v_b_gate, v_hg_norm, v_ret_norm, v_lru_conv_w, v_lru_conv_b, v_lru_wa, v_lru_ba, v_lru_wx, v_lru_bx, v_lru_lambda, v_s5_lam_re, v_s5_lam_im, v_s5_b_re, v_s5_b_im, v_s5_c_re, v_s5_c_im, v_s5_d, v_s5_log_dt, v_s5_glu_w, v_s5_glu_b, v_w_up, v_w_out, v_norm_xa_pre, v_norm_xa_post, v_norm_mem, v_xa_w_q, v_xa_w_kv, v_xa_w_o, v_norm_ffn_pre, v_norm_ffn_post, v_ffn_w_gu, v_ffn_w_down)
    names = ("x", "mem") + WEIGHTS + ("loss_target",) + tuple("m_" + n for n in WEIGHTS) + tuple("v_" + n for n in WEIGHTS)
    out = _step(dict(zip(names, values)))
    order = ["loss", "grad_x"] + [k + n for k in ("grad_", "delta_", "new_m_", "new_v_") for n in WEIGHTS]
    return tuple(out[k] for k in order)
```

```python
import functools

import numpy as np
import jax
import jax.numpy as jnp
from jax import lax
from jax.experimental import pallas as pl
from jax.experimental.pallas import tpu as pltpu

F32 = jnp.float32
BF16 = jnp.bfloat16
EPS = 1e-6
N_DEV = 8
LANES = 128
VMEM_LIMIT = 60 * 1024 * 1024

HEADS = 4
HEAD_DIM = 64
MIX_W = HEADS * HEAD_DIM
HG_CHUNK = 32
RET_CHUNK = 128
S5_GROUPS = 16
S5_GROUP = 16
S5_STATE = 64
LRU_C = 8.0
XA_HEADS = 4

ADAM_LR = 0.001
ADAM_B1 = 0.9
ADAM_B2 = 0.999
ADAM_EPS = 1e-08
ADAM_WD = 0.01
ADAM_STEP = 10

BIG = (("w_in", 2), ("w_gate", 2), ("w_up", 3), ("w_out", 1), ("xa_w_q", 1), ("xa_w_kv", 2), ("xa_w_o", 1),
       ("ffn_w_gu", 2), ("ffn_w_down", 1))
SMALL_SHARDED = (("lru_conv_w", 2), ("s5_glu_w", 1))
WEIGHTS = ("hg_lower_bounds", "norm_mix_pre", "norm_mix_post", "w_in", "w_gate", "b_gate", "hg_norm", "ret_norm",
           "lru_conv_w", "lru_conv_b", "lru_wa", "lru_ba", "lru_wx", "lru_bx", "lru_lambda", "s5_lam_re", "s5_lam_im",
           "s5_b_re", "s5_b_im", "s5_c_re", "s5_c_im", "s5_d", "s5_log_dt", "s5_glu_w", "s5_glu_b", "w_up", "w_out",
           "norm_xa_pre", "norm_xa_post", "norm_mem", "xa_w_q", "xa_w_kv", "xa_w_o", "norm_ffn_pre", "norm_ffn_post",
           "ffn_w_gu", "ffn_w_down")
GATHER_IN = {"in_proj": (1,), "gate_proj": (2, 3, 4, 6), "hg": (5,), "s5": (7,), "merge": (8,)}
_BIG_NAMES = tuple(n for n, _ in BIG)
_SMALL_SHARDED_NAMES = tuple(n for n, _ in SMALL_SHARDED)
REPLICATED = tuple(n for n in WEIGHTS if n not in _BIG_NAMES and n not in _SMALL_SHARDED_NAMES)


def _dot(a, b):
    return jnp.dot(a.astype(BF16), b.astype(BF16), preferred_element_type=F32)


def _dot_nt(a, b):
    return lax.dot_general(a.astype(BF16), b.astype(BF16), (((1,), (1,)), ((), ())), preferred_element_type=F32)


def _dot_tn(a, b):
    return lax.dot_general(a.astype(BF16), b.astype(BF16), (((0,), (0,)), ((), ())), preferred_element_type=F32)


def _dot_exact(a, b):
    return jnp.dot(a, b, precision=lax.Precision.HIGHEST, preferred_element_type=F32)


def _rms(x, g):
    return x * lax.rsqrt(jnp.mean(x * x, axis=-1, keepdims=True) + EPS) * g


def _shift_down(x, d, fill):
    return jnp.concatenate([jnp.full((d, x.shape[1]), fill, x.dtype), x[:-d]], axis=0)


def _shift_up(x, d, fill):
    return jnp.concatenate([x[d:], jnp.full((d, x.shape[1]), fill, x.dtype)], axis=0)


def _cumsum_rows(x):
    d = 1
    while d < x.shape[0]:
        x = x + _shift_down(x, d, 0.0)
        d *= 2
    return x


def _lane_head(shape, dim):
    return lax.shift_right_logical(lax.broadcasted_iota(jnp.int32, shape, dim), 6)


def _head_masks(width=MIX_W):
    head = _lane_head((1, width), 1)
    return [(head == h).astype(F32) for h in range(HEADS)]


def _block_diag_mask():
    return (_lane_head((MIX_W, MIX_W), 0) == _lane_head((MIX_W, MIX_W), 1)).astype(F32)


def _head_rms(o, g):
    ms = _dot_exact(o * o, _block_diag_mask()) * (1.0 / HEAD_DIM)
    return o * lax.rsqrt(ms + EPS) * g


def _swap_pairs(x):
    lane = lax.broadcasted_iota(jnp.int32, x.shape, 1)
    return jnp.where((lane & 1) == 0, jnp.roll(x, -1, axis=1), jnp.roll(x, 1, axis=1))


def _stack_heads(t, masks):
    return jnp.concatenate([t * m for m in masks], axis=0)


@jax.custom_vjp
def _real_scan(a, u, h0):
    return _real_scan_fwd(a, u, h0)[0]


def _real_scan_fwd(a, u, h0):
    t = a.shape[0]
    acc_a, acc_u = a, u
    d = 1
    while d < t:
        acc_u = acc_u + acc_a * _shift_down(acc_u, d, 0.0)
        acc_a = acc_a * _shift_down(acc_a, d, 1.0)
        d *= 2
    h = acc_u + acc_a * h0
    return h, (a, h, h0)


def _real_scan_bwd(res, dh):
    a, h, h0 = res
    t = a.shape[0]
    acc_a = _shift_up(a, 1, 0.0)
    g = dh
    d = 1
    while d < t:
        g = g + acc_a * _shift_up(g, d, 0.0)
        acc_a = acc_a * _shift_up(acc_a, d, 1.0)
        d *= 2
    h_prev = jnp.concatenate([h0, h[:-1]], axis=0)
    return g * h_prev, g, (a * g)[0:1]


_real_scan.defvjp(_real_scan_fwd, _real_scan_bwd)


def _cmul(ar, ai, br, bi):
    return ar * br - ai * bi, ar * bi + ai * br


def _geometric_sums(ar, ai, ur, ui, forward):
    shift = _shift_down if forward else _shift_up
    t = ur.shape[0]
    pr, pi, sr, si = ar, ai, ur, ui
    d = 1
    while d < t:
        mr, mi = _cmul(pr, pi, shift(sr, d, 0.0), shift(si, d, 0.0))
        sr, si = sr + mr, si + mi
        pr, pi = _cmul(pr, pi, pr, pi)
        d *= 2
    return sr, si


@jax.custom_vjp
def _complex_scan(ar, ai, ur, ui, h0r, h0i):
    return _complex_scan_fwd(ar, ai, ur, ui, h0r, h0i)[0]


def _complex_scan_fwd(ar, ai, ur, ui, h0r, h0i):
    first = lax.broadcasted_iota(jnp.int32, ur.shape, 0) == 0
    cr, ci = _cmul(ar, ai, h0r, h0i)
    hr, hi = _geometric_sums(ar, ai, ur + jnp.where(first, cr, 0.0), ui + jnp.where(first, ci, 0.0), True)
    return (hr, hi), (ar, ai, hr, hi, h0r, h0i)


def _complex_scan_bwd(res, dh):
    ar, ai, hr, hi, h0r, h0i = res
    gr, gi = _geometric_sums(ar, -ai, dh[0], dh[1], False)
    qr = jnp.concatenate([h0r, hr[:-1]], axis=0)
    qi = jnp.concatenate([h0i, hi[:-1]], axis=0)
    dar = jnp.sum(gr * qr + gi * qi, axis=0, keepdims=True)
    dai = jnp.sum(gi * qr - gr * qi, axis=0, keepdims=True)
    d0r, d0i = _cmul(ar, -ai, gr[0:1], gi[0:1])
    return dar, dai, gr, gi, d0r, d0i


_complex_scan.defvjp(_complex_scan_fwd, _complex_scan_bwd)


def fn_norm(st, rows, params):
    (x,), (g,) = rows, params
    return (), (_rms(x, g),), ()


def fn_keep_norm(st, rows, params):
    (x,), (g,) = rows, params
    return (), (x, _rms(x, g)), ()


def fn_hgrn2(st, rows, params):
    (state,) = st
    (z,) = rows
    lb, norm_g = params
    q, f_logit, v_all, g = (z[:, k * MIX_W:(k + 1) * MIX_W] for k in range(4))
    f = lb + (1.0 - lb) * jax.nn.sigmoid(f_logit)
    log_f = jnp.log(f)
    k_all = 1.0 - f
    q_all = jax.nn.silu(q)
    masks = _head_masks()
    bd = _block_diag_mask()
    c = HG_CHUNK
    col = lax.broadcasted_iota(jnp.int32, (c, HEADS * c), 1) & (c - 1)
    causal = col <= lax.broadcasted_iota(jnp.int32, (c, HEADS * c), 0)
    outs = []
    for n in range(z.shape[0] // c):
        sl = slice(n * c, (n + 1) * c)
        lf = log_f[sl]
        b = _cumsum_rows(lf)
        b_end = jnp.sum(lf, axis=0, keepdims=True)
        q_dec = q_all[sl] * jnp.exp(b)
        k_inv = k_all[sl] * jnp.exp(-b)
        k_end = k_all[sl] * jnp.exp(b_end - b)
        v = v_all[sl]
        scores = jnp.where(causal, _dot_nt(q_dec, _stack_heads(k_inv, masks)), 0.0)
        outs.append(_dot(scores, _stack_heads(v, masks)) + _dot_nt(q_dec, state))
        state = state * jnp.exp(b_end) + _dot_tn(v, k_end) * bd
    o = jnp.concatenate(outs, axis=0) if len(outs) > 1 else outs[0]
    return (state,), (_head_rms(o, norm_g) * jax.nn.silu(g),), ()


def fn_retention(st, rows, params):
    (state,) = st
    z, cos_t, sin_t = rows
    norm_g, xi, zeta, decay, g_end = params
    q, k, v_all, g = (z[:, i * MIX_W:(i + 1) * MIX_W] for i in range(4))
    q_all = q * cos_t + _swap_pairs(q) * sin_t
    k_all = (k * cos_t + _swap_pairs(k) * sin_t) * (HEAD_DIM ** -0.5)
    masks = _head_masks()
    bd = _block_diag_mask()
    c = RET_CHUNK
    outs = []
    for n in range(z.shape[0] // c):
        sl = slice(n * c, (n + 1) * c)
        qc, kc, v = q_all[sl], k_all[sl], v_all[sl]
        scores = _dot_nt(qc, _stack_heads(kc, masks)) * decay
        outs.append(_dot(scores, _stack_heads(v, masks)) + _dot_nt(qc * xi, state))
        state = state * g_end + _dot_tn(v, kc * zeta) * bd
    o = jnp.concatenate(outs, axis=0) if len(outs) > 1 else outs[0]
    return (state,), (_head_rms(o, norm_g) * jax.nn.silu(g),), ()


def fn_rglru(st, rows, params):
    tail_x, tail_h = st
    (z,) = rows
    conv_w, conv_b, wa, ba, wx, bx, sp = params
    t = z.shape[0]
    xg, xi = z[:, :MIX_W], z[:, MIX_W:]
    full = jnp.concatenate([tail_x, xi], axis=0)
    xc = conv_b
    for k in range(4):
        xc = xc + conv_w[k:k + 1] * full[5 + k:5 + k + t]
    r = jax.nn.sigmoid(_dot(xc, wa) + ba)
    ig = jax.nn.sigmoid(_dot(xc, wx) + bx)
    log_a = -LRU_C * r * sp
    a = jnp.exp(log_a)
    one_minus_a2 = -jnp.tanh(log_a) * (a * a + 1.0)
    u = jnp.sqrt(one_minus_a2) * (ig * xc)
    h = _real_scan(a, u, tail_h[7:8])
    return (xi[t - 8:], h[t - 8:]), (h * jax.nn.gelu(xg),), ()


def fn_s5(st, rows, params):
    tail_r, tail_i = st
    (u,) = rows
    bt_re, bt_im, lb_re, lb_im, ct_re, ct_im, d, glu_w, glu_b = params
    t = u.shape[0]
    bu_re = _dot(u, bt_re)
    bu_im = _dot(u, bt_im)
    h_re, h_im = _complex_scan(lb_re, lb_im, bu_re, bu_im, tail_r[7:8], tail_i[7:8])
    y = _dot(h_re, ct_re) - _dot(h_im, ct_im) + d * u
    act = jax.nn.gelu(y)
    out = act * jax.nn.sigmoid(_dot(act, glu_w) + glu_b)
    return (h_re[t - 8:], h_im[t - 8:]), (out,), ()


def fn_merge(st, rows, params):
    ya, yb, yc, yd, gl, x = rows
    w0, w1, w2, w3, b_gate, w_out, g_post = params
    d = x.shape[1]
    mix = None
    for n, (y, w) in enumerate(((ya, w0), (yb, w1), (yc, w2), (yd, w3))):
        gate = jax.nn.sigmoid(gl[:, n * d:(n + 1) * d] + b_gate[:, n * d:(n + 1) * d])
        term = gate * _dot(y, w)
        mix = term if mix is None else mix + term
    return (), (x + _rms(_dot(mix, w_out), g_post),), ()


def fn_mem(st, rows, params):
    (mem,), (g, wk, wv) = rows, params
    m = _rms(mem, g)
    return (), (_dot(m, wk), _dot(m, wv)), ()


def fn_xattn(st, rows, params):
    (x,) = rows
    g_pre, wq, k, v, wo, g_post, g_next = params
    d = x.shape[1]
    dh = d // XA_HEADS
    q = _dot(_rms(x, g_pre), wq)
    heads = []
    for h in range(XA_HEADS):
        sl = slice(h * dh, (h + 1) * dh)
        s = _dot_nt(q[:, sl], k[:, sl]) * (dh ** -0.5)
        heads.append(_dot(jax.nn.softmax(s, axis=-1), v[:, sl]))
    x2 = x + _rms(_dot(jnp.concatenate(heads, axis=1), wo), g_post)
    return (), (x2, _rms(x2, g_next)), ()


def fn_glu(st, rows, params):
    (gu,) = rows
    f = gu.shape[1] // 2
    return (), (jax.nn.silu(gu[:, :f]) * gu[:, f:],), ()


def fn_res_norm(st, rows, params):
    (x, o), (g_post, g_next) = rows, params
    xn = x + _rms(o, g_post)
    return (), (xn, _rms(xn, g_next)), ()


def fn_res(st, rows, params):
    (x, o), (g_post,) = rows, params
    return (), (x + _rms(o, g_post),), ()


def fn_res_loss(st, rows, params):
    (x, o, target), (g_post,) = rows, params
    err = x + _rms(o, g_post) - target
    inv_d = 1.0 / x.shape[1]
    loss = 0.5 * inv_d * jnp.sum(err * err)
    return (), (err * inv_d,), (jnp.full((8, LANES), loss, F32),)


def _params():
    return pltpu.CompilerParams(dimension_semantics=("arbitrary",), vmem_limit_bytes=VMEM_LIMIT)


def _row_spec(tb, width, colblk, nb, reverse):
    if reverse:
        return pl.BlockSpec((tb, width), lambda i: (nb - 1 - i, colblk))
    return pl.BlockSpec((tb, width), lambda i: (i, colblk))


def _full_spec(shape):
    return pl.BlockSpec(shape, lambda i: (0,) * len(shape))


def _saved_spec(shape, nb, reverse):
    if reverse:
        return pl.BlockSpec((1,) + shape, lambda i: (nb - 1 - i, 0, 0))
    return pl.BlockSpec((1,) + shape, lambda i: (i, 0, 0))


def _param_value(ref):
    v = ref[...]
    return v if v.dtype == BF16 else v.astype(F32)


def _side(side):
    arrays, gather = side if side is not None else ((), False)
    return list(arrays), gather, len(arrays)


_ANY = pl.BlockSpec(memory_space=pl.ANY)


def stage_fwd(name, fn, rows, params, *, tb, outs, states=(), accs=(), side=None):
    n_rows = rows[0][0].shape[0]
    nb = n_rows // tb
    nr, npar, no, na, ns = len(rows), len(params), len(outs), len(accs), len(states)
    side_arrays, gather, nx = _side(side)

    def body(*refs):
        row_refs, par_refs, side_in = refs[:nr], refs[nr:nr + npar], refs[nr + npar:nr + npar + nx]
        o = nr + npar + nx
        out_refs, acc_refs = refs[o:o + no], refs[o + no:o + no + na]
        saved_refs = refs[o + no + na:o + no + na + ns]
        o = o + no + na + ns
        side_out, st_refs, sems = refs[o:o + nx], refs[o + nx:o + nx + ns], refs[o + nx + ns:]
        i = pl.program_id(0)
        if nx:
            start, wait = _exchange_copies(side_in, side_out, *sems, gather)
            pl.when(i == 0)(start)

        @pl.when(i == 0)
        def _():
            for r in st_refs + acc_refs:
                r[...] = jnp.zeros_like(r)

        st = tuple(r[...] for r in st_refs)
        for sv, s in zip(saved_refs, st):
            sv[0] = s
        new_st, out_vals, acc_vals = fn(st, tuple(r[...].astype(F32) for r in row_refs),
                                        tuple(_param_value(r) for r in par_refs))
        for r, v in zip(out_refs, out_vals):
            r[...] = v.astype(r.dtype)
        for r, v in zip(acc_refs, acc_vals):
            r[...] += v
        for r, v in zip(st_refs, new_st):
            r[...] = v
        if nx:
            pl.when(i == nb - 1)(wait)

    res = pl.pallas_call(
        body, name=name, grid=(nb,),
        in_specs=[_row_spec(tb, w, cb, nb, False) for _, w, cb in rows] + [_full_spec(p.shape) for p in params]
        + [_ANY] * nx,
        out_specs=[_row_spec(tb, w, 0, nb, False) for w, _ in outs] + [_full_spec(s) for s in accs]
        + [_saved_spec(s, nb, False) for s in states] + [_ANY] * nx,
        out_shape=[jax.ShapeDtypeStruct((n_rows, w), dt) for w, dt in outs]
        + [jax.ShapeDtypeStruct(s, F32) for s in accs] + [jax.ShapeDtypeStruct((nb,) + s, F32) for s in states]
        + _exchange_shapes(side_arrays, gather),
        scratch_shapes=[pltpu.VMEM(s, F32) for s in states] + (_exchange_sems(nx) if nx else []),
        compiler_params=_params(),
    )(*[a for a, _, _ in rows], *params, *side_arrays)
    base = (res[:no], res[no:no + na], res[no + na:no + na + ns])
    return base + (res[no + na + ns:],) if nx else base


def stage_bwd(name, fn, rows, params, cts, *, tb, saved=(), diff_rows=(), diff_params=(), row_dtypes=None, side=None):
    n_rows = rows[0][0].shape[0]
    nb = n_rows // tb
    nr, npar, ns, nc = len(rows), len(params), len(saved), len(cts)
    ndr, ndp = len(diff_rows), len(diff_params)
    row_dtypes = row_dtypes or (F32,) * ndr
    state_shapes = [s.shape[1:] for s in saved]
    side_arrays, gather, nx = _side(side)

    def body(*refs):
        row_refs, par_refs = refs[:nr], refs[nr:nr + npar]
        o = nr + npar
        saved_refs, ct_refs, side_in = refs[o:o + ns], refs[o + ns:o + ns + nc], refs[o + ns + nc:o + ns + nc + nx]
        o = o + ns + nc + nx
        drow_refs, dpar_refs, side_out = refs[o:o + ndr], refs[o + ndr:o + ndr + ndp], refs[o + ndr + ndp:o + ndr + ndp + nx]
        o = o + ndr + ndp + nx
        dst_refs, sems = refs[o:o + ns], refs[o + ns:]
        i = pl.program_id(0)
        if nx:
            start, wait = _exchange_copies(side_in, side_out, *sems, gather)
            pl.when(i == 0)(start)

        @pl.when(i == 0)
        def _():
            for r in dst_refs + dpar_refs:
                r[...] = jnp.zeros_like(r)

        st = tuple(r[0] for r in saved_refs)
        row_vals = [r[...].astype(F32) for r in row_refs]
        par_vals = [_param_value(r) for r in par_refs]

        def f(st_, dr_, dp_):
            rv, pv = list(row_vals), list(par_vals)
            for k, v in zip(diff_rows, dr_):
                rv[k] = v
            for k, v in zip(diff_params, dp_):
                pv[k] = v
            new_st, out_vals, _ = fn(st_, tuple(rv), tuple(pv))
            return new_st, out_vals

        _, vjp = jax.vjp(f, st, tuple(row_vals[k] for k in diff_rows), tuple(par_vals[k] for k in diff_params))
        g_st, g_rows, g_par = vjp((tuple(r[...] for r in dst_refs), tuple(r[...].astype(F32) for r in ct_refs)))
        for r, v in zip(drow_refs, g_rows):
            r[...] = v.astype(r.dtype)
        for r, v in zip(dpar_refs, g_par):
            r[...] += v
        for r, v in zip(dst_refs, g_st):
            r[...] = v
        if nx:
            pl.when(i == nb - 1)(wait)

    res = pl.pallas_call(
        body, name=name, grid=(nb,),
        in_specs=[_row_spec(tb, w, cb, nb, True) for _, w, cb in rows] + [_full_spec(p.shape) for p in params]
        + [_saved_spec(s, nb, True) for s in state_shapes] + [_row_spec(tb, c.shape[1], 0, nb, True) for c in cts]
        + [_ANY] * nx,
        out_specs=[_row_spec(tb, rows[k][1], 0, nb, True) for k in diff_rows]
        + [_full_spec(params[k].shape) for k in diff_params] + [_ANY] * nx,
        out_shape=[jax.ShapeDtypeStruct((n_rows, rows[k][1]), dt) for k, dt in zip(diff_rows, row_dtypes)]
        + [jax.ShapeDtypeStruct(params[k].shape, F32) for k in diff_params] + _exchange_shapes(side_arrays, gather),
        scratch_shapes=[pltpu.VMEM(s, F32) for s in state_shapes] + (_exchange_sems(nx) if nx else []),
        compiler_params=_params(),
    )(*[a for a, _, _ in rows], *params, *saved, *cts, *side_arrays)
    base = (res[:ndr], res[ndr:ndr + ndp])
    return base + (res[ndr + ndp:],) if nx else base


def _pick(n, target):
    if n <= target:
        return n
    best = None
    for t in range(LANES, target + 1, LANES):
        if n % t == 0:
            best = t
    assert best is not None, n
    return best


def _mesh_position():
    return lax.axis_index("x"), lax.axis_index("y"), lax.axis_index("c")


def _peer(pos, k):
    x, y, c = pos
    px = 1 - x if k & 4 else x
    py = 1 - y if k & 2 else y
    pc = 1 - c if k & 1 else c
    return (px, py, pc), 4 * px + 2 * py + pc


def _exchange_copies(x_refs, o_refs, send_sems, recv_sems, local_sems, gather):
    pos = _mesh_position()
    me = 4 * pos[0] + 2 * pos[1] + pos[2]
    pairs = list(enumerate(zip(x_refs, o_refs)))

    def remote(k, a, src, dst):
        peer, _ = _peer(pos, k)
        return pltpu.make_async_remote_copy(src_ref=src, dst_ref=dst, send_sem=send_sems.at[k - 1, a],
                                            recv_sem=recv_sems.at[k - 1, a], device_id=peer,
                                            device_id_type=pl.DeviceIdType.MESH)

    def local(a, x, o):
        return pltpu.make_async_copy(x if gather else x.at[me], o.at[me], local_sems.at[a])

    def start():
        for a, (x, o) in pairs:
            local(a, x, o).start()
        for k in range(1, N_DEV):
            peer_idx = _peer(pos, k)[1]
            for a, (x, o) in pairs:
                remote(k, a, x if gather else x.at[peer_idx], o.at[me]).start()

    def wait():
        for k in range(1, N_DEV):
            peer_idx = _peer(pos, k)[1]
            for a, (x, o) in pairs:
                arrival = remote(k, a, x if gather else x.at[me], o.at[peer_idx])
                arrival.wait_recv()
                arrival.wait_send()
        for a, (x, o) in pairs:
            local(a, x, o).wait()

    return start, wait


def _exchange_shapes(arrays, gather):
    return [jax.ShapeDtypeStruct(((N_DEV,) + x.shape) if gather else x.shape, x.dtype) for x in arrays]


def _exchange_sems(n):
    return [pltpu.SemaphoreType.DMA((N_DEV - 1, n)), pltpu.SemaphoreType.DMA((N_DEV - 1, n)),
            pltpu.SemaphoreType.DMA((n,))]


def _exchange(name, arrays, gather):
    n = len(arrays)

    def body(*refs):
        start, wait = _exchange_copies(refs[:n], refs[n:2 * n], *refs[2 * n:], gather)
        start()
        wait()

    return pl.pallas_call(
        body, name=name,
        in_specs=[pl.BlockSpec(memory_space=pl.ANY)] * n, out_specs=[pl.BlockSpec(memory_space=pl.ANY)] * n,
        out_shape=_exchange_shapes(arrays, gather), scratch_shapes=_exchange_sems(n),
    )(*arrays)


def matmul(name, a, b, mode, out_dtype, add=None, side=None):
    if mode == "tn":
        k, m = a.shape
    else:
        m, k = a.shape
    n = b.shape[0] if mode == "nt" else b.shape[1]
    if mode == "tn":
        tm, tn, tk = _pick(m, 1408), _pick(n, 512), _pick(k, 1024)
    else:
        tm, tn, tk = _pick(m, 1024), _pick(n, 512), _pick(k, 2816)
    nk = k // tk
    grid = (m // tm, n // tn, nk)
    a_spec = pl.BlockSpec((tk, tm), lambda i, j, kk: (kk, i)) if mode == "tn" else pl.BlockSpec((tm, tk), lambda i, j, kk: (i, kk))
    b_spec = pl.BlockSpec((tn, tk), lambda i, j, kk: (j, kk)) if mode == "nt" else pl.BlockSpec((tk, tn), lambda i, j, kk: (kk, j))
    o_spec = pl.BlockSpec((tm, tn), lambda i, j, kk: (i, j))
    dims = {"nn": (((1,), (0,)), ((), ())), "nt": (((1,), (1,)), ((), ())), "tn": (((0,), (0,)), ((), ()))}[mode]
    has_add = add is not None
    side_arrays, gather = side if side is not None else ((), False)
    ns = len(side_arrays)
    n_in = 2 + has_add

    def body(*refs):
        a_ref, b_ref = refs[0], refs[1]
        side_in, o_ref, side_out = refs[n_in:n_in + ns], refs[n_in + ns], refs[n_in + ns + 1:n_in + 2 * ns + 1]
        scratch = refs[n_in + 2 * ns + 1:]
        ids = [pl.program_id(d) for d in range(3)]
        if ns:
            start, wait = _exchange_copies(side_in, side_out, *scratch[-3:], gather)
            pl.when((ids[0] == 0) & (ids[1] == 0) & (ids[2] == 0))(start)
        part = lax.dot_general(a_ref[...].astype(BF16), b_ref[...].astype(BF16), dims, preferred_element_type=F32)
        if nk == 1:
            o_ref[...] = (part + refs[2][...].astype(F32) if has_add else part).astype(o_ref.dtype)
        else:
            acc_ref = scratch[0]

            @pl.when(ids[2] == 0)
            def _():
                acc_ref[...] = part + refs[2][...].astype(F32) if has_add else part

            @pl.when(ids[2] > 0)
            def _():
                acc_ref[...] += part

            @pl.when(ids[2] == nk - 1)
            def _():
                o_ref[...] = acc_ref[...].astype(o_ref.dtype)
        if ns:
            pl.when((ids[0] == grid[0] - 1) & (ids[1] == grid[1] - 1) & (ids[2] == nk - 1))(wait)

    any_spec = pl.BlockSpec(memory_space=pl.ANY)
    res = pl.pallas_call(
        body, name=name, grid=grid,
        in_specs=[a_spec, b_spec] + ([o_spec] if has_add else []) + [any_spec] * ns,
        out_specs=[o_spec] + [any_spec] * ns,
        out_shape=[jax.ShapeDtypeStruct((m, n), out_dtype)] + _exchange_shapes(side_arrays, gather),
        scratch_shapes=([pltpu.VMEM((tm, tn), F32)] if nk > 1 else []) + (_exchange_sems(ns) if ns else []),
        compiler_params=pltpu.CompilerParams(
            dimension_semantics=("arbitrary",) * 3 if ns else ("parallel", "parallel", "arbitrary"),
            vmem_limit_bytes=VMEM_LIMIT),
    )(a, b, *([add] if has_add else []), *side_arrays)
    return (res[0], res[1:]) if ns else res[0]


def matmul_fused(name, a, bs, mode, n, extras, epilogue, out_dtypes):
    m, k = a.shape
    tm, tn = _pick(m, 1024), _pick(n, 512)
    dims = {"nn": (((1,), (0,)), ((), ())), "nt": (((1,), (1,)), ((), ()))}[mode]
    nb, nx = len(bs), len(extras)

    def b_spec(off):
        if mode == "nt":
            return pl.BlockSpec((tn, k), lambda i, j: (j + off, 0))
        return pl.BlockSpec((k, tn), lambda i, j: (0, j + off))

    def body(*refs):
        a_val = refs[0][...].astype(BF16)
        parts = tuple(lax.dot_general(a_val, r[...].astype(BF16), dims, preferred_element_type=F32)
                      for r in refs[1:1 + nb])
        tiles = tuple(r[...].astype(F32) for r in refs[1 + nb:1 + nb + nx])
        for r, v in zip(refs[1 + nb + nx:], epilogue(parts, tiles)):
            r[...] = v.astype(r.dtype)

    tile = pl.BlockSpec((tm, tn), lambda i, j: (i, j))
    return pl.pallas_call(
        body, name=name, grid=(m // tm, n // tn),
        in_specs=[pl.BlockSpec((tm, k), lambda i, j: (i, 0))] + [b_spec(off) for _, off in bs] + [tile] * nx,
        out_specs=[tile] * len(out_dtypes), out_shape=[jax.ShapeDtypeStruct((m, n), dt) for dt in out_dtypes],
        compiler_params=pltpu.CompilerParams(dimension_semantics=("parallel", "parallel"),
                                             vmem_limit_bytes=VMEM_LIMIT),
    )(a, *[b for b, _ in bs], *extras)


def _glu_fwd_tiles(parts, tiles):
    gate, up = parts
    return gate, up, jax.nn.silu(gate) * up


def _glu_bwd_tiles(parts, tiles):
    (da,), (gate, up) = parts, tiles
    _, vjp = jax.vjp(lambda g, u: jax.nn.silu(g) * u, gate, up)
    return vjp(da)


def adamw(name, w, m, v, gparts):
    layers, rows, cols = w.shape
    parts = gparts[0].shape[0]
    tr = 8
    while tr * 2 * cols <= 65536 and rows % (tr * 2) == 0:
        tr *= 2
    nblk = rows // tr
    c1 = 1.0 - ADAM_B1 ** ADAM_STEP
    c2 = 1.0 - ADAM_B2 ** ADAM_STEP

    def body(*refs):
        w_ref, m_ref, v_ref = refs[:3]
        g_refs = refs[3:3 + layers]
        go_ref, d_ref, mo_ref, vo_ref = refs[3 + layers:]
        layer = pl.program_id(0)
        g = None
        for ll, g_ref in enumerate(g_refs):
            s = g_ref[0].astype(F32)
            for p in range(1, parts):
                s = s + g_ref[p].astype(F32)
            g = s if g is None else jnp.where(layer == ll, s, g)
        m_new = ADAM_B1 * m_ref[...] + (1.0 - ADAM_B1) * g
        v_new = ADAM_B2 * v_ref[...] + (1.0 - ADAM_B2) * (g * g)
        m_hat = m_new / c1
        v_hat = v_new / c2
        go_ref[...] = g
        d_ref[...] = -ADAM_LR * (m_hat / (jnp.sqrt(v_hat) + ADAM_EPS) + ADAM_WD * w_ref[...])
        mo_ref[...] = m_new
        vo_ref[...] = v_new

    def part_spec(ll):
        return pl.BlockSpec((parts, tr, cols),
                            lambda l, i: (0, jnp.where(l == ll, i, jnp.where(l < ll, 0, nblk - 1)), 0))

    spec = pl.BlockSpec((None, tr, cols), lambda l, i: (l, i, 0))
    return pl.pallas_call(
        body, name=name, grid=(layers, nblk),
        in_specs=[spec, spec, spec] + [part_spec(ll) for ll in range(layers)],
        out_specs=[spec] * 4, out_shape=[jax.ShapeDtypeStruct(w.shape, F32)] * 4,
        compiler_params=pltpu.CompilerParams(dimension_semantics=("arbitrary", "arbitrary"),
                                             vmem_limit_bytes=VMEM_LIMIT),
    )(w, m, v, *gparts)


def _pack(arrays, dtype, row_multiple):
    flat = jnp.concatenate([a.astype(dtype).reshape(-1) for a in arrays])
    unit = row_multiple * LANES
    pad = (-flat.shape[0]) % unit
    if pad:
        flat = jnp.concatenate([flat, jnp.zeros((pad,), dtype)])
    return flat.reshape(-1, LANES)


def _unpack(flat2d, shapes):
    flat = flat2d.reshape(-1)
    out, off = [], 0
    for s in shapes:
        n = int(np.prod(s))
        out.append(flat[off:off + n].reshape(s))
        off += n
    return out


def _unpack_stacked(stacked, shapes):
    flat = stacked.reshape(N_DEV, -1)
    out, off = [], 0
    for s in shapes:
        n = int(np.prod(s))
        out.append(flat[:, off:off + n].reshape((N_DEV,) + tuple(s)))
        off += n
    return out


def _merge_shards(stacked, axis):
    t = jnp.moveaxis(stacked, 0, axis)
    s = t.shape
    return t.reshape(s[:axis] + (s[axis] * s[axis + 1],) + s[axis + 2:])


def _split_shards(full, axis):
    s = full.shape
    t = full.reshape(s[:axis] + (N_DEV, s[axis] // N_DEV) + s[axis + 1:])
    return jnp.moveaxis(t, axis, 0)


def _lower_bounds(hg_lower_bounds):
    p = jax.nn.softmax(hg_lower_bounds, axis=0)
    return jnp.cumsum(p, axis=0) - p[0:1]


def _prep_layer(p):
    def row(v):
        return v.reshape(1, -1)

    eye_b = jnp.eye(HEADS, dtype=F32)
    eye_g = jnp.eye(S5_GROUPS, dtype=F32)
    step = jnp.exp(p["s5_log_dt"])[:, None]
    lam_re, lam_im = p["s5_lam_re"], p["s5_lam_im"]
    mag = jnp.exp(lam_re * step)
    lb_re = mag * jnp.cos(lam_im * step)
    lb_im = mag * jnp.sin(lam_im * step)
    den = lam_re * lam_re + lam_im * lam_im
    f_re = ((lb_re - 1.0) * lam_re + lb_im * lam_im) / den
    f_im = (lb_im * lam_re - (lb_re - 1.0) * lam_im) / den
    bb_re = f_re[..., None] * p["s5_b_re"] - f_im[..., None] * p["s5_b_im"]
    bb_im = f_re[..., None] * p["s5_b_im"] + f_im[..., None] * p["s5_b_re"]
    width = S5_GROUPS * S5_GROUP
    n_state = S5_GROUPS * S5_STATE
    return dict(
        lb=row(p["lb"]), hg_norm=row(p["hg_norm"]), ret_norm=row(p["ret_norm"]),
        conv_w=p["lru_conv_w"], conv_b=row(p["lru_conv_b"]),
        wa=jnp.einsum("nij,nm->nimj", p["lru_wa"], eye_b).reshape(MIX_W, MIX_W), ba=row(p["lru_ba"]),
        wx=jnp.einsum("nij,nm->nimj", p["lru_wx"], eye_b).reshape(MIX_W, MIX_W), bx=row(p["lru_bx"]),
        sp=row(jax.nn.softplus(-p["lru_lambda"])),
        bt_re=jnp.einsum("gnp,gh->gphn", bb_re, eye_g).reshape(width, n_state),
        bt_im=jnp.einsum("gnp,gh->gphn", bb_im, eye_g).reshape(width, n_state),
        lb_re=row(lb_re), lb_im=row(lb_im),
        ct_re=jnp.einsum("gpn,gh->gnhp", p["s5_c_re"], eye_g).reshape(n_state, width),
        ct_im=jnp.einsum("gpn,gh->gnhp", p["s5_c_im"], eye_g).reshape(n_state, width),
        s5_d=row(p["s5_d"]), glu_w=p["s5_glu_w"], glu_b=row(p["s5_glu_b"]),
        b_gate=row(p["b_gate"]),
        norm_mix_pre=row(p["norm_mix_pre"]), norm_mix_post=row(p["norm_mix_post"]),
        norm_xa_pre=row(p["norm_xa_pre"]), norm_xa_post=row(p["norm_xa_post"]), norm_mem=row(p["norm_mem"]),
        norm_ffn_pre=row(p["norm_ffn_pre"]), norm_ffn_post=row(p["norm_ffn_post"]),
    )


_PREP_INPUTS = ("hg_norm", "ret_norm", "lru_conv_w", "lru_conv_b", "lru_wa", "lru_ba", "lru_wx", "lru_bx", "lru_lambda",
                "s5_lam_re", "s5_lam_im", "s5_b_re", "s5_b_im", "s5_c_re", "s5_c_im", "s5_d", "s5_log_dt", "s5_glu_w",
                "s5_glu_b", "b_gate", "norm_mix_pre", "norm_mix_post", "norm_xa_pre", "norm_xa_post", "norm_mem",
                "norm_ffn_pre", "norm_ffn_post")


def _retention_constants():
    lg = np.log1p(-np.power(2.0, -5.0 - np.arange(HEADS)))
    idx = np.arange(RET_CHUNK)

    def lanes(per_head_rows):
        return np.repeat(per_head_rows.T[:, :, None], HEAD_DIM, axis=2).reshape(RET_CHUNK, MIX_W)

    xi = lanes(np.exp((idx + 1.0)[None, :] * lg[:, None]))
    zeta = lanes(np.exp((RET_CHUNK - 1.0 - idx)[None, :] * lg[:, None]))
    rel = idx[:, None] - idx[None, :]
    decay = np.where(rel[None] >= 0, np.exp(np.maximum(rel, 0)[None] * lg[:, None, None]), 0.0)
    decay = np.transpose(decay, (1, 0, 2)).reshape(RET_CHUNK, HEADS * RET_CHUNK)
    g_end = np.repeat(np.exp(RET_CHUNK * lg), HEAD_DIM)[None, :]
    return tuple(jnp.asarray(a, F32) for a in (xi, zeta, decay, g_end))


def _rotary_tables(seq):
    pos = jnp.arange(seq, dtype=F32)
    inv_freq = 10000.0 ** (-jnp.arange(0, HEAD_DIM, 2, dtype=F32) / HEAD_DIM)
    ang = pos[:, None] * inv_freq[None, :]
    cos, sin = jnp.cos(ang), jnp.sin(ang)
    cos_t = jnp.tile(jnp.repeat(cos, 2, axis=1), (1, HEADS))
    sin_t = jnp.tile(jnp.stack([-sin, sin], axis=-1).reshape(seq, HEAD_DIM), (1, HEADS))
    return cos_t, sin_t


TB_HG = 128
TB_RET = 128
TB_LRU = 256
TB_S5 = 128
TB_ROW = 256

_STATE = (MIX_W, MIX_W)
_TAIL = (8, MIX_W)
_S5_TAIL = (8, S5_GROUPS * S5_STATE)


def _mixer_operands(l, z, kp, rot, ret_c):
    xi, zeta, decay, g_end = ret_c
    return dict(
        hg=dict(name=f"hgrn2_{l}", fn=fn_hgrn2, rows=[(z, 4 * MIX_W, 0)], params=[kp["lb"], kp["hg_norm"]],
                tb=TB_HG, states=(_STATE,), diff_params=(0, 1)),
        ret=dict(name=f"retention_{l}", fn=fn_retention, rows=[(z, 4 * MIX_W, 1), (rot[0], MIX_W, 0), (rot[1], MIX_W, 0)],
                 params=[kp["ret_norm"], xi, zeta, decay, g_end], tb=TB_RET, states=(_STATE,), diff_params=(0,)),
        lru=dict(name=f"rglru_{l}", fn=fn_rglru, rows=[(z, 2 * MIX_W, 4)],
                 params=[kp["conv_w"], kp["conv_b"], kp["wa"], kp["ba"], kp["wx"], kp["bx"], kp["sp"]],
                 tb=TB_LRU, states=(_TAIL, _TAIL), diff_params=(0, 1, 2, 3, 4, 5, 6)),
        s5=dict(name=f"s5_{l}", fn=fn_s5, rows=[(z, MIX_W, 10)],
                params=[kp["bt_re"], kp["bt_im"], kp["lb_re"], kp["lb_im"], kp["ct_re"], kp["ct_im"], kp["s5_d"],
                        kp["glu_w"], kp["glu_b"]],
                tb=TB_S5, states=(_S5_TAIL, _S5_TAIL), diff_params=tuple(range(9))),
    )


def _layer_forward(l, x, h, mem, w_in, shards, next_w_in_shard, kp, rot, ret_c, g_next, target):
    d = x.shape[1]
    sv = dict(x=x, h=h)
    bw = {"w_in": w_in}

    def gather(idx):
        return [shards[i] for i in idx], True

    def take(idx, stacked):
        for i, s in zip(idx, stacked):
            bw[_BIG_NAMES[i]] = _merge_shards(s, BIG[i][1] - 1)

    z, got = matmul(f"in_proj_{l}", h, w_in, "nn", F32, side=gather(GATHER_IN["in_proj"]))
    take(GATHER_IN["in_proj"], got)
    gl, got = matmul(f"gate_proj_{l}", h, bw["w_gate"], "nn", BF16, side=gather(GATHER_IN["gate_proj"]))
    take(GATHER_IN["gate_proj"], got)
    sv.update(z=z, gl=gl)
    ops = _mixer_operands(l, z, kp, rot, ret_c)
    ys = []
    for key in ("hg", "ret", "lru", "s5"):
        o = ops[key]
        idx = GATHER_IN.get(key)
        res = stage_fwd(o["name"] + "_fwd", o["fn"], o["rows"], o["params"], tb=o["tb"], outs=[(MIX_W, F32)],
                        states=o["states"], side=gather(idx) if idx else None)
        if idx:
            take(idx, res[3])
        ys.append(res[0][0])
        sv[key + "_states"] = res[2]
    sv["ys"] = ys
    merge_params = [bw["w_up"][n] for n in range(4)] + [kp["b_gate"], bw["w_out"], kp["norm_mix_post"]]
    merge_rows = [(y, MIX_W, 0) for y in ys] + [(gl, 4 * d, 0), (x, d, 0)]
    (x1,), _, _, got = stage_fwd(f"merge_{l}_fwd", fn_merge, merge_rows, merge_params, tb=TB_ROW, outs=[(d, F32)],
                                 side=gather(GATHER_IN["merge"]))
    take(GATHER_IN["merge"], got)
    wk, wv = bw["xa_w_kv"][:, :d], bw["xa_w_kv"][:, d:]
    mem_params = [kp["norm_mem"], wk, wv]
    (k, v), _, _ = stage_fwd(f"mem_{l}_fwd", fn_mem, [(mem, d, 0)], mem_params, tb=mem.shape[0],
                             outs=[(d, BF16), (d, BF16)])
    xa_params = [kp["norm_xa_pre"], bw["xa_w_q"], k, v, bw["xa_w_o"], kp["norm_xa_post"], kp["norm_ffn_pre"]]
    res = stage_fwd(f"xattn_{l}_fwd", fn_xattn, [(x1, d, 0)], xa_params, tb=TB_ROW, outs=[(d, F32), (d, BF16)],
                    side=([next_w_in_shard], True) if next_w_in_shard is not None else None)
    x2, h3 = res[0]
    next_w_in = _merge_shards(res[3][0], BIG[0][1] - 1) if next_w_in_shard is not None else None
    f = bw["ffn_w_gu"].shape[1] // 2
    up_block = f // _pick(f, 512)
    gate, up, a = matmul_fused(f"ffn_gu_{l}", h3, [(bw["ffn_w_gu"], 0), (bw["ffn_w_gu"], up_block)], "nn", f, [],
                               _glu_fwd_tiles, (BF16, BF16, BF16))
    o3 = matmul(f"ffn_down_{l}", a, bw["ffn_w_down"], "nn", F32)
    sv.update(x1=x1, k=k, v=v, x2=x2, h3=h3, gate=gate, up=up, a=a, o3=o3, merge_params=merge_params,
              merge_rows=merge_rows, mem_params=mem_params, xa_params=xa_params)
    if g_next is not None:
        (x3, hn), _, _ = stage_fwd(f"res_{l}_fwd", fn_res_norm, [(x2, d, 0), (o3, d, 0)], [kp["norm_ffn_post"], g_next],
                                   tb=TB_ROW, outs=[(d, F32), (d, BF16)])
        return x3, hn, sv, bw, next_w_in
    (dy,), (loss,), _ = stage_fwd(f"loss_{l}_fwd", fn_res_loss, [(x2, d, 0), (o3, d, 0), (target, d, 0)],
                                  [kp["norm_ffn_post"]], tb=TB_ROW, outs=[(d, F32)], accs=[(8, LANES)])
    return dy, loss[0, 0], sv, bw, next_w_in


def _layer_backward(l, sv, mem, bw, kp, rot, ret_c, g_next, dx3, dhn):
    d = sv["x"].shape[1]
    gk = {}
    parts = [None] * len(BIG)

    def scatter(*items):
        return [_split_shards(g.astype(BF16), BIG[i][1] - 1) for i, g in items], False

    def took(got, *idx):
        for i, p in zip(idx, got):
            parts[i] = p

    res_rows = [(sv["x2"], d, 0), (sv["o3"], d, 0)]
    if g_next is not None:
        (dx2, do3), (gk["norm_ffn_post"], gk["g_next"]) = stage_bwd(
            f"res_{l}_bwd", fn_res_norm, res_rows, [kp["norm_ffn_post"], g_next], [dx3, dhn], tb=TB_ROW,
            diff_rows=(0, 1), diff_params=(0, 1))
    else:
        (dx2, do3), (gk["norm_ffn_post"],) = stage_bwd(
            f"res_{l}_bwd", fn_res, res_rows, [kp["norm_ffn_post"]], [dx3], tb=TB_ROW, diff_rows=(0, 1), diff_params=(0,))
    f = sv["a"].shape[1]
    g_down = matmul(f"ffn_down_dw_{l}", sv["a"], do3, "tn", BF16)
    d_gate, d_up = matmul_fused(f"ffn_down_dx_{l}", do3, [(bw["ffn_w_down"], 0)], "nt", f, [sv["gate"], sv["up"]],
                                _glu_bwd_tiles, (BF16, BF16))
    dh3, got = matmul(f"ffn_gate_dx_{l}", d_gate, bw["ffn_w_gu"][:, :f], "nt", F32, side=scatter((8, g_down)))
    took(got, 8)
    dh3 = matmul(f"ffn_up_dx_{l}", d_up, bw["ffn_w_gu"][:, f:], "nt", F32, add=dh3)
    g_gu = jnp.concatenate([matmul(f"ffn_gate_dw_{l}", sv["h3"], d_gate, "tn", BF16),
                            matmul(f"ffn_up_dw_{l}", sv["h3"], d_up, "tn", BF16)], axis=1)
    (dx1,), xa_g, got = stage_bwd(f"xattn_{l}_bwd", fn_xattn, [(sv["x1"], d, 0)], sv["xa_params"], [dx2, dh3],
                                  tb=TB_ROW, diff_rows=(0,), diff_params=tuple(range(7)), side=scatter((7, g_gu)))
    took(got, 7)
    gk["norm_xa_pre"], g_q, dk, dv, g_o, gk["norm_xa_post"], gk["norm_ffn_pre"] = xa_g
    _, (gk["norm_mem"], dwk, dwv) = stage_bwd(f"mem_{l}_bwd", fn_mem, [(mem, d, 0)], sv["mem_params"], [dk, dv],
                                              tb=mem.shape[0], diff_params=(0, 1, 2))
    g_kv = jnp.concatenate([dwk, dwv], axis=1)
    merge_d, merge_g, got = stage_bwd(f"merge_{l}_bwd", fn_merge, sv["merge_rows"], sv["merge_params"], [dx1],
                                      tb=TB_ROW, diff_rows=tuple(range(6)), diff_params=tuple(range(7)),
                                      row_dtypes=(F32, F32, F32, F32, BF16, F32),
                                      side=scatter((4, g_q), (6, g_o), (5, g_kv)))
    took(got, 4, 6, 5)
    dys, dgl, dx0 = merge_d[:4], merge_d[4], merge_d[5]
    g_up = jnp.stack(merge_g[:4])
    gk["b_gate"], g_out, gk["norm_mix_post"] = merge_g[4:]
    ops = _mixer_operands(l, sv["z"], kp, rot, ret_c)
    dz = {}
    for key, dy in zip(("hg", "ret", "lru", "s5"), dys):
        o = ops[key]
        res = stage_bwd(o["name"] + "_bwd", o["fn"], o["rows"], o["params"], [dy], tb=o["tb"],
                        saved=sv[key + "_states"], diff_rows=(0,), diff_params=o["diff_params"], row_dtypes=(BF16,),
                        side=scatter((2, g_up), (3, g_out)) if key == "hg" else None)
        if key == "hg":
            took(res[2], 2, 3)
        dz[key] = res[0][0]
        gk[key] = res[1]
    g_gate = matmul(f"gate_proj_dw_{l}", sv["h"], dgl, "tn", BF16)
    keys = ("hg", "ret", "lru", "s5")
    g_in = jnp.concatenate([matmul(f"in_proj_dw_{key}_{l}", sv["h"], dz[key], "tn", BF16) for key in keys], axis=1)
    dh, got = matmul(f"gate_proj_dx_{l}", dgl, bw["w_gate"], "nt", F32, side=scatter((1, g_gate)))
    took(got, 1)
    col = 0
    for key in keys:
        width = dz[key].shape[1]
        w_cols = bw["w_in"][:, col:col + width]
        if key == "hg":
            dh, got = matmul(f"in_proj_dx_{key}_{l}", dz[key], w_cols, "nt", F32, add=dh, side=scatter((0, g_in)))
            took(got, 0)
        else:
            dh = matmul(f"in_proj_dx_{key}_{l}", dz[key], w_cols, "nt", F32, add=dh)
        col += width
    return dx0, dh, gk, parts


def _kernel_grads_to_prep(gk):
    hg, ret, lru, s5 = gk["hg"], gk["ret"], gk["lru"], gk["s5"]
    return dict(
        lb=hg[0], hg_norm=hg[1], ret_norm=ret[0],
        conv_w=lru[0], conv_b=lru[1], wa=lru[2], ba=lru[3], wx=lru[4], bx=lru[5], sp=lru[6],
        bt_re=s5[0], bt_im=s5[1], lb_re=s5[2], lb_im=s5[3], ct_re=s5[4], ct_im=s5[5], s5_d=s5[6], glu_w=s5[7],
        glu_b=s5[8], b_gate=gk["b_gate"], norm_mix_pre=gk["norm_mix_pre"], norm_mix_post=gk["norm_mix_post"],
        norm_xa_pre=gk["norm_xa_pre"], norm_xa_post=gk["norm_xa_post"], norm_mem=gk["norm_mem"],
        norm_ffn_pre=gk["norm_ffn_pre"], norm_ffn_post=gk["norm_ffn_post"],
    )


def _step(inp):
    x, mem, target = inp["x"][0], inp["mem"][0], inp["loss_target"][0]
    seq = x.shape[0]
    depth = inp["w_in"].shape[0]
    me = 4 * lax.axis_index("x") + 2 * lax.axis_index("y") + lax.axis_index("c")

    small_shapes = [inp[n].shape for n in _SMALL_SHARDED_NAMES]
    (small_stacked,) = _exchange("gather_small", [_pack([inp[n] for n in _SMALL_SHARDED_NAMES], F32, 8)], True)
    small_all = _unpack_stacked(small_stacked, small_shapes)
    full_small = {n: _merge_shards(s, ax) for (n, ax), s in zip(SMALL_SHARDED, small_all)}

    lbs, lbs_vjp = jax.vjp(_lower_bounds, inp["hg_lower_bounds"])
    kps, prep_vjps = [], []
    for l in range(depth):
        p = {n: (full_small[n][l] if n in full_small else inp[n][l]) for n in _PREP_INPUTS}
        p["lb"] = lbs[l]
        kp, vj = jax.vjp(_prep_layer, p)
        kps.append(kp)
        prep_vjps.append(vj)
    rot = _rotary_tables(seq)
    ret_c = _retention_constants()

    (h,), _, _ = stage_fwd("norm_in_fwd", fn_norm, [(x, x.shape[1], 0)], [kps[0]["norm_mix_pre"]], tb=TB_ROW,
                           outs=[(x.shape[1], BF16)])
    def shards(l):
        return [inp[n][l].astype(BF16) for n in _BIG_NAMES]

    saved, bws = [], []
    xs = x
    (stacked,) = _exchange("gather_w_in_0", [shards(0)[0]], True)
    w_in = _merge_shards(stacked, BIG[0][1] - 1)
    for l in range(depth):
        last = l + 1 == depth
        g_next = None if last else kps[l + 1]["norm_mix_pre"]
        xs, h, sv, bw, w_in = _layer_forward(l, xs, h, mem, w_in, shards(l), None if last else shards(l + 1)[0],
                                             kps[l], rot, ret_c, g_next, target)
        saved.append(sv)
        bws.append(bw)
    dy, loss_local = xs, h

    big_parts = [None] * depth
    gks = [None] * depth
    dx, dh = dy, None
    for l in reversed(range(depth)):
        g_next = kps[l + 1]["norm_mix_pre"] if l + 1 < depth else None
        dx, dh, gks[l], big_parts[l] = _layer_backward(l, saved[l], mem, bws[l], kps[l], rot, ret_c, g_next, dx, dh)
    (grad_x,), (g_pre0,) = stage_bwd("norm_in_bwd", fn_keep_norm, [(x, x.shape[1], 0)], [kps[0]["norm_mix_pre"]],
                                     [dx, dh], tb=TB_ROW, diff_rows=(0,), diff_params=(0,))
    for l in range(depth):
        gks[l]["norm_mix_pre"] = g_pre0 if l == 0 else gks[l - 1]["g_next"]

    small_grads = {n: [None] * depth for n in _PREP_INPUTS}
    d_lbs = []
    for l in range(depth):
        (gp,) = prep_vjps[l](_kernel_grads_to_prep(gks[l]))
        d_lbs.append(gp["lb"])
        for n in _PREP_INPUTS:
            small_grads[n][l] = gp[n]
    small_local = {n: jnp.stack(v) for n, v in small_grads.items()}
    (small_local["hg_lower_bounds"],) = lbs_vjp(jnp.stack(d_lbs))
    small_names = REPLICATED + _SMALL_SHARDED_NAMES
    full_shapes = [small_local[n].shape for n in small_names]
    (small_parts,) = _exchange("gather_small_grads", [_pack([small_local[n] for n in small_names], F32, 64)], True)

    out = {}
    kinds = ("grad_", "delta_", "new_m_", "new_v_")
    for i, n in enumerate(_BIG_NAMES):
        shape = inp[n].shape
        three = (shape[0], int(np.prod(shape[1:-1])), shape[-1])
        res = adamw("adamw_" + n, *[inp[pre + n].reshape(three) for pre in ("", "m_", "v_")],
                    [big_parts[l][i].reshape((N_DEV,) + three[1:]) for l in range(depth)])
        for kind, a in zip(kinds, res):
            out[kind + n] = a.reshape(shape)
    zeros = [jnp.zeros(s, F32) for s in full_shapes[len(REPLICATED):]]
    res = adamw("adamw_small", *[_pack([inp[pre + n] for n in REPLICATED] + zeros, F32, 64)[None] for pre in ("", "m_", "v_")],
                [small_parts])
    res = [r[0] for r in res]
    summed = _unpack(res[0], full_shapes)
    for kind, flat in zip(("grad_", "delta_", "new_m_", "new_v_"), res):
        for n, a in zip(REPLICATED, _unpack(flat, full_shapes[:len(REPLICATED)])):
            out[kind + n] = a
    shard_g = []
    for (n, ax), g_full in zip(SMALL_SHARDED, summed[len(REPLICATED):]):
        width = inp[n].shape[ax]
        shard_g.append(lax.dynamic_slice_in_dim(g_full, me * width, width, axis=ax))
    res = adamw("adamw_small_sharded",
                *[_pack([inp[pre + n] for n in _SMALL_SHARDED_NAMES], F32, 8)[None] for pre in ("", "m_", "v_")],
                [_pack(shard_g, F32, 8)[None]])
    res = [r[0] for r in res]
    for kind, flat in zip(("grad_", "delta_", "new_m_", "new_v_"), res):
        for n, a in zip(_SMALL_SHARDED_NAMES, _unpack(flat, small_shapes)):
            out[kind + n] = a

    out["loss"] = lax.psum(loss_local, ("x", "y", "c"))
    out["grad_x"] = grad_x[None]
    return out


def kernel(x, mem, hg_lower_bounds, norm_mix_pre, norm_mix_post, w_in, w_gate, b_gate, hg_norm, ret_norm, lru_conv_w, lru_conv_b, lru_wa, lru_ba, lru_wx, lru_bx, lru_lambda, s5_lam_re, s5_lam_im, s5_b_re, s5_b_im, s5_c_re, s5_c_im, s5_d, s5_log_dt, s5_glu_w, s5_glu_b, w_up, w_out, norm_xa_pre, norm_xa_post, norm_mem, xa_w_q, xa_w_kv, xa_w_o, norm_ffn_pre, norm_ffn_post, ffn_w_gu, ffn_w_down, loss_target, m_hg_lower_bounds, m_norm_mix_pre, m_norm_mix_post, m_w_in, m_w_gate, m_b_gate, m_hg_norm, m_ret_norm, m_lru_conv_w, m_lru_conv_b, m_lru_wa, m_lru_ba, m_lru_wx, m_lru_bx, m_lru_lambda, m_s5_lam_re, m_s5_lam_im, m_s5_b_re, m_s5_b_im, m_s5_c_re, m_s5_c_im, m_s5_d, m_s5_log_dt, m_s5_glu_w, m_s5_glu_b, m_w_up, m_w_out, m_norm_xa_pre, m_norm_xa_post, m_norm_mem, m_xa_w_q, m_xa_w_kv, m_xa_w_o, m_norm_ffn_pre, m_norm_ffn_post, m_ffn_w_gu, m_ffn_w_down, v_hg_lower_bounds, v_norm_mix_pre, v_norm_mix_post, v_w_in, v_w_gate, v_b_gate, v_hg_norm, v_ret_norm, v_lru_conv_w, v_lru_conv_b, v_lru_wa, v_lru_ba, v_lru_wx, v_lru_bx, v_lru_lambda, v_s5_lam_re, v_s5_lam_im, v_s5_b_re, v_s5_b_im, v_s5_c_re, v_s5_c_im, v_s5_d, v_s5_log_dt, v_s5_glu_w, v_s5_glu_b, v_w_up, v_w_out, v_norm_xa_pre, v_norm_xa_post, v_norm_mem, v_xa_w_q, v_xa_w_kv, v_xa_w_o, v_norm_ffn_pre, v_norm_ffn_post, v_ffn_w_gu, v_ffn_w_down):
    values = (x, mem, hg_lower_bounds, norm_mix_pre, norm_mix_post, w_in, w_gate, b_gate, hg_norm, ret_norm, lru_conv_w, lru_conv_b, lru_wa, lru_ba, lru_wx, lru_bx, lru_lambda, s5_lam_re, s5_lam_im, s5_b_re, s5_b_im, s5_c_re, s5_c_im, s5_d, s5_log_dt, s5_glu_w, s5_glu_b, w_up, w_out, norm_xa_pre, norm_xa_post, norm_mem, xa_w_q, xa_w_kv, xa_w_o, norm_ffn_pre, norm_ffn_post, ffn_w_gu, ffn_w_down, loss_target, m_hg_lower_bounds, m_norm_mix_pre, m_norm_mix_post, m_w_in, m_w_gate, m_b_gate, m_hg_norm, m_ret_norm, m_lru_conv_w, m_lru_conv_b, m_lru_wa, m_lru_ba, m_lru_wx, m_lru_bx, m_lru_lambda, m_s5_lam_re, m_s5_lam_im, m_s5_b_re, m_s5_b_im, m_s5_c_re, m_s5_c_im, m_s5_d, m_s5_log_dt, m_s5_glu_w, m_s5_glu_b, m_w_up, m_w_out, m_norm_xa_pre, m_norm_xa_post, m_norm_mem, m_xa_w_q, m_xa_w_kv, m_xa_w_o, m_norm_ffn_pre, m_norm_ffn_post, m_ffn_w_gu, m_ffn_w_down, v_hg_lower_bounds, v_norm_mix_pre, v_norm_mix_post, v_w_in, v_w_gate, v_b_gate, v_hg_norm, v_ret_norm, v_lru_conv_w, v_lru_conv_b, v_lru_wa, v_lru_ba, v_lru_wx, v_lru_bx, v_lru_lambda, v_s5_lam_re, v_s5_lam_im, v_s5_b_re, v_s5_b_im, v_s5_c_re, v_s5_c_im, v_s5_d, v_s5_log_dt, v_s5_glu_w, v_s5_glu_b, v_w_up, v_w_out, v_norm_xa_pre, v_norm_xa_post, v_norm_mem, v_xa_w_q, v_xa_w_kv, v_xa_w_o, v_norm_ffn_pre, v_norm_ffn_post, v_ffn_w_gu, v_ffn_w_down)
    names = ("x", "mem") + WEIGHTS + ("loss_target",) + tuple("m_" + n for n in WEIGHTS) + tuple("v_" + n for n in WEIGHTS)
    out = _step(dict(zip(names, values)))
    order = ["loss", "grad_x"] + [k + n for k in ("grad_", "delta_", "new_m_", "new_v_") for n in WEIGHTS]
    return tuple(out[k] for k in order)
```

```python
import functools

import numpy as np
import jax
import jax.numpy as jnp
from jax import lax
from jax.experimental import pallas as pl
from jax.experimental.pallas import tpu as pltpu

F32 = jnp.float32
BF16 = jnp.bfloat16
EPS = 1e-6
N_DEV = 8
LANES = 128
SUBLANES = 8
VMEM_LIMIT = 60 * 1024 * 1024

HEADS = 4
HEAD_DIM = 64
MIX_W = HEADS * HEAD_DIM
HG_CHUNK = 32
RET_CHUNK = 128
S5_GROUPS = 16
S5_GROUP = 16
S5_STATE = 64
LRU_C = 8.0
XA_HEADS = 4

ADAM_LR = 0.001
ADAM_B1 = 0.9
ADAM_B2 = 0.999
ADAM_EPS = 1e-08
ADAM_WD = 0.01
ADAM_STEP = 10

BIG = (("w_in", 2), ("w_gate", 2), ("w_up", 3), ("w_out", 1), ("xa_w_q", 1), ("xa_w_kv", 2), ("xa_w_o", 1),
       ("ffn_w_gu", 2), ("ffn_w_down", 1))
SMALL_SHARDED = (("lru_conv_w", 2), ("s5_glu_w", 1))
WEIGHTS = ("hg_lower_bounds", "norm_mix_pre", "norm_mix_post", "w_in", "w_gate", "b_gate", "hg_norm", "ret_norm",
           "lru_conv_w", "lru_conv_b", "lru_wa", "lru_ba", "lru_wx", "lru_bx", "lru_lambda", "s5_lam_re", "s5_lam_im",
           "s5_b_re", "s5_b_im", "s5_c_re", "s5_c_im", "s5_d", "s5_log_dt", "s5_glu_w", "s5_glu_b", "w_up", "w_out",
           "norm_xa_pre", "norm_xa_post", "norm_mem", "xa_w_q", "xa_w_kv", "xa_w_o", "norm_ffn_pre", "norm_ffn_post",
           "ffn_w_gu", "ffn_w_down")
GATHER_IN = {"in_proj": (1,), "gate_proj": (2, 3, 4, 6), "hg": (5,), "s5": (7,), "merge": (8,)}
_BIG_NAMES = tuple(n for n, _ in BIG)
_SMALL_SHARDED_NAMES = tuple(n for n, _ in SMALL_SHARDED)
REPLICATED = tuple(n for n in WEIGHTS if n not in _BIG_NAMES and n not in _SMALL_SHARDED_NAMES)


def _dot(a, b):
    return jnp.dot(a.astype(BF16), b.astype(BF16), preferred_element_type=F32)


def _dot_nt(a, b):
    return lax.dot_general(a.astype(BF16), b.astype(BF16), (((1,), (1,)), ((), ())), preferred_element_type=F32)


def _dot_tn(a, b):
    return lax.dot_general(a.astype(BF16), b.astype(BF16), (((0,), (0,)), ((), ())), preferred_element_type=F32)


def _dot_exact(a, b):
    return jnp.dot(a, b, precision=lax.Precision.HIGHEST, preferred_element_type=F32)


def _rms(x, g):
    return x * lax.rsqrt(jnp.mean(x * x, axis=-1, keepdims=True) + EPS) * g


def _shift_down(x, d, fill):
    return jnp.concatenate([jnp.full((d, x.shape[1]), fill, x.dtype), x[:-d]], axis=0)


def _shift_up(x, d, fill):
    return jnp.concatenate([x[d:], jnp.full((d, x.shape[1]), fill, x.dtype)], axis=0)


def _cumsum_rows(x):
    d = 1
    while d < x.shape[0]:
        x = x + _shift_down(x, d, 0.0)
        d *= 2
    return x


def _lane_head(shape, dim):
    return lax.shift_right_logical(lax.broadcasted_iota(jnp.int32, shape, dim), 6)


def _head_masks(width=MIX_W):
    head = _lane_head((1, width), 1)
    return [(head == h).astype(F32) for h in range(HEADS)]


def _block_diag_mask():
    return (_lane_head((MIX_W, MIX_W), 0) == _lane_head((MIX_W, MIX_W), 1)).astype(F32)


def _head_rms(o, g):
    ms = _dot_exact(o * o, _block_diag_mask()) * (1.0 / HEAD_DIM)
    return o * lax.rsqrt(ms + EPS) * g


def _swap_pairs(x):
    lane = lax.broadcasted_iota(jnp.int32, x.shape, 1)
    return jnp.where((lane & 1) == 0, jnp.roll(x, -1, axis=1), jnp.roll(x, 1, axis=1))


def _stack_heads(t, masks):
    return jnp.concatenate([t * m for m in masks], axis=0)


@jax.custom_vjp
def _real_scan(a, u, h0):
    return _real_scan_fwd(a, u, h0)[0]


def _real_scan_fwd(a, u, h0):
    t = a.shape[0]
    acc_a, acc_u = a, u
    d = 1
    while d < t:
        acc_u = acc_u + acc_a * _shift_down(acc_u, d, 0.0)
        acc_a = acc_a * _shift_down(acc_a, d, 1.0)
        d *= 2
    h = acc_u + acc_a * h0
    return h, (a, h, h0)


def _real_scan_bwd(res, dh):
    a, h, h0 = res
    t = a.shape[0]
    acc_a = _shift_up(a, 1, 0.0)
    g = dh
    d = 1
    while d < t:
        g = g + acc_a * _shift_up(g, d, 0.0)
        acc_a = acc_a * _shift_up(acc_a, d, 1.0)
        d *= 2
    h_prev = jnp.concatenate([h0, h[:-1]], axis=0)
    return g * h_prev, g, (a * g)[0:1]


_real_scan.defvjp(_real_scan_fwd, _real_scan_bwd)


def _cmul(ar, ai, br, bi):
    return ar * br - ai * bi, ar * bi + ai * br


def _geometric_sums(ar, ai, ur, ui, forward):
    shift = _shift_down if forward else _shift_up
    t = ur.shape[0]
    g = SUBLANES
    in_group = lax.broadcasted_iota(jnp.int32, ur.shape, 0) & (g - 1)
    pr, pi, sr, si = ar, ai, ur, ui
    d = 1
    while d < g:
        keep = in_group >= d if forward else in_group < g - d
        mr, mi = _cmul(pr, pi, jnp.where(keep, shift(sr, d, 0.0), 0.0), jnp.where(keep, shift(si, d, 0.0), 0.0))
        sr, si = sr + mr, si + mi
        pr, pi = _cmul(pr, pi, pr, pi)
        d *= 2
    row = lax.broadcasted_iota(jnp.int32, (g, ur.shape[1]), 0)
    qr, qi = ar, ai
    tr, ti = jnp.zeros((g, ur.shape[1]), F32), jnp.zeros((g, ur.shape[1]), F32)
    for r in range(g):
        here = row == (r if forward else g - 1 - r)
        tr, ti = jnp.where(here, qr, tr), jnp.where(here, qi, ti)
        qr, qi = _cmul(qr, qi, ar, ai)
    outs_r, outs_i = [], []
    cr = ci = None
    order = range(t // g) if forward else reversed(range(t // g))
    for n in order:
        br, bi = sr[n * g:(n + 1) * g], si[n * g:(n + 1) * g]
        if cr is not None:
            mr, mi = _cmul(tr, ti, cr, ci)
            br, bi = br + mr, bi + mi
        edge = slice(g - 1, g) if forward else slice(0, 1)
        cr, ci = br[edge], bi[edge]
        outs_r.append(br)
        outs_i.append(bi)
    if not forward:
        outs_r.reverse()
        outs_i.reverse()
    return jnp.concatenate(outs_r, axis=0), jnp.concatenate(outs_i, axis=0)


@jax.custom_vjp
def _complex_scan(ar, ai, ur, ui, h0r, h0i):
    return _complex_scan_fwd(ar, ai, ur, ui, h0r, h0i)[0]


def _complex_scan_fwd(ar, ai, ur, ui, h0r, h0i):
    first = lax.broadcasted_iota(jnp.int32, ur.shape, 0) == 0
    cr, ci = _cmul(ar, ai, h0r, h0i)
    hr, hi = _geometric_sums(ar, ai, ur + jnp.where(first, cr, 0.0), ui + jnp.where(first, ci, 0.0), True)
    return (hr, hi), (ar, ai, hr, hi, h0r, h0i)


def _complex_scan_bwd(res, dh):
    ar, ai, hr, hi, h0r, h0i = res
    gr, gi = _geometric_sums(ar, -ai, dh[0], dh[1], False)
    qr = jnp.concatenate([h0r, hr[:-1]], axis=0)
    qi = jnp.concatenate([h0i, hi[:-1]], axis=0)
    dar = jnp.sum(gr * qr + gi * qi, axis=0, keepdims=True)
    dai = jnp.sum(gi * qr - gr * qi, axis=0, keepdims=True)
    d0r, d0i = _cmul(ar, -ai, gr[0:1], gi[0:1])
    return dar, dai, gr, gi, d0r, d0i


_complex_scan.defvjp(_complex_scan_fwd, _complex_scan_bwd)


def fn_norm(st, rows, params):
    (x,), (g,) = rows, params
    return (), (_rms(x, g),), ()


def fn_keep_norm(st, rows, params):
    (x,), (g,) = rows, params
    return (), (x, _rms(x, g)), ()


def fn_hgrn2(st, rows, params):
    (state,) = st
    (z,) = rows
    lb, norm_g = params
    q, f_logit, v_all, g = (z[:, k * MIX_W:(k + 1) * MIX_W] for k in range(4))
    f = lb + (1.0 - lb) * jax.nn.sigmoid(f_logit)
    log_f = jnp.log(f)
    k_all = 1.0 - f
    q_all = jax.nn.silu(q)
    masks = _head_masks()
    bd = _block_diag_mask()
    c = HG_CHUNK
    col = lax.broadcasted_iota(jnp.int32, (c, HEADS * c), 1) & (c - 1)
    causal = col <= lax.broadcasted_iota(jnp.int32, (c, HEADS * c), 0)
    outs = []
    for n in range(z.shape[0] // c):
        sl = slice(n * c, (n + 1) * c)
        lf = log_f[sl]
        b = _cumsum_rows(lf)
        b_end = jnp.sum(lf, axis=0, keepdims=True)
        q_dec = q_all[sl] * jnp.exp(b)
        k_inv = k_all[sl] * jnp.exp(-b)
        k_end = k_all[sl] * jnp.exp(b_end - b)
        v = v_all[sl]
        scores = jnp.where(causal, _dot_nt(q_dec, _stack_heads(k_inv, masks)), 0.0)
        outs.append(_dot(scores, _stack_heads(v, masks)) + _dot_nt(q_dec, state))
        state = state * jnp.exp(b_end) + _dot_tn(v, k_end) * bd
    o = jnp.concatenate(outs, axis=0) if len(outs) > 1 else outs[0]
    return (state,), (_head_rms(o, norm_g) * jax.nn.silu(g),), ()


def fn_retention(st, rows, params):
    (state,) = st
    z, cos_t, sin_t = rows
    norm_g, xi, zeta, decay, g_end = params
    q, k, v_all, g = (z[:, i * MIX_W:(i + 1) * MIX_W] for i in range(4))
    q_all = q * cos_t + _swap_pairs(q) * sin_t
    k_all = (k * cos_t + _swap_pairs(k) * sin_t) * (HEAD_DIM ** -0.5)
    masks = _head_masks()
    bd = _block_diag_mask()
    c = RET_CHUNK
    outs = []
    for n in range(z.shape[0] // c):
        sl = slice(n * c, (n + 1) * c)
        qc, kc, v = q_all[sl], k_all[sl], v_all[sl]
        scores = _dot_nt(qc, _stack_heads(kc, masks)) * decay
        outs.append(_dot(scores, _stack_heads(v, masks)) + _dot_nt(qc * xi, state))
        state = state * g_end + _dot_tn(v, kc * zeta) * bd
    o = jnp.concatenate(outs, axis=0) if len(outs) > 1 else outs[0]
    return (state,), (_head_rms(o, norm_g) * jax.nn.silu(g),), ()


def fn_rglru(st, rows, params):
    tail_x, tail_h = st
    (z,) = rows
    conv_w, conv_b, wa, ba, wx, bx, sp = params
    t = z.shape[0]
    xg, xi = z[:, :MIX_W], z[:, MIX_W:]
    full = jnp.concatenate([tail_x, xi], axis=0)
    xc = conv_b
    for k in range(4):
        xc = xc + conv_w[k:k + 1] * full[5 + k:5 + k + t]
    r = jax.nn.sigmoid(_dot(xc, wa) + ba)
    ig = jax.nn.sigmoid(_dot(xc, wx) + bx)
    log_a = -LRU_C * r * sp
    a = jnp.exp(log_a)
    one_minus_a2 = -jnp.tanh(log_a) * (a * a + 1.0)
    u = jnp.sqrt(one_minus_a2) * (ig * xc)
    h = _real_scan(a, u, tail_h[7:8])
    return (xi[t - 8:], h[t - 8:]), (h * jax.nn.gelu(xg),), ()


def fn_s5(st, rows, params):
    tail_r, tail_i = st
    (u,) = rows
    bt_re, bt_im, lb_re, lb_im, ct_re, ct_im, d, glu_w, glu_b = params
    t = u.shape[0]
    bu_re = _dot(u, bt_re)
    bu_im = _dot(u, bt_im)
    h_re, h_im = _complex_scan(lb_re, lb_im, bu_re, bu_im, tail_r[7:8], tail_i[7:8])
    y = _dot(h_re, ct_re) - _dot(h_im, ct_im) + d * u
    act = jax.nn.gelu(y)
    out = act * jax.nn.sigmoid(_dot(act, glu_w) + glu_b)
    return (h_re[t - 8:], h_im[t - 8:]), (out,), ()


def fn_merge(st, rows, params):
    ya, yb, yc, yd, gl, x = rows
    w0, w1, w2, w3, b_gate, w_out, g_post = params
    d = x.shape[1]
    mix = None
    for n, (y, w) in enumerate(((ya, w0), (yb, w1), (yc, w2), (yd, w3))):
        gate = jax.nn.sigmoid(gl[:, n * d:(n + 1) * d] + b_gate[:, n * d:(n + 1) * d])
        term = gate * _dot(y, w)
        mix = term if mix is None else mix + term
    return (), (x + _rms(_dot(mix, w_out), g_post),), ()


def fn_mem(st, rows, params):
    (mem,), (g, wk, wv) = rows, params
    m = _rms(mem, g)
    return (), (_dot(m, wk), _dot(m, wv)), ()


def fn_xattn(st, rows, params):
    (x,) = rows
    g_pre, wq, k, v, wo, g_post, g_next = params
    d = x.shape[1]
    dh = d // XA_HEADS
    q = _dot(_rms(x, g_pre), wq)
    heads = []
    for h in range(XA_HEADS):
        sl = slice(h * dh, (h + 1) * dh)
        s = _dot_nt(q[:, sl], k[:, sl]) * (dh ** -0.5)
        heads.append(_dot(jax.nn.softmax(s, axis=-1), v[:, sl]))
    x2 = x + _rms(_dot(jnp.concatenate(heads, axis=1), wo), g_post)
    return (), (x2, _rms(x2, g_next)), ()


def fn_res_norm(st, rows, params):
    (x, o), (g_post, g_next) = rows, params
    xn = x + _rms(o, g_post)
    return (), (xn, _rms(xn, g_next)), ()


def fn_res(st, rows, params):
    (x, o), (g_post,) = rows, params
    return (), (x + _rms(o, g_post),), ()


def fn_res_loss(st, rows, params):
    (x, o, target), (g_post,) = rows, params
    err = x + _rms(o, g_post) - target
    inv_d = 1.0 / x.shape[1]
    loss = 0.5 * inv_d * jnp.sum(err * err)
    return (), (err * inv_d,), (jnp.full((8, LANES), loss, F32),)


def _params():
    return pltpu.CompilerParams(dimension_semantics=("arbitrary",), vmem_limit_bytes=VMEM_LIMIT)


def _row_spec(tb, width, colblk, nb, reverse):
    if reverse:
        return pl.BlockSpec((tb, width), lambda i: (nb - 1 - i, colblk))
    return pl.BlockSpec((tb, width), lambda i: (i, colblk))


def _full_spec(shape):
    return pl.BlockSpec(shape, lambda i: (0,) * len(shape))


def _saved_spec(shape, nb, reverse):
    if reverse:
        return pl.BlockSpec((1,) + shape, lambda i: (nb - 1 - i, 0, 0))
    return pl.BlockSpec((1,) + shape, lambda i: (i, 0, 0))


def _param_value(ref):
    v = ref[...]
    return v if v.dtype == BF16 else v.astype(F32)


def _side(side):
    arrays, gather = side if side is not None else ((), False)
    return list(arrays), gather, len(arrays)


_ANY = pl.BlockSpec(memory_space=pl.ANY)


def stage_fwd(name, fn, rows, params, *, tb, outs, states=(), accs=(), side=None):
    n_rows = rows[0][0].shape[0]
    nb = n_rows // tb
    nr, npar, no, na, ns = len(rows), len(params), len(outs), len(accs), len(states)
    side_arrays, gather, nx = _side(side)

    def body(*refs):
        row_refs, par_refs, side_in = refs[:nr], refs[nr:nr + npar], refs[nr + npar:nr + npar + nx]
        o = nr + npar + nx
        out_refs, acc_refs = refs[o:o + no], refs[o + no:o + no + na]
        saved_refs = refs[o + no + na:o + no + na + ns]
        o = o + no + na + ns
        side_out, st_refs, sems = refs[o:o + nx], refs[o + nx:o + nx + ns], refs[o + nx + ns:]
        i = pl.program_id(0)
        if nx:
            start, wait = _exchange_copies(side_in, side_out, *sems, gather)
            pl.when(i == 0)(start)

        @pl.when(i == 0)
        def _():
            for r in st_refs + acc_refs:
                r[...] = jnp.zeros_like(r)

        st = tuple(r[...] for r in st_refs)
        for sv, s in zip(saved_refs, st):
            sv[0] = s
        new_st, out_vals, acc_vals = fn(st, tuple(r[...].astype(F32) for r in row_refs),
                                        tuple(_param_value(r) for r in par_refs))
        for r, v in zip(out_refs, out_vals):
            r[...] = v.astype(r.dtype)
        for r, v in zip(acc_refs, acc_vals):
            r[...] += v
        for r, v in zip(st_refs, new_st):
            r[...] = v
        if nx:
            pl.when(i == nb - 1)(wait)

    res = pl.pallas_call(
        body, name=name, grid=(nb,),
        in_specs=[_row_spec(tb, w, cb, nb, False) for _, w, cb in rows] + [_full_spec(p.shape) for p in params]
        + [_ANY] * nx,
        out_specs=[_row_spec(tb, w, 0, nb, False) for w, _ in outs] + [_full_spec(s) for s in accs]
        + [_saved_spec(s, nb, False) for s in states] + [_ANY] * nx,
        out_shape=[jax.ShapeDtypeStruct((n_rows, w), dt) for w, dt in outs]
        + [jax.ShapeDtypeStruct(s, F32) for s in accs] + [jax.ShapeDtypeStruct((nb,) + s, F32) for s in states]
        + _exchange_shapes(side_arrays, gather),
        scratch_shapes=[pltpu.VMEM(s, F32) for s in states] + (_exchange_sems(nx) if nx else []),
        compiler_params=_params(),
    )(*[a for a, _, _ in rows], *params, *side_arrays)
    base = (res[:no], res[no:no + na], res[no + na:no + na + ns])
    return base + (res[no + na + ns:],) if nx else base


def stage_bwd(name, fn, rows, params, cts, *, tb, saved=(), diff_rows=(), diff_params=(), row_dtypes=None, side=None):
    n_rows = rows[0][0].shape[0]
    nb = n_rows // tb
    nr, npar, ns, nc = len(rows), len(params), len(saved), len(cts)
    ndr, ndp = len(diff_rows), len(diff_params)
    row_dtypes = row_dtypes or (F32,) * ndr
    state_shapes = [s.shape[1:] for s in saved]
    side_arrays, gather, nx = _side(side)

    def body(*refs):
        row_refs, par_refs = refs[:nr], refs[nr:nr + npar]
        o = nr + npar
        saved_refs, ct_refs, side_in = refs[o:o + ns], refs[o + ns:o + ns + nc], refs[o + ns + nc:o + ns + nc + nx]
        o = o + ns + nc + nx
        drow_refs, dpar_refs, side_out = refs[o:o + ndr], refs[o + ndr:o + ndr + ndp], refs[o + ndr + ndp:o + ndr + ndp + nx]
        o = o + ndr + ndp + nx
        dst_refs, sems = refs[o:o + ns], refs[o + ns:]
        i = pl.program_id(0)
        if nx:
            start, wait = _exchange_copies(side_in, side_out, *sems, gather)
            pl.when(i == 0)(start)

        @pl.when(i == 0)
        def _():
            for r in dst_refs + dpar_refs:
                r[...] = jnp.zeros_like(r)

        st = tuple(r[0] for r in saved_refs)
        row_vals = [r[...].astype(F32) for r in row_refs]
        par_vals = [_param_value(r) for r in par_refs]

        def f(st_, dr_, dp_):
            rv, pv = list(row_vals), list(par_vals)
            for k, v in zip(diff_rows, dr_):
                rv[k] = v
            for k, v in zip(diff_params, dp_):
                pv[k] = v
            new_st, out_vals, _ = fn(st_, tuple(rv), tuple(pv))
            return new_st, out_vals

        _, vjp = jax.vjp(f, st, tuple(row_vals[k] for k in diff_rows), tuple(par_vals[k] for k in diff_params))
        g_st, g_rows, g_par = vjp((tuple(r[...] for r in dst_refs), tuple(r[...].astype(F32) for r in ct_refs)))
        for r, v in zip(drow_refs, g_rows):
            r[...] = v.astype(r.dtype)
        for r, v in zip(dpar_refs, g_par):
            r[...] += v
        for r, v in zip(dst_refs, g_st):
            r[...] = v
        if nx:
            pl.when(i == nb - 1)(wait)

    res = pl.pallas_call(
        body, name=name, grid=(nb,),
        in_specs=[_row_spec(tb, w, cb, nb, True) for _, w, cb in rows] + [_full_spec(p.shape) for p in params]
        + [_saved_spec(s, nb, True) for s in state_shapes] + [_row_spec(tb, c.shape[1], 0, nb, True) for c in cts]
        + [_ANY] * nx,
        out_specs=[_row_spec(tb, rows[k][1], 0, nb, True) for k in diff_rows]
        + [_full_spec(params[k].shape) for k in diff_params] + [_ANY] * nx,
        out_shape=[jax.ShapeDtypeStruct((n_rows, rows[k][1]), dt) for k, dt in zip(diff_rows, row_dtypes)]
        + [jax.ShapeDtypeStruct(params[k].shape, F32) for k in diff_params] + _exchange_shapes(side_arrays, gather),
        scratch_shapes=[pltpu.VMEM(s, F32) for s in state_shapes] + (_exchange_sems(nx) if nx else []),
        compiler_params=_params(),
    )(*[a for a, _, _ in rows], *params, *saved, *cts, *side_arrays)
    base = (res[:ndr], res[ndr:ndr + ndp])
    return base + (res[ndr + ndp:],) if nx else base


def _pick(n, target):
    if n <= target:
        return n
    best = None
    for t in range(LANES, target + 1, LANES):
        if n % t == 0:
            best = t
    assert best is not None, n
    return best


def _mesh_position():
    return lax.axis_index("x"), lax.axis_index("y"), lax.axis_index("c")


def _peer(pos, k):
    x, y, c = pos
    px = 1 - x if k & 4 else x
    py = 1 - y if k & 2 else y
    pc = 1 - c if k & 1 else c
    return (px, py, pc), 4 * px + 2 * py + pc


def _exchange_copies(x_refs, o_refs, send_sems, recv_sems, local_sems, gather):
    pos = _mesh_position()
    me = 4 * pos[0] + 2 * pos[1] + pos[2]
    pairs = list(enumerate(zip(x_refs, o_refs)))

    def remote(k, a, src, dst):
        peer, _ = _peer(pos, k)
        return pltpu.make_async_remote_copy(src_ref=src, dst_ref=dst, send_sem=send_sems.at[k - 1, a],
                                            recv_sem=recv_sems.at[k - 1, a], device_id=peer,
                                            device_id_type=pl.DeviceIdType.MESH)

    def local(a, x, o):
        return pltpu.make_async_copy(x if gather else x.at[me], o.at[me], local_sems.at[a])

    def start():
        for a, (x, o) in pairs:
            local(a, x, o).start()
        for k in range(1, N_DEV):
            peer_idx = _peer(pos, k)[1]
            for a, (x, o) in pairs:
                remote(k, a, x if gather else x.at[peer_idx], o.at[me]).start()

    def wait():
        for k in range(1, N_DEV):
            peer_idx = _peer(pos, k)[1]
            for a, (x, o) in pairs:
                arrival = remote(k, a, x if gather else x.at[me], o.at[peer_idx])
                arrival.wait_recv()
                arrival.wait_send()
        for a, (x, o) in pairs:
            local(a, x, o).wait()

    return start, wait


def _exchange_shapes(arrays, gather):
    return [jax.ShapeDtypeStruct(((N_DEV,) + x.shape) if gather else x.shape, x.dtype) for x in arrays]


def _exchange_sems(n):
    return [pltpu.SemaphoreType.DMA((N_DEV - 1, n)), pltpu.SemaphoreType.DMA((N_DEV - 1, n)),
            pltpu.SemaphoreType.DMA((n,))]


def _exchange(name, arrays, gather):
    n = len(arrays)

    def body(*refs):
        start, wait = _exchange_copies(refs[:n], refs[n:2 * n], *refs[2 * n:], gather)
        start()
        wait()

    return pl.pallas_call(
        body, name=name,
        in_specs=[pl.BlockSpec(memory_space=pl.ANY)] * n, out_specs=[pl.BlockSpec(memory_space=pl.ANY)] * n,
        out_shape=_exchange_shapes(arrays, gather), scratch_shapes=_exchange_sems(n),
    )(*arrays)


def matmul(name, a, b, mode, out_dtype, add=None, side=None):
    if mode == "tn":
        k, m = a.shape
    else:
        m, k = a.shape
    n = b.shape[0] if mode == "nt" else b.shape[1]
    if mode == "tn":
        tm, tn, tk = _pick(m, 1408), _pick(n, 1408), _pick(k, 1024)
    else:
        tm, tn, tk = _pick(m, 1024), _pick(n, 512), _pick(k, 2816)
    nk = k // tk
    grid = (m // tm, n // tn, nk)
    a_spec = pl.BlockSpec((tk, tm), lambda i, j, kk: (kk, i)) if mode == "tn" else pl.BlockSpec((tm, tk), lambda i, j, kk: (i, kk))
    b_spec = pl.BlockSpec((tn, tk), lambda i, j, kk: (j, kk)) if mode == "nt" else pl.BlockSpec((tk, tn), lambda i, j, kk: (kk, j))
    o_spec = pl.BlockSpec((tm, tn), lambda i, j, kk: (i, j))
    dims = {"nn": (((1,), (0,)), ((), ())), "nt": (((1,), (1,)), ((), ())), "tn": (((0,), (0,)), ((), ()))}[mode]
    has_add = add is not None
    side_arrays, gather = side if side is not None else ((), False)
    ns = len(side_arrays)
    n_in = 2 + has_add

    def body(*refs):
        a_ref, b_ref = refs[0], refs[1]
        side_in, o_ref, side_out = refs[n_in:n_in + ns], refs[n_in + ns], refs[n_in + ns + 1:n_in + 2 * ns + 1]
        scratch = refs[n_in + 2 * ns + 1:]
        ids = [pl.program_id(d) for d in range(3)]
        if ns:
            start, wait = _exchange_copies(side_in, side_out, *scratch[-3:], gather)
            pl.when((ids[0] == 0) & (ids[1] == 0) & (ids[2] == 0))(start)
        part = lax.dot_general(a_ref[...].astype(BF16), b_ref[...].astype(BF16), dims, preferred_element_type=F32)
        if nk == 1:
            o_ref[...] = (part + refs[2][...].astype(F32) if has_add else part).astype(o_ref.dtype)
        else:
            acc_ref = scratch[0]

            @pl.when(ids[2] == 0)
            def _():
                acc_ref[...] = part + refs[2][...].astype(F32) if has_add else part

            @pl.when(ids[2] > 0)
            def _():
                acc_ref[...] += part

            @pl.when(ids[2] == nk - 1)
            def _():
                o_ref[...] = acc_ref[...].astype(o_ref.dtype)
        if ns:
            pl.when((ids[0] == grid[0] - 1) & (ids[1] == grid[1] - 1) & (ids[2] == nk - 1))(wait)

    any_spec = pl.BlockSpec(memory_space=pl.ANY)
    res = pl.pallas_call(
        body, name=name, grid=grid,
        in_specs=[a_spec, b_spec] + ([o_spec] if has_add else []) + [any_spec] * ns,
        out_specs=[o_spec] + [any_spec] * ns,
        out_shape=[jax.ShapeDtypeStruct((m, n), out_dtype)] + _exchange_shapes(side_arrays, gather),
        scratch_shapes=([pltpu.VMEM((tm, tn), F32)] if nk > 1 else []) + (_exchange_sems(ns) if ns else []),
        compiler_params=pltpu.CompilerParams(
            dimension_semantics=("arbitrary",) * 3 if ns else ("parallel", "parallel", "arbitrary"),
            vmem_limit_bytes=VMEM_LIMIT),
    )(a, b, *([add] if has_add else []), *side_arrays)
    return (res[0], res[1:]) if ns else res[0]


def matmul_fused(name, a, bs, mode, n, extras, epilogue, out_dtypes):
    m, k = a.shape
    tm, tn = _pick(m, 1024), _pick(n, 512)
    dims = {"nn": (((1,), (0,)), ((), ())), "nt": (((1,), (1,)), ((), ()))}[mode]
    nb, nx = len(bs), len(extras)

    def b_spec(off):
        if mode == "nt":
            return pl.BlockSpec((tn, k), lambda i, j: (j + off, 0))
        return pl.BlockSpec((k, tn), lambda i, j: (0, j + off))

    def body(*refs):
        a_val = refs[0][...].astype(BF16)
        parts = tuple(lax.dot_general(a_val, r[...].astype(BF16), dims, preferred_element_type=F32)
                      for r in refs[1:1 + nb])
        tiles = tuple(r[...].astype(F32) for r in refs[1 + nb:1 + nb + nx])
        for r, v in zip(refs[1 + nb + nx:], epilogue(parts, tiles)):
            r[...] = v.astype(r.dtype)

    tile = pl.BlockSpec((tm, tn), lambda i, j: (i, j))
    return pl.pallas_call(
        body, name=name, grid=(m // tm, n // tn),
        in_specs=[pl.BlockSpec((tm, k), lambda i, j: (i, 0))] + [b_spec(off) for _, off in bs] + [tile] * nx,
        out_specs=[tile] * len(out_dtypes), out_shape=[jax.ShapeDtypeStruct((m, n), dt) for dt in out_dtypes],
        compiler_params=pltpu.CompilerParams(dimension_semantics=("parallel", "parallel"),
                                             vmem_limit_bytes=VMEM_LIMIT),
    )(a, *[b for b, _ in bs], *extras)


def _glu_fwd_tiles(parts, tiles):
    gate, up = parts
    return gate, up, jax.nn.silu(gate) * up


def _glu_bwd_tiles(parts, tiles):
    (da,), (gate, up) = parts, tiles
    _, vjp = jax.vjp(lambda g, u: jax.nn.silu(g) * u, gate, up)
    return vjp(da)


def adamw(name, w, m, v, gparts):
    layers, rows, cols = w.shape
    parts = gparts[0].shape[0]
    tr = 8
    while tr * 2 * cols <= 65536 and rows % (tr * 2) == 0:
        tr *= 2
    nblk = rows // tr
    c1 = 1.0 - ADAM_B1 ** ADAM_STEP
    c2 = 1.0 - ADAM_B2 ** ADAM_STEP

    def body(*refs):
        w_ref, m_ref, v_ref = refs[:3]
        g_refs = refs[3:3 + layers]
        go_ref, d_ref, mo_ref, vo_ref = refs[3 + layers:]
        layer = pl.program_id(0)
        g = None
        for ll, g_ref in enumerate(g_refs):
            s = g_ref[0].astype(F32)
            for p in range(1, parts):
                s = s + g_ref[p].astype(F32)
            g = s if g is None else jnp.where(layer == ll, s, g)
        m_new = ADAM_B1 * m_ref[...] + (1.0 - ADAM_B1) * g
        v_new = ADAM_B2 * v_ref[...] + (1.0 - ADAM_B2) * (g * g)
        m_hat = m_new / c1
        v_hat = v_new / c2
        go_ref[...] = g
        d_ref[...] = -ADAM_LR * (m_hat / (jnp.sqrt(v_hat) + ADAM_EPS) + ADAM_WD * w_ref[...])
        mo_ref[...] = m_new
        vo_ref[...] = v_new

    def part_spec(ll):
        return pl.BlockSpec((parts, tr, cols),
                            lambda l, i: (0, jnp.where(l == ll, i, jnp.where(l < ll, 0, nblk - 1)), 0))

    spec = pl.BlockSpec((None, tr, cols), lambda l, i: (l, i, 0))
    return pl.pallas_call(
        body, name=name, grid=(layers, nblk),
        in_specs=[spec, spec, spec] + [part_spec(ll) for ll in range(layers)],
        out_specs=[spec] * 4, out_shape=[jax.ShapeDtypeStruct(w.shape, F32)] * 4,
        compiler_params=pltpu.CompilerParams(dimension_semantics=("arbitrary", "arbitrary"),
                                             vmem_limit_bytes=VMEM_LIMIT),
    )(w, m, v, *gparts)


def _pack(arrays, dtype, row_multiple):
    flat = jnp.concatenate([a.astype(dtype).reshape(-1) for a in arrays])
    unit = row_multiple * LANES
    pad = (-flat.shape[0]) % unit
    if pad:
        flat = jnp.concatenate([flat, jnp.zeros((pad,), dtype)])
    return flat.reshape(-1, LANES)


def _unpack(flat2d, shapes):
    flat = flat2d.reshape(-1)
    out, off = [], 0
    for s in shapes:
        n = int(np.prod(s))
        out.append(flat[off:off + n].reshape(s))
        off += n
    return out


def _unpack_stacked(stacked, shapes):
    flat = stacked.reshape(N_DEV, -1)
    out, off = [], 0
    for s in shapes:
        n = int(np.prod(s))
        out.append(flat[:, off:off + n].reshape((N_DEV,) + tuple(s)))
        off += n
    return out


def _merge_shards(stacked, axis):
    t = jnp.moveaxis(stacked, 0, axis)
    s = t.shape
    return t.reshape(s[:axis] + (s[axis] * s[axis + 1],) + s[axis + 2:])


def _split_shards(full, axis):
    s = full.shape
    t = full.reshape(s[:axis] + (N_DEV, s[axis] // N_DEV) + s[axis + 1:])
    return jnp.moveaxis(t, axis, 0)


def _lower_bounds(hg_lower_bounds):
    p = jax.nn.softmax(hg_lower_bounds, axis=0)
    return jnp.cumsum(p, axis=0) - p[0:1]


def _prep_layer(p):
    def row(v):
        return v.reshape(1, -1)

    eye_b = jnp.eye(HEADS, dtype=F32)
    eye_g = jnp.eye(S5_GROUPS, dtype=F32)
    step = jnp.exp(p["s5_log_dt"])[:, None]
    lam_re, lam_im = p["s5_lam_re"], p["s5_lam_im"]
    mag = jnp.exp(lam_re * step)
    lb_re = mag * jnp.cos(lam_im * step)
    lb_im = mag * jnp.sin(lam_im * step)
    den = lam_re * lam_re + lam_im * lam_im
    f_re = ((lb_re - 1.0) * lam_re + lb_im * lam_im) / den
    f_im = (lb_im * lam_re - (lb_re - 1.0) * lam_im) / den
    bb_re = f_re[..., None] * p["s5_b_re"] - f_im[..., None] * p["s5_b_im"]
    bb_im = f_re[..., None] * p["s5_b_im"] + f_im[..., None] * p["s5_b_re"]
    width = S5_GROUPS * S5_GROUP
    n_state = S5_GROUPS * S5_STATE
    return dict(
        lb=row(p["lb"]), hg_norm=row(p["hg_norm"]), ret_norm=row(p["ret_norm"]),
        conv_w=p["lru_conv_w"], conv_b=row(p["lru_conv_b"]),
        wa=jnp.einsum("nij,nm->nimj", p["lru_wa"], eye_b).reshape(MIX_W, MIX_W).astype(BF16), ba=row(p["lru_ba"]),
        wx=jnp.einsum("nij,nm->nimj", p["lru_wx"], eye_b).reshape(MIX_W, MIX_W).astype(BF16), bx=row(p["lru_bx"]),
        sp=row(jax.nn.softplus(-p["lru_lambda"])),
        bt_re=jnp.einsum("gnp,gh->gphn", bb_re, eye_g).reshape(width, n_state).astype(BF16),
        bt_im=jnp.einsum("gnp,gh->gphn", bb_im, eye_g).reshape(width, n_state).astype(BF16),
        lb_re=row(lb_re), lb_im=row(lb_im),
        ct_re=jnp.einsum("gpn,gh->gnhp", p["s5_c_re"], eye_g).reshape(n_state, width).astype(BF16),
        ct_im=jnp.einsum("gpn,gh->gnhp", p["s5_c_im"], eye_g).reshape(n_state, width).astype(BF16),
        s5_d=row(p["s5_d"]), glu_w=p["s5_glu_w"].astype(BF16), glu_b=row(p["s5_glu_b"]),
        b_gate=row(p["b_gate"]),
        norm_mix_pre=row(p["norm_mix_pre"]), norm_mix_post=row(p["norm_mix_post"]),
        norm_xa_pre=row(p["norm_xa_pre"]), norm_xa_post=row(p["norm_xa_post"]), norm_mem=row(p["norm_mem"]),
        norm_ffn_pre=row(p["norm_ffn_pre"]), norm_ffn_post=row(p["norm_ffn_post"]),
    )


_PREP_INPUTS = ("hg_norm", "ret_norm", "lru_conv_w", "lru_conv_b", "lru_wa", "lru_ba", "lru_wx", "lru_bx", "lru_lambda",
                "s5_lam_re", "s5_lam_im", "s5_b_re", "s5_b_im", "s5_c_re", "s5_c_im", "s5_d", "s5_log_dt", "s5_glu_w",
                "s5_glu_b", "b_gate", "norm_mix_pre", "norm_mix_post", "norm_xa_pre", "norm_xa_post", "norm_mem",
                "norm_ffn_pre", "norm_ffn_post")


def _retention_constants():
    lg = np.log1p(-np.power(2.0, -5.0 - np.arange(HEADS)))
    idx = np.arange(RET_CHUNK)

    def lanes(per_head_rows):
        return np.repeat(per_head_rows.T[:, :, None], HEAD_DIM, axis=2).reshape(RET_CHUNK, MIX_W)

    xi = lanes(np.exp((idx + 1.0)[None, :] * lg[:, None]))
    zeta = lanes(np.exp((RET_CHUNK - 1.0 - idx)[None, :] * lg[:, None]))
    rel = idx[:, None] - idx[None, :]
    decay = np.where(rel[None] >= 0, np.exp(np.maximum(rel, 0)[None] * lg[:, None, None]), 0.0)
    decay = np.transpose(decay, (1, 0, 2)).reshape(RET_CHUNK, HEADS * RET_CHUNK)
    g_end = np.repeat(np.exp(RET_CHUNK * lg), HEAD_DIM)[None, :]
    return tuple(jnp.asarray(a, F32) for a in (xi, zeta, decay, g_end))


def _rotary_tables(seq):
    pos = jnp.arange(seq, dtype=F32)
    inv_freq = 10000.0 ** (-jnp.arange(0, HEAD_DIM, 2, dtype=F32) / HEAD_DIM)
    ang = pos[:, None] * inv_freq[None, :]
    cos, sin = jnp.cos(ang), jnp.sin(ang)
    cos_t = jnp.tile(jnp.repeat(cos, 2, axis=1), (1, HEADS))
    sin_t = jnp.tile(jnp.stack([-sin, sin], axis=-1).reshape(seq, HEAD_DIM), (1, HEADS))
    return cos_t, sin_t


TB_HG = 512
TB_RET = 512
TB_LRU = 256
TB_S5 = 256
TB_ROW = 256
TB_XA = 512

_STATE = (MIX_W, MIX_W)
_TAIL = (8, MIX_W)
_S5_TAIL = (8, S5_GROUPS * S5_STATE)


def _mixer_operands(l, z, kp, rot, ret_c):
    xi, zeta, decay, g_end = ret_c
    return dict(
        hg=dict(name=f"hgrn2_{l}", fn=fn_hgrn2, rows=[(z, 4 * MIX_W, 0)], params=[kp["lb"], kp["hg_norm"]],
                tb=TB_HG, states=(_STATE,), diff_params=(0, 1)),
        ret=dict(name=f"retention_{l}", fn=fn_retention, rows=[(z, 4 * MIX_W, 1), (rot[0], MIX_W, 0), (rot[1], MIX_W, 0)],
                 params=[kp["ret_norm"], xi, zeta, decay, g_end], tb=TB_RET, states=(_STATE,), diff_params=(0,)),
        lru=dict(name=f"rglru_{l}", fn=fn_rglru, rows=[(z, 2 * MIX_W, 4)],
                 params=[kp["conv_w"], kp["conv_b"], kp["wa"], kp["ba"], kp["wx"], kp["bx"], kp["sp"]],
                 tb=TB_LRU, states=(_TAIL, _TAIL), diff_params=(0, 1, 2, 3, 4, 5, 6)),
        s5=dict(name=f"s5_{l}", fn=fn_s5, rows=[(z, MIX_W, 10)],
                params=[kp["bt_re"], kp["bt_im"], kp["lb_re"], kp["lb_im"], kp["ct_re"], kp["ct_im"], kp["s5_d"],
                        kp["glu_w"], kp["glu_b"]],
                tb=TB_S5, states=(_S5_TAIL, _S5_TAIL), diff_params=tuple(range(9))),
    )


def _layer_forward(l, x, h, mem, w_in, shards, next_w_in_shard, kp, rot, ret_c, g_next, target):
    d = x.shape[1]
    sv = dict(x=x, h=h)
    bw = {"w_in": w_in}

    def gather(idx):
        return [shards[i] for i in idx], True

    def take(idx, stacked):
        for i, s in zip(idx, stacked):
            bw[_BIG_NAMES[i]] = _merge_shards(s, BIG[i][1] - 1)

    z, got = matmul(f"in_proj_{l}", h, w_in, "nn", F32, side=gather(GATHER_IN["in_proj"]))
    take(GATHER_IN["in_proj"], got)
    gl, got = matmul(f"gate_proj_{l}", h, bw["w_gate"], "nn", BF16, side=gather(GATHER_IN["gate_proj"]))
    take(GATHER_IN["gate_proj"], got)
    sv.update(z=z, gl=gl)
    ops = _mixer_operands(l, z, kp, rot, ret_c)
    ys = []
    for key in ("hg", "ret", "lru", "s5"):
        o = ops[key]
        idx = GATHER_IN.get(key)
        res = stage_fwd(o["name"] + "_fwd", o["fn"], o["rows"], o["params"], tb=o["tb"], outs=[(MIX_W, F32)],
                        states=o["states"], side=gather(idx) if idx else None)
        if idx:
            take(idx, res[3])
        ys.append(res[0][0])
        sv[key + "_states"] = res[2]
    sv["ys"] = ys
    merge_params = [bw["w_up"][n] for n in range(4)] + [kp["b_gate"], bw["w_out"], kp["norm_mix_post"]]
    merge_rows = [(y, MIX_W, 0) for y in ys] + [(gl, 4 * d, 0), (x, d, 0)]
    (x1,), _, _, got = stage_fwd(f"merge_{l}_fwd", fn_merge, merge_rows, merge_params, tb=TB_ROW, outs=[(d, F32)],
                                 side=gather(GATHER_IN["merge"]))
    take(GATHER_IN["merge"], got)
    wk, wv = bw["xa_w_kv"][:, :d], bw["xa_w_kv"][:, d:]
    mem_params = [kp["norm_mem"], wk, wv]
    (k, v), _, _ = stage_fwd(f"mem_{l}_fwd", fn_mem, [(mem, d, 0)], mem_params, tb=mem.shape[0],
                             outs=[(d, BF16), (d, BF16)])
    xa_params = [kp["norm_xa_pre"], bw["xa_w_q"], k, v, bw["xa_w_o"], kp["norm_xa_post"], kp["norm_ffn_pre"]]
    res = stage_fwd(f"xattn_{l}_fwd", fn_xattn, [(x1, d, 0)], xa_params, tb=TB_XA, outs=[(d, F32), (d, BF16)],
                    side=([next_w_in_shard], True) if next_w_in_shard is not None else None)
    x2, h3 = res[0]
    next_w_in = _merge_shards(res[3][0], BIG[0][1] - 1) if next_w_in_shard is not None else None
    f = bw["ffn_w_gu"].shape[1] // 2
    up_block = f // _pick(f, 512)
    gate, up, a = matmul_fused(f"ffn_gu_{l}", h3, [(bw["ffn_w_gu"], 0), (bw["ffn_w_gu"], up_block)], "nn", f, [],
                               _glu_fwd_tiles, (BF16, BF16, BF16))
    o3 = matmul(f"ffn_down_{l}", a, bw["ffn_w_down"], "nn", F32)
    sv.update(x1=x1, k=k, v=v, x2=x2, h3=h3, gate=gate, up=up, a=a, o3=o3, merge_params=merge_params,
              merge_rows=merge_rows, mem_params=mem_params, xa_params=xa_params)
    if g_next is not None:
        (x3, hn), _, _ = stage_fwd(f"res_{l}_fwd", fn_res_norm, [(x2, d, 0), (o3, d, 0)], [kp["norm_ffn_post"], g_next],
                                   tb=TB_ROW, outs=[(d, F32), (d, BF16)])
        return x3, hn, sv, bw, next_w_in
    (dy,), (loss,), _ = stage_fwd(f"loss_{l}_fwd", fn_res_loss, [(x2, d, 0), (o3, d, 0), (target, d, 0)],
                                  [kp["norm_ffn_post"]], tb=TB_ROW, outs=[(d, F32)], accs=[(8, LANES)])
    return dy, loss[0, 0], sv, bw, next_w_in


def _layer_backward(l, sv, mem, bw, kp, rot, ret_c, g_next, dx3, dhn):
    d = sv["x"].shape[1]
    gk = {}
    parts = [None] * len(BIG)

    def scatter(*items):
        return [_split_shards(g.astype(BF16), BIG[i][1] - 1) for i, g in items], False

    def took(got, *idx):
        for i, p in zip(idx, got):
            parts[i] = p

    res_rows = [(sv["x2"], d, 0), (sv["o3"], d, 0)]
    if g_next is not None:
        (dx2, do3), (gk["norm_ffn_post"], gk["g_next"]) = stage_bwd(
            f"res_{l}_bwd", fn_res_norm, res_rows, [kp["norm_ffn_post"], g_next], [dx3, dhn], tb=TB_ROW,
            diff_rows=(0, 1), diff_params=(0, 1))
    else:
        (dx2, do3), (gk["norm_ffn_post"],) = stage_bwd(
            f"res_{l}_bwd", fn_res, res_rows, [kp["norm_ffn_post"]], [dx3], tb=TB_ROW, diff_rows=(0, 1), diff_params=(0,))
    f = sv["a"].shape[1]
    g_down = matmul(f"ffn_down_dw_{l}", sv["a"], do3, "tn", BF16)
    d_gate, d_up = matmul_fused(f"ffn_down_dx_{l}", do3, [(bw["ffn_w_down"], 0)], "nt", f, [sv["gate"], sv["up"]],
                                _glu_bwd_tiles, (BF16, BF16))
    dh3, got = matmul(f"ffn_gate_dx_{l}", d_gate, bw["ffn_w_gu"][:, :f], "nt", F32, side=scatter((8, g_down)))
    took(got, 8)
    dh3 = matmul(f"ffn_up_dx_{l}", d_up, bw["ffn_w_gu"][:, f:], "nt", F32, add=dh3)
    g_gu = jnp.concatenate([matmul(f"ffn_gate_dw_{l}", sv["h3"], d_gate, "tn", BF16),
                            matmul(f"ffn_up_dw_{l}", sv["h3"], d_up, "tn", BF16)], axis=1)
    (dx1,), xa_g, got = stage_bwd(f"xattn_{l}_bwd", fn_xattn, [(sv["x1"], d, 0)], sv["xa_params"], [dx2, dh3],
                                  tb=TB_XA, diff_rows=(0,), diff_params=tuple(range(7)), side=scatter((7, g_gu)))
    took(got, 7)
    gk["norm_xa_pre"], g_q, dk, dv, g_o, gk["norm_xa_post"], gk["norm_ffn_pre"] = xa_g
    _, (gk["norm_mem"], dwk, dwv) = stage_bwd(f"mem_{l}_bwd", fn_mem, [(mem, d, 0)], sv["mem_params"], [dk, dv],
                                              tb=mem.shape[0], diff_params=(0, 1, 2))
    g_kv = jnp.concatenate([dwk, dwv], axis=1)
    merge_d, merge_g, got = stage_bwd(f"merge_{l}_bwd", fn_merge, sv["merge_rows"], sv["merge_params"], [dx1],
                                      tb=TB_ROW, diff_rows=tuple(range(6)), diff_params=tuple(range(7)),
                                      row_dtypes=(F32, F32, F32, F32, BF16, F32),
                                      side=scatter((4, g_q), (6, g_o), (5, g_kv)))
    took(got, 4, 6, 5)
    dys, dgl, dx0 = merge_d[:4], merge_d[4], merge_d[5]
    g_up = jnp.stack(merge_g[:4])
    gk["b_gate"], g_out, gk["norm_mix_post"] = merge_g[4:]
    ops = _mixer_operands(l, sv["z"], kp, rot, ret_c)
    dz = {}
    for key, dy in zip(("hg", "ret", "lru", "s5"), dys):
        o = ops[key]
        res = stage_bwd(o["name"] + "_bwd", o["fn"], o["rows"], o["params"], [dy], tb=o["tb"],
                        saved=sv[key + "_states"], diff_rows=(0,), diff_params=o["diff_params"], row_dtypes=(BF16,),
                        side=scatter((2, g_up), (3, g_out)) if key == "hg" else None)
        if key == "hg":
            took(res[2], 2, 3)
        dz[key] = res[0][0]
        gk[key] = res[1]
    g_gate = matmul(f"gate_proj_dw_{l}", sv["h"], dgl, "tn", BF16)
    keys = ("hg", "ret", "lru", "s5")
    g_in = jnp.concatenate([matmul(f"in_proj_dw_{key}_{l}", sv["h"], dz[key], "tn", BF16) for key in keys], axis=1)
    dh, got = matmul(f"gate_proj_dx_{l}", dgl, bw["w_gate"], "nt", F32, side=scatter((1, g_gate)))
    took(got, 1)
    col = 0
    for key in keys:
        width = dz[key].shape[1]
        w_cols = bw["w_in"][:, col:col + width]
        if key == "hg":
            dh, got = matmul(f"in_proj_dx_{key}_{l}", dz[key], w_cols, "nt", F32, add=dh, side=scatter((0, g_in)))
            took(got, 0)
        else:
            dh = matmul(f"in_proj_dx_{key}_{l}", dz[key], w_cols, "nt", F32, add=dh)
        col += width
    return dx0, dh, gk, parts


def _kernel_grads_to_prep(gk):
    hg, ret, lru, s5 = gk["hg"], gk["ret"], gk["lru"], gk["s5"]
    return dict(
        lb=hg[0], hg_norm=hg[1], ret_norm=ret[0],
        conv_w=lru[0], conv_b=lru[1], wa=lru[2], ba=lru[3], wx=lru[4], bx=lru[5], sp=lru[6],
        bt_re=s5[0], bt_im=s5[1], lb_re=s5[2], lb_im=s5[3], ct_re=s5[4], ct_im=s5[5], s5_d=s5[6], glu_w=s5[7],
        glu_b=s5[8], b_gate=gk["b_gate"], norm_mix_pre=gk["norm_mix_pre"], norm_mix_post=gk["norm_mix_post"],
        norm_xa_pre=gk["norm_xa_pre"], norm_xa_post=gk["norm_xa_post"], norm_mem=gk["norm_mem"],
        norm_ffn_pre=gk["norm_ffn_pre"], norm_ffn_post=gk["norm_ffn_post"],
    )


def _step(inp):
    x, mem, target = inp["x"][0], inp["mem"][0], inp["loss_target"][0]
    seq = x.shape[0]
    depth = inp["w_in"].shape[0]
    me = 4 * lax.axis_index("x") + 2 * lax.axis_index("y") + lax.axis_index("c")

    small_shapes = [inp[n].shape for n in _SMALL_SHARDED_NAMES]
    (small_stacked,) = _exchange("gather_small", [_pack([inp[n] for n in _SMALL_SHARDED_NAMES], F32, 8)], True)
    small_all = _unpack_stacked(small_stacked, small_shapes)
    full_small = {n: _merge_shards(s, ax) for (n, ax), s in zip(SMALL_SHARDED, small_all)}

    lbs, lbs_vjp = jax.vjp(_lower_bounds, inp["hg_lower_bounds"])
    kps, prep_vjps = [], []
    for l in range(depth):
        p = {n: (full_small[n][l] if n in full_small else inp[n][l]) for n in _PREP_INPUTS}
        p["lb"] = lbs[l]
        kp, vj = jax.vjp(_prep_layer, p)
        kps.append(kp)
        prep_vjps.append(vj)
    rot = _rotary_tables(seq)
    ret_c = _retention_constants()

    (h,), _, _ = stage_fwd("norm_in_fwd", fn_norm, [(x, x.shape[1], 0)], [kps[0]["norm_mix_pre"]], tb=TB_ROW,
                           outs=[(x.shape[1], BF16)])
    def shards(l):
        return [inp[n][l].astype(BF16) for n in _BIG_NAMES]

    saved, bws = [], []
    xs = x
    (stacked,) = _exchange("gather_w_in_0", [shards(0)[0]], True)
    w_in = _merge_shards(stacked, BIG[0][1] - 1)
    for l in range(depth):
        last = l + 1 == depth
        g_next = None if last else kps[l + 1]["norm_mix_pre"]
        xs, h, sv, bw, w_in = _layer_forward(l, xs, h, mem, w_in, shards(l), None if last else shards(l + 1)[0],
                                             kps[l], rot, ret_c, g_next, target)
        saved.append(sv)
        bws.append(bw)
    dy, loss_local = xs, h

    big_parts = [None] * depth
    gks = [None] * depth
    dx, dh = dy, None
    for l in reversed(range(depth)):
        g_next = kps[l + 1]["norm_mix_pre"] if l + 1 < depth else None
        dx, dh, gks[l], big_parts[l] = _layer_backward(l, saved[l], mem, bws[l], kps[l], rot, ret_c, g_next, dx, dh)
    (grad_x,), (g_pre0,) = stage_bwd("norm_in_bwd", fn_keep_norm, [(x, x.shape[1], 0)], [kps[0]["norm_mix_pre"]],
                                     [dx, dh], tb=TB_ROW, diff_rows=(0,), diff_params=(0,))
    for l in range(depth):
        gks[l]["norm_mix_pre"] = g_pre0 if l == 0 else gks[l - 1]["g_next"]

    small_grads = {n: [None] * depth for n in _PREP_INPUTS}
    d_lbs = []
    for l in range(depth):
        (gp,) = prep_vjps[l]({k: g.astype(kps[l][k].dtype) for k, g in _kernel_grads_to_prep(gks[l]).items()})
        d_lbs.append(gp["lb"])
        for n in _PREP_INPUTS:
            small_grads[n][l] = gp[n]
    small_local = {n: jnp.stack(v) for n, v in small_grads.items()}
    (small_local["hg_lower_bounds"],) = lbs_vjp(jnp.stack(d_lbs))
    small_names = REPLICATED + _SMALL_SHARDED_NAMES
    full_shapes = [small_local[n].shape for n in small_names]
    (small_parts,) = _exchange("gather_small_grads", [_pack([small_local[n] for n in small_names], F32, 64)], True)

    out = {}
    kinds = ("grad_", "delta_", "new_m_", "new_v_")
    for i, n in enumerate(_BIG_NAMES):
        shape = inp[n].shape
        three = (shape[0], int(np.prod(shape[1:-1])), shape[-1])
        res = adamw("adamw_" + n, *[inp[pre + n].reshape(three) for pre in ("", "m_", "v_")],
                    [big_parts[l][i].reshape((N_DEV,) + three[1:]) for l in range(depth)])
        for kind, a in zip(kinds, res):
            out[kind + n] = a.reshape(shape)
    zeros = [jnp.zeros(s, F32) for s in full_shapes[len(REPLICATED):]]
    res = adamw("adamw_small", *[_pack([inp[pre + n] for n in REPLICATED] + zeros, F32, 64)[None] for pre in ("", "m_", "v_")],
                [small_parts])
    res = [r[0] for r in res]
    summed = _unpack(res[0], full_shapes)
    for kind, flat in zip(("grad_", "delta_", "new_m_", "new_v_"), res):
        for n, a in zip(REPLICATED, _unpack(flat, full_shapes[:len(REPLICATED)])):
            out[kind + n] = a
    shard_g = []
    for (n, ax), g_full in zip(SMALL_SHARDED, summed[len(REPLICATED):]):
        width = inp[n].shape[ax]
        shard_g.append(lax.dynamic_slice_in_dim(g_full, me * width, width, axis=ax))
    res = adamw("adamw_small_sharded",
                *[_pack([inp[pre + n] for n in _SMALL_SHARDED_NAMES], F32, 8)[None] for pre in ("", "m_", "v_")],
                [_pack(shard_g, F32, 8)[None]])
    res = [r[0] for r in res]
    for kind, flat in zip(("grad_", "delta_", "new_m_", "new_v_"), res):
        for n, a in zip(_SMALL_SHARDED_NAMES, _unpack(flat, small_shapes)):
            out[kind + n] = a

    out["loss"] = lax.psum(loss_local, ("x", "y", "c"))
    out["grad_x"] = grad_x[None]
    return out


def kernel(x, mem, hg_lower_bounds, norm_mix_pre, norm_mix_post, w_in, w_gate, b_gate, hg_norm, ret_norm, lru_conv_w, lru_conv_b, lru_wa, lru_ba, lru_wx, lru_bx, lru_lambda, s5_lam_re, s5_lam_im, s5_b_re, s5_b_im, s5_c_re, s5_c_im, s5_d, s5_log_dt, s5_glu_w, s5_glu_b, w_up, w_out, norm_xa_pre, norm_xa_post, norm_mem, xa_w_q, xa_w_kv, xa_w_o, norm_ffn_pre, norm_ffn_post, ffn_w_gu, ffn_w_down, loss_target, m_hg_lower_bounds, m_norm_mix_pre, m_norm_mix_post, m_w_in, m_w_gate, m_b_gate, m_hg_norm, m_ret_norm, m_lru_conv_w, m_lru_conv_b, m_lru_wa, m_lru_ba, m_lru_wx, m_lru_bx, m_lru_lambda, m_s5_lam_re, m_s5_lam_im, m_s5_b_re, m_s5_b_im, m_s5_c_re, m_s5_c_im, m_s5_d, m_s5_log_dt, m_s5_glu_w, m_s5_glu_b, m_w_up, m_w_out, m_norm_xa_pre, m_norm_xa_post, m_norm_mem, m_xa_w_q, m_xa_w_kv, m_xa_w_o, m_norm_ffn_pre, m_norm_ffn_post, m_ffn_w_gu, m_ffn_w_down, v_hg_lower_bounds, v_norm_mix_pre, v_norm_mix_post, v_w_in, v_w_gate, v_b_gate, v_hg_norm, v_ret_norm, v_lru_conv_w, v_lru_conv_b, v_lru_wa, v_lru_ba, v_lru_wx, v_lru_bx, v_lru_lambda, v_s5_lam_re, v_s5_lam_im, v_s5_b_re, v_s5_b_im, v_s5_c_re, v_s5_c_im, v_s5_d, v_s5_log_dt, v_s5_glu_w, v_s5_glu_b, v_w_up, v_w_out, v_norm_xa_pre, v_norm_xa_post, v_norm_mem, v_xa_w_q, v_xa_w_kv, v_xa_w_o, v_norm_ffn_pre, v_norm_ffn_post, v_ffn_w_gu, v_ffn_w_down):
    values = (x, mem, hg_lower_bounds, norm_mix_pre, norm_mix_post, w_in, w_gate, b_gate, hg_norm, ret_norm, lru_conv_w, lru_conv_b, lru_wa, lru_ba, lru_wx, lru_bx, lru_lambda, s5_lam_re, s5_lam_im, s5_b_re, s5_b_im, s5_c_re, s5_c_im, s5_d, s5_log_dt, s5_glu_w, s5_glu_b, w_up, w_out, norm_xa_pre, norm_xa_post, norm_mem, xa_w_q, xa_w_kv, xa_w_o, norm_ffn_pre, norm_ffn_post, ffn_w_gu, ffn_w_down, loss_target, m_hg_lower_bounds, m_norm_mix_pre, m_norm_mix_post, m_w_in, m_w_gate, m_b_gate, m_hg_norm, m_ret_norm, m_lru_conv_w, m_lru_conv_b, m_lru_wa, m_lru_ba, m_lru_wx, m_lru_bx, m_lru_lambda, m_s5_lam_re, m_s5_lam_im, m_s5_b_re, m_s5_b_im, m_s5_c_re, m_s5_c_im, m_s5_d, m_s5_log_dt, m_s5_glu_w, m_s5_glu_b, m_w_up, m_w_out, m_norm_xa_pre, m_norm_xa_post, m_norm_mem, m_xa_w_q, m_xa_w_kv, m_xa_w_o, m_norm_ffn_pre, m_norm_ffn_post, m_ffn_w_gu, m_ffn_w_down, v_hg_lower_bounds, v_norm_mix_pre, v_norm_mix_post, v_w_in, v_w_gate, v_b_gate, v_hg_norm, v_ret_norm, v_lru_conv_w, v_lru_conv_b, v_lru_wa, v_lru_ba, v_lru_wx, v_lru_bx, v_lru_lambda, v_s5_lam_re, v_s5_lam_im, v_s5_b_re, v_s5_b_im, v_s5_c_re, v_s5_c_im, v_s5_d, v_s5_log_dt, v_s5_glu_w, v_s5_glu_b, v_w_up, v_w_out, v_norm_xa_pre, v_norm_xa_post, v_norm_mem, v_xa_w_q, v_xa_w_kv, v_xa_w_o, v_norm_ffn_pre, v_norm_ffn_post, v_ffn_w_gu, v_ffn_w_down)
    names = ("x", "mem") + WEIGHTS + ("loss_target",) + tuple("m_" + n for n in WEIGHTS) + tuple("v_" + n for n in WEIGHTS)
    out = _step(dict(zip(names, values)))
    order = ["loss", "grad_x"] + [k + n for k in ("grad_", "delta_", "new_m_", "new_v_") for n in WEIGHTS]
    return tuple(out[k] for k in order)
```

```python
import functools

import numpy as np
import jax
import jax.numpy as jnp
from jax import lax
from jax.experimental import pallas as pl
from jax.experimental.pallas import tpu as pltpu

F32 = jnp.float32
BF16 = jnp.bfloat16
EPS = 1e-6
N_DEV = 8
LANES = 128
SUBLANES = 8
VMEM_LIMIT = 60 * 1024 * 1024

HEADS = 4
HEAD_DIM = 64
MIX_W = HEADS * HEAD_DIM
HG_CHUNK = 32
RET_CHUNK = 128
S5_GROUPS = 16
S5_GROUP = 16
S5_STATE = 64
LRU_C = 8.0
XA_HEADS = 4

ADAM_LR = 0.001
ADAM_B1 = 0.9
ADAM_B2 = 0.999
ADAM_EPS = 1e-08
ADAM_WD = 0.01
ADAM_STEP = 10

BIG = (("w_in", 2), ("w_gate", 2), ("w_up", 3), ("w_out", 1), ("xa_w_q", 1), ("xa_w_kv", 2), ("xa_w_o", 1),
       ("ffn_w_gu", 2), ("ffn_w_down", 1))
SMALL_SHARDED = (("lru_conv_w", 2), ("s5_glu_w", 1))
WEIGHTS = ("hg_lower_bounds", "norm_mix_pre", "norm_mix_post", "w_in", "w_gate", "b_gate", "hg_norm", "ret_norm",
           "lru_conv_w", "lru_conv_b", "lru_wa", "lru_ba", "lru_wx", "lru_bx", "lru_lambda", "s5_lam_re", "s5_lam_im",
           "s5_b_re", "s5_b_im", "s5_c_re", "s5_c_im", "s5_d", "s5_log_dt", "s5_glu_w", "s5_glu_b", "w_up", "w_out",
           "norm_xa_pre", "norm_xa_post", "norm_mem", "xa_w_q", "xa_w_kv", "xa_w_o", "norm_ffn_pre", "norm_ffn_post",
           "ffn_w_gu", "ffn_w_down")
GATHER_IN = {"in_proj": (1,), "gate_proj": (2, 3, 4, 6), "hg": (5,), "xattn": (8,)}
GU = 7
_BIG_NAMES = tuple(n for n, _ in BIG)
_SMALL_SHARDED_NAMES = tuple(n for n, _ in SMALL_SHARDED)
REPLICATED = tuple(n for n in WEIGHTS if n not in _BIG_NAMES and n not in _SMALL_SHARDED_NAMES)


def _dot(a, b):
    return jnp.dot(a.astype(BF16), b.astype(BF16), preferred_element_type=F32)


def _dot_nt(a, b):
    return lax.dot_general(a.astype(BF16), b.astype(BF16), (((1,), (1,)), ((), ())), preferred_element_type=F32)


def _dot_tn(a, b):
    return lax.dot_general(a.astype(BF16), b.astype(BF16), (((0,), (0,)), ((), ())), preferred_element_type=F32)


def _dot_exact(a, b):
    return jnp.dot(a, b, precision=lax.Precision.HIGHEST, preferred_element_type=F32)


def _rms(x, g):
    return x * lax.rsqrt(jnp.mean(x * x, axis=-1, keepdims=True) + EPS) * g


def _shift_down(x, d, fill):
    return jnp.concatenate([jnp.full((d, x.shape[1]), fill, x.dtype), x[:-d]], axis=0)


def _shift_up(x, d, fill):
    return jnp.concatenate([x[d:], jnp.full((d, x.shape[1]), fill, x.dtype)], axis=0)


def _cumsum_rows(x):
    d = 1
    while d < x.shape[0]:
        x = x + _shift_down(x, d, 0.0)
        d *= 2
    return x


def _lane_head(shape, dim):
    return lax.shift_right_logical(lax.broadcasted_iota(jnp.int32, shape, dim), 6)


def _head_masks(width=MIX_W):
    head = _lane_head((1, width), 1)
    return [(head == h).astype(F32) for h in range(HEADS)]


def _block_diag_mask():
    return (_lane_head((MIX_W, MIX_W), 0) == _lane_head((MIX_W, MIX_W), 1)).astype(F32)


def _head_rms(o, g):
    ms = _dot_exact(o * o, _block_diag_mask()) * (1.0 / HEAD_DIM)
    return o * lax.rsqrt(ms + EPS) * g


def _swap_pairs(x):
    lane = lax.broadcasted_iota(jnp.int32, x.shape, 1)
    return jnp.where((lane & 1) == 0, jnp.roll(x, -1, axis=1), jnp.roll(x, 1, axis=1))


def _stack_heads(t, masks):
    return jnp.concatenate([t * m for m in masks], axis=0)


@jax.custom_vjp
def _real_scan(a, u, h0):
    return _real_scan_fwd(a, u, h0)[0]


def _real_scan_fwd(a, u, h0):
    t = a.shape[0]
    acc_a, acc_u = a, u
    d = 1
    while d < t:
        acc_u = acc_u + acc_a * _shift_down(acc_u, d, 0.0)
        acc_a = acc_a * _shift_down(acc_a, d, 1.0)
        d *= 2
    h = acc_u + acc_a * h0
    return h, (a, h, h0)


def _real_scan_bwd(res, dh):
    a, h, h0 = res
    t = a.shape[0]
    acc_a = _shift_up(a, 1, 0.0)
    g = dh
    d = 1
    while d < t:
        g = g + acc_a * _shift_up(g, d, 0.0)
        acc_a = acc_a * _shift_up(acc_a, d, 1.0)
        d *= 2
    h_prev = jnp.concatenate([h0, h[:-1]], axis=0)
    return g * h_prev, g, (a * g)[0:1]


_real_scan.defvjp(_real_scan_fwd, _real_scan_bwd)


def _cmul(ar, ai, br, bi):
    return ar * br - ai * bi, ar * bi + ai * br


def _geometric_sums(ar, ai, ur, ui, forward):
    shift = _shift_down if forward else _shift_up
    t = ur.shape[0]
    g = SUBLANES
    in_group = lax.broadcasted_iota(jnp.int32, ur.shape, 0) & (g - 1)
    pr, pi, sr, si = ar, ai, ur, ui
    d = 1
    while d < g:
        keep = in_group >= d if forward else in_group < g - d
        mr, mi = _cmul(pr, pi, jnp.where(keep, shift(sr, d, 0.0), 0.0), jnp.where(keep, shift(si, d, 0.0), 0.0))
        sr, si = sr + mr, si + mi
        pr, pi = _cmul(pr, pi, pr, pi)
        d *= 2
    row = lax.broadcasted_iota(jnp.int32, (g, ur.shape[1]), 0)
    qr, qi = ar, ai
    tr, ti = jnp.zeros((g, ur.shape[1]), F32), jnp.zeros((g, ur.shape[1]), F32)
    for r in range(g):
        here = row == (r if forward else g - 1 - r)
        tr, ti = jnp.where(here, qr, tr), jnp.where(here, qi, ti)
        qr, qi = _cmul(qr, qi, ar, ai)
    outs_r, outs_i = [], []
    cr = ci = None
    order = range(t // g) if forward else reversed(range(t // g))
    for n in order:
        br, bi = sr[n * g:(n + 1) * g], si[n * g:(n + 1) * g]
        if cr is not None:
            mr, mi = _cmul(tr, ti, cr, ci)
            br, bi = br + mr, bi + mi
        edge = slice(g - 1, g) if forward else slice(0, 1)
        cr, ci = br[edge], bi[edge]
        outs_r.append(br)
        outs_i.append(bi)
    if not forward:
        outs_r.reverse()
        outs_i.reverse()
    return jnp.concatenate(outs_r, axis=0), jnp.concatenate(outs_i, axis=0)


@jax.custom_vjp
def _complex_scan(ar, ai, ur, ui, h0r, h0i):
    return _complex_scan_fwd(ar, ai, ur, ui, h0r, h0i)[0]


def _complex_scan_fwd(ar, ai, ur, ui, h0r, h0i):
    first = lax.broadcasted_iota(jnp.int32, ur.shape, 0) == 0
    cr, ci = _cmul(ar, ai, h0r, h0i)
    hr, hi = _geometric_sums(ar, ai, ur + jnp.where(first, cr, 0.0), ui + jnp.where(first, ci, 0.0), True)
    return (hr, hi), (ar, ai, hr, hi, h0r, h0i)


def _complex_scan_bwd(res, dh):
    ar, ai, hr, hi, h0r, h0i = res
    gr, gi = _geometric_sums(ar, -ai, dh[0], dh[1], False)
    qr = jnp.concatenate([h0r, hr[:-1]], axis=0)
    qi = jnp.concatenate([h0i, hi[:-1]], axis=0)
    dar = jnp.sum(gr * qr + gi * qi, axis=0, keepdims=True)
    dai = jnp.sum(gi * qr - gr * qi, axis=0, keepdims=True)
    d0r, d0i = _cmul(ar, -ai, gr[0:1], gi[0:1])
    return dar, dai, gr, gi, d0r, d0i


_complex_scan.defvjp(_complex_scan_fwd, _complex_scan_bwd)


def fn_norm(st, rows, params):
    (x,), (g,) = rows, params
    return (), (_rms(x, g),), ()


def fn_keep_norm(st, rows, params):
    (x,), (g,) = rows, params
    return (), (x, _rms(x, g)), ()


def fn_hgrn2(st, rows, params):
    (state,) = st
    (z,) = rows
    lb, norm_g = params
    q, f_logit, v_all, g = (z[:, k * MIX_W:(k + 1) * MIX_W] for k in range(4))
    f = lb + (1.0 - lb) * jax.nn.sigmoid(f_logit)
    log_f = jnp.log(f)
    k_all = 1.0 - f
    q_all = jax.nn.silu(q)
    masks = _head_masks()
    bd = _block_diag_mask()
    c = HG_CHUNK
    col = lax.broadcasted_iota(jnp.int32, (c, HEADS * c), 1) & (c - 1)
    causal = col <= lax.broadcasted_iota(jnp.int32, (c, HEADS * c), 0)
    outs = []
    for n in range(z.shape[0] // c):
        sl = slice(n * c, (n + 1) * c)
        lf = log_f[sl]
        b = _cumsum_rows(lf)
        b_end = jnp.sum(lf, axis=0, keepdims=True)
        q_dec = q_all[sl] * jnp.exp(b)
        k_inv = k_all[sl] * jnp.exp(-b)
        k_end = k_all[sl] * jnp.exp(b_end - b)
        v = v_all[sl]
        scores = jnp.where(causal, _dot_nt(q_dec, _stack_heads(k_inv, masks)), 0.0)
        outs.append(_dot(scores, _stack_heads(v, masks)) + _dot_nt(q_dec, state))
        state = state * jnp.exp(b_end) + _dot_tn(v, k_end) * bd
    o = jnp.concatenate(outs, axis=0) if len(outs) > 1 else outs[0]
    return (state,), (_head_rms(o, norm_g) * jax.nn.silu(g),), ()


def fn_retention(st, rows, params):
    (state,) = st
    z, cos_t, sin_t = rows
    norm_g, xi, zeta, decay, g_end = params
    q, k, v_all, g = (z[:, i * MIX_W:(i + 1) * MIX_W] for i in range(4))
    q_all = q * cos_t + _swap_pairs(q) * sin_t
    k_all = (k * cos_t + _swap_pairs(k) * sin_t) * (HEAD_DIM ** -0.5)
    masks = _head_masks()
    bd = _block_diag_mask()
    c = RET_CHUNK
    outs = []
    for n in range(z.shape[0] // c):
        sl = slice(n * c, (n + 1) * c)
        qc, kc, v = q_all[sl], k_all[sl], v_all[sl]
        scores = _dot_nt(qc, _stack_heads(kc, masks)) * decay
        outs.append(_dot(scores, _stack_heads(v, masks)) + _dot_nt(qc * xi, state))
        state = state * g_end + _dot_tn(v, kc * zeta) * bd
    o = jnp.concatenate(outs, axis=0) if len(outs) > 1 else outs[0]
    return (state,), (_head_rms(o, norm_g) * jax.nn.silu(g),), ()


def fn_rglru(st, rows, params):
    tail_x, tail_h = st
    (z,) = rows
    conv_w, conv_b, wa, ba, wx, bx, sp = params
    t = z.shape[0]
    xg, xi = z[:, :MIX_W], z[:, MIX_W:]
    full = jnp.concatenate([tail_x, xi], axis=0)
    xc = conv_b
    for k in range(4):
        xc = xc + conv_w[k:k + 1] * full[5 + k:5 + k + t]
    r = jax.nn.sigmoid(_dot(xc, wa) + ba)
    ig = jax.nn.sigmoid(_dot(xc, wx) + bx)
    log_a = -LRU_C * r * sp
    a = jnp.exp(log_a)
    one_minus_a2 = -jnp.tanh(log_a) * (a * a + 1.0)
    u = jnp.sqrt(one_minus_a2) * (ig * xc)
    h = _real_scan(a, u, tail_h[7:8])
    return (xi[t - 8:], h[t - 8:]), (h * jax.nn.gelu(xg),), ()


def fn_s5(st, rows, params):
    tail_r, tail_i = st
    (u,) = rows
    bt_re, bt_im, lb_re, lb_im, ct_re, ct_im, d, glu_w, glu_b = params
    t = u.shape[0]
    bu_re = _dot(u, bt_re)
    bu_im = _dot(u, bt_im)
    h_re, h_im = _complex_scan(lb_re, lb_im, bu_re, bu_im, tail_r[7:8], tail_i[7:8])
    y = _dot(h_re, ct_re) - _dot(h_im, ct_im) + d * u
    act = jax.nn.gelu(y)
    out = act * jax.nn.sigmoid(_dot(act, glu_w) + glu_b)
    return (h_re[t - 8:], h_im[t - 8:]), (out,), ()


def fn_merge(st, rows, params):
    ya, yb, yc, yd, gl, x = rows
    w0, w1, w2, w3, b_gate, w_out, g_post = params
    d = x.shape[1]
    mix = None
    for n, (y, w) in enumerate(((ya, w0), (yb, w1), (yc, w2), (yd, w3))):
        gate = jax.nn.sigmoid(gl[:, n * d:(n + 1) * d] + b_gate[:, n * d:(n + 1) * d])
        term = gate * _dot(y, w)
        mix = term if mix is None else mix + term
    return (), (x + _rms(_dot(mix, w_out), g_post),), ()


def fn_mem(st, rows, params):
    (mem,), (g, wk, wv) = rows, params
    m = _rms(mem, g)
    return (), (_dot(m, wk), _dot(m, wv)), ()


def fn_xattn(st, rows, params):
    (x,) = rows
    g_pre, wq, k, v, wo, g_post, g_next = params
    d = x.shape[1]
    dh = d // XA_HEADS
    q = _dot(_rms(x, g_pre), wq)
    heads = []
    for h in range(XA_HEADS):
        sl = slice(h * dh, (h + 1) * dh)
        s = _dot_nt(q[:, sl], k[:, sl]) * (dh ** -0.5)
        heads.append(_dot(jax.nn.softmax(s, axis=-1), v[:, sl]))
    x2 = x + _rms(_dot(jnp.concatenate(heads, axis=1), wo), g_post)
    return (), (x2, _rms(x2, g_next)), ()


def fn_res_norm(st, rows, params):
    (x, o), (g_post, g_next) = rows, params
    xn = x + _rms(o, g_post)
    return (), (xn, _rms(xn, g_next)), ()


def fn_res(st, rows, params):
    (x, o), (g_post,) = rows, params
    return (), (x + _rms(o, g_post),), ()


def fn_res_loss(st, rows, params):
    (x, o, target), (g_post,) = rows, params
    err = x + _rms(o, g_post) - target
    inv_d = 1.0 / x.shape[1]
    loss = 0.5 * inv_d * jnp.sum(err * err)
    return (), (err * inv_d,), (jnp.full((8, LANES), loss, F32),)


def _params():
    return pltpu.CompilerParams(dimension_semantics=("arbitrary",), vmem_limit_bytes=VMEM_LIMIT)


def _row_spec(tb, width, colblk, nb, reverse):
    if reverse:
        return pl.BlockSpec((tb, width), lambda i: (nb - 1 - i, colblk))
    return pl.BlockSpec((tb, width), lambda i: (i, colblk))


def _full_spec(shape):
    return pl.BlockSpec(shape, lambda i: (0,) * len(shape))


def _saved_spec(shape, nb, reverse):
    if reverse:
        return pl.BlockSpec((1,) + shape, lambda i: (nb - 1 - i, 0, 0))
    return pl.BlockSpec((1,) + shape, lambda i: (i, 0, 0))


def _param_value(ref):
    v = ref[...]
    return v if v.dtype == BF16 else v.astype(F32)


def _side(side):
    arrays, gather = side if side is not None else ((), False)
    return list(arrays), gather, len(arrays)


_ANY = pl.BlockSpec(memory_space=pl.ANY)


def stage_fwd(name, fn, rows, params, *, tb, outs, states=(), accs=(), side=None):
    n_rows = rows[0][0].shape[0]
    nb = n_rows // tb
    nr, npar, no, na, ns = len(rows), len(params), len(outs), len(accs), len(states)
    side_arrays, gather, nx = _side(side)

    def body(*refs):
        row_refs, par_refs, side_in = refs[:nr], refs[nr:nr + npar], refs[nr + npar:nr + npar + nx]
        o = nr + npar + nx
        out_refs, acc_refs = refs[o:o + no], refs[o + no:o + no + na]
        saved_refs = refs[o + no + na:o + no + na + ns]
        o = o + no + na + ns
        side_out, st_refs, sems = refs[o:o + nx], refs[o + nx:o + nx + ns], refs[o + nx + ns:]
        i = pl.program_id(0)
        if nx:
            start, wait = _exchange_copies(side_in, side_out, *sems, gather)
            pl.when(i == 0)(start)

        @pl.when(i == 0)
        def _():
            for r in st_refs + acc_refs:
                r[...] = jnp.zeros_like(r)

        st = tuple(r[...] for r in st_refs)
        for sv, s in zip(saved_refs, st):
            sv[0] = s
        new_st, out_vals, acc_vals = fn(st, tuple(r[...].astype(F32) for r in row_refs),
                                        tuple(_param_value(r) for r in par_refs))
        for r, v in zip(out_refs, out_vals):
            r[...] = v.astype(r.dtype)
        for r, v in zip(acc_refs, acc_vals):
            r[...] += v
        for r, v in zip(st_refs, new_st):
            r[...] = v
        if nx:
            pl.when(i == nb - 1)(wait)

    res = pl.pallas_call(
        body, name=name, grid=(nb,),
        in_specs=[_row_spec(tb, w, cb, nb, False) for _, w, cb in rows] + [_full_spec(p.shape) for p in params]
        + [_ANY] * nx,
        out_specs=[_row_spec(tb, w, 0, nb, False) for w, _ in outs] + [_full_spec(s) for s in accs]
        + [_saved_spec(s, nb, False) for s in states] + [_ANY] * nx,
        out_shape=[jax.ShapeDtypeStruct((n_rows, w), dt) for w, dt in outs]
        + [jax.ShapeDtypeStruct(s, F32) for s in accs] + [jax.ShapeDtypeStruct((nb,) + s, F32) for s in states]
        + _exchange_shapes(side_arrays, gather),
        scratch_shapes=[pltpu.VMEM(s, F32) for s in states] + (_exchange_sems(nx) if nx else []),
        compiler_params=_params(),
    )(*[a for a, _, _ in rows], *params, *side_arrays)
    base = (res[:no], res[no:no + na], res[no + na:no + na + ns])
    return base + (res[no + na + ns:],) if nx else base


def stage_bwd(name, fn, rows, params, cts, *, tb, saved=(), diff_rows=(), diff_params=(), row_dtypes=None, side=None,
              into=None):
    n_rows = rows[0][0].shape[0]
    nb = n_rows // tb
    nr, npar, ns, nc = len(rows), len(params), len(saved), len(cts)
    ndr, ndp = len(diff_rows), len(diff_params)
    row_dtypes = row_dtypes or (F32,) * ndr
    state_shapes = [s.shape[1:] for s in saved]
    side_arrays, gather, nx = _side(side)
    into_buffer = [into[0]] if into is not None and into[0] is not None else []
    na = len(into_buffer)
    if into is not None:
        assert ndr == 1
        drow_specs = [_row_spec(tb, rows[diff_rows[0]][1], rows[diff_rows[0]][2], nb, True)]
        drow_shapes = [jax.ShapeDtypeStruct((n_rows, into[1]), row_dtypes[0])]
    else:
        drow_specs = [_row_spec(tb, rows[k][1], 0, nb, True) for k in diff_rows]
        drow_shapes = [jax.ShapeDtypeStruct((n_rows, rows[k][1]), dt) for k, dt in zip(diff_rows, row_dtypes)]

    def body(*refs):
        row_refs, par_refs = refs[:nr], refs[nr:nr + npar]
        o = nr + npar
        saved_refs, ct_refs, side_in = refs[o:o + ns], refs[o + ns:o + ns + nc], refs[o + ns + nc:o + ns + nc + nx]
        o = o + ns + nc + nx + na
        drow_refs, dpar_refs, side_out = refs[o:o + ndr], refs[o + ndr:o + ndr + ndp], refs[o + ndr + ndp:o + ndr + ndp + nx]
        o = o + ndr + ndp + nx
        dst_refs, sems = refs[o:o + ns], refs[o + ns:]
        i = pl.program_id(0)
        if nx:
            start, wait = _exchange_copies(side_in, side_out, *sems, gather)
            pl.when(i == 0)(start)

        @pl.when(i == 0)
        def _():
            for r in dst_refs + dpar_refs:
                r[...] = jnp.zeros_like(r)

        st = tuple(r[0] for r in saved_refs)
        row_vals = [r[...].astype(F32) for r in row_refs]
        par_vals = [_param_value(r) for r in par_refs]

        def f(st_, dr_, dp_):
            rv, pv = list(row_vals), list(par_vals)
            for k, v in zip(diff_rows, dr_):
                rv[k] = v
            for k, v in zip(diff_params, dp_):
                pv[k] = v
            new_st, out_vals, _ = fn(st_, tuple(rv), tuple(pv))
            return new_st, out_vals

        _, vjp = jax.vjp(f, st, tuple(row_vals[k] for k in diff_rows), tuple(par_vals[k] for k in diff_params))
        g_st, g_rows, g_par = vjp((tuple(r[...] for r in dst_refs), tuple(r[...].astype(F32) for r in ct_refs)))
        for r, v in zip(drow_refs, g_rows):
            r[...] = v.astype(r.dtype)
        for r, v in zip(dpar_refs, g_par):
            r[...] += v
        for r, v in zip(dst_refs, g_st):
            r[...] = v
        if nx:
            pl.when(i == nb - 1)(wait)

    res = pl.pallas_call(
        body, name=name, grid=(nb,),
        in_specs=[_row_spec(tb, w, cb, nb, True) for _, w, cb in rows] + [_full_spec(p.shape) for p in params]
        + [_saved_spec(s, nb, True) for s in state_shapes] + [_row_spec(tb, c.shape[1], 0, nb, True) for c in cts]
        + [_ANY] * (nx + na),
        out_specs=drow_specs + [_full_spec(params[k].shape) for k in diff_params] + [_ANY] * nx,
        out_shape=drow_shapes + [jax.ShapeDtypeStruct(params[k].shape, F32) for k in diff_params]
        + _exchange_shapes(side_arrays, gather),
        scratch_shapes=[pltpu.VMEM(s, F32) for s in state_shapes] + (_exchange_sems(nx) if nx else []),
        input_output_aliases={nr + npar + ns + nc + nx: 0} if na else {},
        compiler_params=_params(),
    )(*[a for a, _, _ in rows], *params, *saved, *cts, *side_arrays, *into_buffer)
    base = (res[:ndr], res[ndr:ndr + ndp])
    return base + (res[ndr + ndp:],) if nx else base


def _pick(n, target):
    if n <= target:
        return n
    best = None
    for t in range(LANES, target + 1, LANES):
        if n % t == 0:
            best = t
    assert best is not None, n
    return best


def _mesh_position():
    return lax.axis_index("x"), lax.axis_index("y"), lax.axis_index("c")


def _peer(pos, k):
    x, y, c = pos
    px = 1 - x if k & 4 else x
    py = 1 - y if k & 2 else y
    pc = 1 - c if k & 1 else c
    return (px, py, pc), 4 * px + 2 * py + pc


def _exchange_copies(x_refs, o_refs, send_sems, recv_sems, local_sems, gather):
    pos = _mesh_position()
    me = 4 * pos[0] + 2 * pos[1] + pos[2]
    pairs = list(enumerate(zip(x_refs, o_refs)))

    def remote(k, a, src, dst):
        peer, _ = _peer(pos, k)
        return pltpu.make_async_remote_copy(src_ref=src, dst_ref=dst, send_sem=send_sems.at[k - 1, a],
                                            recv_sem=recv_sems.at[k - 1, a], device_id=peer,
                                            device_id_type=pl.DeviceIdType.MESH)

    def local(a, x, o):
        return pltpu.make_async_copy(x if gather else x.at[me], o.at[me], local_sems.at[a])

    def start():
        for a, (x, o) in pairs:
            local(a, x, o).start()
        for k in range(1, N_DEV):
            peer_idx = _peer(pos, k)[1]
            for a, (x, o) in pairs:
                remote(k, a, x if gather else x.at[peer_idx], o.at[me]).start()

    def wait():
        for k in range(1, N_DEV):
            peer_idx = _peer(pos, k)[1]
            for a, (x, o) in pairs:
                arrival = remote(k, a, x if gather else x.at[me], o.at[peer_idx])
                arrival.wait_recv()
                arrival.wait_send()
        for a, (x, o) in pairs:
            local(a, x, o).wait()

    return start, wait


def _exchange_shapes(arrays, gather):
    return [jax.ShapeDtypeStruct(((N_DEV,) + x.shape) if gather else x.shape, x.dtype) for x in arrays]


def _exchange_sems(n):
    return [pltpu.SemaphoreType.DMA((N_DEV - 1, n)), pltpu.SemaphoreType.DMA((N_DEV - 1, n)),
            pltpu.SemaphoreType.DMA((n,))]


def _exchange(name, arrays, gather):
    n = len(arrays)

    def body(*refs):
        start, wait = _exchange_copies(refs[:n], refs[n:2 * n], *refs[2 * n:], gather)
        start()
        wait()

    return pl.pallas_call(
        body, name=name,
        in_specs=[pl.BlockSpec(memory_space=pl.ANY)] * n, out_specs=[pl.BlockSpec(memory_space=pl.ANY)] * n,
        out_shape=_exchange_shapes(arrays, gather), scratch_shapes=_exchange_sems(n),
    )(*arrays)


def _pick_n(n):
    return 1408 if n % 1408 == 0 else _pick(n, 512)


def matmul(name, a, b, mode, out_dtype, add=None, side=None):
    if mode == "tn":
        k, m = a.shape
    else:
        m, k = a.shape
    n = b.shape[0] if mode == "nt" else b.shape[1]
    if mode == "tn":
        tm, tn, tk = _pick(m, 1408), _pick(n, 1408), _pick(k, 1024)
    else:
        tm, tn, tk = _pick(m, 1024), _pick_n(n), _pick(k, 2816)
    nk = k // tk
    grid = (m // tm, n // tn, nk)
    a_spec = pl.BlockSpec((tk, tm), lambda i, j, kk: (kk, i)) if mode == "tn" else pl.BlockSpec((tm, tk), lambda i, j, kk: (i, kk))
    b_spec = pl.BlockSpec((tn, tk), lambda i, j, kk: (j, kk)) if mode == "nt" else pl.BlockSpec((tk, tn), lambda i, j, kk: (kk, j))
    o_spec = pl.BlockSpec((tm, tn), lambda i, j, kk: (i, j))
    dims = {"nn": (((1,), (0,)), ((), ())), "nt": (((1,), (1,)), ((), ())), "tn": (((0,), (0,)), ((), ()))}[mode]
    has_add = add is not None
    side_arrays, gather = side if side is not None else ((), False)
    ns = len(side_arrays)
    n_in = 2 + has_add

    def body(*refs):
        a_ref, b_ref = refs[0], refs[1]
        side_in, o_ref, side_out = refs[n_in:n_in + ns], refs[n_in + ns], refs[n_in + ns + 1:n_in + 2 * ns + 1]
        scratch = refs[n_in + 2 * ns + 1:]
        ids = [pl.program_id(d) for d in range(3)]
        if ns:
            start, wait = _exchange_copies(side_in, side_out, *scratch[-3:], gather)
            pl.when((ids[0] == 0) & (ids[1] == 0) & (ids[2] == 0))(start)
        part = lax.dot_general(a_ref[...].astype(BF16), b_ref[...].astype(BF16), dims, preferred_element_type=F32)
        if nk == 1:
            o_ref[...] = (part + refs[2][...].astype(F32) if has_add else part).astype(o_ref.dtype)
        else:
            acc_ref = scratch[0]

            @pl.when(ids[2] == 0)
            def _():
                acc_ref[...] = part + refs[2][...].astype(F32) if has_add else part

            @pl.when(ids[2] > 0)
            def _():
                acc_ref[...] += part

            @pl.when(ids[2] == nk - 1)
            def _():
                o_ref[...] = acc_ref[...].astype(o_ref.dtype)
        if ns:
            pl.when((ids[0] == grid[0] - 1) & (ids[1] == grid[1] - 1) & (ids[2] == nk - 1))(wait)

    any_spec = pl.BlockSpec(memory_space=pl.ANY)
    res = pl.pallas_call(
        body, name=name, grid=grid,
        in_specs=[a_spec, b_spec] + ([o_spec] if has_add else []) + [any_spec] * ns,
        out_specs=[o_spec] + [any_spec] * ns,
        out_shape=[jax.ShapeDtypeStruct((m, n), out_dtype)] + _exchange_shapes(side_arrays, gather),
        scratch_shapes=([pltpu.VMEM((tm, tn), F32)] if nk > 1 else []) + (_exchange_sems(ns) if ns else []),
        compiler_params=pltpu.CompilerParams(
            dimension_semantics=("arbitrary",) * 3 if ns else ("parallel", "parallel", "arbitrary"),
            vmem_limit_bytes=VMEM_LIMIT),
    )(a, b, *([add] if has_add else []), *side_arrays)
    return (res[0], res[1:]) if ns else res[0]


def matmul_fused(name, a, bs, mode, n, extras, epilogue, out_dtypes):
    m, k = a.shape
    tm, tn = _pick(m, 1024), _pick_n(n)
    dims = {"nn": (((1,), (0,)), ((), ())), "nt": (((1,), (1,)), ((), ()))}[mode]
    nb, nx = len(bs), len(extras)

    def b_spec(off):
        if mode == "nt":
            return pl.BlockSpec((tn, k), lambda i, j: (j + off, 0))
        return pl.BlockSpec((k, tn), lambda i, j: (0, j + off))

    def body(*refs):
        a_val = refs[0][...].astype(BF16)
        parts = tuple(lax.dot_general(a_val, r[...].astype(BF16), dims, preferred_element_type=F32)
                      for r in refs[1:1 + nb])
        tiles = tuple(r[...].astype(F32) for r in refs[1 + nb:1 + nb + nx])
        for r, v in zip(refs[1 + nb + nx:], epilogue(parts, tiles)):
            r[...] = v.astype(r.dtype)

    tile = pl.BlockSpec((tm, tn), lambda i, j: (i, j))
    return pl.pallas_call(
        body, name=name, grid=(m // tm, n // tn),
        in_specs=[pl.BlockSpec((tm, k), lambda i, j: (i, 0))] + [b_spec(off) for _, off in bs] + [tile] * nx,
        out_specs=[tile] * len(out_dtypes), out_shape=[jax.ShapeDtypeStruct((m, n), dt) for dt in out_dtypes],
        compiler_params=pltpu.CompilerParams(dimension_semantics=("parallel", "parallel"),
                                             vmem_limit_bytes=VMEM_LIMIT),
    )(a, *[b for b, _ in bs], *extras)


def _glu_fwd_tiles(parts, tiles):
    gate, up = parts
    return gate, up, jax.nn.silu(gate) * up


def _glu_bwd_tiles(parts, tiles):
    (da,), (gate, up) = parts, tiles
    _, vjp = jax.vjp(lambda g, u: jax.nn.silu(g) * u, gate, up)
    return vjp(da)


def adamw(name, w, m, v, gparts, side=None):
    layers, rows, cols = w.shape
    parts = gparts[0].shape[0]
    tr = 8
    while tr * 2 * cols <= 65536 and rows % (tr * 2) == 0:
        tr *= 2
    nblk = rows // tr
    c1 = 1.0 - ADAM_B1 ** ADAM_STEP
    c2 = 1.0 - ADAM_B2 ** ADAM_STEP
    side_arrays, gather, nx = _side(side)

    def body(*refs):
        w_ref, m_ref, v_ref = refs[:3]
        g_refs = refs[3:3 + layers]
        side_in = refs[3 + layers:3 + layers + nx]
        go_ref, d_ref, mo_ref, vo_ref = refs[3 + layers + nx:7 + layers + nx]
        side_out, sems = refs[7 + layers + nx:7 + layers + 2 * nx], refs[7 + layers + 2 * nx:]
        layer = pl.program_id(0)
        if nx:
            start, wait = _exchange_copies(side_in, side_out, *sems, gather)
            pl.when((layer == 0) & (pl.program_id(1) == 0))(start)
        g = None
        for ll, g_ref in enumerate(g_refs):
            s = g_ref[0].astype(F32)
            for p in range(1, parts):
                s = s + g_ref[p].astype(F32)
            g = s if g is None else jnp.where(layer == ll, s, g)
        m_new = ADAM_B1 * m_ref[...] + (1.0 - ADAM_B1) * g
        v_new = ADAM_B2 * v_ref[...] + (1.0 - ADAM_B2) * (g * g)
        m_hat = m_new / c1
        v_hat = v_new / c2
        go_ref[...] = g
        d_ref[...] = -ADAM_LR * (m_hat / (jnp.sqrt(v_hat) + ADAM_EPS) + ADAM_WD * w_ref[...])
        mo_ref[...] = m_new
        vo_ref[...] = v_new
        if nx:
            pl.when((layer == layers - 1) & (pl.program_id(1) == nblk - 1))(wait)

    def part_spec(ll):
        return pl.BlockSpec((parts, tr, cols),
                            lambda l, i: (0, jnp.where(l == ll, i, jnp.where(l < ll, 0, nblk - 1)), 0))

    spec = pl.BlockSpec((None, tr, cols), lambda l, i: (l, i, 0))
    res = pl.pallas_call(
        body, name=name, grid=(layers, nblk),
        in_specs=[spec, spec, spec] + [part_spec(ll) for ll in range(layers)] + [_ANY] * nx,
        out_specs=[spec] * 4 + [_ANY] * nx,
        out_shape=[jax.ShapeDtypeStruct(w.shape, F32)] * 4 + _exchange_shapes(side_arrays, gather),
        scratch_shapes=_exchange_sems(nx) if nx else [],
        compiler_params=pltpu.CompilerParams(dimension_semantics=("arbitrary", "arbitrary"),
                                             vmem_limit_bytes=VMEM_LIMIT),
    )(w, m, v, *gparts, *side_arrays)
    return (res[:4], res[4:]) if nx else res


def _pack(arrays, dtype, row_multiple):
    flat = jnp.concatenate([a.astype(dtype).reshape(-1) for a in arrays])
    unit = row_multiple * LANES
    pad = (-flat.shape[0]) % unit
    if pad:
        flat = jnp.concatenate([flat, jnp.zeros((pad,), dtype)])
    return flat.reshape(-1, LANES)


def _unpack(flat2d, shapes):
    flat = flat2d.reshape(-1)
    out, off = [], 0
    for s in shapes:
        n = int(np.prod(s))
        out.append(flat[off:off + n].reshape(s))
        off += n
    return out


def _unpack_stacked(stacked, shapes):
    flat = stacked.reshape(N_DEV, -1)
    out, off = [], 0
    for s in shapes:
        n = int(np.prod(s))
        out.append(flat[:, off:off + n].reshape((N_DEV,) + tuple(s)))
        off += n
    return out


def _merge_shards(stacked, axis):
    t = jnp.moveaxis(stacked, 0, axis)
    s = t.shape
    return t.reshape(s[:axis] + (s[axis] * s[axis + 1],) + s[axis + 2:])


def _split_shards(full, axis):
    s = full.shape
    t = full.reshape(s[:axis] + (N_DEV, s[axis] // N_DEV) + s[axis + 1:])
    return jnp.moveaxis(t, axis, 0)


def _lower_bounds(hg_lower_bounds):
    p = jax.nn.softmax(hg_lower_bounds, axis=0)
    return jnp.cumsum(p, axis=0) - p[0:1]


def _prep_layer(p):
    def row(v):
        return v.reshape(1, -1)

    eye_b = jnp.eye(HEADS, dtype=F32)
    eye_g = jnp.eye(S5_GROUPS, dtype=F32)
    step = jnp.exp(p["s5_log_dt"])[:, None]
    lam_re, lam_im = p["s5_lam_re"], p["s5_lam_im"]
    mag = jnp.exp(lam_re * step)
    lb_re = mag * jnp.cos(lam_im * step)
    lb_im = mag * jnp.sin(lam_im * step)
    den = lam_re * lam_re + lam_im * lam_im
    f_re = ((lb_re - 1.0) * lam_re + lb_im * lam_im) / den
    f_im = (lb_im * lam_re - (lb_re - 1.0) * lam_im) / den
    bb_re = f_re[..., None] * p["s5_b_re"] - f_im[..., None] * p["s5_b_im"]
    bb_im = f_re[..., None] * p["s5_b_im"] + f_im[..., None] * p["s5_b_re"]
    width = S5_GROUPS * S5_GROUP
    n_state = S5_GROUPS * S5_STATE
    return dict(
        lb=row(p["lb"]), hg_norm=row(p["hg_norm"]), ret_norm=row(p["ret_norm"]),
        conv_w=p["lru_conv_w"], conv_b=row(p["lru_conv_b"]),
        wa=jnp.einsum("nij,nm->nimj", p["lru_wa"], eye_b).reshape(MIX_W, MIX_W).astype(BF16), ba=row(p["lru_ba"]),
        wx=jnp.einsum("nij,nm->nimj", p["lru_wx"], eye_b).reshape(MIX_W, MIX_W).astype(BF16), bx=row(p["lru_bx"]),
        sp=row(jax.nn.softplus(-p["lru_lambda"])),
        bt_re=jnp.einsum("gnp,gh->gphn", bb_re, eye_g).reshape(width, n_state).astype(BF16),
        bt_im=jnp.einsum("gnp,gh->gphn", bb_im, eye_g).reshape(width, n_state).astype(BF16),
        lb_re=row(lb_re), lb_im=row(lb_im),
        ct_re=jnp.einsum("gpn,gh->gnhp", p["s5_c_re"], eye_g).reshape(n_state, width).astype(BF16),
        ct_im=jnp.einsum("gpn,gh->gnhp", p["s5_c_im"], eye_g).reshape(n_state, width).astype(BF16),
        s5_d=row(p["s5_d"]), glu_w=p["s5_glu_w"].astype(BF16), glu_b=row(p["s5_glu_b"]),
        b_gate=row(p["b_gate"]),
        norm_mix_pre=row(p["norm_mix_pre"]), norm_mix_post=row(p["norm_mix_post"]),
        norm_xa_pre=row(p["norm_xa_pre"]), norm_xa_post=row(p["norm_xa_post"]), norm_mem=row(p["norm_mem"]),
        norm_ffn_pre=row(p["norm_ffn_pre"]), norm_ffn_post=row(p["norm_ffn_post"]),
    )


_PREP_INPUTS = ("hg_norm", "ret_norm", "lru_conv_w", "lru_conv_b", "lru_wa", "lru_ba", "lru_wx", "lru_bx", "lru_lambda",
                "s5_lam_re", "s5_lam_im", "s5_b_re", "s5_b_im", "s5_c_re", "s5_c_im", "s5_d", "s5_log_dt", "s5_glu_w",
                "s5_glu_b", "b_gate", "norm_mix_pre", "norm_mix_post", "norm_xa_pre", "norm_xa_post", "norm_mem",
                "norm_ffn_pre", "norm_ffn_post")


def _retention_constants():
    lg = np.log1p(-np.power(2.0, -5.0 - np.arange(HEADS)))
    idx = np.arange(RET_CHUNK)

    def lanes(per_head_rows):
        return np.repeat(per_head_rows.T[:, :, None], HEAD_DIM, axis=2).reshape(RET_CHUNK, MIX_W)

    xi = lanes(np.exp((idx + 1.0)[None, :] * lg[:, None]))
    zeta = lanes(np.exp((RET_CHUNK - 1.0 - idx)[None, :] * lg[:, None]))
    rel = idx[:, None] - idx[None, :]
    decay = np.where(rel[None] >= 0, np.exp(np.maximum(rel, 0)[None] * lg[:, None, None]), 0.0)
    decay = np.transpose(decay, (1, 0, 2)).reshape(RET_CHUNK, HEADS * RET_CHUNK)
    g_end = np.repeat(np.exp(RET_CHUNK * lg), HEAD_DIM)[None, :]
    return tuple(jnp.asarray(a, F32) for a in (xi, zeta, decay, g_end))


def _rotary_tables(seq):
    pos = jnp.arange(seq, dtype=F32)
    inv_freq = 10000.0 ** (-jnp.arange(0, HEAD_DIM, 2, dtype=F32) / HEAD_DIM)
    ang = pos[:, None] * inv_freq[None, :]
    cos, sin = jnp.cos(ang), jnp.sin(ang)
    cos_t = jnp.tile(jnp.repeat(cos, 2, axis=1), (1, HEADS))
    sin_t = jnp.tile(jnp.stack([-sin, sin], axis=-1).reshape(seq, HEAD_DIM), (1, HEADS))
    return cos_t, sin_t


TB_HG = 512
TB_RET = 512
TB_LRU = 256
TB_S5 = 256
TB_ROW = 256
TB_XA = 512

_STATE = (MIX_W, MIX_W)
_TAIL = (8, MIX_W)
_S5_TAIL = (8, S5_GROUPS * S5_STATE)


def _mixer_operands(l, z, kp, rot, ret_c):
    xi, zeta, decay, g_end = ret_c
    return dict(
        hg=dict(name=f"hgrn2_{l}", fn=fn_hgrn2, rows=[(z, 4 * MIX_W, 0)], params=[kp["lb"], kp["hg_norm"]],
                tb=TB_HG, states=(_STATE,), diff_params=(0, 1)),
        ret=dict(name=f"retention_{l}", fn=fn_retention, rows=[(z, 4 * MIX_W, 1), (rot[0], MIX_W, 0), (rot[1], MIX_W, 0)],
                 params=[kp["ret_norm"], xi, zeta, decay, g_end], tb=TB_RET, states=(_STATE,), diff_params=(0,)),
        lru=dict(name=f"rglru_{l}", fn=fn_rglru, rows=[(z, 2 * MIX_W, 4)],
                 params=[kp["conv_w"], kp["conv_b"], kp["wa"], kp["ba"], kp["wx"], kp["bx"], kp["sp"]],
                 tb=TB_LRU, states=(_TAIL, _TAIL), diff_params=(0, 1, 2, 3, 4, 5, 6)),
        s5=dict(name=f"s5_{l}", fn=fn_s5, rows=[(z, MIX_W, 10)],
                params=[kp["bt_re"], kp["bt_im"], kp["lb_re"], kp["lb_im"], kp["ct_re"], kp["ct_im"], kp["s5_d"],
                        kp["glu_w"], kp["glu_b"]],
                tb=TB_S5, states=(_S5_TAIL, _S5_TAIL), diff_params=tuple(range(9))),
    )


def _layer_forward(l, x, h, mem, w_in, shards, next_w_in_shard, kp, rot, ret_c, g_next, target):
    d = x.shape[1]
    sv = dict(x=x, h=h)
    bw = {"w_in": w_in}

    def gather(idx):
        return [shards[i] for i in idx], True

    def take(idx, stacked):
        for i, s in zip(idx, stacked):
            bw[_BIG_NAMES[i]] = _merge_shards(s, BIG[i][1] - 1)

    z, got = matmul(f"in_proj_{l}", h, w_in, "nn", F32, side=gather(GATHER_IN["in_proj"]))
    take(GATHER_IN["in_proj"], got)
    gl, got = matmul(f"gate_proj_{l}", h, bw["w_gate"], "nn", BF16, side=gather(GATHER_IN["gate_proj"]))
    take(GATHER_IN["gate_proj"], got)
    sv.update(z=z, gl=gl)
    ops = _mixer_operands(l, z, kp, rot, ret_c)
    ys = []
    gu_shard = shards[GU]
    half = gu_shard.shape[0] // 2
    for key in ("hg", "ret", "lru", "s5"):
        o = ops[key]
        idx = GATHER_IN.get(key)
        side = gather(idx) if idx else ([gu_shard[:half]], True) if key == "s5" else None
        res = stage_fwd(o["name"] + "_fwd", o["fn"], o["rows"], o["params"], tb=o["tb"], outs=[(MIX_W, F32)],
                        states=o["states"], side=side)
        if idx:
            take(idx, res[3])
        elif side:
            gu_top = _merge_shards(res[3][0], BIG[GU][1] - 1)
        ys.append(res[0][0])
        sv[key + "_states"] = res[2]
    sv["ys"] = ys
    merge_params = [bw["w_up"][n] for n in range(4)] + [kp["b_gate"], bw["w_out"], kp["norm_mix_post"]]
    merge_rows = [(y, MIX_W, 0) for y in ys] + [(gl, 4 * d, 0), (x, d, 0)]
    (x1,), _, _, got = stage_fwd(f"merge_{l}_fwd", fn_merge, merge_rows, merge_params, tb=TB_ROW, outs=[(d, F32)],
                                 side=([gu_shard[half:]], True))
    bw["ffn_w_gu"] = jnp.concatenate([gu_top, _merge_shards(got[0], BIG[GU][1] - 1)], axis=0)
    wk, wv = bw["xa_w_kv"][:, :d], bw["xa_w_kv"][:, d:]
    mem_params = [kp["norm_mem"], wk, wv]
    (k, v), _, _ = stage_fwd(f"mem_{l}_fwd", fn_mem, [(mem, d, 0)], mem_params, tb=mem.shape[0],
                             outs=[(d, BF16), (d, BF16)])
    xa_params = [kp["norm_xa_pre"], bw["xa_w_q"], k, v, bw["xa_w_o"], kp["norm_xa_post"], kp["norm_ffn_pre"]]
    (x2, h3), _, _, got = stage_fwd(f"xattn_{l}_fwd", fn_xattn, [(x1, d, 0)], xa_params, tb=TB_XA,
                                    outs=[(d, F32), (d, BF16)], side=gather(GATHER_IN["xattn"]))
    take(GATHER_IN["xattn"], got)
    f = bw["ffn_w_gu"].shape[1] // 2
    up_block = f // _pick_n(f)
    gate, up, a = matmul_fused(f"ffn_gu_{l}", h3, [(bw["ffn_w_gu"], 0), (bw["ffn_w_gu"], up_block)], "nn", f, [],
                               _glu_fwd_tiles, (BF16, BF16, BF16))
    if next_w_in_shard is not None:
        o3, got = matmul(f"ffn_down_{l}", a, bw["ffn_w_down"], "nn", F32, side=([next_w_in_shard], True))
        next_w_in = _merge_shards(got[0], BIG[0][1] - 1)
    else:
        o3, next_w_in = matmul(f"ffn_down_{l}", a, bw["ffn_w_down"], "nn", F32), None
    sv.update(x1=x1, k=k, v=v, x2=x2, h3=h3, gate=gate, up=up, a=a, o3=o3, merge_params=merge_params,
              merge_rows=merge_rows, mem_params=mem_params, xa_params=xa_params)
    if g_next is not None:
        (x3, hn), _, _ = stage_fwd(f"res_{l}_fwd", fn_res_norm, [(x2, d, 0), (o3, d, 0)], [kp["norm_ffn_post"], g_next],
                                   tb=TB_ROW, outs=[(d, F32), (d, BF16)])
        return x3, hn, sv, bw, next_w_in
    (dy,), (loss,), _ = stage_fwd(f"loss_{l}_fwd", fn_res_loss, [(x2, d, 0), (o3, d, 0), (target, d, 0)],
                                  [kp["norm_ffn_post"]], tb=TB_ROW, outs=[(d, F32)], accs=[(8, LANES)])
    return dy, loss[0, 0], sv, bw, next_w_in


def _layer_backward(l, sv, mem, bw, kp, rot, ret_c, g_next, dx3, dhn):
    d = sv["x"].shape[1]
    gk = {}
    parts = [None] * len(BIG)

    def scatter(*items):
        return [_split_shards(g.astype(BF16), BIG[i][1] - 1) for i, g in items], False

    def took(got, *idx):
        for i, p in zip(idx, got):
            parts[i] = p

    res_rows = [(sv["x2"], d, 0), (sv["o3"], d, 0)]
    if g_next is not None:
        (dx2, do3), (gk["norm_ffn_post"], gk["g_next"]) = stage_bwd(
            f"res_{l}_bwd", fn_res_norm, res_rows, [kp["norm_ffn_post"], g_next], [dx3, dhn], tb=TB_ROW,
            diff_rows=(0, 1), diff_params=(0, 1))
    else:
        (dx2, do3), (gk["norm_ffn_post"],) = stage_bwd(
            f"res_{l}_bwd", fn_res, res_rows, [kp["norm_ffn_post"]], [dx3], tb=TB_ROW, diff_rows=(0, 1), diff_params=(0,))
    f = sv["a"].shape[1]
    g_down = matmul(f"ffn_down_dw_{l}", sv["a"], do3, "tn", BF16)
    d_gate, d_up = matmul_fused(f"ffn_down_dx_{l}", do3, [(bw["ffn_w_down"], 0)], "nt", f, [sv["gate"], sv["up"]],
                                _glu_bwd_tiles, (BF16, BF16))
    dh3, got = matmul(f"ffn_gate_dx_{l}", d_gate, bw["ffn_w_gu"][:, :f], "nt", F32, side=scatter((8, g_down)))
    took(got, 8)
    dh3 = matmul(f"ffn_up_dx_{l}", d_up, bw["ffn_w_gu"][:, f:], "nt", F32, add=dh3)
    g_gu = jnp.concatenate([matmul(f"ffn_gate_dw_{l}", sv["h3"], d_gate, "tn", BF16),
                            matmul(f"ffn_up_dw_{l}", sv["h3"], d_up, "tn", BF16)], axis=1)
    (dx1,), xa_g, got = stage_bwd(f"xattn_{l}_bwd", fn_xattn, [(sv["x1"], d, 0)], sv["xa_params"], [dx2, dh3],
                                  tb=TB_XA, diff_rows=(0,), diff_params=tuple(range(7)), side=scatter((7, g_gu)))
    took(got, 7)
    gk["norm_xa_pre"], g_q, dk, dv, g_o, gk["norm_xa_post"], gk["norm_ffn_pre"] = xa_g
    _, (gk["norm_mem"], dwk, dwv) = stage_bwd(f"mem_{l}_bwd", fn_mem, [(mem, d, 0)], sv["mem_params"], [dk, dv],
                                              tb=mem.shape[0], diff_params=(0, 1, 2))
    g_kv = jnp.concatenate([dwk, dwv], axis=1)
    merge_d, merge_g, got = stage_bwd(f"merge_{l}_bwd", fn_merge, sv["merge_rows"], sv["merge_params"], [dx1],
                                      tb=TB_ROW, diff_rows=tuple(range(6)), diff_params=tuple(range(7)),
                                      row_dtypes=(F32, F32, F32, F32, BF16, F32),
                                      side=scatter((4, g_q), (6, g_o), (5, g_kv)))
    took(got, 4, 6, 5)
    dys, dgl, dx0 = merge_d[:4], merge_d[4], merge_d[5]
    g_up = jnp.stack(merge_g[:4])
    gk["b_gate"], g_out, gk["norm_mix_post"] = merge_g[4:]
    ops = _mixer_operands(l, sv["z"], kp, rot, ret_c)
    dz = None
    for key, dy in zip(("hg", "ret", "lru", "s5"), dys):
        o = ops[key]
        res = stage_bwd(o["name"] + "_bwd", o["fn"], o["rows"], o["params"], [dy], tb=o["tb"],
                        saved=sv[key + "_states"], diff_rows=(0,), diff_params=o["diff_params"], row_dtypes=(BF16,),
                        side=scatter((2, g_up), (3, g_out)) if key == "hg" else None, into=(dz, sv["z"].shape[1]))
        if key == "hg":
            took(res[2], 2, 3)
        dz = res[0][0]
        gk[key] = res[1]
    g_gate = matmul(f"gate_proj_dw_{l}", sv["h"], dgl, "tn", BF16)
    g_in = matmul(f"in_proj_dw_{l}", sv["h"], dz, "tn", BF16)
    dh, got = matmul(f"gate_proj_dx_{l}", dgl, bw["w_gate"], "nt", F32, side=scatter((1, g_gate)))
    took(got, 1)
    dh, got = matmul(f"in_proj_dx_{l}", dz, bw["w_in"], "nt", F32, add=dh, side=scatter((0, g_in)))
    took(got, 0)
    return dx0, dh, gk, parts


def _kernel_grads_to_prep(gk):
    hg, ret, lru, s5 = gk["hg"], gk["ret"], gk["lru"], gk["s5"]
    return dict(
        lb=hg[0], hg_norm=hg[1], ret_norm=ret[0],
        conv_w=lru[0], conv_b=lru[1], wa=lru[2], ba=lru[3], wx=lru[4], bx=lru[5], sp=lru[6],
        bt_re=s5[0], bt_im=s5[1], lb_re=s5[2], lb_im=s5[3], ct_re=s5[4], ct_im=s5[5], s5_d=s5[6], glu_w=s5[7],
        glu_b=s5[8], b_gate=gk["b_gate"], norm_mix_pre=gk["norm_mix_pre"], norm_mix_post=gk["norm_mix_post"],
        norm_xa_pre=gk["norm_xa_pre"], norm_xa_post=gk["norm_xa_post"], norm_mem=gk["norm_mem"],
        norm_ffn_pre=gk["norm_ffn_pre"], norm_ffn_post=gk["norm_ffn_post"],
    )


def _step(inp):
    x, mem, target = inp["x"][0], inp["mem"][0], inp["loss_target"][0]
    seq = x.shape[0]
    depth = inp["w_in"].shape[0]
    me = 4 * lax.axis_index("x") + 2 * lax.axis_index("y") + lax.axis_index("c")

    small_shapes = [inp[n].shape for n in _SMALL_SHARDED_NAMES]
    (small_stacked,) = _exchange("gather_small", [_pack([inp[n] for n in _SMALL_SHARDED_NAMES], F32, 8)], True)
    small_all = _unpack_stacked(small_stacked, small_shapes)
    full_small = {n: _merge_shards(s, ax) for (n, ax), s in zip(SMALL_SHARDED, small_all)}

    lbs, lbs_vjp = jax.vjp(_lower_bounds, inp["hg_lower_bounds"])
    kps, prep_vjps = [], []
    for l in range(depth):
        p = {n: (full_small[n][l] if n in full_small else inp[n][l]) for n in _PREP_INPUTS}
        p["lb"] = lbs[l]
        kp, vj = jax.vjp(_prep_layer, p)
        kps.append(kp)
        prep_vjps.append(vj)
    rot = _rotary_tables(seq)
    ret_c = _retention_constants()

    def shards(l):
        return [inp[n][l].astype(BF16) for n in _BIG_NAMES]

    (h,), _, _, (stacked,) = stage_fwd("norm_in_fwd", fn_norm, [(x, x.shape[1], 0)], [kps[0]["norm_mix_pre"]],
                                       tb=TB_ROW, outs=[(x.shape[1], BF16)], side=([shards(0)[0]], True))
    saved, bws = [], []
    xs = x
    w_in = _merge_shards(stacked, BIG[0][1] - 1)
    for l in range(depth):
        last = l + 1 == depth
        g_next = None if last else kps[l + 1]["norm_mix_pre"]
        xs, h, sv, bw, w_in = _layer_forward(l, xs, h, mem, w_in, shards(l), None if last else shards(l + 1)[0],
                                             kps[l], rot, ret_c, g_next, target)
        saved.append(sv)
        bws.append(bw)
    dy, loss_local = xs, h

    big_parts = [None] * depth
    gks = [None] * depth
    dx, dh = dy, None
    for l in reversed(range(depth)):
        g_next = kps[l + 1]["norm_mix_pre"] if l + 1 < depth else None
        dx, dh, gks[l], big_parts[l] = _layer_backward(l, saved[l], mem, bws[l], kps[l], rot, ret_c, g_next, dx, dh)
    (grad_x,), (g_pre0,) = stage_bwd("norm_in_bwd", fn_keep_norm, [(x, x.shape[1], 0)], [kps[0]["norm_mix_pre"]],
                                     [dx, dh], tb=TB_ROW, diff_rows=(0,), diff_params=(0,))
    for l in range(depth):
        gks[l]["norm_mix_pre"] = g_pre0 if l == 0 else gks[l - 1]["g_next"]

    small_grads = {n: [None] * depth for n in _PREP_INPUTS}
    d_lbs = []
    for l in range(depth):
        (gp,) = prep_vjps[l]({k: g.astype(kps[l][k].dtype) for k, g in _kernel_grads_to_prep(gks[l]).items()})
        d_lbs.append(gp["lb"])
        for n in _PREP_INPUTS:
            small_grads[n][l] = gp[n]
    small_local = {n: jnp.stack(v) for n, v in small_grads.items()}
    (small_local["hg_lower_bounds"],) = lbs_vjp(jnp.stack(d_lbs))
    small_names = REPLICATED + _SMALL_SHARDED_NAMES
    full_shapes = [small_local[n].shape for n in small_names]
    small_send = _pack([small_local[n] for n in small_names], BF16, 64)

    out = {}
    kinds = ("grad_", "delta_", "new_m_", "new_v_")
    small_parts = None
    for i, n in sorted(enumerate(_BIG_NAMES), key=lambda t: -int(np.prod(inp[t[1]].shape))):
        shape = inp[n].shape
        three = (shape[0], int(np.prod(shape[1:-1])), shape[-1])
        res = adamw("adamw_" + n, *[inp[pre + n].reshape(three) for pre in ("", "m_", "v_")],
                    [big_parts[l][i].reshape((N_DEV,) + three[1:]) for l in range(depth)],
                    side=([small_send], True) if small_parts is None else None)
        if small_parts is None:
            res, (small_parts,) = res
        for kind, a in zip(kinds, res):
            out[kind + n] = a.reshape(shape)
    zeros = [jnp.zeros(s, F32) for s in full_shapes[len(REPLICATED):]]
    res = adamw("adamw_small", *[_pack([inp[pre + n] for n in REPLICATED] + zeros, F32, 64)[None] for pre in ("", "m_", "v_")],
                [small_parts])
    res = [r[0] for r in res]
    summed = _unpack(res[0], full_shapes)
    for kind, flat in zip(("grad_", "delta_", "new_m_", "new_v_"), res):
        for n, a in zip(REPLICATED, _unpack(flat, full_shapes[:len(REPLICATED)])):
            out[kind + n] = a
    shard_g = []
    for (n, ax), g_full in zip(SMALL_SHARDED, summed[len(REPLICATED):]):
        width = inp[n].shape[ax]
        shard_g.append(lax.dynamic_slice_in_dim(g_full, me * width, width, axis=ax))
    res = adamw("adamw_small_sharded",
                *[_pack([inp[pre + n] for n in _SMALL_SHARDED_NAMES], F32, 8)[None] for pre in ("", "m_", "v_")],
                [_pack(shard_g, F32, 8)[None]])
    res = [r[0] for r in res]
    for kind, flat in zip(("grad_", "delta_", "new_m_", "new_v_"), res):
        for n, a in zip(_SMALL_SHARDED_NAMES, _unpack(flat, small_shapes)):
            out[kind + n] = a

    out["loss"] = lax.psum(loss_local, ("x", "y", "c"))
    out["grad_x"] = grad_x[None]
    return out


def kernel(x, mem, hg_lower_bounds, norm_mix_pre, norm_mix_post, w_in, w_gate, b_gate, hg_norm, ret_norm, lru_conv_w, lru_conv_b, lru_wa, lru_ba, lru_wx, lru_bx, lru_lambda, s5_lam_re, s5_lam_im, s5_b_re, s5_b_im, s5_c_re, s5_c_im, s5_d, s5_log_dt, s5_glu_w, s5_glu_b, w_up, w_out, norm_xa_pre, norm_xa_post, norm_mem, xa_w_q, xa_w_kv, xa_w_o, norm_ffn_pre, norm_ffn_post, ffn_w_gu, ffn_w_down, loss_target, m_hg_lower_bounds, m_norm_mix_pre, m_norm_mix_post, m_w_in, m_w_gate, m_b_gate, m_hg_norm, m_ret_norm, m_lru_conv_w, m_lru_conv_b, m_lru_wa, m_lru_ba, m_lru_wx, m_lru_bx, m_lru_lambda, m_s5_lam_re, m_s5_lam_im, m_s5_b_re, m_s5_b_im, m_s5_c_re, m_s5_c_im, m_s5_d, m_s5_log_dt, m_s5_glu_w, m_s5_glu_b, m_w_up, m_w_out, m_norm_xa_pre, m_norm_xa_post, m_norm_mem, m_xa_w_q, m_xa_w_kv, m_xa_w_o, m_norm_ffn_pre, m_norm_ffn_post, m_ffn_w_gu, m_ffn_w_down, v_hg_lower_bounds, v_norm_mix_pre, v_norm_mix_post, v_w_in, v_w_gate, v_b_gate, v_hg_norm, v_ret_norm, v_lru_conv_w, v_lru_conv_b, v_lru_wa, v_lru_ba, v_lru_wx, v_lru_bx, v_lru_lambda, v_s5_lam_re, v_s5_lam_im, v_s5_b_re, v_s5_b_im, v_s5_c_re, v_s5_c_im, v_s5_d, v_s5_log_dt, v_s5_glu_w, v_s5_glu_b, v_w_up, v_w_out, v_norm_xa_pre, v_norm_xa_post, v_norm_mem, v_xa_w_q, v_xa_w_kv, v_xa_w_o, v_norm_ffn_pre, v_norm_ffn_post, v_ffn_w_gu, v_ffn_w_down):
    values = (x, mem, hg_lower_bounds, norm_mix_pre, norm_mix_post, w_in, w_gate, b_gate, hg_norm, ret_norm, lru_conv_w, lru_conv_b, lru_wa, lru_ba, lru_wx, lru_bx, lru_lambda, s5_lam_re, s5_lam_im, s5_b_re, s5_b_im, s5_c_re, s5_c_im, s5_d, s5_log_dt, s5_glu_w, s5_glu_b, w_up, w_out, norm_xa_pre, norm_xa_post, norm_mem, xa_w_q, xa_w_kv, xa_w_o, norm_ffn_pre, norm_ffn_post, ffn_w_gu, ffn_w_down, loss_target, m_hg_lower_bounds, m_norm_mix_pre, m_norm_mix_post, m_w_in, m_w_gate, m_b_gate, m_hg_norm, m_ret_norm, m_lru_conv_w, m_lru_conv_b, m_lru_wa, m_lru_ba, m_lru_wx, m_lru_bx, m_lru_lambda, m_s5_lam_re, m_s5_lam_im, m_s5_b_re, m_s5_b_im, m_s5_c_re, m_s5_c_im, m_s5_d, m_s5_log_dt, m_s5_glu_w, m_s5_glu_b, m_w_up, m_w_out, m_norm_xa_pre, m_norm_xa_post, m_norm_mem, m_xa_w_q, m_xa_w_kv, m_xa_w_o, m_norm_ffn_pre, m_norm_ffn_post, m_ffn_w_gu, m_ffn_w_down, v_hg_lower_bounds, v_norm_mix_pre, v_norm_mix_post, v_w_in, v_w_gate, v_b_gate, v_hg_norm, v_ret_norm, v_lru_conv_w, v_lru_conv_b, v_lru_wa, v_lru_ba, v_lru_wx, v_lru_bx, v_lru_lambda, v_s5_lam_re, v_s5_lam_im, v_s5_b_re, v_s5_b_im, v_s5_c_re, v_s5_c_im, v_s5_d, v_s5_log_dt, v_s5_glu_w, v_s5_glu_b, v_w_up, v_w_out, v_norm_xa_pre, v_norm_xa_post, v_norm_mem, v_xa_w_q, v_xa_w_kv, v_xa_w_o, v_norm_ffn_pre, v_norm_ffn_post, v_ffn_w_gu, v_ffn_w_down)
    names = ("x", "mem") + WEIGHTS + ("loss_target",) + tuple("m_" + n for n in WEIGHTS) + tuple("v_" + n for n in WEIGHTS)
    out = _step(dict(zip(names, values)))
    order = ["loss", "grad_x"] + [k + n for k in ("grad_", "delta_", "new_m_", "new_v_") for n in WEIGHTS]
    return tuple(out[k] for k in order)
```

```python
import functools

import numpy as np
import jax
import jax.numpy as jnp
from jax import lax
from jax.experimental import pallas as pl
from jax.experimental.pallas import tpu as pltpu

F32 = jnp.float32
BF16 = jnp.bfloat16
EPS = 1e-6
N_DEV = 8
LANES = 128
SUBLANES = 8
VMEM_LIMIT = 60 * 1024 * 1024

HEADS = 4
HEAD_DIM = 64
MIX_W = HEADS * HEAD_DIM
HG_CHUNK = 32
RET_CHUNK = 128
S5_GROUPS = 16
S5_GROUP = 16
S5_STATE = 64
LRU_C = 8.0
XA_HEADS = 4

ADAM_LR = 0.001
ADAM_B1 = 0.9
ADAM_B2 = 0.999
ADAM_EPS = 1e-08
ADAM_WD = 0.01
ADAM_STEP = 10

BIG = (("w_in", 2), ("w_gate", 2), ("w_up", 3), ("w_out", 1), ("xa_w_q", 1), ("xa_w_kv", 2), ("xa_w_o", 1),
       ("ffn_w_gu", 2), ("ffn_w_down", 1))
SMALL_SHARDED = (("lru_conv_w", 2), ("s5_glu_w", 1))
WEIGHTS = ("hg_lower_bounds", "norm_mix_pre", "norm_mix_post", "w_in", "w_gate", "b_gate", "hg_norm", "ret_norm",
           "lru_conv_w", "lru_conv_b", "lru_wa", "lru_ba", "lru_wx", "lru_bx", "lru_lambda", "s5_lam_re", "s5_lam_im",
           "s5_b_re", "s5_b_im", "s5_c_re", "s5_c_im", "s5_d", "s5_log_dt", "s5_glu_w", "s5_glu_b", "w_up", "w_out",
           "norm_xa_pre", "norm_xa_post", "norm_mem", "xa_w_q", "xa_w_kv", "xa_w_o", "norm_ffn_pre", "norm_ffn_post",
           "ffn_w_gu", "ffn_w_down")
GATHER_IN = {"in_proj": (1,), "gate_proj": (7,), "hg": (5,), "s5": (2, 3, 4, 6), "merge": (8,)}
_BIG_NAMES = tuple(n for n, _ in BIG)
_SMALL_SHARDED_NAMES = tuple(n for n, _ in SMALL_SHARDED)
REPLICATED = tuple(n for n in WEIGHTS if n not in _BIG_NAMES and n not in _SMALL_SHARDED_NAMES)


def _dot(a, b):
    return jnp.dot(a.astype(BF16), b.astype(BF16), preferred_element_type=F32)


def _dot_nt(a, b):
    return lax.dot_general(a.astype(BF16), b.astype(BF16), (((1,), (1,)), ((), ())), preferred_element_type=F32)


def _dot_tn(a, b):
    return lax.dot_general(a.astype(BF16), b.astype(BF16), (((0,), (0,)), ((), ())), preferred_element_type=F32)


def _dot_exact(a, b):
    return jnp.dot(a, b, precision=lax.Precision.HIGHEST, preferred_element_type=F32)


def _rms(x, g):
    return x * lax.rsqrt(jnp.mean(x * x, axis=-1, keepdims=True) + EPS) * g


def _shift_down(x, d, fill):
    return jnp.concatenate([jnp.full((d, x.shape[1]), fill, x.dtype), x[:-d]], axis=0)


def _shift_up(x, d, fill):
    return jnp.concatenate([x[d:], jnp.full((d, x.shape[1]), fill, x.dtype)], axis=0)


def _cumsum_rows(x):
    d = 1
    while d < x.shape[0]:
        x = x + _shift_down(x, d, 0.0)
        d *= 2
    return x


def _lane_head(shape, dim):
    return lax.shift_right_logical(lax.broadcasted_iota(jnp.int32, shape, dim), 6)


def _head_masks(width=MIX_W):
    head = _lane_head((1, width), 1)
    return [(head == h).astype(F32) for h in range(HEADS)]


def _block_diag_mask():
    return (_lane_head((MIX_W, MIX_W), 0) == _lane_head((MIX_W, MIX_W), 1)).astype(F32)


def _head_rms(o, g):
    ms = _dot_exact(o * o, _block_diag_mask()) * (1.0 / HEAD_DIM)
    return o * lax.rsqrt(ms + EPS) * g


def _swap_pairs(x):
    lane = lax.broadcasted_iota(jnp.int32, x.shape, 1)
    return jnp.where((lane & 1) == 0, jnp.roll(x, -1, axis=1), jnp.roll(x, 1, axis=1))


def _stack_heads(t, masks):
    return jnp.concatenate([t * m for m in masks], axis=0)


@jax.custom_vjp
def _real_scan(a, u, h0):
    return _real_scan_fwd(a, u, h0)[0]


def _real_scan_fwd(a, u, h0):
    t = a.shape[0]
    acc_a, acc_u = a, u
    d = 1
    while d < t:
        acc_u = acc_u + acc_a * _shift_down(acc_u, d, 0.0)
        acc_a = acc_a * _shift_down(acc_a, d, 1.0)
        d *= 2
    h = acc_u + acc_a * h0
    return h, (a, h, h0)


def _real_scan_bwd(res, dh):
    a, h, h0 = res
    t = a.shape[0]
    acc_a = _shift_up(a, 1, 0.0)
    g = dh
    d = 1
    while d < t:
        g = g + acc_a * _shift_up(g, d, 0.0)
        acc_a = acc_a * _shift_up(acc_a, d, 1.0)
        d *= 2
    h_prev = jnp.concatenate([h0, h[:-1]], axis=0)
    return g * h_prev, g, (a * g)[0:1]


_real_scan.defvjp(_real_scan_fwd, _real_scan_bwd)


def _cmul(ar, ai, br, bi):
    return ar * br - ai * bi, ar * bi + ai * br


def _geometric_sums(ar, ai, ur, ui, forward):
    shift = _shift_down if forward else _shift_up
    t = ur.shape[0]
    g = SUBLANES
    in_group = lax.broadcasted_iota(jnp.int32, ur.shape, 0) & (g - 1)
    pr, pi, sr, si = ar, ai, ur, ui
    d = 1
    while d < g:
        keep = in_group >= d if forward else in_group < g - d
        mr, mi = _cmul(pr, pi, jnp.where(keep, shift(sr, d, 0.0), 0.0), jnp.where(keep, shift(si, d, 0.0), 0.0))
        sr, si = sr + mr, si + mi
        pr, pi = _cmul(pr, pi, pr, pi)
        d *= 2
    row = lax.broadcasted_iota(jnp.int32, (g, ur.shape[1]), 0)
    qr, qi = ar, ai
    tr, ti = jnp.zeros((g, ur.shape[1]), F32), jnp.zeros((g, ur.shape[1]), F32)
    for r in range(g):
        here = row == (r if forward else g - 1 - r)
        tr, ti = jnp.where(here, qr, tr), jnp.where(here, qi, ti)
        qr, qi = _cmul(qr, qi, ar, ai)
    outs_r, outs_i = [], []
    cr = ci = None
    order = range(t // g) if forward else reversed(range(t // g))
    for n in order:
        br, bi = sr[n * g:(n + 1) * g], si[n * g:(n + 1) * g]
        if cr is not None:
            mr, mi = _cmul(tr, ti, cr, ci)
            br, bi = br + mr, bi + mi
        edge = slice(g - 1, g) if forward else slice(0, 1)
        cr, ci = br[edge], bi[edge]
        outs_r.append(br)
        outs_i.append(bi)
    if not forward:
        outs_r.reverse()
        outs_i.reverse()
    return jnp.concatenate(outs_r, axis=0), jnp.concatenate(outs_i, axis=0)


@jax.custom_vjp
def _complex_scan(ar, ai, ur, ui, h0r, h0i):
    return _complex_scan_fwd(ar, ai, ur, ui, h0r, h0i)[0]


def _complex_scan_fwd(ar, ai, ur, ui, h0r, h0i):
    first = lax.broadcasted_iota(jnp.int32, ur.shape, 0) == 0
    cr, ci = _cmul(ar, ai, h0r, h0i)
    hr, hi = _geometric_sums(ar, ai, ur + jnp.where(first, cr, 0.0), ui + jnp.where(first, ci, 0.0), True)
    return (hr, hi), (ar, ai, hr, hi, h0r, h0i)


def _complex_scan_bwd(res, dh):
    ar, ai, hr, hi, h0r, h0i = res
    gr, gi = _geometric_sums(ar, -ai, dh[0], dh[1], False)
    qr = jnp.concatenate([h0r, hr[:-1]], axis=0)
    qi = jnp.concatenate([h0i, hi[:-1]], axis=0)
    dar = jnp.sum(gr * qr + gi * qi, axis=0, keepdims=True)
    dai = jnp.sum(gi * qr - gr * qi, axis=0, keepdims=True)
    d0r, d0i = _cmul(ar, -ai, gr[0:1], gi[0:1])
    return dar, dai, gr, gi, d0r, d0i


_complex_scan.defvjp(_complex_scan_fwd, _complex_scan_bwd)


def fn_norm(st, rows, params):
    (x,), (g,) = rows, params
    return (), (_rms(x, g),), ()


def fn_keep_norm(st, rows, params):
    (x,), (g,) = rows, params
    return (), (x, _rms(x, g)), ()


def fn_hgrn2(st, rows, params):
    (state,) = st
    (z,) = rows
    lb, norm_g = params
    q, f_logit, v_all, g = (z[:, k * MIX_W:(k + 1) * MIX_W] for k in range(4))
    f = lb + (1.0 - lb) * jax.nn.sigmoid(f_logit)
    log_f = jnp.log(f)
    k_all = 1.0 - f
    q_all = jax.nn.silu(q)
    masks = _head_masks()
    bd = _block_diag_mask()
    c = HG_CHUNK
    col = lax.broadcasted_iota(jnp.int32, (c, HEADS * c), 1) & (c - 1)
    causal = col <= lax.broadcasted_iota(jnp.int32, (c, HEADS * c), 0)
    outs = []
    for n in range(z.shape[0] // c):
        sl = slice(n * c, (n + 1) * c)
        lf = log_f[sl]
        b = _cumsum_rows(lf)
        b_end = jnp.sum(lf, axis=0, keepdims=True)
        q_dec = q_all[sl] * jnp.exp(b)
        k_inv = k_all[sl] * jnp.exp(-b)
        k_end = k_all[sl] * jnp.exp(b_end - b)
        v = v_all[sl]
        scores = jnp.where(causal, _dot_nt(q_dec, _stack_heads(k_inv, masks)), 0.0)
        outs.append(_dot(scores, _stack_heads(v, masks)) + _dot_nt(q_dec, state))
        state = state * jnp.exp(b_end) + _dot_tn(v, k_end) * bd
    o = jnp.concatenate(outs, axis=0) if len(outs) > 1 else outs[0]
    return (state,), (_head_rms(o, norm_g) * jax.nn.silu(g),), ()


def fn_retention(st, rows, params):
    (state,) = st
    z, cos_t, sin_t = rows
    norm_g, xi, zeta, decay, g_end = params
    q, k, v_all, g = (z[:, i * MIX_W:(i + 1) * MIX_W] for i in range(4))
    q_all = q * cos_t + _swap_pairs(q) * sin_t
    k_all = (k * cos_t + _swap_pairs(k) * sin_t) * (HEAD_DIM ** -0.5)
    masks = _head_masks()
    bd = _block_diag_mask()
    c = RET_CHUNK
    outs = []
    for n in range(z.shape[0] // c):
        sl = slice(n * c, (n + 1) * c)
        qc, kc, v = q_all[sl], k_all[sl], v_all[sl]
        scores = _dot_nt(qc, _stack_heads(kc, masks)) * decay
        outs.append(_dot(scores, _stack_heads(v, masks)) + _dot_nt(qc * xi, state))
        state = state * g_end + _dot_tn(v, kc * zeta) * bd
    o = jnp.concatenate(outs, axis=0) if len(outs) > 1 else outs[0]
    return (state,), (_head_rms(o, norm_g) * jax.nn.silu(g),), ()


def fn_rglru(st, rows, params):
    tail_x, tail_h = st
    (z,) = rows
    conv_w, conv_b, wa, ba, wx, bx, sp = params
    t = z.shape[0]
    xg, xi = z[:, :MIX_W], z[:, MIX_W:]
    full = jnp.concatenate([tail_x, xi], axis=0)
    xc = conv_b
    for k in range(4):
        xc = xc + conv_w[k:k + 1] * full[5 + k:5 + k + t]
    r = jax.nn.sigmoid(_dot(xc, wa) + ba)
    ig = jax.nn.sigmoid(_dot(xc, wx) + bx)
    log_a = -LRU_C * r * sp
    a = jnp.exp(log_a)
    one_minus_a2 = -jnp.tanh(log_a) * (a * a + 1.0)
    u = jnp.sqrt(one_minus_a2) * (ig * xc)
    h = _real_scan(a, u, tail_h[7:8])
    return (xi[t - 8:], h[t - 8:]), (h * jax.nn.gelu(xg),), ()


def fn_s5(st, rows, params):
    tail_r, tail_i = st
    (u,) = rows
    bt_re, bt_im, lb_re, lb_im, ct_re, ct_im, d, glu_w, glu_b = params
    t = u.shape[0]
    bu_re = _dot(u, bt_re)
    bu_im = _dot(u, bt_im)
    h_re, h_im = _complex_scan(lb_re, lb_im, bu_re, bu_im, tail_r[7:8], tail_i[7:8])
    y = _dot(h_re, ct_re) - _dot(h_im, ct_im) + d * u
    act = jax.nn.gelu(y)
    out = act * jax.nn.sigmoid(_dot(act, glu_w) + glu_b)
    return (h_re[t - 8:], h_im[t - 8:]), (out,), ()


def fn_merge(st, rows, params):
    ya, yb, yc, yd, gl, x = rows
    w0, w1, w2, w3, b_gate, w_out, g_post = params
    d = x.shape[1]
    mix = None
    for n, (y, w) in enumerate(((ya, w0), (yb, w1), (yc, w2), (yd, w3))):
        gate = jax.nn.sigmoid(gl[:, n * d:(n + 1) * d] + b_gate[:, n * d:(n + 1) * d])
        term = gate * _dot(y, w)
        mix = term if mix is None else mix + term
    return (), (x + _rms(_dot(mix, w_out), g_post),), ()


def fn_mem(st, rows, params):
    (mem,), (g, wk, wv) = rows, params
    m = _rms(mem, g)
    return (), (_dot(m, wk), _dot(m, wv)), ()


def fn_xattn(st, rows, params):
    (x,) = rows
    g_pre, wq, k, v, wo, g_post, g_next = params
    d = x.shape[1]
    dh = d // XA_HEADS
    q = _dot(_rms(x, g_pre), wq)
    heads = []
    for h in range(XA_HEADS):
        sl = slice(h * dh, (h + 1) * dh)
        s = _dot_nt(q[:, sl], k[:, sl]) * (dh ** -0.5)
        heads.append(_dot(jax.nn.softmax(s, axis=-1), v[:, sl]))
    x2 = x + _rms(_dot(jnp.concatenate(heads, axis=1), wo), g_post)
    return (), (x2, _rms(x2, g_next)), ()


def fn_res_norm(st, rows, params):
    (x, o), (g_post, g_next) = rows, params
    xn = x + _rms(o, g_post)
    return (), (xn, _rms(xn, g_next)), ()


def fn_res(st, rows, params):
    (x, o), (g_post,) = rows, params
    return (), (x + _rms(o, g_post),), ()


def fn_res_loss(st, rows, params):
    (x, o, target), (g_post,) = rows, params
    err = x + _rms(o, g_post) - target
    inv_d = 1.0 / x.shape[1]
    loss = 0.5 * inv_d * jnp.sum(err * err)
    return (), (err * inv_d,), (jnp.full((8, LANES), loss, F32),)


def _params():
    return pltpu.CompilerParams(dimension_semantics=("arbitrary",), vmem_limit_bytes=VMEM_LIMIT)


def _row_spec(tb, width, colblk, nb, reverse):
    if reverse:
        return pl.BlockSpec((tb, width), lambda i: (nb - 1 - i, colblk))
    return pl.BlockSpec((tb, width), lambda i: (i, colblk))


def _full_spec(shape):
    return pl.BlockSpec(shape, lambda i: (0,) * len(shape), pipeline_mode=pl.Buffered(1))


def _saved_spec(shape, nb, reverse):
    if reverse:
        return pl.BlockSpec((1,) + shape, lambda i: (nb - 1 - i, 0, 0))
    return pl.BlockSpec((1,) + shape, lambda i: (i, 0, 0))


def _param_value(ref):
    v = ref[...]
    return v if v.dtype == BF16 else v.astype(F32)


def _side(side):
    arrays, gather = side if side is not None else ((), False)
    return list(arrays), gather, len(arrays)


_ANY = pl.BlockSpec(memory_space=pl.ANY)


def stage_fwd(name, fn, rows, params, *, tb, outs, states=(), accs=(), side=None):
    n_rows = rows[0][0].shape[0]
    nb = n_rows // tb
    nr, npar, no, na, ns = len(rows), len(params), len(outs), len(accs), len(states)
    side_arrays, gather, nx = _side(side)

    def body(*refs):
        row_refs, par_refs, side_in = refs[:nr], refs[nr:nr + npar], refs[nr + npar:nr + npar + nx]
        o = nr + npar + nx
        out_refs, acc_refs = refs[o:o + no], refs[o + no:o + no + na]
        saved_refs = refs[o + no + na:o + no + na + ns]
        o = o + no + na + ns
        side_out, st_refs, sems = refs[o:o + nx], refs[o + nx:o + nx + ns], refs[o + nx + ns:]
        i = pl.program_id(0)
        if nx:
            start, wait = _exchange_copies(side_in, side_out, *sems, gather)
            pl.when(i == 0)(start)

        @pl.when(i == 0)
        def _():
            for r in st_refs + acc_refs:
                r[...] = jnp.zeros_like(r)

        st = tuple(r[...] for r in st_refs)
        for sv, s in zip(saved_refs, st):
            sv[0] = s
        new_st, out_vals, acc_vals = fn(st, tuple(r[...].astype(F32) for r in row_refs),
                                        tuple(_param_value(r) for r in par_refs))
        for r, v in zip(out_refs, out_vals):
            r[...] = v.astype(r.dtype)
        for r, v in zip(acc_refs, acc_vals):
            r[...] += v
        for r, v in zip(st_refs, new_st):
            r[...] = v
        if nx:
            pl.when(i == nb - 1)(wait)

    res = pl.pallas_call(
        body, name=name, grid=(nb,),
        in_specs=[_row_spec(tb, w, cb, nb, False) for _, w, cb in rows] + [_full_spec(p.shape) for p in params]
        + [_ANY] * nx,
        out_specs=[_row_spec(tb, w, 0, nb, False) for w, _ in outs] + [_full_spec(s) for s in accs]
        + [_saved_spec(s, nb, False) for s in states] + [_ANY] * nx,
        out_shape=[jax.ShapeDtypeStruct((n_rows, w), dt) for w, dt in outs]
        + [jax.ShapeDtypeStruct(s, F32) for s in accs] + [jax.ShapeDtypeStruct((nb,) + s, F32) for s in states]
        + _exchange_shapes(side_arrays, gather),
        scratch_shapes=[pltpu.VMEM(s, F32) for s in states] + (_exchange_sems(nx) if nx else []),
        compiler_params=_params(),
    )(*[a for a, _, _ in rows], *params, *side_arrays)
    base = (res[:no], res[no:no + na], res[no + na:no + na + ns])
    return base + (res[no + na + ns:],) if nx else base


def stage_bwd(name, fn, rows, params, cts, *, tb, saved=(), diff_rows=(), diff_params=(), row_dtypes=None, side=None,
              into=None):
    n_rows = rows[0][0].shape[0]
    nb = n_rows // tb
    nr, npar, ns, nc = len(rows), len(params), len(saved), len(cts)
    ndr, ndp = len(diff_rows), len(diff_params)
    row_dtypes = row_dtypes or (F32,) * ndr
    state_shapes = [s.shape[1:] for s in saved]
    side_arrays, gather, nx = _side(side)
    into_buffer = [into[0]] if into is not None and into[0] is not None else []
    na = len(into_buffer)
    if into is not None:
        assert ndr == 1
        drow_specs = [_row_spec(tb, rows[diff_rows[0]][1], rows[diff_rows[0]][2], nb, True)]
        drow_shapes = [jax.ShapeDtypeStruct((n_rows, into[1]), row_dtypes[0])]
    else:
        drow_specs = [_row_spec(tb, rows[k][1], 0, nb, True) for k in diff_rows]
        drow_shapes = [jax.ShapeDtypeStruct((n_rows, rows[k][1]), dt) for k, dt in zip(diff_rows, row_dtypes)]

    def body(*refs):
        row_refs, par_refs = refs[:nr], refs[nr:nr + npar]
        o = nr + npar
        saved_refs, ct_refs, side_in = refs[o:o + ns], refs[o + ns:o + ns + nc], refs[o + ns + nc:o + ns + nc + nx]
        o = o + ns + nc + nx + na
        drow_refs, dpar_refs, side_out = refs[o:o + ndr], refs[o + ndr:o + ndr + ndp], refs[o + ndr + ndp:o + ndr + ndp + nx]
        o = o + ndr + ndp + nx
        dst_refs, sems = refs[o:o + ns], refs[o + ns:]
        i = pl.program_id(0)
        if nx:
            start, wait = _exchange_copies(side_in, side_out, *sems, gather)
            pl.when(i == 0)(start)

        @pl.when(i == 0)
        def _():
            for r in dst_refs + dpar_refs:
                r[...] = jnp.zeros_like(r)

        st = tuple(r[0] for r in saved_refs)
        row_vals = [r[...].astype(F32) for r in row_refs]
        par_vals = [_param_value(r) for r in par_refs]

        def f(st_, dr_, dp_):
            rv, pv = list(row_vals), list(par_vals)
            for k, v in zip(diff_rows, dr_):
                rv[k] = v
            for k, v in zip(diff_params, dp_):
                pv[k] = v
            new_st, out_vals, _ = fn(st_, tuple(rv), tuple(pv))
            return new_st, out_vals

        _, vjp = jax.vjp(f, st, tuple(row_vals[k] for k in diff_rows), tuple(par_vals[k] for k in diff_params))
        g_st, g_rows, g_par = vjp((tuple(r[...] for r in dst_refs), tuple(r[...].astype(F32) for r in ct_refs)))
        for r, v in zip(drow_refs, g_rows):
            r[...] = v.astype(r.dtype)
        for r, v in zip(dpar_refs, g_par):
            r[...] += v
        for r, v in zip(dst_refs, g_st):
            r[...] = v
        if nx:
            pl.when(i == nb - 1)(wait)

    res = pl.pallas_call(
        body, name=name, grid=(nb,),
        in_specs=[_row_spec(tb, w, cb, nb, True) for _, w, cb in rows] + [_full_spec(p.shape) for p in params]
        + [_saved_spec(s, nb, True) for s in state_shapes] + [_row_spec(tb, c.shape[1], 0, nb, True) for c in cts]
        + [_ANY] * (nx + na),
        out_specs=drow_specs + [_full_spec(params[k].shape) for k in diff_params] + [_ANY] * nx,
        out_shape=drow_shapes + [jax.ShapeDtypeStruct(params[k].shape, F32) for k in diff_params]
        + _exchange_shapes(side_arrays, gather),
        scratch_shapes=[pltpu.VMEM(s, F32) for s in state_shapes] + (_exchange_sems(nx) if nx else []),
        input_output_aliases={nr + npar + ns + nc + nx: 0} if na else {},
        compiler_params=_params(),
    )(*[a for a, _, _ in rows], *params, *saved, *cts, *side_arrays, *into_buffer)
    base = (res[:ndr], res[ndr:ndr + ndp])
    return base + (res[ndr + ndp:],) if nx else base


def _pick(n, target):
    if n <= target:
        return n
    best = None
    for t in range(LANES, target + 1, LANES):
        if n % t == 0:
            best = t
    assert best is not None, n
    return best


def _mesh_position():
    return lax.axis_index("x"), lax.axis_index("y"), lax.axis_index("c")


def _peer(pos, k):
    x, y, c = pos
    px = 1 - x if k & 4 else x
    py = 1 - y if k & 2 else y
    pc = 1 - c if k & 1 else c
    return (px, py, pc), 4 * px + 2 * py + pc


def _exchange_copies(x_refs, o_refs, send_sems, recv_sems, local_sems, gather):
    pos = _mesh_position()
    me = 4 * pos[0] + 2 * pos[1] + pos[2]
    pairs = list(enumerate(zip(x_refs, o_refs)))

    def remote(k, a, src, dst):
        peer, _ = _peer(pos, k)
        return pltpu.make_async_remote_copy(src_ref=src, dst_ref=dst, send_sem=send_sems.at[k - 1, a],
                                            recv_sem=recv_sems.at[k - 1, a], device_id=peer,
                                            device_id_type=pl.DeviceIdType.MESH)

    def local(a, x, o):
        return pltpu.make_async_copy(x if gather else x.at[me], o.at[me], local_sems.at[a])

    def start():
        for a, (x, o) in pairs:
            local(a, x, o).start()
        for k in range(1, N_DEV):
            peer_idx = _peer(pos, k)[1]
            for a, (x, o) in pairs:
                remote(k, a, x if gather else x.at[peer_idx], o.at[me]).start()

    def wait():
        for k in range(1, N_DEV):
            peer_idx = _peer(pos, k)[1]
            for a, (x, o) in pairs:
                arrival = remote(k, a, x if gather else x.at[me], o.at[peer_idx])
                arrival.wait_recv()
                arrival.wait_send()
        for a, (x, o) in pairs:
            local(a, x, o).wait()

    return start, wait


def _exchange_shapes(arrays, gather):
    return [jax.ShapeDtypeStruct(((N_DEV,) + x.shape) if gather else x.shape, x.dtype) for x in arrays]


def _exchange_sems(n):
    return [pltpu.SemaphoreType.DMA((N_DEV - 1, n)), pltpu.SemaphoreType.DMA((N_DEV - 1, n)),
            pltpu.SemaphoreType.DMA((n,))]


def _exchange(name, arrays, gather):
    n = len(arrays)

    def body(*refs):
        start, wait = _exchange_copies(refs[:n], refs[n:2 * n], *refs[2 * n:], gather)
        start()
        wait()

    return pl.pallas_call(
        body, name=name,
        in_specs=[pl.BlockSpec(memory_space=pl.ANY)] * n, out_specs=[pl.BlockSpec(memory_space=pl.ANY)] * n,
        out_shape=_exchange_shapes(arrays, gather), scratch_shapes=_exchange_sems(n),
    )(*arrays)


def _pick_n(n):
    return 1408 if n % 1408 == 0 else _pick(n, 512)


def matmul(name, a, b, mode, out_dtype, add=None, side=None):
    if mode == "tn":
        k, m = a.shape
    else:
        m, k = a.shape
    n = b.shape[0] if mode == "nt" else b.shape[1]
    if mode == "tn":
        tm, tn, tk = _pick(m, 1408), _pick(n, 1408), _pick(k, 1024)
    else:
        tm, tn, tk = _pick(m, 1024), _pick_n(n), _pick(k, 2816)
    nk = k // tk
    grid = (m // tm, n // tn, nk)
    a_spec = pl.BlockSpec((tk, tm), lambda i, j, kk: (kk, i)) if mode == "tn" else pl.BlockSpec((tm, tk), lambda i, j, kk: (i, kk))
    b_spec = pl.BlockSpec((tn, tk), lambda i, j, kk: (j, kk)) if mode == "nt" else pl.BlockSpec((tk, tn), lambda i, j, kk: (kk, j))
    o_spec = pl.BlockSpec((tm, tn), lambda i, j, kk: (i, j))
    dims = {"nn": (((1,), (0,)), ((), ())), "nt": (((1,), (1,)), ((), ())), "tn": (((0,), (0,)), ((), ()))}[mode]
    has_add = add is not None
    side_arrays, gather = side if side is not None else ((), False)
    ns = len(side_arrays)
    n_in = 2 + has_add

    def body(*refs):
        a_ref, b_ref = refs[0], refs[1]
        side_in, o_ref, side_out = refs[n_in:n_in + ns], refs[n_in + ns], refs[n_in + ns + 1:n_in + 2 * ns + 1]
        scratch = refs[n_in + 2 * ns + 1:]
        ids = [pl.program_id(d) for d in range(3)]
        if ns:
            start, wait = _exchange_copies(side_in, side_out, *scratch[-3:], gather)
            pl.when((ids[0] == 0) & (ids[1] == 0) & (ids[2] == 0))(start)
        part = lax.dot_general(a_ref[...].astype(BF16), b_ref[...].astype(BF16), dims, preferred_element_type=F32)
        if nk == 1:
            o_ref[...] = (part + refs[2][...].astype(F32) if has_add else part).astype(o_ref.dtype)
        else:
            acc_ref = scratch[0]

            @pl.when(ids[2] == 0)
            def _():
                acc_ref[...] = part + refs[2][...].astype(F32) if has_add else part

            @pl.when(ids[2] > 0)
            def _():
                acc_ref[...] += part

            @pl.when(ids[2] == nk - 1)
            def _():
                o_ref[...] = acc_ref[...].astype(o_ref.dtype)
        if ns:
            pl.when((ids[0] == grid[0] - 1) & (ids[1] == grid[1] - 1) & (ids[2] == nk - 1))(wait)

    any_spec = pl.BlockSpec(memory_space=pl.ANY)
    res = pl.pallas_call(
        body, name=name, grid=grid,
        in_specs=[a_spec, b_spec] + ([o_spec] if has_add else []) + [any_spec] * ns,
        out_specs=[o_spec] + [any_spec] * ns,
        out_shape=[jax.ShapeDtypeStruct((m, n), out_dtype)] + _exchange_shapes(side_arrays, gather),
        scratch_shapes=([pltpu.VMEM((tm, tn), F32)] if nk > 1 else []) + (_exchange_sems(ns) if ns else []),
        compiler_params=pltpu.CompilerParams(
            dimension_semantics=("arbitrary",) * 3 if ns else ("parallel", "parallel", "arbitrary"),
            vmem_limit_bytes=VMEM_LIMIT),
    )(a, b, *([add] if has_add else []), *side_arrays)
    return (res[0], res[1:]) if ns else res[0]


def matmul_fused(name, a, bs, mode, n, extras, epilogue, out_dtypes):
    m, k = a.shape
    tm, tn = _pick(m, 1024), _pick_n(n)
    dims = {"nn": (((1,), (0,)), ((), ())), "nt": (((1,), (1,)), ((), ()))}[mode]
    nb, nx = len(bs), len(extras)

    def b_spec(off):
        if mode == "nt":
            return pl.BlockSpec((tn, k), lambda i, j: (j + off, 0))
        return pl.BlockSpec((k, tn), lambda i, j: (0, j + off))

    def body(*refs):
        a_val = refs[0][...].astype(BF16)
        parts = tuple(lax.dot_general(a_val, r[...].astype(BF16), dims, preferred_element_type=F32)
                      for r in refs[1:1 + nb])
        tiles = tuple(r[...].astype(F32) for r in refs[1 + nb:1 + nb + nx])
        for r, v in zip(refs[1 + nb + nx:], epilogue(parts, tiles)):
            r[...] = v.astype(r.dtype)

    tile = pl.BlockSpec((tm, tn), lambda i, j: (i, j))
    return pl.pallas_call(
        body, name=name, grid=(m // tm, n // tn),
        in_specs=[pl.BlockSpec((tm, k), lambda i, j: (i, 0))] + [b_spec(off) for _, off in bs] + [tile] * nx,
        out_specs=[tile] * len(out_dtypes), out_shape=[jax.ShapeDtypeStruct((m, n), dt) for dt in out_dtypes],
        compiler_params=pltpu.CompilerParams(dimension_semantics=("parallel", "parallel"),
                                             vmem_limit_bytes=VMEM_LIMIT),
    )(a, *[b for b, _ in bs], *extras)


def _glu_fwd_tiles(parts, tiles):
    gate, up = parts
    return gate, up, jax.nn.silu(gate) * up


def _glu_bwd_tiles(parts, tiles):
    (da,), (gate, up) = parts, tiles
    _, vjp = jax.vjp(lambda g, u: jax.nn.silu(g) * u, gate, up)
    return vjp(da)


def adamw(name, w, m, v, gparts, side=None):
    layers, rows, cols = w.shape
    parts = gparts[0].shape[0]
    tr = 8
    while tr * 2 * cols <= 65536 and rows % (tr * 2) == 0:
        tr *= 2
    nblk = rows // tr
    c1 = 1.0 - ADAM_B1 ** ADAM_STEP
    c2 = 1.0 - ADAM_B2 ** ADAM_STEP
    side_arrays, gather, nx = _side(side)

    def body(*refs):
        w_ref, m_ref, v_ref = refs[:3]
        g_refs = refs[3:3 + layers]
        side_in = refs[3 + layers:3 + layers + nx]
        go_ref, d_ref, mo_ref, vo_ref = refs[3 + layers + nx:7 + layers + nx]
        side_out, sems = refs[7 + layers + nx:7 + layers + 2 * nx], refs[7 + layers + 2 * nx:]
        layer = pl.program_id(0)
        if nx:
            start, wait = _exchange_copies(side_in, side_out, *sems, gather)
            pl.when((layer == 0) & (pl.program_id(1) == 0))(start)
        g = None
        for ll, g_ref in enumerate(g_refs):
            s = g_ref[0].astype(F32)
            for p in range(1, parts):
                s = s + g_ref[p].astype(F32)
            g = s if g is None else jnp.where(layer == ll, s, g)
        m_new = ADAM_B1 * m_ref[...] + (1.0 - ADAM_B1) * g
        v_new = ADAM_B2 * v_ref[...] + (1.0 - ADAM_B2) * (g * g)
        m_hat = m_new / c1
        v_hat = v_new / c2
        go_ref[...] = g
        d_ref[...] = -ADAM_LR * (m_hat / (jnp.sqrt(v_hat) + ADAM_EPS) + ADAM_WD * w_ref[...])
        mo_ref[...] = m_new
        vo_ref[...] = v_new
        if nx:
            pl.when((layer == layers - 1) & (pl.program_id(1) == nblk - 1))(wait)

    def part_spec(ll):
        return pl.BlockSpec((parts, tr, cols),
                            lambda l, i: (0, jnp.where(l == ll, i, jnp.where(l < ll, 0, nblk - 1)), 0))

    spec = pl.BlockSpec((None, tr, cols), lambda l, i: (l, i, 0))
    res = pl.pallas_call(
        body, name=name, grid=(layers, nblk),
        in_specs=[spec, spec, spec] + [part_spec(ll) for ll in range(layers)] + [_ANY] * nx,
        out_specs=[spec] * 4 + [_ANY] * nx,
        out_shape=[jax.ShapeDtypeStruct(w.shape, F32)] * 4 + _exchange_shapes(side_arrays, gather),
        scratch_shapes=_exchange_sems(nx) if nx else [],
        compiler_params=pltpu.CompilerParams(dimension_semantics=("arbitrary", "arbitrary"),
                                             vmem_limit_bytes=VMEM_LIMIT),
    )(w, m, v, *gparts, *side_arrays)
    return (res[:4], res[4:]) if nx else res


def _adamw_math(w, m, v, g):
    m_new = ADAM_B1 * m + (1.0 - ADAM_B1) * g
    v_new = ADAM_B2 * v + (1.0 - ADAM_B2) * (g * g)
    m_hat = m_new / (1.0 - ADAM_B1 ** ADAM_STEP)
    v_hat = v_new / (1.0 - ADAM_B2 ** ADAM_STEP)
    return -ADAM_LR * (m_hat / (jnp.sqrt(v_hat) + ADAM_EPS) + ADAM_WD * w), m_new, v_new


def adamw_many(name, ws, ms, vs, gparts):
    n = len(ws)

    def body(*refs):
        ins, outs = refs[:4 * n], refs[4 * n:]
        for i in range(n):
            g_ref = ins[3 * n + i]
            g = g_ref[0].astype(F32)
            for p in range(1, g_ref.shape[0]):
                g = g + g_ref[p].astype(F32)
            delta, m_new, v_new = _adamw_math(ins[i][...], ins[n + i][...], ins[2 * n + i][...], g)
            for r, val in zip(outs[4 * i:4 * i + 4], (g, delta, m_new, v_new)):
                r[...] = val

    vmem = pl.BlockSpec(memory_space=pltpu.VMEM)
    res = pl.pallas_call(
        body, name=name, in_specs=[vmem] * (4 * n), out_specs=[vmem] * (4 * n),
        out_shape=[jax.ShapeDtypeStruct(w.shape, F32) for w in ws for _ in range(4)],
        compiler_params=pltpu.CompilerParams(vmem_limit_bytes=VMEM_LIMIT),
    )(*ws, *ms, *vs, *gparts)
    return [res[4 * i:4 * i + 4] for i in range(n)]


def _pack(arrays, dtype, row_multiple):
    flat = jnp.concatenate([a.astype(dtype).reshape(-1) for a in arrays])
    unit = row_multiple * LANES
    pad = (-flat.shape[0]) % unit
    if pad:
        flat = jnp.concatenate([flat, jnp.zeros((pad,), dtype)])
    return flat.reshape(-1, LANES)


def _unpack(flat2d, shapes):
    flat = flat2d.reshape(-1)
    out, off = [], 0
    for s in shapes:
        n = int(np.prod(s))
        out.append(flat[off:off + n].reshape(s))
        off += n
    return out


def _unpack_stacked(stacked, shapes):
    flat = stacked.reshape(N_DEV, -1)
    out, off = [], 0
    for s in shapes:
        n = int(np.prod(s))
        out.append(flat[:, off:off + n].reshape((N_DEV,) + tuple(s)))
        off += n
    return out


def _merge_shards(stacked, axis):
    t = jnp.moveaxis(stacked, 0, axis)
    s = t.shape
    return t.reshape(s[:axis] + (s[axis] * s[axis + 1],) + s[axis + 2:])


def _split_shards(full, axis, n=N_DEV):
    s = full.shape
    t = full.reshape(s[:axis] + (n, s[axis] // n) + s[axis + 1:])
    return jnp.moveaxis(t, axis, 0)


def _lower_bounds(hg_lower_bounds):
    p = jax.nn.softmax(hg_lower_bounds, axis=0)
    return jnp.cumsum(p, axis=0) - p[0:1]


def _prep_layer(p):
    def row(v):
        return v.reshape(1, -1)

    eye_b = jnp.eye(HEADS, dtype=F32)
    eye_g = jnp.eye(S5_GROUPS, dtype=F32)
    step = jnp.exp(p["s5_log_dt"])[:, None]
    lam_re, lam_im = p["s5_lam_re"], p["s5_lam_im"]
    mag = jnp.exp(lam_re * step)
    lb_re = mag * jnp.cos(lam_im * step)
    lb_im = mag * jnp.sin(lam_im * step)
    den = lam_re * lam_re + lam_im * lam_im
    f_re = ((lb_re - 1.0) * lam_re + lb_im * lam_im) / den
    f_im = (lb_im * lam_re - (lb_re - 1.0) * lam_im) / den
    bb_re = f_re[..., None] * p["s5_b_re"] - f_im[..., None] * p["s5_b_im"]
    bb_im = f_re[..., None] * p["s5_b_im"] + f_im[..., None] * p["s5_b_re"]
    width = S5_GROUPS * S5_GROUP
    n_state = S5_GROUPS * S5_STATE
    return dict(
        lb=row(p["lb"]), hg_norm=row(p["hg_norm"]), ret_norm=row(p["ret_norm"]),
        conv_w=p["lru_conv_w"], conv_b=row(p["lru_conv_b"]),
        wa=jnp.einsum("nij,nm->nimj", p["lru_wa"], eye_b).reshape(MIX_W, MIX_W).astype(BF16), ba=row(p["lru_ba"]),
        wx=jnp.einsum("nij,nm->nimj", p["lru_wx"], eye_b).reshape(MIX_W, MIX_W).astype(BF16), bx=row(p["lru_bx"]),
        sp=row(jax.nn.softplus(-p["lru_lambda"])),
        bt_re=jnp.einsum("gnp,gh->gphn", bb_re, eye_g).reshape(width, n_state).astype(BF16),
        bt_im=jnp.einsum("gnp,gh->gphn", bb_im, eye_g).reshape(width, n_state).astype(BF16),
        lb_re=row(lb_re), lb_im=row(lb_im),
        ct_re=jnp.einsum("gpn,gh->gnhp", p["s5_c_re"], eye_g).reshape(n_state, width).astype(BF16),
        ct_im=jnp.einsum("gpn,gh->gnhp", p["s5_c_im"], eye_g).reshape(n_state, width).astype(BF16),
        s5_d=row(p["s5_d"]), glu_w=p["s5_glu_w"].astype(BF16), glu_b=row(p["s5_glu_b"]),
        b_gate=row(p["b_gate"]),
        norm_mix_pre=row(p["norm_mix_pre"]), norm_mix_post=row(p["norm_mix_post"]),
        norm_xa_pre=row(p["norm_xa_pre"]), norm_xa_post=row(p["norm_xa_post"]), norm_mem=row(p["norm_mem"]),
        norm_ffn_pre=row(p["norm_ffn_pre"]), norm_ffn_post=row(p["norm_ffn_post"]),
    )


_PREP_INPUTS = ("hg_norm", "ret_norm", "lru_conv_w", "lru_conv_b", "lru_wa", "lru_ba", "lru_wx", "lru_bx", "lru_lambda",
                "s5_lam_re", "s5_lam_im", "s5_b_re", "s5_b_im", "s5_c_re", "s5_c_im", "s5_d", "s5_log_dt", "s5_glu_w",
                "s5_glu_b", "b_gate", "norm_mix_pre", "norm_mix_post", "norm_xa_pre", "norm_xa_post", "norm_mem",
                "norm_ffn_pre", "norm_ffn_post")


def _retention_constants():
    lg = np.log1p(-np.power(2.0, -5.0 - np.arange(HEADS)))
    idx = np.arange(RET_CHUNK)

    def lanes(per_head_rows):
        return np.repeat(per_head_rows.T[:, :, None], HEAD_DIM, axis=2).reshape(RET_CHUNK, MIX_W)

    xi = lanes(np.exp((idx + 1.0)[None, :] * lg[:, None]))
    zeta = lanes(np.exp((RET_CHUNK - 1.0 - idx)[None, :] * lg[:, None]))
    rel = idx[:, None] - idx[None, :]
    decay = np.where(rel[None] >= 0, np.exp(np.maximum(rel, 0)[None] * lg[:, None, None]), 0.0)
    decay = np.transpose(decay, (1, 0, 2)).reshape(RET_CHUNK, HEADS * RET_CHUNK)
    g_end = np.repeat(np.exp(RET_CHUNK * lg), HEAD_DIM)[None, :]
    return tuple(jnp.asarray(a, F32) for a in (xi, zeta, decay, g_end))


def _rotary_tables(seq):
    pos = jnp.arange(seq, dtype=F32)
    inv_freq = 10000.0 ** (-jnp.arange(0, HEAD_DIM, 2, dtype=F32) / HEAD_DIM)
    ang = pos[:, None] * inv_freq[None, :]
    cos, sin = jnp.cos(ang), jnp.sin(ang)
    cos_t = jnp.tile(jnp.repeat(cos, 2, axis=1), (1, HEADS))
    sin_t = jnp.tile(jnp.stack([-sin, sin], axis=-1).reshape(seq, HEAD_DIM), (1, HEADS))
    return cos_t, sin_t


TB_HG = 512
TB_RET = 512
TB_LRU = 256
TB_S5 = 256
TB_ROW = 256
TB_XA = 512

_STATE = (MIX_W, MIX_W)
_TAIL = (8, MIX_W)
_S5_TAIL = (8, S5_GROUPS * S5_STATE)


def _mixer_operands(l, z, kp, rot, ret_c):
    xi, zeta, decay, g_end = ret_c
    return dict(
        hg=dict(name=f"hgrn2_{l}", fn=fn_hgrn2, rows=[(z, 4 * MIX_W, 0)], params=[kp["lb"], kp["hg_norm"]],
                tb=TB_HG, states=(_STATE,), diff_params=(0, 1)),
        ret=dict(name=f"retention_{l}", fn=fn_retention, rows=[(z, 4 * MIX_W, 1), (rot[0], MIX_W, 0), (rot[1], MIX_W, 0)],
                 params=[kp["ret_norm"], xi, zeta, decay, g_end], tb=TB_RET, states=(_STATE,), diff_params=(0,)),
        lru=dict(name=f"rglru_{l}", fn=fn_rglru, rows=[(z, 2 * MIX_W, 4)],
                 params=[kp["conv_w"], kp["conv_b"], kp["wa"], kp["ba"], kp["wx"], kp["bx"], kp["sp"]],
                 tb=TB_LRU, states=(_TAIL, _TAIL), diff_params=(0, 1, 2, 3, 4, 5, 6)),
        s5=dict(name=f"s5_{l}", fn=fn_s5, rows=[(z, MIX_W, 10)],
                params=[kp["bt_re"], kp["bt_im"], kp["lb_re"], kp["lb_im"], kp["ct_re"], kp["ct_im"], kp["s5_d"],
                        kp["glu_w"], kp["glu_b"]],
                tb=TB_S5, states=(_S5_TAIL, _S5_TAIL), diff_params=tuple(range(9))),
    )


def _layer_forward(l, x, h, mem, w_in, shards, next_w_in_shard, kp, rot, ret_c, g_next, target):
    d = x.shape[1]
    sv = dict(x=x, h=h)
    bw = {"w_in": w_in}

    def gather(idx):
        return [shards[i] for i in idx], True

    def take(idx, stacked):
        for i, s in zip(idx, stacked):
            bw[_BIG_NAMES[i]] = _merge_shards(s, BIG[i][1] - 1)

    z, got = matmul(f"in_proj_{l}", h, w_in, "nn", F32, side=gather(GATHER_IN["in_proj"]))
    take(GATHER_IN["in_proj"], got)
    gl, got = matmul(f"gate_proj_{l}", h, bw["w_gate"], "nn", BF16, side=gather(GATHER_IN["gate_proj"]))
    take(GATHER_IN["gate_proj"], got)
    sv.update(z=z, gl=gl)
    ops = _mixer_operands(l, z, kp, rot, ret_c)
    ys = []
    for key in ("hg", "ret", "lru", "s5"):
        o = ops[key]
        idx = GATHER_IN.get(key)
        res = stage_fwd(o["name"] + "_fwd", o["fn"], o["rows"], o["params"], tb=o["tb"], outs=[(MIX_W, F32)],
                        states=o["states"], side=gather(idx) if idx else None)
        if idx:
            take(idx, res[3])
        ys.append(res[0][0])
        sv[key + "_states"] = res[2]
    sv["ys"] = ys
    merge_params = [bw["w_up"][n] for n in range(4)] + [kp["b_gate"], bw["w_out"], kp["norm_mix_post"]]
    merge_rows = [(y, MIX_W, 0) for y in ys] + [(gl, 4 * d, 0), (x, d, 0)]
    (x1,), _, _, got = stage_fwd(f"merge_{l}_fwd", fn_merge, merge_rows, merge_params, tb=TB_ROW, outs=[(d, F32)],
                                 side=gather(GATHER_IN["merge"]))
    take(GATHER_IN["merge"], got)
    wk, wv = bw["xa_w_kv"][:, :d], bw["xa_w_kv"][:, d:]
    mem_params = [kp["norm_mem"], wk, wv]
    (k, v), _, _ = stage_fwd(f"mem_{l}_fwd", fn_mem, [(mem, d, 0)], mem_params, tb=mem.shape[0],
                             outs=[(d, BF16), (d, BF16)])
    xa_params = [kp["norm_xa_pre"], bw["xa_w_q"], k, v, bw["xa_w_o"], kp["norm_xa_post"], kp["norm_ffn_pre"]]
    res = stage_fwd(f"xattn_{l}_fwd", fn_xattn, [(x1, d, 0)], xa_params, tb=TB_XA, outs=[(d, F32), (d, BF16)],
                    side=([next_w_in_shard], True) if next_w_in_shard is not None else None)
    x2, h3 = res[0]
    next_w_in = _merge_shards(res[3][0], BIG[0][1] - 1) if next_w_in_shard is not None else None
    f = bw["ffn_w_gu"].shape[1] // 2
    up_block = f // _pick_n(f)
    gate, up, a = matmul_fused(f"ffn_gu_{l}", h3, [(bw["ffn_w_gu"], 0), (bw["ffn_w_gu"], up_block)], "nn", f, [],
                               _glu_fwd_tiles, (BF16, BF16, BF16))
    o3 = matmul(f"ffn_down_{l}", a, bw["ffn_w_down"], "nn", F32)
    sv.update(x1=x1, k=k, v=v, x2=x2, h3=h3, gate=gate, up=up, a=a, o3=o3, merge_params=merge_params,
              merge_rows=merge_rows, mem_params=mem_params, xa_params=xa_params)
    if g_next is not None:
        (x3, hn), _, _ = stage_fwd(f"res_{l}_fwd", fn_res_norm, [(x2, d, 0), (o3, d, 0)], [kp["norm_ffn_post"], g_next],
                                   tb=TB_ROW, outs=[(d, F32), (d, BF16)])
        return x3, hn, sv, bw, next_w_in
    (dy,), (loss,), _ = stage_fwd(f"loss_{l}_fwd", fn_res_loss, [(x2, d, 0), (o3, d, 0), (target, d, 0)],
                                  [kp["norm_ffn_post"]], tb=TB_ROW, outs=[(d, F32)], accs=[(8, LANES)])
    return dy, loss[0, 0], sv, bw, next_w_in


def _layer_backward(l, sv, mem, bw, kp, rot, ret_c, g_next, dx3, dhn):
    d = sv["x"].shape[1]
    gk = {}
    parts = [None] * len(BIG)

    def send(i, g):
        axis = BIG[i][1] - 1
        if isinstance(g, tuple):
            return jnp.concatenate([_split_shards(p.astype(BF16), axis, N_DEV // len(g)) for p in g], axis=0)
        return _split_shards(g.astype(BF16), axis)

    def scatter(*items):
        return [send(i, g) for i, g in items], False

    def took(got, *idx):
        for i, p in zip(idx, got):
            parts[i] = p

    res_rows = [(sv["x2"], d, 0), (sv["o3"], d, 0)]
    if g_next is not None:
        (dx2, do3), (gk["norm_ffn_post"], gk["g_next"]) = stage_bwd(
            f"res_{l}_bwd", fn_res_norm, res_rows, [kp["norm_ffn_post"], g_next], [dx3, dhn], tb=TB_ROW,
            diff_rows=(0, 1), diff_params=(0, 1))
    else:
        (dx2, do3), (gk["norm_ffn_post"],) = stage_bwd(
            f"res_{l}_bwd", fn_res, res_rows, [kp["norm_ffn_post"]], [dx3], tb=TB_ROW, diff_rows=(0, 1), diff_params=(0,))
    f = sv["a"].shape[1]
    g_down = matmul(f"ffn_down_dw_{l}", sv["a"], do3, "tn", BF16)
    d_gate, d_up = matmul_fused(f"ffn_down_dx_{l}", do3, [(bw["ffn_w_down"], 0)], "nt", f, [sv["gate"], sv["up"]],
                                _glu_bwd_tiles, (BF16, BF16))
    dh3, got = matmul(f"ffn_gate_dx_{l}", d_gate, bw["ffn_w_gu"][:, :f], "nt", F32, side=scatter((8, g_down)))
    took(got, 8)
    dh3 = matmul(f"ffn_up_dx_{l}", d_up, bw["ffn_w_gu"][:, f:], "nt", F32, add=dh3)
    g_gu = (matmul(f"ffn_gate_dw_{l}", sv["h3"], d_gate, "tn", BF16),
            matmul(f"ffn_up_dw_{l}", sv["h3"], d_up, "tn", BF16))
    (dx1,), xa_g, got = stage_bwd(f"xattn_{l}_bwd", fn_xattn, [(sv["x1"], d, 0)], sv["xa_params"], [dx2, dh3],
                                  tb=TB_XA, diff_rows=(0,), diff_params=tuple(range(7)), side=scatter((7, g_gu)))
    took(got, 7)
    gk["norm_xa_pre"], g_q, dk, dv, g_o, gk["norm_xa_post"], gk["norm_ffn_pre"] = xa_g
    _, (gk["norm_mem"], dwk, dwv) = stage_bwd(f"mem_{l}_bwd", fn_mem, [(mem, d, 0)], sv["mem_params"], [dk, dv],
                                              tb=mem.shape[0], diff_params=(0, 1, 2))
    g_kv = jnp.concatenate([dwk, dwv], axis=1)
    merge_d, merge_g, got = stage_bwd(f"merge_{l}_bwd", fn_merge, sv["merge_rows"], sv["merge_params"], [dx1],
                                      tb=TB_ROW, diff_rows=tuple(range(6)), diff_params=tuple(range(7)),
                                      row_dtypes=(F32, F32, F32, F32, BF16, F32),
                                      side=scatter((4, g_q), (6, g_o), (5, g_kv)))
    took(got, 4, 6, 5)
    dys, dgl, dx0 = merge_d[:4], merge_d[4], merge_d[5]
    g_up = jnp.stack(merge_g[:4])
    gk["b_gate"], g_out, gk["norm_mix_post"] = merge_g[4:]
    ops = _mixer_operands(l, sv["z"], kp, rot, ret_c)
    dz = None
    for key, dy in zip(("hg", "ret", "lru", "s5"), dys):
        o = ops[key]
        res = stage_bwd(o["name"] + "_bwd", o["fn"], o["rows"], o["params"], [dy], tb=o["tb"],
                        saved=sv[key + "_states"], diff_rows=(0,), diff_params=o["diff_params"], row_dtypes=(BF16,),
                        side=scatter((2, g_up), (3, g_out)) if key == "hg" else None, into=(dz, sv["z"].shape[1]))
        if key == "hg":
            took(res[2], 2, 3)
        dz = res[0][0]
        gk[key] = res[1]
    g_gate = matmul(f"gate_proj_dw_{l}", sv["h"], dgl, "tn", BF16)
    g_in = matmul(f"in_proj_dw_{l}", sv["h"], dz, "tn", BF16)
    dh, got = matmul(f"gate_proj_dx_{l}", dgl, bw["w_gate"], "nt", F32, side=scatter((1, g_gate)))
    took(got, 1)
    dh, got = matmul(f"in_proj_dx_{l}", dz, bw["w_in"], "nt", F32, add=dh, side=scatter((0, g_in)))
    took(got, 0)
    return dx0, dh, gk, parts


def _kernel_grads_to_prep(gk):
    hg, ret, lru, s5 = gk["hg"], gk["ret"], gk["lru"], gk["s5"]
    return dict(
        lb=hg[0], hg_norm=hg[1], ret_norm=ret[0],
        conv_w=lru[0], conv_b=lru[1], wa=lru[2], ba=lru[3], wx=lru[4], bx=lru[5], sp=lru[6],
        bt_re=s5[0], bt_im=s5[1], lb_re=s5[2], lb_im=s5[3], ct_re=s5[4], ct_im=s5[5], s5_d=s5[6], glu_w=s5[7],
        glu_b=s5[8], b_gate=gk["b_gate"], norm_mix_pre=gk["norm_mix_pre"], norm_mix_post=gk["norm_mix_post"],
        norm_xa_pre=gk["norm_xa_pre"], norm_xa_post=gk["norm_xa_post"], norm_mem=gk["norm_mem"],
        norm_ffn_pre=gk["norm_ffn_pre"], norm_ffn_post=gk["norm_ffn_post"],
    )


def _step(inp):
    x, mem, target = inp["x"][0], inp["mem"][0], inp["loss_target"][0]
    seq = x.shape[0]
    depth = inp["w_in"].shape[0]
    me = 4 * lax.axis_index("x") + 2 * lax.axis_index("y") + lax.axis_index("c")

    small_shapes = [inp[n].shape for n in _SMALL_SHARDED_NAMES]
    (small_stacked,) = _exchange("gather_small", [_pack([inp[n] for n in _SMALL_SHARDED_NAMES], F32, 8)], True)
    small_all = _unpack_stacked(small_stacked, small_shapes)
    full_small = {n: _merge_shards(s, ax) for (n, ax), s in zip(SMALL_SHARDED, small_all)}

    lbs, lbs_vjp = jax.vjp(_lower_bounds, inp["hg_lower_bounds"])
    kps, prep_vjps = [], []
    for l in range(depth):
        p = {n: (full_small[n][l] if n in full_small else inp[n][l]) for n in _PREP_INPUTS}
        p["lb"] = lbs[l]
        kp, vj = jax.vjp(_prep_layer, p)
        kps.append(kp)
        prep_vjps.append(vj)
    rot = _rotary_tables(seq)
    ret_c = _retention_constants()

    def shards(l):
        return [inp[n][l].astype(BF16) for n in _BIG_NAMES]

    (h,), _, _, (stacked,) = stage_fwd("norm_in_fwd", fn_norm, [(x, x.shape[1], 0)], [kps[0]["norm_mix_pre"]],
                                       tb=TB_ROW, outs=[(x.shape[1], BF16)], side=([shards(0)[0]], True))
    saved, bws = [], []
    xs = x
    w_in = _merge_shards(stacked, BIG[0][1] - 1)
    for l in range(depth):
        last = l + 1 == depth
        g_next = None if last else kps[l + 1]["norm_mix_pre"]
        xs, h, sv, bw, w_in = _layer_forward(l, xs, h, mem, w_in, shards(l), None if last else shards(l + 1)[0],
                                             kps[l], rot, ret_c, g_next, target)
        saved.append(sv)
        bws.append(bw)
    dy, loss_local = xs, h

    big_parts = [None] * depth
    gks = [None] * depth
    dx, dh = dy, None
    for l in reversed(range(depth)):
        g_next = kps[l + 1]["norm_mix_pre"] if l + 1 < depth else None
        dx, dh, gks[l], big_parts[l] = _layer_backward(l, saved[l], mem, bws[l], kps[l], rot, ret_c, g_next, dx, dh)
    (grad_x,), (g_pre0,) = stage_bwd("norm_in_bwd", fn_keep_norm, [(x, x.shape[1], 0)], [kps[0]["norm_mix_pre"]],
                                     [dx, dh], tb=TB_ROW, diff_rows=(0,), diff_params=(0,))
    for l in range(depth):
        gks[l]["norm_mix_pre"] = g_pre0 if l == 0 else gks[l - 1]["g_next"]

    small_grads = {n: [None] * depth for n in _PREP_INPUTS}
    d_lbs = []
    for l in range(depth):
        (gp,) = prep_vjps[l]({k: g.astype(kps[l][k].dtype) for k, g in _kernel_grads_to_prep(gks[l]).items()})
        d_lbs.append(gp["lb"])
        for n in _PREP_INPUTS:
            small_grads[n][l] = gp[n]
    small_local = {n: jnp.stack(v) for n, v in small_grads.items()}
    (small_local["hg_lower_bounds"],) = lbs_vjp(jnp.stack(d_lbs))
    small_names = REPLICATED + _SMALL_SHARDED_NAMES
    small_send = [small_local[n].astype(BF16) for n in small_names]

    out = {}
    kinds = ("grad_", "delta_", "new_m_", "new_v_")
    small_parts = None
    for i, n in sorted(enumerate(_BIG_NAMES), key=lambda t: -int(np.prod(inp[t[1]].shape))):
        shape = inp[n].shape
        three = (shape[0], int(np.prod(shape[1:-1])), shape[-1])
        res = adamw("adamw_" + n, *[inp[pre + n].reshape(three) for pre in ("", "m_", "v_")],
                    [big_parts[l][i].reshape((N_DEV,) + three[1:]) for l in range(depth)],
                    side=(small_send, True) if small_parts is None else None)
        if small_parts is None:
            res, small_parts = res
        for kind, a in zip(kinds, res):
            out[kind + n] = a.reshape(shape)
    small_parts = list(small_parts)
    for j, (n, ax) in enumerate(SMALL_SHARDED):
        width = inp[n].shape[ax]
        k = len(REPLICATED) + j
        small_parts[k] = lax.dynamic_slice_in_dim(small_parts[k], me * width, width, axis=ax + 1)
    res = adamw_many("adamw_small", *[[inp[pre + n] for n in small_names] for pre in ("", "m_", "v_")], small_parts)
    for n, quad in zip(small_names, res):
        for kind, a in zip(kinds, quad):
            out[kind + n] = a

    out["loss"] = lax.psum(loss_local, ("x", "y", "c"))
    out["grad_x"] = grad_x[None]
    return out


def kernel(x, mem, hg_lower_bounds, norm_mix_pre, norm_mix_post, w_in, w_gate, b_gate, hg_norm, ret_norm, lru_conv_w, lru_conv_b, lru_wa, lru_ba, lru_wx, lru_bx, lru_lambda, s5_lam_re, s5_lam_im, s5_b_re, s5_b_im, s5_c_re, s5_c_im, s5_d, s5_log_dt, s5_glu_w, s5_glu_b, w_up, w_out, norm_xa_pre, norm_xa_post, norm_mem, xa_w_q, xa_w_kv, xa_w_o, norm_ffn_pre, norm_ffn_post, ffn_w_gu, ffn_w_down, loss_target, m_hg_lower_bounds, m_norm_mix_pre, m_norm_mix_post, m_w_in, m_w_gate, m_b_gate, m_hg_norm, m_ret_norm, m_lru_conv_w, m_lru_conv_b, m_lru_wa, m_lru_ba, m_lru_wx, m_lru_bx, m_lru_lambda, m_s5_lam_re, m_s5_lam_im, m_s5_b_re, m_s5_b_im, m_s5_c_re, m_s5_c_im, m_s5_d, m_s5_log_dt, m_s5_glu_w, m_s5_glu_b, m_w_up, m_w_out, m_norm_xa_pre, m_norm_xa_post, m_norm_mem, m_xa_w_q, m_xa_w_kv, m_xa_w_o, m_norm_ffn_pre, m_norm_ffn_post, m_ffn_w_gu, m_ffn_w_down, v_hg_lower_bounds, v_norm_mix_pre, v_norm_mix_post, v_w_in, v_w_gate, v_b_gate, v_hg_norm, v_ret_norm, v_lru_conv_w, v_lru_conv_b, v_lru_wa, v_lru_ba, v_lru_wx, v_lru_bx, v_lru_lambda, v_s5_lam_re, v_s5_lam_im, v_s5_b_re, v_s5_b_im, v_s5_c_re, v_s5_c_im, v_s5_d, v_s5_log_dt, v_s5_glu_w, v_s5_glu_b, v_w_up, v_w_out, v_norm_xa_pre, v_norm_xa_post, v_norm_mem, v_xa_w_q, v_xa_w_kv, v_xa_w_o, v_norm_ffn_pre, v_norm_ffn_post, v_ffn_w_gu, v_ffn_w_down):
    values = (x, mem, hg_lower_bounds, norm_mix_pre, norm_mix_post, w_in, w_gate, b_gate, hg_norm, ret_norm, lru_conv_w, lru_conv_b, lru_wa, lru_ba, lru_wx, lru_bx, lru_lambda, s5_lam_re, s5_lam_im, s5_b_re, s5_b_im, s5_c_re, s5_c_im, s5_d, s5_log_dt, s5_glu_w, s5_glu_b, w_up, w_out, norm_xa_pre, norm_xa_post, norm_mem, xa_w_q, xa_w_kv, xa_w_o, norm_ffn_pre, norm_ffn_post, ffn_w_gu, ffn_w_down, loss_target, m_hg_lower_bounds, m_norm_mix_pre, m_norm_mix_post, m_w_in, m_w_gate, m_b_gate, m_hg_norm, m_ret_norm, m_lru_conv_w, m_lru_conv_b, m_lru_wa, m_lru_ba, m_lru_wx, m_lru_bx, m_lru_lambda, m_s5_lam_re, m_s5_lam_im, m_s5_b_re, m_s5_b_im, m_s5_c_re, m_s5_c_im, m_s5_d, m_s5_log_dt, m_s5_glu_w, m_s5_glu_b, m_w_up, m_w_out, m_norm_xa_pre, m_norm_xa_post, m_norm_mem, m_xa_w_q, m_xa_w_kv, m_xa_w_o, m_norm_ffn_pre, m_norm_ffn_post, m_ffn_w_gu, m_ffn_w_down, v_hg_lower_bounds, v_norm_mix_pre, v_norm_mix_post, v_w_in, v_w_gate, v_b_gate, v_hg_norm, v_ret_norm, v_lru_conv_w, v_lru_conv_b, v_lru_wa, v_lru_ba, v_lru_wx, v_lru_bx, v_lru_lambda, v_s5_lam_re, v_s5_lam_im, v_s5_b_re, v_s5_b_im, v_s5_c_re, v_s5_c_im, v_s5_d, v_s5_log_dt, v_s5_glu_w, v_s5_glu_b, v_w_up, v_w_out, v_norm_xa_pre, v_norm_xa_post, v_norm_mem, v_xa_w_q, v_xa_w_kv, v_xa_w_o, v_norm_ffn_pre, v_norm_ffn_post, v_ffn_w_gu, v_ffn_w_down)
    names = ("x", "mem") + WEIGHTS + ("loss_target",) + tuple("m_" + n for n in WEIGHTS) + tuple("v_" + n for n in WEIGHTS)
    out = _step(dict(zip(names, values)))
    order = ["loss", "grad_x"] + [k + n for k in ("grad_", "delta_", "new_m_", "new_v_") for n in WEIGHTS]
    return tuple(out[k] for k in order)
```

```python
import functools

import numpy as np
import jax
import jax.numpy as jnp
from jax import lax
from jax.experimental import pallas as pl
from jax.experimental.pallas import tpu as pltpu

F32 = jnp.float32
BF16 = jnp.bfloat16
EPS = 1e-6
N_DEV = 8
LANES = 128
SUBLANES = 8
VMEM_LIMIT = 60 * 1024 * 1024

HEADS = 4
HEAD_DIM = 64
MIX_W = HEADS * HEAD_DIM
HG_CHUNK = 32
RET_CHUNK = 128
S5_GROUPS = 16
S5_GROUP = 16
S5_STATE = 64
LRU_C = 8.0
XA_HEADS = 4

ADAM_LR = 0.001
ADAM_B1 = 0.9
ADAM_B2 = 0.999
ADAM_EPS = 1e-08
ADAM_WD = 0.01
ADAM_STEP = 10

BIG = (("w_in", 2), ("w_gate", 2), ("w_up", 3), ("w_out", 1), ("xa_w_q", 1), ("xa_w_kv", 2), ("xa_w_o", 1),
       ("ffn_w_gu", 2), ("ffn_w_down", 1))
SMALL_SHARDED = (("lru_conv_w", 2), ("s5_glu_w", 1))
WEIGHTS = ("hg_lower_bounds", "norm_mix_pre", "norm_mix_post", "w_in", "w_gate", "b_gate", "hg_norm", "ret_norm",
           "lru_conv_w", "lru_conv_b", "lru_wa", "lru_ba", "lru_wx", "lru_bx", "lru_lambda", "s5_lam_re", "s5_lam_im",
           "s5_b_re", "s5_b_im", "s5_c_re", "s5_c_im", "s5_d", "s5_log_dt", "s5_glu_w", "s5_glu_b", "w_up", "w_out",
           "norm_xa_pre", "norm_xa_post", "norm_mem", "xa_w_q", "xa_w_kv", "xa_w_o", "norm_ffn_pre", "norm_ffn_post",
           "ffn_w_gu", "ffn_w_down")
GATHER_IN = {"in_proj": (1,), "gate_proj": (7,), "hg": (5,), "s5": (2, 3, 4, 6), "merge": (8,)}
_BIG_NAMES = tuple(n for n, _ in BIG)
_SMALL_SHARDED_NAMES = tuple(n for n, _ in SMALL_SHARDED)
REPLICATED = tuple(n for n in WEIGHTS if n not in _BIG_NAMES and n not in _SMALL_SHARDED_NAMES)


def _dot(a, b):
    return jnp.dot(a.astype(BF16), b.astype(BF16), preferred_element_type=F32)


def _dot_nt(a, b):
    return lax.dot_general(a.astype(BF16), b.astype(BF16), (((1,), (1,)), ((), ())), preferred_element_type=F32)


def _dot_tn(a, b):
    return lax.dot_general(a.astype(BF16), b.astype(BF16), (((0,), (0,)), ((), ())), preferred_element_type=F32)


def _dot_exact(a, b):
    return jnp.dot(a, b, precision=lax.Precision.HIGHEST, preferred_element_type=F32)


def _rms(x, g):
    return x * lax.rsqrt(jnp.mean(x * x, axis=-1, keepdims=True) + EPS) * g


def _shift_down(x, d, fill):
    return jnp.concatenate([jnp.full((d, x.shape[1]), fill, x.dtype), x[:-d]], axis=0)


def _shift_up(x, d, fill):
    return jnp.concatenate([x[d:], jnp.full((d, x.shape[1]), fill, x.dtype)], axis=0)


def _cumsum_rows(x):
    d = 1
    while d < x.shape[0]:
        x = x + _shift_down(x, d, 0.0)
        d *= 2
    return x


def _lane_head(shape, dim):
    return lax.shift_right_logical(lax.broadcasted_iota(jnp.int32, shape, dim), 6)


def _head_masks(width=MIX_W):
    head = _lane_head((1, width), 1)
    return [(head == h).astype(F32) for h in range(HEADS)]


def _block_diag_mask():
    return (_lane_head((MIX_W, MIX_W), 0) == _lane_head((MIX_W, MIX_W), 1)).astype(F32)


def _head_rms(o, g):
    ms = _dot_exact(o * o, _block_diag_mask()) * (1.0 / HEAD_DIM)
    return o * lax.rsqrt(ms + EPS) * g


def _swap_pairs(x):
    lane = lax.broadcasted_iota(jnp.int32, x.shape, 1)
    return jnp.where((lane & 1) == 0, jnp.roll(x, -1, axis=1), jnp.roll(x, 1, axis=1))


def _stack_heads(t, masks):
    return jnp.concatenate([t * m for m in masks], axis=0)


@jax.custom_vjp
def _real_scan(a, u, h0):
    return _real_scan_fwd(a, u, h0)[0]


def _real_scan_fwd(a, u, h0):
    t = a.shape[0]
    acc_a, acc_u = a, u
    d = 1
    while d < t:
        acc_u = acc_u + acc_a * _shift_down(acc_u, d, 0.0)
        acc_a = acc_a * _shift_down(acc_a, d, 1.0)
        d *= 2
    h = acc_u + acc_a * h0
    return h, (a, h, h0)


def _real_scan_bwd(res, dh):
    a, h, h0 = res
    t = a.shape[0]
    acc_a = _shift_up(a, 1, 0.0)
    g = dh
    d = 1
    while d < t:
        g = g + acc_a * _shift_up(g, d, 0.0)
        acc_a = acc_a * _shift_up(acc_a, d, 1.0)
        d *= 2
    h_prev = jnp.concatenate([h0, h[:-1]], axis=0)
    return g * h_prev, g, (a * g)[0:1]


_real_scan.defvjp(_real_scan_fwd, _real_scan_bwd)


def _cmul(ar, ai, br, bi):
    return ar * br - ai * bi, ar * bi + ai * br


def _geometric_sums(ar, ai, ur, ui, forward):
    shift = _shift_down if forward else _shift_up
    t = ur.shape[0]
    g = SUBLANES
    in_group = lax.broadcasted_iota(jnp.int32, ur.shape, 0) & (g - 1)
    pr, pi, sr, si = ar, ai, ur, ui
    d = 1
    while d < g:
        keep = in_group >= d if forward else in_group < g - d
        mr, mi = _cmul(pr, pi, jnp.where(keep, shift(sr, d, 0.0), 0.0), jnp.where(keep, shift(si, d, 0.0), 0.0))
        sr, si = sr + mr, si + mi
        pr, pi = _cmul(pr, pi, pr, pi)
        d *= 2
    row = lax.broadcasted_iota(jnp.int32, (g, ur.shape[1]), 0)
    qr, qi = ar, ai
    tr, ti = jnp.zeros((g, ur.shape[1]), F32), jnp.zeros((g, ur.shape[1]), F32)
    for r in range(g):
        here = row == (r if forward else g - 1 - r)
        tr, ti = jnp.where(here, qr, tr), jnp.where(here, qi, ti)
        qr, qi = _cmul(qr, qi, ar, ai)
    outs_r, outs_i = [], []
    cr = ci = None
    order = range(t // g) if forward else reversed(range(t // g))
    for n in order:
        br, bi = sr[n * g:(n + 1) * g], si[n * g:(n + 1) * g]
        if cr is not None:
            mr, mi = _cmul(tr, ti, cr, ci)
            br, bi = br + mr, bi + mi
        edge = slice(g - 1, g) if forward else slice(0, 1)
        cr, ci = br[edge], bi[edge]
        outs_r.append(br)
        outs_i.append(bi)
    if not forward:
        outs_r.reverse()
        outs_i.reverse()
    return jnp.concatenate(outs_r, axis=0), jnp.concatenate(outs_i, axis=0)


@jax.custom_vjp
def _complex_scan(ar, ai, ur, ui, h0r, h0i):
    return _complex_scan_fwd(ar, ai, ur, ui, h0r, h0i)[0]


def _complex_scan_fwd(ar, ai, ur, ui, h0r, h0i):
    first = lax.broadcasted_iota(jnp.int32, ur.shape, 0) == 0
    cr, ci = _cmul(ar, ai, h0r, h0i)
    hr, hi = _geometric_sums(ar, ai, ur + jnp.where(first, cr, 0.0), ui + jnp.where(first, ci, 0.0), True)
    return (hr, hi), (ar, ai, hr, hi, h0r, h0i)


def _complex_scan_bwd(res, dh):
    ar, ai, hr, hi, h0r, h0i = res
    gr, gi = _geometric_sums(ar, -ai, dh[0], dh[1], False)
    qr = jnp.concatenate([h0r, hr[:-1]], axis=0)
    qi = jnp.concatenate([h0i, hi[:-1]], axis=0)
    dar = jnp.sum(gr * qr + gi * qi, axis=0, keepdims=True)
    dai = jnp.sum(gi * qr - gr * qi, axis=0, keepdims=True)
    d0r, d0i = _cmul(ar, -ai, gr[0:1], gi[0:1])
    return dar, dai, gr, gi, d0r, d0i


_complex_scan.defvjp(_complex_scan_fwd, _complex_scan_bwd)


def fn_norm(st, rows, params):
    (x,), (g,) = rows, params
    return (), (_rms(x, g),), ()


def fn_keep_norm(st, rows, params):
    (x,), (g,) = rows, params
    return (), (x, _rms(x, g)), ()


def fn_hgrn2(st, rows, params):
    (state,) = st
    (z,) = rows
    lb, norm_g = params
    q, f_logit, v_all, g = (z[:, k * MIX_W:(k + 1) * MIX_W] for k in range(4))
    f = lb + (1.0 - lb) * jax.nn.sigmoid(f_logit)
    log_f = jnp.log(f)
    k_all = 1.0 - f
    q_all = jax.nn.silu(q)
    masks = _head_masks()
    bd = _block_diag_mask()
    c = HG_CHUNK
    col = lax.broadcasted_iota(jnp.int32, (c, HEADS * c), 1) & (c - 1)
    causal = col <= lax.broadcasted_iota(jnp.int32, (c, HEADS * c), 0)
    outs = []
    for n in range(z.shape[0] // c):
        sl = slice(n * c, (n + 1) * c)
        lf = log_f[sl]
        b = _cumsum_rows(lf)
        b_end = jnp.sum(lf, axis=0, keepdims=True)
        q_dec = q_all[sl] * jnp.exp(b)
        k_inv = k_all[sl] * jnp.exp(-b)
        k_end = k_all[sl] * jnp.exp(b_end - b)
        v = v_all[sl]
        scores = jnp.where(causal, _dot_nt(q_dec, _stack_heads(k_inv, masks)), 0.0)
        outs.append(_dot(scores, _stack_heads(v, masks)) + _dot_nt(q_dec, state))
        state = state * jnp.exp(b_end) + _dot_tn(v, k_end) * bd
    o = jnp.concatenate(outs, axis=0) if len(outs) > 1 else outs[0]
    return (state,), (_head_rms(o, norm_g) * jax.nn.silu(g),), ()


def fn_retention(st, rows, params):
    (state,) = st
    z, cos_t, sin_t = rows
    norm_g, xi, zeta, decay, g_end = params
    q, k, v_all, g = (z[:, i * MIX_W:(i + 1) * MIX_W] for i in range(4))
    q_all = q * cos_t + _swap_pairs(q) * sin_t
    k_all = (k * cos_t + _swap_pairs(k) * sin_t) * (HEAD_DIM ** -0.5)
    masks = _head_masks()
    bd = _block_diag_mask()
    c = RET_CHUNK
    outs = []
    for n in range(z.shape[0] // c):
        sl = slice(n * c, (n + 1) * c)
        qc, kc, v = q_all[sl], k_all[sl], v_all[sl]
        scores = _dot_nt(qc, _stack_heads(kc, masks)) * decay
        outs.append(_dot(scores, _stack_heads(v, masks)) + _dot_nt(qc * xi, state))
        state = state * g_end + _dot_tn(v, kc * zeta) * bd
    o = jnp.concatenate(outs, axis=0) if len(outs) > 1 else outs[0]
    return (state,), (_head_rms(o, norm_g) * jax.nn.silu(g),), ()


def fn_rglru(st, rows, params):
    tail_x, tail_h = st
    (z,) = rows
    conv_w, conv_b, wa, ba, wx, bx, sp = params
    t = z.shape[0]
    xg, xi = z[:, :MIX_W], z[:, MIX_W:]
    full = jnp.concatenate([tail_x, xi], axis=0)
    xc = conv_b
    for k in range(4):
        xc = xc + conv_w[k:k + 1] * full[5 + k:5 + k + t]
    r = jax.nn.sigmoid(_dot(xc, wa) + ba)
    ig = jax.nn.sigmoid(_dot(xc, wx) + bx)
    log_a = -LRU_C * r * sp
    a = jnp.exp(log_a)
    one_minus_a2 = -jnp.tanh(log_a) * (a * a + 1.0)
    u = jnp.sqrt(one_minus_a2) * (ig * xc)
    h = _real_scan(a, u, tail_h[7:8])
    return (xi[t - 8:], h[t - 8:]), (h * jax.nn.gelu(xg),), ()


def fn_s5(st, rows, params):
    tail_r, tail_i = st
    (u,) = rows
    bt_re, bt_im, lb_re, lb_im, ct_re, ct_im, d, glu_w, glu_b = params
    t = u.shape[0]
    bu_re = _dot(u, bt_re)
    bu_im = _dot(u, bt_im)
    h_re, h_im = _complex_scan(lb_re, lb_im, bu_re, bu_im, tail_r[7:8], tail_i[7:8])
    y = _dot(h_re, ct_re) - _dot(h_im, ct_im) + d * u
    act = jax.nn.gelu(y)
    out = act * jax.nn.sigmoid(_dot(act, glu_w) + glu_b)
    return (h_re[t - 8:], h_im[t - 8:]), (out,), ()


def fn_merge(st, rows, params):
    ya, yb, yc, yd, gl, x = rows
    w0, w1, w2, w3, b_gate, w_out, g_post = params
    d = x.shape[1]
    mix = None
    for n, (y, w) in enumerate(((ya, w0), (yb, w1), (yc, w2), (yd, w3))):
        gate = jax.nn.sigmoid(gl[:, n * d:(n + 1) * d] + b_gate[:, n * d:(n + 1) * d])
        term = gate * _dot(y, w)
        mix = term if mix is None else mix + term
    return (), (x + _rms(_dot(mix, w_out), g_post),), ()


def fn_mem(st, rows, params):
    (mem,), (g, wk, wv) = rows, params
    m = _rms(mem, g)
    return (), (_dot(m, wk), _dot(m, wv)), ()


def fn_xattn(st, rows, params):
    (x,) = rows
    g_pre, wq, k, v, wo, g_post, g_next = params
    d = x.shape[1]
    dh = d // XA_HEADS
    q = _dot(_rms(x, g_pre), wq)
    heads = []
    for h in range(XA_HEADS):
        sl = slice(h * dh, (h + 1) * dh)
        s = _dot_nt(q[:, sl], k[:, sl]) * (dh ** -0.5)
        heads.append(_dot(jax.nn.softmax(s, axis=-1), v[:, sl]))
    x2 = x + _rms(_dot(jnp.concatenate(heads, axis=1), wo), g_post)
    return (), (x2, _rms(x2, g_next)), ()


def fn_res_norm(st, rows, params):
    (x, o), (g_post, g_next) = rows, params
    xn = x + _rms(o, g_post)
    return (), (xn, _rms(xn, g_next)), ()


def fn_loss_head(st, rows, params):
    (x, o, target), (g_post,) = rows, params
    y, vjp = jax.vjp(lambda o_, g_: x + _rms(o_, g_), o, g_post)
    err = y - target
    inv_d = 1.0 / x.shape[1]
    dy = err * inv_d
    do, dg = vjp(dy)
    loss = 0.5 * inv_d * jnp.sum(err * err)
    return (), (dy, do), (jnp.full((8, LANES), loss, F32), dg)


def _params():
    return pltpu.CompilerParams(dimension_semantics=("arbitrary",), vmem_limit_bytes=VMEM_LIMIT)


def _row_spec(tb, width, colblk, nb, reverse):
    if reverse:
        return pl.BlockSpec((tb, width), lambda i: (nb - 1 - i, colblk))
    return pl.BlockSpec((tb, width), lambda i: (i, colblk))


def _full_spec(shape):
    return pl.BlockSpec(shape, lambda i: (0,) * len(shape), pipeline_mode=pl.Buffered(1))


def _saved_spec(shape, nb, reverse):
    if reverse:
        return pl.BlockSpec((1,) + shape, lambda i: (nb - 1 - i, 0, 0))
    return pl.BlockSpec((1,) + shape, lambda i: (i, 0, 0))


def _param_value(ref):
    v = ref[...]
    return v if v.dtype == BF16 else v.astype(F32)


def _side(side):
    arrays, gather = side if side is not None else ((), False)
    return list(arrays), gather, len(arrays)


_ANY = pl.BlockSpec(memory_space=pl.ANY)


def stage_fwd(name, fn, rows, params, *, tb, outs, states=(), accs=(), side=None):
    n_rows = rows[0][0].shape[0]
    nb = n_rows // tb
    nr, npar, no, na, ns = len(rows), len(params), len(outs), len(accs), len(states)
    side_arrays, gather, nx = _side(side)

    def body(*refs):
        row_refs, par_refs, side_in = refs[:nr], refs[nr:nr + npar], refs[nr + npar:nr + npar + nx]
        o = nr + npar + nx
        out_refs, acc_refs = refs[o:o + no], refs[o + no:o + no + na]
        saved_refs = refs[o + no + na:o + no + na + ns]
        o = o + no + na + ns
        side_out, st_refs, sems = refs[o:o + nx], refs[o + nx:o + nx + ns], refs[o + nx + ns:]
        i = pl.program_id(0)
        if nx:
            start, wait = _exchange_copies(side_in, side_out, *sems, gather)
            pl.when(i == 0)(start)

        @pl.when(i == 0)
        def _():
            for r in st_refs + acc_refs:
                r[...] = jnp.zeros_like(r)

        st = tuple(r[...] for r in st_refs)
        for sv, s in zip(saved_refs, st):
            sv[0] = s
        new_st, out_vals, acc_vals = fn(st, tuple(r[...].astype(F32) for r in row_refs),
                                        tuple(_param_value(r) for r in par_refs))
        for r, v in zip(out_refs, out_vals):
            r[...] = v.astype(r.dtype)
        for r, v in zip(acc_refs, acc_vals):
            r[...] += v
        for r, v in zip(st_refs, new_st):
            r[...] = v
        if nx:
            pl.when(i == nb - 1)(wait)

    res = pl.pallas_call(
        body, name=name, grid=(nb,),
        in_specs=[_row_spec(tb, w, cb, nb, False) for _, w, cb in rows] + [_full_spec(p.shape) for p in params]
        + [_ANY] * nx,
        out_specs=[_row_spec(tb, w, 0, nb, False) for w, _ in outs] + [_full_spec(s) for s in accs]
        + [_saved_spec(s, nb, False) for s in states] + [_ANY] * nx,
        out_shape=[jax.ShapeDtypeStruct((n_rows, w), dt) for w, dt in outs]
        + [jax.ShapeDtypeStruct(s, F32) for s in accs] + [jax.ShapeDtypeStruct((nb,) + s, F32) for s in states]
        + _exchange_shapes(side_arrays, gather),
        scratch_shapes=[pltpu.VMEM(s, F32) for s in states] + (_exchange_sems(nx) if nx else []),
        compiler_params=_params(),
    )(*[a for a, _, _ in rows], *params, *side_arrays)
    base = (res[:no], res[no:no + na], res[no + na:no + na + ns])
    return base + (res[no + na + ns:],) if nx else base


def stage_bwd(name, fn, rows, params, cts, *, tb, saved=(), diff_rows=(), diff_params=(), row_dtypes=None, side=None,
              into=None):
    n_rows = rows[0][0].shape[0]
    nb = n_rows // tb
    nr, npar, ns, nc = len(rows), len(params), len(saved), len(cts)
    ndr, ndp = len(diff_rows), len(diff_params)
    row_dtypes = row_dtypes or (F32,) * ndr
    state_shapes = [s.shape[1:] for s in saved]
    side_arrays, gather, nx = _side(side)
    into_buffer = [into[0]] if into is not None and into[0] is not None else []
    na = len(into_buffer)
    if into is not None:
        assert ndr == 1
        drow_specs = [_row_spec(tb, rows[diff_rows[0]][1], rows[diff_rows[0]][2], nb, True)]
        drow_shapes = [jax.ShapeDtypeStruct((n_rows, into[1]), row_dtypes[0])]
    else:
        drow_specs = [_row_spec(tb, rows[k][1], 0, nb, True) for k in diff_rows]
        drow_shapes = [jax.ShapeDtypeStruct((n_rows, rows[k][1]), dt) for k, dt in zip(diff_rows, row_dtypes)]

    def body(*refs):
        row_refs, par_refs = refs[:nr], refs[nr:nr + npar]
        o = nr + npar
        saved_refs, ct_refs, side_in = refs[o:o + ns], refs[o + ns:o + ns + nc], refs[o + ns + nc:o + ns + nc + nx]
        o = o + ns + nc + nx + na
        drow_refs, dpar_refs, side_out = refs[o:o + ndr], refs[o + ndr:o + ndr + ndp], refs[o + ndr + ndp:o + ndr + ndp + nx]
        o = o + ndr + ndp + nx
        dst_refs, sems = refs[o:o + ns], refs[o + ns:]
        i = pl.program_id(0)
        if nx:
            start, wait = _exchange_copies(side_in, side_out, *sems, gather)
            pl.when(i == 0)(start)

        @pl.when(i == 0)
        def _():
            for r in dst_refs + dpar_refs:
                r[...] = jnp.zeros_like(r)

        st = tuple(r[0] for r in saved_refs)
        row_vals = [r[...].astype(F32) for r in row_refs]
        par_vals = [_param_value(r) for r in par_refs]

        def f(st_, dr_, dp_):
            rv, pv = list(row_vals), list(par_vals)
            for k, v in zip(diff_rows, dr_):
                rv[k] = v
            for k, v in zip(diff_params, dp_):
                pv[k] = v
            new_st, out_vals, _ = fn(st_, tuple(rv), tuple(pv))
            return new_st, out_vals

        _, vjp = jax.vjp(f, st, tuple(row_vals[k] for k in diff_rows), tuple(par_vals[k] for k in diff_params))
        g_st, g_rows, g_par = vjp((tuple(r[...] for r in dst_refs), tuple(r[...].astype(F32) for r in ct_refs)))
        for r, v in zip(drow_refs, g_rows):
            r[...] = v.astype(r.dtype)
        for r, v in zip(dpar_refs, g_par):
            r[...] += v
        for r, v in zip(dst_refs, g_st):
            r[...] = v
        if nx:
            pl.when(i == nb - 1)(wait)

    res = pl.pallas_call(
        body, name=name, grid=(nb,),
        in_specs=[_row_spec(tb, w, cb, nb, True) for _, w, cb in rows] + [_full_spec(p.shape) for p in params]
        + [_saved_spec(s, nb, True) for s in state_shapes] + [_row_spec(tb, c.shape[1], 0, nb, True) for c in cts]
        + [_ANY] * (nx + na),
        out_specs=drow_specs + [_full_spec(params[k].shape) for k in diff_params] + [_ANY] * nx,
        out_shape=drow_shapes + [jax.ShapeDtypeStruct(params[k].shape, F32) for k in diff_params]
        + _exchange_shapes(side_arrays, gather),
        scratch_shapes=[pltpu.VMEM(s, F32) for s in state_shapes] + (_exchange_sems(nx) if nx else []),
        input_output_aliases={nr + npar + ns + nc + nx: 0} if na else {},
        compiler_params=_params(),
    )(*[a for a, _, _ in rows], *params, *saved, *cts, *side_arrays, *into_buffer)
    base = (res[:ndr], res[ndr:ndr + ndp])
    return base + (res[ndr + ndp:],) if nx else base


def _pick(n, target):
    if n <= target:
        return n
    best = None
    for t in range(LANES, target + 1, LANES):
        if n % t == 0:
            best = t
    assert best is not None, n
    return best


def _mesh_position():
    return lax.axis_index("x"), lax.axis_index("y"), lax.axis_index("c")


def _peer(pos, k):
    x, y, c = pos
    px = 1 - x if k & 4 else x
    py = 1 - y if k & 2 else y
    pc = 1 - c if k & 1 else c
    return (px, py, pc), 4 * px + 2 * py + pc


def _exchange_copies(x_refs, o_refs, send_sems, recv_sems, local_sems, gather):
    pos = _mesh_position()
    me = 4 * pos[0] + 2 * pos[1] + pos[2]
    pairs = list(enumerate(zip(x_refs, o_refs)))

    def remote(k, a, src, dst):
        peer, _ = _peer(pos, k)
        return pltpu.make_async_remote_copy(src_ref=src, dst_ref=dst, send_sem=send_sems.at[k - 1, a],
                                            recv_sem=recv_sems.at[k - 1, a], device_id=peer,
                                            device_id_type=pl.DeviceIdType.MESH)

    def local(a, x, o):
        return pltpu.make_async_copy(x if gather else x.at[me], o.at[me], local_sems.at[a])

    def start():
        for a, (x, o) in pairs:
            local(a, x, o).start()
        for k in range(1, N_DEV):
            peer_idx = _peer(pos, k)[1]
            for a, (x, o) in pairs:
                remote(k, a, x if gather else x.at[peer_idx], o.at[me]).start()

    def wait():
        for k in range(1, N_DEV):
            peer_idx = _peer(pos, k)[1]
            for a, (x, o) in pairs:
                arrival = remote(k, a, x if gather else x.at[me], o.at[peer_idx])
                arrival.wait_recv()
                arrival.wait_send()
        for a, (x, o) in pairs:
            local(a, x, o).wait()

    return start, wait


def _exchange_shapes(arrays, gather):
    return [jax.ShapeDtypeStruct(((N_DEV,) + x.shape) if gather else x.shape, x.dtype) for x in arrays]


def _exchange_sems(n):
    return [pltpu.SemaphoreType.DMA((N_DEV - 1, n)), pltpu.SemaphoreType.DMA((N_DEV - 1, n)),
            pltpu.SemaphoreType.DMA((n,))]


def _exchange(name, arrays, gather):
    n = len(arrays)

    def body(*refs):
        start, wait = _exchange_copies(refs[:n], refs[n:2 * n], *refs[2 * n:], gather)
        start()
        wait()

    return pl.pallas_call(
        body, name=name,
        in_specs=[pl.BlockSpec(memory_space=pl.ANY)] * n, out_specs=[pl.BlockSpec(memory_space=pl.ANY)] * n,
        out_shape=_exchange_shapes(arrays, gather), scratch_shapes=_exchange_sems(n),
    )(*arrays)


def _pick_n(n):
    return 1408 if n % 1408 == 0 else _pick(n, 512)


def matmul(name, a, b, mode, out_dtype, add=None, side=None):
    if mode == "tn":
        k, m = a.shape
    else:
        m, k = a.shape
    n = b.shape[0] if mode == "nt" else b.shape[1]
    if mode == "tn":
        tm, tn, tk = _pick(m, 1408), _pick(n, 1408), _pick(k, 1024)
    else:
        tm, tn, tk = _pick(m, 1024), _pick_n(n), _pick(k, 2816)
    nk = k // tk
    grid = (m // tm, n // tn, nk)
    a_spec = pl.BlockSpec((tk, tm), lambda i, j, kk: (kk, i)) if mode == "tn" else pl.BlockSpec((tm, tk), lambda i, j, kk: (i, kk))
    b_spec = pl.BlockSpec((tn, tk), lambda i, j, kk: (j, kk)) if mode == "nt" else pl.BlockSpec((tk, tn), lambda i, j, kk: (kk, j))
    o_spec = pl.BlockSpec((tm, tn), lambda i, j, kk: (i, j))
    dims = {"nn": (((1,), (0,)), ((), ())), "nt": (((1,), (1,)), ((), ())), "tn": (((0,), (0,)), ((), ()))}[mode]
    has_add = add is not None
    side_arrays, gather = side if side is not None else ((), False)
    ns = len(side_arrays)
    n_in = 2 + has_add

    def body(*refs):
        a_ref, b_ref = refs[0], refs[1]
        side_in, o_ref, side_out = refs[n_in:n_in + ns], refs[n_in + ns], refs[n_in + ns + 1:n_in + 2 * ns + 1]
        scratch = refs[n_in + 2 * ns + 1:]
        ids = [pl.program_id(d) for d in range(3)]
        if ns:
            start, wait = _exchange_copies(side_in, side_out, *scratch[-3:], gather)
            pl.when((ids[0] == 0) & (ids[1] == 0) & (ids[2] == 0))(start)
        part = lax.dot_general(a_ref[...].astype(BF16), b_ref[...].astype(BF16), dims, preferred_element_type=F32)
        if nk == 1:
            o_ref[...] = (part + refs[2][...].astype(F32) if has_add else part).astype(o_ref.dtype)
        else:
            acc_ref = scratch[0]

            @pl.when(ids[2] == 0)
            def _():
                acc_ref[...] = part + refs[2][...].astype(F32) if has_add else part

            @pl.when(ids[2] > 0)
            def _():
                acc_ref[...] += part

            @pl.when(ids[2] == nk - 1)
            def _():
                o_ref[...] = acc_ref[...].astype(o_ref.dtype)
        if ns:
            pl.when((ids[0] == grid[0] - 1) & (ids[1] == grid[1] - 1) & (ids[2] == nk - 1))(wait)

    any_spec = pl.BlockSpec(memory_space=pl.ANY)
    res = pl.pallas_call(
        body, name=name, grid=grid,
        in_specs=[a_spec, b_spec] + ([o_spec] if has_add else []) + [any_spec] * ns,
        out_specs=[o_spec] + [any_spec] * ns,
        out_shape=[jax.ShapeDtypeStruct((m, n), out_dtype)] + _exchange_shapes(side_arrays, gather),
        scratch_shapes=([pltpu.VMEM((tm, tn), F32)] if nk > 1 else []) + (_exchange_sems(ns) if ns else []),
        compiler_params=pltpu.CompilerParams(
            dimension_semantics=("arbitrary",) * 3 if ns else ("parallel", "parallel", "arbitrary"),
            vmem_limit_bytes=VMEM_LIMIT),
    )(a, b, *([add] if has_add else []), *side_arrays)
    return (res[0], res[1:]) if ns else res[0]


def matmul_fused(name, a, bs, mode, n, extras, epilogue, out_dtypes):
    m, k = a.shape
    tm, tn = _pick(m, 1024), _pick_n(n)
    dims = {"nn": (((1,), (0,)), ((), ())), "nt": (((1,), (1,)), ((), ()))}[mode]
    nb, nx = len(bs), len(extras)

    def b_spec(off):
        if mode == "nt":
            return pl.BlockSpec((tn, k), lambda i, j: (j + off, 0))
        return pl.BlockSpec((k, tn), lambda i, j: (0, j + off))

    def body(*refs):
        a_val = refs[0][...].astype(BF16)
        parts = tuple(lax.dot_general(a_val, r[...].astype(BF16), dims, preferred_element_type=F32)
                      for r in refs[1:1 + nb])
        tiles = tuple(r[...].astype(F32) for r in refs[1 + nb:1 + nb + nx])
        for r, v in zip(refs[1 + nb + nx:], epilogue(parts, tiles)):
            r[...] = v.astype(r.dtype)

    tile = pl.BlockSpec((tm, tn), lambda i, j: (i, j))
    return pl.pallas_call(
        body, name=name, grid=(m // tm, n // tn),
        in_specs=[pl.BlockSpec((tm, k), lambda i, j: (i, 0))] + [b_spec(off) for _, off in bs] + [tile] * nx,
        out_specs=[tile] * len(out_dtypes), out_shape=[jax.ShapeDtypeStruct((m, n), dt) for dt in out_dtypes],
        compiler_params=pltpu.CompilerParams(dimension_semantics=("parallel", "parallel"),
                                             vmem_limit_bytes=VMEM_LIMIT),
    )(a, *[b for b, _ in bs], *extras)


def _glu_fwd_tiles(parts, tiles):
    gate, up = parts
    return gate, up, jax.nn.silu(gate) * up


def _glu_bwd_tiles(parts, tiles):
    (da,), (gate, up) = parts, tiles
    _, vjp = jax.vjp(lambda g, u: jax.nn.silu(g) * u, gate, up)
    return vjp(da)


def adamw(name, w, m, v, gparts, side=None):
    layers, rows, cols = w.shape
    parts = gparts[0].shape[0]
    tr = 8
    while tr * 2 * cols <= 65536 and rows % (tr * 2) == 0:
        tr *= 2
    nblk = rows // tr
    c1 = 1.0 - ADAM_B1 ** ADAM_STEP
    c2 = 1.0 - ADAM_B2 ** ADAM_STEP
    side_arrays, gather, nx = _side(side)

    def body(*refs):
        w_ref, m_ref, v_ref = refs[:3]
        g_refs = refs[3:3 + layers]
        side_in = refs[3 + layers:3 + layers + nx]
        go_ref, d_ref, mo_ref, vo_ref = refs[3 + layers + nx:7 + layers + nx]
        side_out, sems = refs[7 + layers + nx:7 + layers + 2 * nx], refs[7 + layers + 2 * nx:]
        layer = pl.program_id(0)
        if nx:
            start, wait = _exchange_copies(side_in, side_out, *sems, gather)
            pl.when((layer == 0) & (pl.program_id(1) == 0))(start)
        g = None
        for ll, g_ref in enumerate(g_refs):
            s = g_ref[0].astype(F32)
            for p in range(1, parts):
                s = s + g_ref[p].astype(F32)
            g = s if g is None else jnp.where(layer == ll, s, g)
        m_new = ADAM_B1 * m_ref[...] + (1.0 - ADAM_B1) * g
        v_new = ADAM_B2 * v_ref[...] + (1.0 - ADAM_B2) * (g * g)
        m_hat = m_new / c1
        v_hat = v_new / c2
        go_ref[...] = g
        d_ref[...] = -ADAM_LR * (m_hat / (jnp.sqrt(v_hat) + ADAM_EPS) + ADAM_WD * w_ref[...])
        mo_ref[...] = m_new
        vo_ref[...] = v_new
        if nx:
            pl.when((layer == layers - 1) & (pl.program_id(1) == nblk - 1))(wait)

    def part_spec(ll):
        return pl.BlockSpec((parts, tr, cols),
                            lambda l, i: (0, jnp.where(l == ll, i, jnp.where(l < ll, 0, nblk - 1)), 0))

    spec = pl.BlockSpec((None, tr, cols), lambda l, i: (l, i, 0))
    res = pl.pallas_call(
        body, name=name, grid=(layers, nblk),
        in_specs=[spec, spec, spec] + [part_spec(ll) for ll in range(layers)] + [_ANY] * nx,
        out_specs=[spec] * 4 + [_ANY] * nx,
        out_shape=[jax.ShapeDtypeStruct(w.shape, F32)] * 4 + _exchange_shapes(side_arrays, gather),
        scratch_shapes=_exchange_sems(nx) if nx else [],
        compiler_params=pltpu.CompilerParams(dimension_semantics=("arbitrary", "arbitrary"),
                                             vmem_limit_bytes=VMEM_LIMIT),
    )(w, m, v, *gparts, *side_arrays)
    return (res[:4], res[4:]) if nx else res


def _adamw_math(w, m, v, g):
    m_new = ADAM_B1 * m + (1.0 - ADAM_B1) * g
    v_new = ADAM_B2 * v + (1.0 - ADAM_B2) * (g * g)
    m_hat = m_new / (1.0 - ADAM_B1 ** ADAM_STEP)
    v_hat = v_new / (1.0 - ADAM_B2 ** ADAM_STEP)
    return -ADAM_LR * (m_hat / (jnp.sqrt(v_hat) + ADAM_EPS) + ADAM_WD * w), m_new, v_new


def adamw_many(name, ws, ms, vs, garrays, where):
    n, ng = len(ws), len(garrays)

    def body(*refs):
        ins, g_refs, outs = refs[:3 * n], refs[3 * n:3 * n + ng], refs[3 * n + ng:]
        for i, (a, j) in enumerate(where):
            g_ref = g_refs[a]
            g = g_ref[0, j].astype(F32)
            for p in range(1, g_ref.shape[0]):
                g = g + g_ref[p, j].astype(F32)
            delta, m_new, v_new = _adamw_math(ins[i][...], ins[n + i][...], ins[2 * n + i][...], g)
            for r, val in zip(outs[4 * i:4 * i + 4], (g, delta, m_new, v_new)):
                r[...] = val

    vmem = pl.BlockSpec(memory_space=pltpu.VMEM)
    res = pl.pallas_call(
        body, name=name, in_specs=[vmem] * (3 * n + ng), out_specs=[vmem] * (4 * n),
        out_shape=[jax.ShapeDtypeStruct(w.shape, F32) for w in ws for _ in range(4)],
        compiler_params=pltpu.CompilerParams(vmem_limit_bytes=VMEM_LIMIT),
    )(*ws, *ms, *vs, *garrays)
    return [res[4 * i:4 * i + 4] for i in range(n)]


def _pack(arrays, dtype, row_multiple):
    flat = jnp.concatenate([a.astype(dtype).reshape(-1) for a in arrays])
    unit = row_multiple * LANES
    pad = (-flat.shape[0]) % unit
    if pad:
        flat = jnp.concatenate([flat, jnp.zeros((pad,), dtype)])
    return flat.reshape(-1, LANES)


def _unpack(flat2d, shapes):
    flat = flat2d.reshape(-1)
    out, off = [], 0
    for s in shapes:
        n = int(np.prod(s))
        out.append(flat[off:off + n].reshape(s))
        off += n
    return out


def _unpack_stacked(stacked, shapes):
    flat = stacked.reshape(N_DEV, -1)
    out, off = [], 0
    for s in shapes:
        n = int(np.prod(s))
        out.append(flat[:, off:off + n].reshape((N_DEV,) + tuple(s)))
        off += n
    return out


def _merge_shards(stacked, axis):
    t = jnp.moveaxis(stacked, 0, axis)
    s = t.shape
    return t.reshape(s[:axis] + (s[axis] * s[axis + 1],) + s[axis + 2:])


def _split_shards(full, axis, n=N_DEV):
    s = full.shape
    t = full.reshape(s[:axis] + (n, s[axis] // n) + s[axis + 1:])
    return jnp.moveaxis(t, axis, 0)


def _lower_bounds(hg_lower_bounds):
    p = jax.nn.softmax(hg_lower_bounds, axis=0)
    return jnp.cumsum(p, axis=0) - p[0:1]


def _prep_layer(p):
    def row(v):
        return v.reshape(1, -1)

    eye_b = jnp.eye(HEADS, dtype=F32)
    eye_g = jnp.eye(S5_GROUPS, dtype=F32)
    step = jnp.exp(p["s5_log_dt"])[:, None]
    lam_re, lam_im = p["s5_lam_re"], p["s5_lam_im"]
    mag = jnp.exp(lam_re * step)
    lb_re = mag * jnp.cos(lam_im * step)
    lb_im = mag * jnp.sin(lam_im * step)
    den = lam_re * lam_re + lam_im * lam_im
    f_re = ((lb_re - 1.0) * lam_re + lb_im * lam_im) / den
    f_im = (lb_im * lam_re - (lb_re - 1.0) * lam_im) / den
    bb_re = f_re[..., None] * p["s5_b_re"] - f_im[..., None] * p["s5_b_im"]
    bb_im = f_re[..., None] * p["s5_b_im"] + f_im[..., None] * p["s5_b_re"]
    width = S5_GROUPS * S5_GROUP
    n_state = S5_GROUPS * S5_STATE
    return dict(
        lb=row(p["lb"]), hg_norm=row(p["hg_norm"]), ret_norm=row(p["ret_norm"]),
        conv_w=p["lru_conv_w"], conv_b=row(p["lru_conv_b"]),
        wa=jnp.einsum("nij,nm->nimj", p["lru_wa"], eye_b).reshape(MIX_W, MIX_W).astype(BF16), ba=row(p["lru_ba"]),
        wx=jnp.einsum("nij,nm->nimj", p["lru_wx"], eye_b).reshape(MIX_W, MIX_W).astype(BF16), bx=row(p["lru_bx"]),
        sp=row(jax.nn.softplus(-p["lru_lambda"])),
        bt_re=jnp.einsum("gnp,gh->gphn", bb_re, eye_g).reshape(width, n_state).astype(BF16),
        bt_im=jnp.einsum("gnp,gh->gphn", bb_im, eye_g).reshape(width, n_state).astype(BF16),
        lb_re=row(lb_re), lb_im=row(lb_im),
        ct_re=jnp.einsum("gpn,gh->gnhp", p["s5_c_re"], eye_g).reshape(n_state, width).astype(BF16),
        ct_im=jnp.einsum("gpn,gh->gnhp", p["s5_c_im"], eye_g).reshape(n_state, width).astype(BF16),
        s5_d=row(p["s5_d"]), glu_w=p["s5_glu_w"].astype(BF16), glu_b=row(p["s5_glu_b"]),
        b_gate=row(p["b_gate"]),
        norm_mix_pre=row(p["norm_mix_pre"]), norm_mix_post=row(p["norm_mix_post"]),
        norm_xa_pre=row(p["norm_xa_pre"]), norm_xa_post=row(p["norm_xa_post"]), norm_mem=row(p["norm_mem"]),
        norm_ffn_pre=row(p["norm_ffn_pre"]), norm_ffn_post=row(p["norm_ffn_post"]),
    )


_PREP_INPUTS = ("hg_norm", "ret_norm", "lru_conv_w", "lru_conv_b", "lru_wa", "lru_ba", "lru_wx", "lru_bx", "lru_lambda",
                "s5_lam_re", "s5_lam_im", "s5_b_re", "s5_b_im", "s5_c_re", "s5_c_im", "s5_d", "s5_log_dt", "s5_glu_w",
                "s5_glu_b", "b_gate", "norm_mix_pre", "norm_mix_post", "norm_xa_pre", "norm_xa_post", "norm_mem",
                "norm_ffn_pre", "norm_ffn_post")


def _retention_constants():
    lg = np.log1p(-np.power(2.0, -5.0 - np.arange(HEADS)))
    idx = np.arange(RET_CHUNK)

    def lanes(per_head_rows):
        return np.repeat(per_head_rows.T[:, :, None], HEAD_DIM, axis=2).reshape(RET_CHUNK, MIX_W)

    xi = lanes(np.exp((idx + 1.0)[None, :] * lg[:, None]))
    zeta = lanes(np.exp((RET_CHUNK - 1.0 - idx)[None, :] * lg[:, None]))
    rel = idx[:, None] - idx[None, :]
    decay = np.where(rel[None] >= 0, np.exp(np.maximum(rel, 0)[None] * lg[:, None, None]), 0.0)
    decay = np.transpose(decay, (1, 0, 2)).reshape(RET_CHUNK, HEADS * RET_CHUNK)
    g_end = np.repeat(np.exp(RET_CHUNK * lg), HEAD_DIM)[None, :]
    return tuple(jnp.asarray(a, F32) for a in (xi, zeta, decay, g_end))


def _rotary_tables(seq):
    pos = jnp.arange(seq, dtype=F32)
    inv_freq = 10000.0 ** (-jnp.arange(0, HEAD_DIM, 2, dtype=F32) / HEAD_DIM)
    ang = pos[:, None] * inv_freq[None, :]
    cos, sin = jnp.cos(ang), jnp.sin(ang)
    cos_t = jnp.tile(jnp.repeat(cos, 2, axis=1), (1, HEADS))
    sin_t = jnp.tile(jnp.stack([-sin, sin], axis=-1).reshape(seq, HEAD_DIM), (1, HEADS))
    return cos_t, sin_t


TB_HG = 512
TB_RET = 512
TB_LRU = 256
TB_S5 = 256
TB_ROW = 256
TB_XA = 512

_STATE = (MIX_W, MIX_W)
_TAIL = (8, MIX_W)
_S5_TAIL = (8, S5_GROUPS * S5_STATE)


def _mixer_operands(l, z, kp, rot, ret_c):
    xi, zeta, decay, g_end = ret_c
    return dict(
        hg=dict(name=f"hgrn2_{l}", fn=fn_hgrn2, rows=[(z, 4 * MIX_W, 0)], params=[kp["lb"], kp["hg_norm"]],
                tb=TB_HG, states=(_STATE,), diff_params=(0, 1)),
        ret=dict(name=f"retention_{l}", fn=fn_retention, rows=[(z, 4 * MIX_W, 1), (rot[0], MIX_W, 0), (rot[1], MIX_W, 0)],
                 params=[kp["ret_norm"], xi, zeta, decay, g_end], tb=TB_RET, states=(_STATE,), diff_params=(0,)),
        lru=dict(name=f"rglru_{l}", fn=fn_rglru, rows=[(z, 2 * MIX_W, 4)],
                 params=[kp["conv_w"], kp["conv_b"], kp["wa"], kp["ba"], kp["wx"], kp["bx"], kp["sp"]],
                 tb=TB_LRU, states=(_TAIL, _TAIL), diff_params=(0, 1, 2, 3, 4, 5, 6)),
        s5=dict(name=f"s5_{l}", fn=fn_s5, rows=[(z, MIX_W, 10)],
                params=[kp["bt_re"], kp["bt_im"], kp["lb_re"], kp["lb_im"], kp["ct_re"], kp["ct_im"], kp["s5_d"],
                        kp["glu_w"], kp["glu_b"]],
                tb=TB_S5, states=(_S5_TAIL, _S5_TAIL), diff_params=tuple(range(9))),
    )


def _layer_forward(l, x, h, mem, w_in, shards, next_w_in_shard, kp, rot, ret_c, g_next, target):
    d = x.shape[1]
    sv = dict(x=x, h=h)
    bw = {"w_in": w_in}

    def gather(idx):
        return [shards[i] for i in idx], True

    def take(idx, stacked):
        for i, s in zip(idx, stacked):
            bw[_BIG_NAMES[i]] = _merge_shards(s, BIG[i][1] - 1)

    z, got = matmul(f"in_proj_{l}", h, w_in, "nn", F32, side=gather(GATHER_IN["in_proj"]))
    take(GATHER_IN["in_proj"], got)
    gl, got = matmul(f"gate_proj_{l}", h, bw["w_gate"], "nn", BF16, side=gather(GATHER_IN["gate_proj"]))
    take(GATHER_IN["gate_proj"], got)
    sv.update(z=z, gl=gl)
    ops = _mixer_operands(l, z, kp, rot, ret_c)
    ys = []
    for key in ("hg", "ret", "lru", "s5"):
        o = ops[key]
        idx = GATHER_IN.get(key)
        res = stage_fwd(o["name"] + "_fwd", o["fn"], o["rows"], o["params"], tb=o["tb"], outs=[(MIX_W, F32)],
                        states=o["states"], side=gather(idx) if idx else None)
        if idx:
            take(idx, res[3])
        ys.append(res[0][0])
        sv[key + "_states"] = res[2]
    sv["ys"] = ys
    merge_params = [bw["w_up"][n] for n in range(4)] + [kp["b_gate"], bw["w_out"], kp["norm_mix_post"]]
    merge_rows = [(y, MIX_W, 0) for y in ys] + [(gl, 4 * d, 0), (x, d, 0)]
    (x1,), _, _, got = stage_fwd(f"merge_{l}_fwd", fn_merge, merge_rows, merge_params, tb=TB_ROW, outs=[(d, F32)],
                                 side=gather(GATHER_IN["merge"]))
    take(GATHER_IN["merge"], got)
    wk, wv = bw["xa_w_kv"][:, :d], bw["xa_w_kv"][:, d:]
    mem_params = [kp["norm_mem"], wk, wv]
    (k, v), _, _ = stage_fwd(f"mem_{l}_fwd", fn_mem, [(mem, d, 0)], mem_params, tb=mem.shape[0],
                             outs=[(d, BF16), (d, BF16)])
    xa_params = [kp["norm_xa_pre"], bw["xa_w_q"], k, v, bw["xa_w_o"], kp["norm_xa_post"], kp["norm_ffn_pre"]]
    res = stage_fwd(f"xattn_{l}_fwd", fn_xattn, [(x1, d, 0)], xa_params, tb=TB_XA, outs=[(d, F32), (d, BF16)],
                    side=([next_w_in_shard], True) if next_w_in_shard is not None else None)
    x2, h3 = res[0]
    next_w_in = _merge_shards(res[3][0], BIG[0][1] - 1) if next_w_in_shard is not None else None
    f = bw["ffn_w_gu"].shape[1] // 2
    up_block = f // _pick_n(f)
    gate, up, a = matmul_fused(f"ffn_gu_{l}", h3, [(bw["ffn_w_gu"], 0), (bw["ffn_w_gu"], up_block)], "nn", f, [],
                               _glu_fwd_tiles, (BF16, BF16, BF16))
    o3 = matmul(f"ffn_down_{l}", a, bw["ffn_w_down"], "nn", F32)
    sv.update(x1=x1, k=k, v=v, x2=x2, h3=h3, gate=gate, up=up, a=a, o3=o3, merge_params=merge_params,
              merge_rows=merge_rows, mem_params=mem_params, xa_params=xa_params)
    if g_next is not None:
        (x3, hn), _, _ = stage_fwd(f"res_{l}_fwd", fn_res_norm, [(x2, d, 0), (o3, d, 0)], [kp["norm_ffn_post"], g_next],
                                   tb=TB_ROW, outs=[(d, F32), (d, BF16)])
        return x3, hn, sv, bw, next_w_in
    (dx2, do3), (loss, dg_post), _ = stage_fwd(f"loss_{l}", fn_loss_head, [(x2, d, 0), (o3, d, 0), (target, d, 0)],
                                               [kp["norm_ffn_post"]], tb=TB_ROW, outs=[(d, F32), (d, F32)],
                                               accs=[(8, LANES), (1, d)])
    sv["head"] = (dx2, do3, dg_post)
    return None, loss[0, 0], sv, bw, next_w_in


def _layer_backward(l, sv, mem, bw, kp, rot, ret_c, g_next, dx3, dhn):
    d = sv["x"].shape[1]
    gk = {}
    parts = [None] * len(BIG)

    def send(i, g):
        axis = BIG[i][1] - 1
        if isinstance(g, tuple):
            return jnp.concatenate([_split_shards(p.astype(BF16), axis, N_DEV // len(g)) for p in g], axis=0)
        return _split_shards(g.astype(BF16), axis)

    def scatter(*items):
        return [send(i, g) for i, g in items], False

    def took(got, *idx):
        for i, p in zip(idx, got):
            parts[i] = p

    res_rows = [(sv["x2"], d, 0), (sv["o3"], d, 0)]
    if g_next is not None:
        (dx2, do3), (gk["norm_ffn_post"], gk["g_next"]) = stage_bwd(
            f"res_{l}_bwd", fn_res_norm, res_rows, [kp["norm_ffn_post"], g_next], [dx3, dhn], tb=TB_ROW,
            diff_rows=(0, 1), diff_params=(0, 1))
    else:
        dx2, do3, gk["norm_ffn_post"] = sv["head"]
    f = sv["a"].shape[1]
    g_down = matmul(f"ffn_down_dw_{l}", sv["a"], do3, "tn", BF16)
    d_gate, d_up = matmul_fused(f"ffn_down_dx_{l}", do3, [(bw["ffn_w_down"], 0)], "nt", f, [sv["gate"], sv["up"]],
                                _glu_bwd_tiles, (BF16, BF16))
    dh3, got = matmul(f"ffn_gate_dx_{l}", d_gate, bw["ffn_w_gu"][:, :f], "nt", F32, side=scatter((8, g_down)))
    took(got, 8)
    dh3 = matmul(f"ffn_up_dx_{l}", d_up, bw["ffn_w_gu"][:, f:], "nt", F32, add=dh3)
    g_gu = (matmul(f"ffn_gate_dw_{l}", sv["h3"], d_gate, "tn", BF16),
            matmul(f"ffn_up_dw_{l}", sv["h3"], d_up, "tn", BF16))
    (dx1,), xa_g, got = stage_bwd(f"xattn_{l}_bwd", fn_xattn, [(sv["x1"], d, 0)], sv["xa_params"], [dx2, dh3],
                                  tb=TB_XA, diff_rows=(0,), diff_params=tuple(range(7)), side=scatter((7, g_gu)))
    took(got, 7)
    gk["norm_xa_pre"], g_q, dk, dv, g_o, gk["norm_xa_post"], gk["norm_ffn_pre"] = xa_g
    _, (gk["norm_mem"], dwk, dwv) = stage_bwd(f"mem_{l}_bwd", fn_mem, [(mem, d, 0)], sv["mem_params"], [dk, dv],
                                              tb=mem.shape[0], diff_params=(0, 1, 2))
    g_kv = jnp.concatenate([dwk, dwv], axis=1)
    merge_d, merge_g, got = stage_bwd(f"merge_{l}_bwd", fn_merge, sv["merge_rows"], sv["merge_params"], [dx1],
                                      tb=TB_ROW, diff_rows=tuple(range(6)), diff_params=tuple(range(7)),
                                      row_dtypes=(F32, F32, F32, F32, BF16, F32),
                                      side=scatter((4, g_q), (6, g_o), (5, g_kv)))
    took(got, 4, 6, 5)
    dys, dgl, dx0 = merge_d[:4], merge_d[4], merge_d[5]
    g_up = jnp.stack(merge_g[:4])
    gk["b_gate"], g_out, gk["norm_mix_post"] = merge_g[4:]
    ops = _mixer_operands(l, sv["z"], kp, rot, ret_c)
    dz = None
    for key, dy in zip(("hg", "ret", "lru", "s5"), dys):
        o = ops[key]
        res = stage_bwd(o["name"] + "_bwd", o["fn"], o["rows"], o["params"], [dy], tb=o["tb"],
                        saved=sv[key + "_states"], diff_rows=(0,), diff_params=o["diff_params"], row_dtypes=(BF16,),
                        side=scatter((2, g_up), (3, g_out)) if key == "hg" else None, into=(dz, sv["z"].shape[1]))
        if key == "hg":
            took(res[2], 2, 3)
        dz = res[0][0]
        gk[key] = res[1]
    g_gate = matmul(f"gate_proj_dw_{l}", sv["h"], dgl, "tn", BF16)
    g_in = matmul(f"in_proj_dw_{l}", sv["h"], dz, "tn", BF16)
    dh, got = matmul(f"gate_proj_dx_{l}", dgl, bw["w_gate"], "nt", F32, side=scatter((1, g_gate)))
    took(got, 1)
    dh, got = matmul(f"in_proj_dx_{l}", dz, bw["w_in"], "nt", F32, add=dh, side=scatter((0, g_in)))
    took(got, 0)
    return dx0, dh, gk, parts


def _kernel_grads_to_prep(gk):
    hg, ret, lru, s5 = gk["hg"], gk["ret"], gk["lru"], gk["s5"]
    return dict(
        lb=hg[0], hg_norm=hg[1], ret_norm=ret[0],
        conv_w=lru[0], conv_b=lru[1], wa=lru[2], ba=lru[3], wx=lru[4], bx=lru[5], sp=lru[6],
        bt_re=s5[0], bt_im=s5[1], lb_re=s5[2], lb_im=s5[3], ct_re=s5[4], ct_im=s5[5], s5_d=s5[6], glu_w=s5[7],
        glu_b=s5[8], b_gate=gk["b_gate"], norm_mix_pre=gk["norm_mix_pre"], norm_mix_post=gk["norm_mix_post"],
        norm_xa_pre=gk["norm_xa_pre"], norm_xa_post=gk["norm_xa_post"], norm_mem=gk["norm_mem"],
        norm_ffn_pre=gk["norm_ffn_pre"], norm_ffn_post=gk["norm_ffn_post"],
    )


def _step(inp):
    x, mem, target = inp["x"][0], inp["mem"][0], inp["loss_target"][0]
    seq = x.shape[0]
    depth = inp["w_in"].shape[0]
    me = 4 * lax.axis_index("x") + 2 * lax.axis_index("y") + lax.axis_index("c")

    small_shapes = [inp[n].shape for n in _SMALL_SHARDED_NAMES]
    (small_stacked,) = _exchange("gather_small", [_pack([inp[n] for n in _SMALL_SHARDED_NAMES], F32, 8)], True)
    small_all = _unpack_stacked(small_stacked, small_shapes)
    full_small = {n: _merge_shards(s, ax) for (n, ax), s in zip(SMALL_SHARDED, small_all)}

    lbs, lbs_vjp = jax.vjp(_lower_bounds, inp["hg_lower_bounds"])
    kps, prep_vjps = [], []
    for l in range(depth):
        p = {n: (full_small[n][l] if n in full_small else inp[n][l]) for n in _PREP_INPUTS}
        p["lb"] = lbs[l]
        kp, vj = jax.vjp(_prep_layer, p)
        kps.append(kp)
        prep_vjps.append(vj)
    rot = _rotary_tables(seq)
    ret_c = _retention_constants()

    def shards(l):
        return [inp[n][l].astype(BF16) for n in _BIG_NAMES]

    (h,), _, _, (stacked,) = stage_fwd("norm_in_fwd", fn_norm, [(x, x.shape[1], 0)], [kps[0]["norm_mix_pre"]],
                                       tb=TB_ROW, outs=[(x.shape[1], BF16)], side=([shards(0)[0]], True))
    saved, bws = [], []
    xs = x
    w_in = _merge_shards(stacked, BIG[0][1] - 1)
    for l in range(depth):
        last = l + 1 == depth
        g_next = None if last else kps[l + 1]["norm_mix_pre"]
        xs, h, sv, bw, w_in = _layer_forward(l, xs, h, mem, w_in, shards(l), None if last else shards(l + 1)[0],
                                             kps[l], rot, ret_c, g_next, target)
        saved.append(sv)
        bws.append(bw)
    dy, loss_local = xs, h

    big_parts = [None] * depth
    gks = [None] * depth
    dx, dh = dy, None
    for l in reversed(range(depth)):
        g_next = kps[l + 1]["norm_mix_pre"] if l + 1 < depth else None
        dx, dh, gks[l], big_parts[l] = _layer_backward(l, saved[l], mem, bws[l], kps[l], rot, ret_c, g_next, dx, dh)
    (grad_x,), (g_pre0,) = stage_bwd("norm_in_bwd", fn_keep_norm, [(x, x.shape[1], 0)], [kps[0]["norm_mix_pre"]],
                                     [dx, dh], tb=TB_ROW, diff_rows=(0,), diff_params=(0,))
    for l in range(depth):
        gks[l]["norm_mix_pre"] = g_pre0 if l == 0 else gks[l - 1]["g_next"]

    small_grads = {n: [None] * depth for n in _PREP_INPUTS}
    d_lbs = []
    for l in range(depth):
        (gp,) = prep_vjps[l]({k: g.astype(kps[l][k].dtype) for k, g in _kernel_grads_to_prep(gks[l]).items()})
        d_lbs.append(gp["lb"])
        for n in _PREP_INPUTS:
            small_grads[n][l] = gp[n]
    small_local = {n: jnp.stack(v) for n, v in small_grads.items()}
    (small_local["hg_lower_bounds"],) = lbs_vjp(jnp.stack(d_lbs))
    small_names = REPLICATED + _SMALL_SHARDED_NAMES
    groups = {}
    for n in small_names:
        groups.setdefault(small_local[n].shape, []).append(n)
    small_send = [jnp.stack([small_local[n].astype(BF16) for n in names]) for names in groups.values()]
    where = {n: (a, j) for a, names in enumerate(groups.values()) for j, n in enumerate(names)}

    out = {}
    kinds = ("grad_", "delta_", "new_m_", "new_v_")
    small_parts = None
    for i, n in sorted(enumerate(_BIG_NAMES), key=lambda t: -int(np.prod(inp[t[1]].shape))):
        shape = inp[n].shape
        three = (shape[0], int(np.prod(shape[1:-1])), shape[-1])
        res = adamw("adamw_" + n, *[inp[pre + n].reshape(three) for pre in ("", "m_", "v_")],
                    [big_parts[l][i].reshape((N_DEV,) + three[1:]) for l in range(depth)],
                    side=(small_send, True) if small_parts is None else None)
        if small_parts is None:
            res, small_parts = res
        for kind, a in zip(kinds, res):
            out[kind + n] = a.reshape(shape)
    small_parts = list(small_parts)
    for n, ax in SMALL_SHARDED:
        a, j = where[n]
        assert j == 0 and len(list(groups.values())[a]) == 1
        width = inp[n].shape[ax]
        small_parts[a] = lax.dynamic_slice_in_dim(small_parts[a], me * width, width, axis=ax + 2)
    res = adamw_many("adamw_small", *[[inp[pre + n] for n in small_names] for pre in ("", "m_", "v_")], small_parts,
                     [where[n] for n in small_names])
    for n, quad in zip(small_names, res):
        for kind, a in zip(kinds, quad):
            out[kind + n] = a

    out["loss"] = lax.psum(loss_local, ("x", "y", "c"))
    out["grad_x"] = grad_x[None]
    return out


def kernel(x, mem, hg_lower_bounds, norm_mix_pre, norm_mix_post, w_in, w_gate, b_gate, hg_norm, ret_norm, lru_conv_w, lru_conv_b, lru_wa, lru_ba, lru_wx, lru_bx, lru_lambda, s5_lam_re, s5_lam_im, s5_b_re, s5_b_im, s5_c_re, s5_c_im, s5_d, s5_log_dt, s5_glu_w, s5_glu_b, w_up, w_out, norm_xa_pre, norm_xa_post, norm_mem, xa_w_q, xa_w_kv, xa_w_o, norm_ffn_pre, norm_ffn_post, ffn_w_gu, ffn_w_down, loss_target, m_hg_lower_bounds, m_norm_mix_pre, m_norm_mix_post, m_w_in, m_w_gate, m_b_gate, m_hg_norm, m_ret_norm, m_lru_conv_w, m_lru_conv_b, m_lru_wa, m_lru_ba, m_lru_wx, m_lru_bx, m_lru_lambda, m_s5_lam_re, m_s5_lam_im, m_s5_b_re, m_s5_b_im, m_s5_c_re, m_s5_c_im, m_s5_d, m_s5_log_dt, m_s5_glu_w, m_s5_glu_b, m_w_up, m_w_out, m_norm_xa_pre, m_norm_xa_post, m_norm_mem, m_xa_w_q, m_xa_w_kv, m_xa_w_o, m_norm_ffn_pre, m_norm_ffn_post, m_ffn_w_gu, m_ffn_w_down, v_hg_lower_bounds, v_norm_mix_pre, v_norm_mix_post, v_w_in, v_w_gate, v_b_gate, v_hg_norm, v_ret_norm, v_lru_conv_w, v_lru_conv_b, v_lru_wa, v_lru_ba, v_lru_wx, v_lru_bx, v_lru_lambda, v_s5_lam_re, v_s5_lam_im, v_s5_b_re, v_s5_b_im, v_s5_c_re, v_s5_c_im, v_s5_d, v_s5_log_dt, v_s5_glu_w, v_s5_glu_b, v_w_up, v_w_out, v_norm_xa_pre, v_norm_xa_post, v_norm_mem, v_xa_w_q, v_xa_w_kv, v_xa_w_o, v_norm_ffn_pre, v_norm_ffn_post, v_ffn_w_gu, v_ffn_w_down):
    values = (x, mem, hg_lower_bounds, norm_mix_pre, norm_mix_post, w_in, w_gate, b_gate, hg_norm, ret_norm, lru_conv_w, lru_conv_b, lru_wa, lru_ba, lru_wx, lru_bx, lru_lambda, s5_lam_re, s5_lam_im, s5_b_re, s5_b_im, s5_c_re, s5_c_im, s5_d, s5_log_dt, s5_glu_w, s5_glu_b, w_up, w_out, norm_xa_pre, norm_xa_post, norm_mem, xa_w_q, xa_w_kv, xa_w_o, norm_ffn_pre, norm_ffn_post, ffn_w_gu, ffn_w_down, loss_target, m_hg_lower_bounds, m_norm_mix_pre, m_norm_mix_post, m_w_in, m_w_gate, m_b_gate, m_hg_norm, m_ret_norm, m_lru_conv_w, m_lru_conv_b, m_lru_wa, m_lru_ba, m_lru_wx, m_lru_bx, m_lru_lambda, m_s5_lam_re, m_s5_lam_im, m_s5_b_re, m_s5_b_im, m_s5_c_re, m_s5_c_im, m_s5_d, m_s5_log_dt, m_s5_glu_w, m_s5_glu_b, m_w_up, m_w_out, m_norm_xa_pre, m_norm_xa_post, m_norm_mem, m_xa_w_q, m_xa_w_kv, m_xa_w_o, m_norm_ffn_pre, m_norm_ffn_post, m_ffn_w_gu, m_ffn_w_down, v_hg_lower_bounds, v_norm_mix_pre, v_norm_mix_post, v_w_in, v_w_gate, v_b_gate, v_hg_norm, v_ret_norm, v_lru_conv_w, v_lru_conv_b, v_lru_wa, v_lru_ba, v_lru_wx, v_lru_bx, v_lru_lambda, v_s5_lam_re, v_s5_lam_im, v_s5_b_re, v_s5_b_im, v_s5_c_re, v_s5_c_im, v_s5_d, v_s5_log_dt, v_s5_glu_w, v_s5_glu_b, v_w_up, v_w_out, v_norm_xa_pre, v_norm_xa_post, v_norm_mem, v_xa_w_q, v_xa_w_kv, v_xa_w_o, v_norm_ffn_pre, v_norm_ffn_post, v_ffn_w_gu, v_ffn_w_down)
    names = ("x", "mem") + WEIGHTS + ("loss_target",) + tuple("m_" + n for n in WEIGHTS) + tuple("v_" + n for n in WEIGHTS)
    out = _step(dict(zip(names, values)))
    order = ["loss", "grad_x"] + [k + n for k in ("grad_", "delta_", "new_m_", "new_v_") for n in WEIGHTS]
    return tuple(out[k] for k in order)
```

```python
import functools

import numpy as np
import jax
import jax.numpy as jnp
from jax import lax
from jax.experimental import pallas as pl
from jax.experimental.pallas import tpu as pltpu

F32 = jnp.float32
BF16 = jnp.bfloat16
EPS = 1e-6
N_DEV = 8
LANES = 128
SUBLANES = 8
VMEM_LIMIT = 60 * 1024 * 1024

HEADS = 4
HEAD_DIM = 64
MIX_W = HEADS * HEAD_DIM
HG_CHUNK = 32
RET_CHUNK = 128
S5_GROUPS = 16
S5_GROUP = 16
S5_STATE = 64
LRU_C = 8.0
XA_HEADS = 4

ADAM_LR = 0.001
ADAM_B1 = 0.9
ADAM_B2 = 0.999
ADAM_EPS = 1e-08
ADAM_WD = 0.01
ADAM_STEP = 10

BIG = (("w_in", 2), ("w_gate", 2), ("w_up", 3), ("w_out", 1), ("xa_w_q", 1), ("xa_w_kv", 2), ("xa_w_o", 1),
       ("ffn_w_gu", 2), ("ffn_w_down", 1))
SMALL_SHARDED = (("lru_conv_w", 2), ("s5_glu_w", 1))
WEIGHTS = ("hg_lower_bounds", "norm_mix_pre", "norm_mix_post", "w_in", "w_gate", "b_gate", "hg_norm", "ret_norm",
           "lru_conv_w", "lru_conv_b", "lru_wa", "lru_ba", "lru_wx", "lru_bx", "lru_lambda", "s5_lam_re", "s5_lam_im",
           "s5_b_re", "s5_b_im", "s5_c_re", "s5_c_im", "s5_d", "s5_log_dt", "s5_glu_w", "s5_glu_b", "w_up", "w_out",
           "norm_xa_pre", "norm_xa_post", "norm_mem", "xa_w_q", "xa_w_kv", "xa_w_o", "norm_ffn_pre", "norm_ffn_post",
           "ffn_w_gu", "ffn_w_down")
GATHER_IN = {"in_proj": (1,), "gate_proj": (7,), "hg": (5,), "s5": (2, 3, 4, 6), "merge": (8,)}
_BIG_NAMES = tuple(n for n, _ in BIG)
_SMALL_SHARDED_NAMES = tuple(n for n, _ in SMALL_SHARDED)
REPLICATED = tuple(n for n in WEIGHTS if n not in _BIG_NAMES and n not in _SMALL_SHARDED_NAMES)


def _dot(a, b):
    return jnp.dot(a.astype(BF16), b.astype(BF16), preferred_element_type=F32)


def _dot_nt(a, b):
    return lax.dot_general(a.astype(BF16), b.astype(BF16), (((1,), (1,)), ((), ())), preferred_element_type=F32)


def _dot_tn(a, b):
    return lax.dot_general(a.astype(BF16), b.astype(BF16), (((0,), (0,)), ((), ())), preferred_element_type=F32)


def _dot_exact(a, b):
    return jnp.dot(a, b, precision=lax.Precision.HIGHEST, preferred_element_type=F32)


def _rms(x, g):
    return x * lax.rsqrt(jnp.mean(x * x, axis=-1, keepdims=True) + EPS) * g


def _shift_down(x, d, fill):
    return jnp.concatenate([jnp.full((d, x.shape[1]), fill, x.dtype), x[:-d]], axis=0)


def _shift_up(x, d, fill):
    return jnp.concatenate([x[d:], jnp.full((d, x.shape[1]), fill, x.dtype)], axis=0)


def _cumsum_rows(x):
    d = 1
    while d < x.shape[0]:
        x = x + _shift_down(x, d, 0.0)
        d *= 2
    return x


def _lane_head(shape, dim):
    return lax.shift_right_logical(lax.broadcasted_iota(jnp.int32, shape, dim), 6)


def _head_masks(width=MIX_W):
    head = _lane_head((1, width), 1)
    return [(head == h).astype(F32) for h in range(HEADS)]


def _block_diag_mask():
    return (_lane_head((MIX_W, MIX_W), 0) == _lane_head((MIX_W, MIX_W), 1)).astype(F32)


def _head_rms(o, g):
    ms = _dot_exact(o * o, _block_diag_mask()) * (1.0 / HEAD_DIM)
    return o * lax.rsqrt(ms + EPS) * g


def _swap_pairs(x):
    lane = lax.broadcasted_iota(jnp.int32, x.shape, 1)
    return jnp.where((lane & 1) == 0, jnp.roll(x, -1, axis=1), jnp.roll(x, 1, axis=1))


def _stack_heads(t, masks):
    return jnp.concatenate([t * m for m in masks], axis=0)


@jax.custom_vjp
def _real_scan(a, u, h0):
    return _real_scan_fwd(a, u, h0)[0]


def _real_scan_fwd(a, u, h0):
    t = a.shape[0]
    acc_a, acc_u = a, u
    d = 1
    while d < t:
        acc_u = acc_u + acc_a * _shift_down(acc_u, d, 0.0)
        acc_a = acc_a * _shift_down(acc_a, d, 1.0)
        d *= 2
    h = acc_u + acc_a * h0
    return h, (a, h, h0)


def _real_scan_bwd(res, dh):
    a, h, h0 = res
    t = a.shape[0]
    acc_a = _shift_up(a, 1, 0.0)
    g = dh
    d = 1
    while d < t:
        g = g + acc_a * _shift_up(g, d, 0.0)
        acc_a = acc_a * _shift_up(acc_a, d, 1.0)
        d *= 2
    h_prev = jnp.concatenate([h0, h[:-1]], axis=0)
    return g * h_prev, g, (a * g)[0:1]


_real_scan.defvjp(_real_scan_fwd, _real_scan_bwd)


def _cmul(ar, ai, br, bi):
    return ar * br - ai * bi, ar * bi + ai * br


def _geometric_sums(ar, ai, ur, ui, forward):
    shift = _shift_down if forward else _shift_up
    t = ur.shape[0]
    g = SUBLANES
    in_group = lax.broadcasted_iota(jnp.int32, ur.shape, 0) & (g - 1)
    pr, pi, sr, si = ar, ai, ur, ui
    d = 1
    while d < g:
        keep = in_group >= d if forward else in_group < g - d
        mr, mi = _cmul(pr, pi, jnp.where(keep, shift(sr, d, 0.0), 0.0), jnp.where(keep, shift(si, d, 0.0), 0.0))
        sr, si = sr + mr, si + mi
        pr, pi = _cmul(pr, pi, pr, pi)
        d *= 2
    row = lax.broadcasted_iota(jnp.int32, (g, ur.shape[1]), 0)
    qr, qi = ar, ai
    tr, ti = jnp.zeros((g, ur.shape[1]), F32), jnp.zeros((g, ur.shape[1]), F32)
    for r in range(g):
        here = row == (r if forward else g - 1 - r)
        tr, ti = jnp.where(here, qr, tr), jnp.where(here, qi, ti)
        qr, qi = _cmul(qr, qi, ar, ai)
    outs_r, outs_i = [], []
    cr = ci = None
    order = range(t // g) if forward else reversed(range(t // g))
    for n in order:
        br, bi = sr[n * g:(n + 1) * g], si[n * g:(n + 1) * g]
        if cr is not None:
            mr, mi = _cmul(tr, ti, cr, ci)
            br, bi = br + mr, bi + mi
        edge = slice(g - 1, g) if forward else slice(0, 1)
        cr, ci = br[edge], bi[edge]
        outs_r.append(br)
        outs_i.append(bi)
    if not forward:
        outs_r.reverse()
        outs_i.reverse()
    return jnp.concatenate(outs_r, axis=0), jnp.concatenate(outs_i, axis=0)


@jax.custom_vjp
def _complex_scan(ar, ai, ur, ui, h0r, h0i):
    return _complex_scan_fwd(ar, ai, ur, ui, h0r, h0i)[0]


def _complex_scan_fwd(ar, ai, ur, ui, h0r, h0i):
    first = lax.broadcasted_iota(jnp.int32, ur.shape, 0) == 0
    cr, ci = _cmul(ar, ai, h0r, h0i)
    hr, hi = _geometric_sums(ar, ai, ur + jnp.where(first, cr, 0.0), ui + jnp.where(first, ci, 0.0), True)
    return (hr, hi), (ar, ai, hr, hi, h0r, h0i)


def _complex_scan_bwd(res, dh):
    ar, ai, hr, hi, h0r, h0i = res
    gr, gi = _geometric_sums(ar, -ai, dh[0], dh[1], False)
    qr = jnp.concatenate([h0r, hr[:-1]], axis=0)
    qi = jnp.concatenate([h0i, hi[:-1]], axis=0)
    dar = jnp.sum(gr * qr + gi * qi, axis=0, keepdims=True)
    dai = jnp.sum(gi * qr - gr * qi, axis=0, keepdims=True)
    d0r, d0i = _cmul(ar, -ai, gr[0:1], gi[0:1])
    return dar, dai, gr, gi, d0r, d0i


_complex_scan.defvjp(_complex_scan_fwd, _complex_scan_bwd)


def fn_norm(st, rows, params):
    (x,), (g,) = rows, params
    return (), (_rms(x, g),), ()


def fn_keep_norm(st, rows, params):
    (x,), (g,) = rows, params
    return (), (x, _rms(x, g)), ()


def fn_hgrn2(st, rows, params):
    (state,) = st
    (z,) = rows
    lb, norm_g = params
    q, f_logit, v_all, g = (z[:, k * MIX_W:(k + 1) * MIX_W] for k in range(4))
    f = lb + (1.0 - lb) * jax.nn.sigmoid(f_logit)
    log_f = jnp.log(f)
    k_all = 1.0 - f
    q_all = jax.nn.silu(q)
    masks = _head_masks()
    bd = _block_diag_mask()
    c = HG_CHUNK
    col = lax.broadcasted_iota(jnp.int32, (c, HEADS * c), 1) & (c - 1)
    causal = col <= lax.broadcasted_iota(jnp.int32, (c, HEADS * c), 0)
    outs = []
    for n in range(z.shape[0] // c):
        sl = slice(n * c, (n + 1) * c)
        lf = log_f[sl]
        b = _cumsum_rows(lf)
        b_end = jnp.sum(lf, axis=0, keepdims=True)
        q_dec = q_all[sl] * jnp.exp(b)
        k_inv = k_all[sl] * jnp.exp(-b)
        k_end = k_all[sl] * jnp.exp(b_end - b)
        v = v_all[sl]
        scores = jnp.where(causal, _dot_nt(q_dec, _stack_heads(k_inv, masks)), 0.0)
        outs.append(_dot(scores, _stack_heads(v, masks)) + _dot_nt(q_dec, state))
        state = state * jnp.exp(b_end) + _dot_tn(v, k_end) * bd
    o = jnp.concatenate(outs, axis=0) if len(outs) > 1 else outs[0]
    return (state,), (_head_rms(o, norm_g) * jax.nn.silu(g),), ()


def fn_retention(st, rows, params):
    (state,) = st
    z, cos_t, sin_t = rows
    norm_g, xi, zeta, decay, g_end = params
    q, k, v_all, g = (z[:, i * MIX_W:(i + 1) * MIX_W] for i in range(4))
    q_all = q * cos_t + _swap_pairs(q) * sin_t
    k_all = (k * cos_t + _swap_pairs(k) * sin_t) * (HEAD_DIM ** -0.5)
    masks = _head_masks()
    bd = _block_diag_mask()
    c = RET_CHUNK
    outs = []
    for n in range(z.shape[0] // c):
        sl = slice(n * c, (n + 1) * c)
        qc, kc, v = q_all[sl], k_all[sl], v_all[sl]
        scores = _dot_nt(qc, _stack_heads(kc, masks)) * decay
        outs.append(_dot(scores, _stack_heads(v, masks)) + _dot_nt(qc * xi, state))
        state = state * g_end + _dot_tn(v, kc * zeta) * bd
    o = jnp.concatenate(outs, axis=0) if len(outs) > 1 else outs[0]
    return (state,), (_head_rms(o, norm_g) * jax.nn.silu(g),), ()


def fn_rglru(st, rows, params):
    tail_x, tail_h = st
    (z,) = rows
    conv_w, conv_b, wa, ba, wx, bx, sp = params
    t = z.shape[0]
    xg, xi = z[:, :MIX_W], z[:, MIX_W:]
    full = jnp.concatenate([tail_x, xi], axis=0)
    xc = conv_b
    for k in range(4):
        xc = xc + conv_w[k:k + 1] * full[5 + k:5 + k + t]
    r = jax.nn.sigmoid(_dot(xc, wa) + ba)
    ig = jax.nn.sigmoid(_dot(xc, wx) + bx)
    log_a = -LRU_C * r * sp
    a = jnp.exp(log_a)
    one_minus_a2 = -jnp.tanh(log_a) * (a * a + 1.0)
    u = jnp.sqrt(one_minus_a2) * (ig * xc)
    h = _real_scan(a, u, tail_h[7:8])
    return (xi[t - 8:], h[t - 8:]), (h * jax.nn.gelu(xg),), ()


def fn_s5(st, rows, params):
    tail_r, tail_i = st
    (u,) = rows
    bt_re, bt_im, lb_re, lb_im, ct_re, ct_im, d, glu_w, glu_b = params
    t = u.shape[0]
    bu_re = _dot(u, bt_re)
    bu_im = _dot(u, bt_im)
    h_re, h_im = _complex_scan(lb_re, lb_im, bu_re, bu_im, tail_r[7:8], tail_i[7:8])
    y = _dot(h_re, ct_re) - _dot(h_im, ct_im) + d * u
    act = jax.nn.gelu(y)
    out = act * jax.nn.sigmoid(_dot(act, glu_w) + glu_b)
    return (h_re[t - 8:], h_im[t - 8:]), (out,), ()


def fn_merge(st, rows, params):
    ya, yb, yc, yd, gl, x = rows
    w0, w1, w2, w3, b_gate, w_out, g_post = params
    d = x.shape[1]
    mix = None
    for n, (y, w) in enumerate(((ya, w0), (yb, w1), (yc, w2), (yd, w3))):
        gate = jax.nn.sigmoid(gl[:, n * d:(n + 1) * d] + b_gate[:, n * d:(n + 1) * d])
        term = gate * _dot(y, w)
        mix = term if mix is None else mix + term
    return (), (x + _rms(_dot(mix, w_out), g_post),), ()


def fn_mem(st, rows, params):
    (mem,), (g, wk, wv) = rows, params
    m = _rms(mem, g)
    return (), (_dot(m, wk), _dot(m, wv)), ()


def fn_xattn(st, rows, params):
    (x,) = rows
    g_pre, wq, k, v, wo, g_post, g_next = params
    d = x.shape[1]
    dh = d // XA_HEADS
    q = _dot(_rms(x, g_pre), wq)
    heads = []
    for h in range(XA_HEADS):
        sl = slice(h * dh, (h + 1) * dh)
        s = _dot_nt(q[:, sl], k[:, sl]) * (dh ** -0.5)
        heads.append(_dot(jax.nn.softmax(s, axis=-1), v[:, sl]))
    x2 = x + _rms(_dot(jnp.concatenate(heads, axis=1), wo), g_post)
    return (), (x2, _rms(x2, g_next)), ()


def fn_res_norm(st, rows, params):
    (x, o), (g_post, g_next) = rows, params
    xn = x + _rms(o, g_post)
    return (), (xn, _rms(xn, g_next)), ()


def fn_loss_head(st, rows, params):
    (x, o, target), (g_post,) = rows, params
    y, vjp = jax.vjp(lambda o_, g_: x + _rms(o_, g_), o, g_post)
    err = y - target
    inv_d = 1.0 / x.shape[1]
    dy = err * inv_d
    do, dg = vjp(dy)
    loss = 0.5 * inv_d * jnp.sum(err * err)
    return (), (dy, do), (jnp.full((8, LANES), loss, F32), dg)


def _params():
    return pltpu.CompilerParams(dimension_semantics=("arbitrary",), vmem_limit_bytes=VMEM_LIMIT)


def _row_spec(tb, width, colblk, nb, reverse):
    if reverse:
        return pl.BlockSpec((tb, width), lambda i: (nb - 1 - i, colblk))
    return pl.BlockSpec((tb, width), lambda i: (i, colblk))


def _full_spec(shape):
    return pl.BlockSpec(shape, lambda i: (0,) * len(shape), pipeline_mode=pl.Buffered(1))


def _saved_spec(shape, nb, reverse):
    if reverse:
        return pl.BlockSpec((1,) + shape, lambda i: (nb - 1 - i, 0, 0))
    return pl.BlockSpec((1,) + shape, lambda i: (i, 0, 0))


def _param_value(ref):
    v = ref[...]
    return v if v.dtype == BF16 else v.astype(F32)


def _side(side):
    arrays, gather = side if side is not None else ((), False)
    return list(arrays), gather, len(arrays)


_ANY = pl.BlockSpec(memory_space=pl.ANY)


def stage_fwd(name, fn, rows, params, *, tb, outs, states=(), accs=(), side=None):
    n_rows = rows[0][0].shape[0]
    nb = n_rows // tb
    nr, npar, no, na, ns = len(rows), len(params), len(outs), len(accs), len(states)
    side_arrays, gather, nx = _side(side)

    def body(*refs):
        row_refs, par_refs, side_in = refs[:nr], refs[nr:nr + npar], refs[nr + npar:nr + npar + nx]
        o = nr + npar + nx
        out_refs, acc_refs = refs[o:o + no], refs[o + no:o + no + na]
        saved_refs = refs[o + no + na:o + no + na + ns]
        o = o + no + na + ns
        side_out, st_refs, sems = refs[o:o + nx], refs[o + nx:o + nx + ns], refs[o + nx + ns:]
        i = pl.program_id(0)
        if nx:
            start, wait = _exchange_copies(side_in, side_out, *sems, gather)
            pl.when(i == 0)(start)

        @pl.when(i == 0)
        def _():
            for r in st_refs + acc_refs:
                r[...] = jnp.zeros_like(r)

        st = tuple(r[...] for r in st_refs)
        for sv, s in zip(saved_refs, st):
            sv[0] = s
        new_st, out_vals, acc_vals = fn(st, tuple(r[...].astype(F32) for r in row_refs),
                                        tuple(_param_value(r) for r in par_refs))
        for r, v in zip(out_refs, out_vals):
            r[...] = v.astype(r.dtype)
        for r, v in zip(acc_refs, acc_vals):
            r[...] += v
        for r, v in zip(st_refs, new_st):
            r[...] = v
        if nx:
            pl.when(i == nb - 1)(wait)

    res = pl.pallas_call(
        body, name=name, grid=(nb,),
        in_specs=[_row_spec(tb, w, cb, nb, False) for _, w, cb in rows] + [_full_spec(p.shape) for p in params]
        + [_ANY] * nx,
        out_specs=[_row_spec(tb, w, 0, nb, False) for w, _ in outs] + [_full_spec(s) for s in accs]
        + [_saved_spec(s, nb, False) for s in states] + [_ANY] * nx,
        out_shape=[jax.ShapeDtypeStruct((n_rows, w), dt) for w, dt in outs]
        + [jax.ShapeDtypeStruct(s, F32) for s in accs] + [jax.ShapeDtypeStruct((nb,) + s, F32) for s in states]
        + _exchange_shapes(side_arrays, gather),
        scratch_shapes=[pltpu.VMEM(s, F32) for s in states] + (_exchange_sems(nx) if nx else []),
        compiler_params=_params(),
    )(*[a for a, _, _ in rows], *params, *side_arrays)
    base = (res[:no], res[no:no + na], res[no + na:no + na + ns])
    return base + (res[no + na + ns:],) if nx else base


def stage_bwd(name, fn, rows, params, cts, *, tb, saved=(), diff_rows=(), diff_params=(), row_dtypes=None, side=None,
              into=None):
    n_rows = rows[0][0].shape[0]
    nb = n_rows // tb
    nr, npar, ns, nc = len(rows), len(params), len(saved), len(cts)
    ndr, ndp = len(diff_rows), len(diff_params)
    row_dtypes = row_dtypes or (F32,) * ndr
    state_shapes = [s.shape[1:] for s in saved]
    side_arrays, gather, nx = _side(side)
    into_buffer = [into[0]] if into is not None and into[0] is not None else []
    na = len(into_buffer)
    if into is not None:
        assert ndr == 1
        drow_specs = [_row_spec(tb, rows[diff_rows[0]][1], rows[diff_rows[0]][2], nb, True)]
        drow_shapes = [jax.ShapeDtypeStruct((n_rows, into[1]), row_dtypes[0])]
    else:
        drow_specs = [_row_spec(tb, rows[k][1], 0, nb, True) for k in diff_rows]
        drow_shapes = [jax.ShapeDtypeStruct((n_rows, rows[k][1]), dt) for k, dt in zip(diff_rows, row_dtypes)]

    def body(*refs):
        row_refs, par_refs = refs[:nr], refs[nr:nr + npar]
        o = nr + npar
        saved_refs, ct_refs, side_in = refs[o:o + ns], refs[o + ns:o + ns + nc], refs[o + ns + nc:o + ns + nc + nx]
        o = o + ns + nc + nx + na
        drow_refs, dpar_refs, side_out = refs[o:o + ndr], refs[o + ndr:o + ndr + ndp], refs[o + ndr + ndp:o + ndr + ndp + nx]
        o = o + ndr + ndp + nx
        dst_refs, sems = refs[o:o + ns], refs[o + ns:]
        i = pl.program_id(0)
        if nx:
            start, wait = _exchange_copies(side_in, side_out, *sems, gather)
            pl.when(i == 0)(start)

        @pl.when(i == 0)
        def _():
            for r in dst_refs + dpar_refs:
                r[...] = jnp.zeros_like(r)

        st = tuple(r[0] for r in saved_refs)
        row_vals = [r[...].astype(F32) for r in row_refs]
        par_vals = [_param_value(r) for r in par_refs]

        def f(st_, dr_, dp_):
            rv, pv = list(row_vals), list(par_vals)
            for k, v in zip(diff_rows, dr_):
                rv[k] = v
            for k, v in zip(diff_params, dp_):
                pv[k] = v
            new_st, out_vals, _ = fn(st_, tuple(rv), tuple(pv))
            return new_st, out_vals

        _, vjp = jax.vjp(f, st, tuple(row_vals[k] for k in diff_rows), tuple(par_vals[k] for k in diff_params))
        g_st, g_rows, g_par = vjp((tuple(r[...] for r in dst_refs), tuple(r[...].astype(F32) for r in ct_refs)))
        for r, v in zip(drow_refs, g_rows):
            r[...] = v.astype(r.dtype)
        for r, v in zip(dpar_refs, g_par):
            r[...] += v
        for r, v in zip(dst_refs, g_st):
            r[...] = v
        if nx:
            pl.when(i == nb - 1)(wait)

    res = pl.pallas_call(
        body, name=name, grid=(nb,),
        in_specs=[_row_spec(tb, w, cb, nb, True) for _, w, cb in rows] + [_full_spec(p.shape) for p in params]
        + [_saved_spec(s, nb, True) for s in state_shapes] + [_row_spec(tb, c.shape[1], 0, nb, True) for c in cts]
        + [_ANY] * (nx + na),
        out_specs=drow_specs + [_full_spec(params[k].shape) for k in diff_params] + [_ANY] * nx,
        out_shape=drow_shapes + [jax.ShapeDtypeStruct(params[k].shape, F32) for k in diff_params]
        + _exchange_shapes(side_arrays, gather),
        scratch_shapes=[pltpu.VMEM(s, F32) for s in state_shapes] + (_exchange_sems(nx) if nx else []),
        input_output_aliases={nr + npar + ns + nc + nx: 0} if na else {},
        compiler_params=_params(),
    )(*[a for a, _, _ in rows], *params, *saved, *cts, *side_arrays, *into_buffer)
    base = (res[:ndr], res[ndr:ndr + ndp])
    return base + (res[ndr + ndp:],) if nx else base


def _pick(n, target):
    if n <= target:
        return n
    best = None
    for t in range(LANES, target + 1, LANES):
        if n % t == 0:
            best = t
    assert best is not None, n
    return best


def _mesh_position():
    return lax.axis_index("x"), lax.axis_index("y"), lax.axis_index("c")


def _peer(pos, k):
    x, y, c = pos
    px = 1 - x if k & 4 else x
    py = 1 - y if k & 2 else y
    pc = 1 - c if k & 1 else c
    return (px, py, pc), 4 * px + 2 * py + pc


def _exchange_copies(x_refs, o_refs, send_sems, recv_sems, local_sems, gather):
    pos = _mesh_position()
    me = 4 * pos[0] + 2 * pos[1] + pos[2]
    pairs = list(enumerate(zip(x_refs, o_refs)))

    def remote(k, a, src, dst):
        peer, _ = _peer(pos, k)
        return pltpu.make_async_remote_copy(src_ref=src, dst_ref=dst, send_sem=send_sems.at[k - 1, a],
                                            recv_sem=recv_sems.at[k - 1, a], device_id=peer,
                                            device_id_type=pl.DeviceIdType.MESH)

    def local(a, x, o):
        return pltpu.make_async_copy(x if gather else x.at[me], o.at[me], local_sems.at[a])

    def start():
        for a, (x, o) in pairs:
            local(a, x, o).start()
        for k in range(1, N_DEV):
            peer_idx = _peer(pos, k)[1]
            for a, (x, o) in pairs:
                remote(k, a, x if gather else x.at[peer_idx], o.at[me]).start()

    def wait():
        for k in range(1, N_DEV):
            peer_idx = _peer(pos, k)[1]
            for a, (x, o) in pairs:
                arrival = remote(k, a, x if gather else x.at[me], o.at[peer_idx])
                arrival.wait_recv()
                arrival.wait_send()
        for a, (x, o) in pairs:
            local(a, x, o).wait()

    return start, wait


def _exchange_shapes(arrays, gather):
    return [jax.ShapeDtypeStruct(((N_DEV,) + x.shape) if gather else x.shape, x.dtype) for x in arrays]


def _exchange_sems(n):
    return [pltpu.SemaphoreType.DMA((N_DEV - 1, n)), pltpu.SemaphoreType.DMA((N_DEV - 1, n)),
            pltpu.SemaphoreType.DMA((n,))]


def _exchange(name, arrays, gather):
    n = len(arrays)

    def body(*refs):
        start, wait = _exchange_copies(refs[:n], refs[n:2 * n], *refs[2 * n:], gather)
        start()
        wait()

    return pl.pallas_call(
        body, name=name,
        in_specs=[pl.BlockSpec(memory_space=pl.ANY)] * n, out_specs=[pl.BlockSpec(memory_space=pl.ANY)] * n,
        out_shape=_exchange_shapes(arrays, gather), scratch_shapes=_exchange_sems(n),
    )(*arrays)


def _pick_n(n):
    return 1408 if n % 1408 == 0 else _pick(n, 512)


def matmul(name, a, b, mode, out_dtype, add=None, side=None):
    if mode == "tn":
        k, m = a.shape
    else:
        m, k = a.shape
    n = b.shape[0] if mode == "nt" else b.shape[1]
    if mode == "tn":
        tm, tn, tk = _pick(m, 1408), _pick(n, 1408), _pick(k, 1024)
    else:
        tm, tn, tk = _pick(m, 1024), _pick_n(n), _pick(k, 2816)
    nk = k // tk
    grid = (m // tm, n // tn, nk)
    a_spec = pl.BlockSpec((tk, tm), lambda i, j, kk: (kk, i)) if mode == "tn" else pl.BlockSpec((tm, tk), lambda i, j, kk: (i, kk))
    b_spec = pl.BlockSpec((tn, tk), lambda i, j, kk: (j, kk)) if mode == "nt" else pl.BlockSpec((tk, tn), lambda i, j, kk: (kk, j))
    o_spec = pl.BlockSpec((tm, tn), lambda i, j, kk: (i, j))
    dims = {"nn": (((1,), (0,)), ((), ())), "nt": (((1,), (1,)), ((), ())), "tn": (((0,), (0,)), ((), ()))}[mode]
    has_add = add is not None
    side_arrays, gather = side if side is not None else ((), False)
    ns = len(side_arrays)
    n_in = 2 + has_add

    def body(*refs):
        a_ref, b_ref = refs[0], refs[1]
        side_in, o_ref, side_out = refs[n_in:n_in + ns], refs[n_in + ns], refs[n_in + ns + 1:n_in + 2 * ns + 1]
        scratch = refs[n_in + 2 * ns + 1:]
        ids = [pl.program_id(d) for d in range(3)]
        if ns:
            start, wait = _exchange_copies(side_in, side_out, *scratch[-3:], gather)
            pl.when((ids[0] == 0) & (ids[1] == 0) & (ids[2] == 0))(start)
        part = lax.dot_general(a_ref[...].astype(BF16), b_ref[...].astype(BF16), dims, preferred_element_type=F32)
        if nk == 1:
            o_ref[...] = (part + refs[2][...].astype(F32) if has_add else part).astype(o_ref.dtype)
        else:
            acc_ref = scratch[0]

            @pl.when(ids[2] == 0)
            def _():
                acc_ref[...] = part + refs[2][...].astype(F32) if has_add else part

            @pl.when(ids[2] > 0)
            def _():
                acc_ref[...] += part

            @pl.when(ids[2] == nk - 1)
            def _():
                o_ref[...] = acc_ref[...].astype(o_ref.dtype)
        if ns:
            pl.when((ids[0] == grid[0] - 1) & (ids[1] == grid[1] - 1) & (ids[2] == nk - 1))(wait)

    any_spec = pl.BlockSpec(memory_space=pl.ANY)
    res = pl.pallas_call(
        body, name=name, grid=grid,
        in_specs=[a_spec, b_spec] + ([o_spec] if has_add else []) + [any_spec] * ns,
        out_specs=[o_spec] + [any_spec] * ns,
        out_shape=[jax.ShapeDtypeStruct((m, n), out_dtype)] + _exchange_shapes(side_arrays, gather),
        scratch_shapes=([pltpu.VMEM((tm, tn), F32)] if nk > 1 else []) + (_exchange_sems(ns) if ns else []),
        compiler_params=pltpu.CompilerParams(
            dimension_semantics=("arbitrary",) * 3 if ns else ("parallel", "parallel", "arbitrary"),
            vmem_limit_bytes=VMEM_LIMIT),
    )(a, b, *([add] if has_add else []), *side_arrays)
    return (res[0], res[1:]) if ns else res[0]


def matmul_fused(name, a, bs, mode, n, extras, epilogue, out_dtypes):
    m, k = a.shape
    tm, tn = _pick(m, 1024), _pick_n(n)
    dims = {"nn": (((1,), (0,)), ((), ())), "nt": (((1,), (1,)), ((), ()))}[mode]
    nb, nx = len(bs), len(extras)

    def b_spec(off):
        if mode == "nt":
            return pl.BlockSpec((tn, k), lambda i, j: (j + off, 0))
        return pl.BlockSpec((k, tn), lambda i, j: (0, j + off))

    def body(*refs):
        a_val = refs[0][...].astype(BF16)
        parts = tuple(lax.dot_general(a_val, r[...].astype(BF16), dims, preferred_element_type=F32)
                      for r in refs[1:1 + nb])
        tiles = tuple(r[...].astype(F32) for r in refs[1 + nb:1 + nb + nx])
        for r, v in zip(refs[1 + nb + nx:], epilogue(parts, tiles)):
            r[...] = v.astype(r.dtype)

    tile = pl.BlockSpec((tm, tn), lambda i, j: (i, j))
    return pl.pallas_call(
        body, name=name, grid=(m // tm, n // tn),
        in_specs=[pl.BlockSpec((tm, k), lambda i, j: (i, 0))] + [b_spec(off) for _, off in bs] + [tile] * nx,
        out_specs=[tile] * len(out_dtypes), out_shape=[jax.ShapeDtypeStruct((m, n), dt) for dt in out_dtypes],
        compiler_params=pltpu.CompilerParams(dimension_semantics=("parallel", "parallel"),
                                             vmem_limit_bytes=VMEM_LIMIT),
    )(a, *[b for b, _ in bs], *extras)


def _glu_fwd_tiles(parts, tiles):
    gate, up = parts
    return gate, up, jax.nn.silu(gate) * up


def _glu_bwd_tiles(parts, tiles):
    (da,), (gate, up) = parts, tiles
    _, vjp = jax.vjp(lambda g, u: jax.nn.silu(g) * u, gate, up)
    return vjp(da)


def adamw(name, w, m, v, gparts, side=None):
    layers, rows, cols = w.shape
    parts = gparts[0].shape[0]
    tr = 8
    while tr * 2 * cols <= 65536 and rows % (tr * 2) == 0:
        tr *= 2
    nblk = rows // tr
    c1 = 1.0 - ADAM_B1 ** ADAM_STEP
    c2 = 1.0 - ADAM_B2 ** ADAM_STEP
    side_arrays, gather, nx = _side(side)

    def body(*refs):
        w_ref, m_ref, v_ref = refs[:3]
        g_refs = refs[3:3 + layers]
        side_in = refs[3 + layers:3 + layers + nx]
        go_ref, d_ref, mo_ref, vo_ref = refs[3 + layers + nx:7 + layers + nx]
        side_out, sems = refs[7 + layers + nx:7 + layers + 2 * nx], refs[7 + layers + 2 * nx:]
        layer = pl.program_id(0)
        if nx:
            start, wait = _exchange_copies(side_in, side_out, *sems, gather)
            pl.when((layer == 0) & (pl.program_id(1) == 0))(start)
        g = None
        for ll, g_ref in enumerate(g_refs):
            s = g_ref[0].astype(F32)
            for p in range(1, parts):
                s = s + g_ref[p].astype(F32)
            g = s if g is None else jnp.where(layer == ll, s, g)
        m_new = ADAM_B1 * m_ref[...] + (1.0 - ADAM_B1) * g
        v_new = ADAM_B2 * v_ref[...] + (1.0 - ADAM_B2) * (g * g)
        m_hat = m_new / c1
        v_hat = v_new / c2
        go_ref[...] = g
        d_ref[...] = -ADAM_LR * (m_hat / (jnp.sqrt(v_hat) + ADAM_EPS) + ADAM_WD * w_ref[...])
        mo_ref[...] = m_new
        vo_ref[...] = v_new
        if nx:
            pl.when((layer == layers - 1) & (pl.program_id(1) == nblk - 1))(wait)

    def part_spec(ll):
        return pl.BlockSpec((parts, tr, cols),
                            lambda l, i: (0, jnp.where(l == ll, i, jnp.where(l < ll, 0, nblk - 1)), 0))

    spec = pl.BlockSpec((None, tr, cols), lambda l, i: (l, i, 0))
    res = pl.pallas_call(
        body, name=name, grid=(layers, nblk),
        in_specs=[spec, spec, spec] + [part_spec(ll) for ll in range(layers)] + [_ANY] * nx,
        out_specs=[spec] * 4 + [_ANY] * nx,
        out_shape=[jax.ShapeDtypeStruct(w.shape, F32)] * 4 + _exchange_shapes(side_arrays, gather),
        scratch_shapes=_exchange_sems(nx) if nx else [],
        compiler_params=pltpu.CompilerParams(dimension_semantics=("arbitrary", "arbitrary"),
                                             vmem_limit_bytes=VMEM_LIMIT),
    )(w, m, v, *gparts, *side_arrays)
    return (res[:4], res[4:]) if nx else res


def _adamw_math(w, m, v, g):
    m_new = ADAM_B1 * m + (1.0 - ADAM_B1) * g
    v_new = ADAM_B2 * v + (1.0 - ADAM_B2) * (g * g)
    m_hat = m_new / (1.0 - ADAM_B1 ** ADAM_STEP)
    v_hat = v_new / (1.0 - ADAM_B2 ** ADAM_STEP)
    return -ADAM_LR * (m_hat / (jnp.sqrt(v_hat) + ADAM_EPS) + ADAM_WD * w), m_new, v_new


def adamw_many(name, ws, ms, vs, garrays, where):
    n, ng = len(ws), len(garrays)

    def body(*refs):
        ins, g_refs, outs = refs[:3 * n], refs[3 * n:3 * n + ng], refs[3 * n + ng:]
        for i, sources in enumerate(where):
            for a, j, layer in sources:
                g_ref = g_refs[a]
                g = g_ref[0, j].astype(F32)
                for p in range(1, g_ref.shape[0]):
                    g = g + g_ref[p, j].astype(F32)
                sl = slice(None) if layer is None else slice(layer, layer + 1)
                delta, m_new, v_new = _adamw_math(ins[i][sl], ins[n + i][sl], ins[2 * n + i][sl], g)
                for r, val in zip(outs[4 * i:4 * i + 4], (g, delta, m_new, v_new)):
                    r[sl] = val

    vmem = pl.BlockSpec(memory_space=pltpu.VMEM)
    res = pl.pallas_call(
        body, name=name, in_specs=[vmem] * (3 * n + ng), out_specs=[vmem] * (4 * n),
        out_shape=[jax.ShapeDtypeStruct(w.shape, F32) for w in ws for _ in range(4)],
        compiler_params=pltpu.CompilerParams(vmem_limit_bytes=VMEM_LIMIT),
    )(*ws, *ms, *vs, *garrays)
    return [res[4 * i:4 * i + 4] for i in range(n)]


def _pack(arrays, dtype, row_multiple):
    flat = jnp.concatenate([a.astype(dtype).reshape(-1) for a in arrays])
    unit = row_multiple * LANES
    pad = (-flat.shape[0]) % unit
    if pad:
        flat = jnp.concatenate([flat, jnp.zeros((pad,), dtype)])
    return flat.reshape(-1, LANES)


def _unpack(flat2d, shapes):
    flat = flat2d.reshape(-1)
    out, off = [], 0
    for s in shapes:
        n = int(np.prod(s))
        out.append(flat[off:off + n].reshape(s))
        off += n
    return out


def _unpack_stacked(stacked, shapes):
    flat = stacked.reshape(N_DEV, -1)
    out, off = [], 0
    for s in shapes:
        n = int(np.prod(s))
        out.append(flat[:, off:off + n].reshape((N_DEV,) + tuple(s)))
        off += n
    return out


def _merge_shards(stacked, axis):
    t = jnp.moveaxis(stacked, 0, axis)
    s = t.shape
    return t.reshape(s[:axis] + (s[axis] * s[axis + 1],) + s[axis + 2:])


def _split_shards(full, axis, n=N_DEV):
    s = full.shape
    t = full.reshape(s[:axis] + (n, s[axis] // n) + s[axis + 1:])
    return jnp.moveaxis(t, axis, 0)


def _lower_bounds(hg_lower_bounds):
    p = jax.nn.softmax(hg_lower_bounds, axis=0)
    return jnp.cumsum(p, axis=0) - p[0:1]


def _prep_layer(p):
    def row(v):
        return v.reshape(1, -1)

    eye_b = jnp.eye(HEADS, dtype=F32)
    eye_g = jnp.eye(S5_GROUPS, dtype=F32)
    step = jnp.exp(p["s5_log_dt"])[:, None]
    lam_re, lam_im = p["s5_lam_re"], p["s5_lam_im"]
    mag = jnp.exp(lam_re * step)
    lb_re = mag * jnp.cos(lam_im * step)
    lb_im = mag * jnp.sin(lam_im * step)
    den = lam_re * lam_re + lam_im * lam_im
    f_re = ((lb_re - 1.0) * lam_re + lb_im * lam_im) / den
    f_im = (lb_im * lam_re - (lb_re - 1.0) * lam_im) / den
    bb_re = f_re[..., None] * p["s5_b_re"] - f_im[..., None] * p["s5_b_im"]
    bb_im = f_re[..., None] * p["s5_b_im"] + f_im[..., None] * p["s5_b_re"]
    width = S5_GROUPS * S5_GROUP
    n_state = S5_GROUPS * S5_STATE
    return dict(
        lb=row(p["lb"]), hg_norm=row(p["hg_norm"]), ret_norm=row(p["ret_norm"]),
        conv_w=p["lru_conv_w"], conv_b=row(p["lru_conv_b"]),
        wa=jnp.einsum("nij,nm->nimj", p["lru_wa"], eye_b).reshape(MIX_W, MIX_W).astype(BF16), ba=row(p["lru_ba"]),
        wx=jnp.einsum("nij,nm->nimj", p["lru_wx"], eye_b).reshape(MIX_W, MIX_W).astype(BF16), bx=row(p["lru_bx"]),
        sp=row(jax.nn.softplus(-p["lru_lambda"])),
        bt_re=jnp.einsum("gnp,gh->gphn", bb_re, eye_g).reshape(width, n_state).astype(BF16),
        bt_im=jnp.einsum("gnp,gh->gphn", bb_im, eye_g).reshape(width, n_state).astype(BF16),
        lb_re=row(lb_re), lb_im=row(lb_im),
        ct_re=jnp.einsum("gpn,gh->gnhp", p["s5_c_re"], eye_g).reshape(n_state, width).astype(BF16),
        ct_im=jnp.einsum("gpn,gh->gnhp", p["s5_c_im"], eye_g).reshape(n_state, width).astype(BF16),
        s5_d=row(p["s5_d"]), glu_w=p["s5_glu_w"].astype(BF16), glu_b=row(p["s5_glu_b"]),
        b_gate=row(p["b_gate"]),
        norm_mix_pre=row(p["norm_mix_pre"]), norm_mix_post=row(p["norm_mix_post"]),
        norm_xa_pre=row(p["norm_xa_pre"]), norm_xa_post=row(p["norm_xa_post"]), norm_mem=row(p["norm_mem"]),
        norm_ffn_pre=row(p["norm_ffn_pre"]), norm_ffn_post=row(p["norm_ffn_post"]),
    )


_PREP_INPUTS = ("hg_norm", "ret_norm", "lru_conv_w", "lru_conv_b", "lru_wa", "lru_ba", "lru_wx", "lru_bx", "lru_lambda",
                "s5_lam_re", "s5_lam_im", "s5_b_re", "s5_b_im", "s5_c_re", "s5_c_im", "s5_d", "s5_log_dt", "s5_glu_w",
                "s5_glu_b", "b_gate", "norm_mix_pre", "norm_mix_post", "norm_xa_pre", "norm_xa_post", "norm_mem",
                "norm_ffn_pre", "norm_ffn_post")


def _retention_constants():
    lg = np.log1p(-np.power(2.0, -5.0 - np.arange(HEADS)))
    idx = np.arange(RET_CHUNK)

    def lanes(per_head_rows):
        return np.repeat(per_head_rows.T[:, :, None], HEAD_DIM, axis=2).reshape(RET_CHUNK, MIX_W)

    xi = lanes(np.exp((idx + 1.0)[None, :] * lg[:, None]))
    zeta = lanes(np.exp((RET_CHUNK - 1.0 - idx)[None, :] * lg[:, None]))
    rel = idx[:, None] - idx[None, :]
    decay = np.where(rel[None] >= 0, np.exp(np.maximum(rel, 0)[None] * lg[:, None, None]), 0.0)
    decay = np.transpose(decay, (1, 0, 2)).reshape(RET_CHUNK, HEADS * RET_CHUNK)
    g_end = np.repeat(np.exp(RET_CHUNK * lg), HEAD_DIM)[None, :]
    return tuple(jnp.asarray(a, F32) for a in (xi, zeta, decay, g_end))


def _rotary_tables(seq):
    pos = jnp.arange(seq, dtype=F32)
    inv_freq = 10000.0 ** (-jnp.arange(0, HEAD_DIM, 2, dtype=F32) / HEAD_DIM)
    ang = pos[:, None] * inv_freq[None, :]
    cos, sin = jnp.cos(ang), jnp.sin(ang)
    cos_t = jnp.tile(jnp.repeat(cos, 2, axis=1), (1, HEADS))
    sin_t = jnp.tile(jnp.stack([-sin, sin], axis=-1).reshape(seq, HEAD_DIM), (1, HEADS))
    return cos_t, sin_t


TB_HG = 512
TB_RET = 512
TB_LRU = 256
TB_S5 = 256
TB_ROW = 256
TB_XA = 512

_STATE = (MIX_W, MIX_W)
_TAIL = (8, MIX_W)
_S5_TAIL = (8, S5_GROUPS * S5_STATE)


def _mixer_operands(l, z, kp, rot, ret_c):
    xi, zeta, decay, g_end = ret_c
    return dict(
        hg=dict(name=f"hgrn2_{l}", fn=fn_hgrn2, rows=[(z, 4 * MIX_W, 0)], params=[kp["lb"], kp["hg_norm"]],
                tb=TB_HG, states=(_STATE,), diff_params=(0, 1)),
        ret=dict(name=f"retention_{l}", fn=fn_retention, rows=[(z, 4 * MIX_W, 1), (rot[0], MIX_W, 0), (rot[1], MIX_W, 0)],
                 params=[kp["ret_norm"], xi, zeta, decay, g_end], tb=TB_RET, states=(_STATE,), diff_params=(0,)),
        lru=dict(name=f"rglru_{l}", fn=fn_rglru, rows=[(z, 2 * MIX_W, 4)],
                 params=[kp["conv_w"], kp["conv_b"], kp["wa"], kp["ba"], kp["wx"], kp["bx"], kp["sp"]],
                 tb=TB_LRU, states=(_TAIL, _TAIL), diff_params=(0, 1, 2, 3, 4, 5, 6)),
        s5=dict(name=f"s5_{l}", fn=fn_s5, rows=[(z, MIX_W, 10)],
                params=[kp["bt_re"], kp["bt_im"], kp["lb_re"], kp["lb_im"], kp["ct_re"], kp["ct_im"], kp["s5_d"],
                        kp["glu_w"], kp["glu_b"]],
                tb=TB_S5, states=(_S5_TAIL, _S5_TAIL), diff_params=tuple(range(9))),
    )


def _layer_forward(l, x, h, mem, w_in, shards, next_w_in_shard, kp, rot, ret_c, g_next, target):
    d = x.shape[1]
    sv = dict(x=x, h=h)
    bw = {"w_in": w_in}

    def gather(idx):
        return [shards[i] for i in idx], True

    def take(idx, stacked):
        for i, s in zip(idx, stacked):
            bw[_BIG_NAMES[i]] = _merge_shards(s, BIG[i][1] - 1)

    z, got = matmul(f"in_proj_{l}", h, w_in, "nn", F32, side=gather(GATHER_IN["in_proj"]))
    take(GATHER_IN["in_proj"], got)
    gl, got = matmul(f"gate_proj_{l}", h, bw["w_gate"], "nn", BF16, side=gather(GATHER_IN["gate_proj"]))
    take(GATHER_IN["gate_proj"], got)
    sv.update(z=z, gl=gl)
    ops = _mixer_operands(l, z, kp, rot, ret_c)
    ys = []
    for key in ("hg", "ret", "lru", "s5"):
        o = ops[key]
        idx = GATHER_IN.get(key)
        res = stage_fwd(o["name"] + "_fwd", o["fn"], o["rows"], o["params"], tb=o["tb"], outs=[(MIX_W, F32)],
                        states=o["states"], side=gather(idx) if idx else None)
        if idx:
            take(idx, res[3])
        ys.append(res[0][0])
        sv[key + "_states"] = res[2]
    sv["ys"] = ys
    merge_params = [bw["w_up"][n] for n in range(4)] + [kp["b_gate"], bw["w_out"], kp["norm_mix_post"]]
    merge_rows = [(y, MIX_W, 0) for y in ys] + [(gl, 4 * d, 0), (x, d, 0)]
    (x1,), _, _, got = stage_fwd(f"merge_{l}_fwd", fn_merge, merge_rows, merge_params, tb=TB_ROW, outs=[(d, F32)],
                                 side=gather(GATHER_IN["merge"]))
    take(GATHER_IN["merge"], got)
    wk, wv = bw["xa_w_kv"][:, :d], bw["xa_w_kv"][:, d:]
    mem_params = [kp["norm_mem"], wk, wv]
    (k, v), _, _ = stage_fwd(f"mem_{l}_fwd", fn_mem, [(mem, d, 0)], mem_params, tb=mem.shape[0],
                             outs=[(d, BF16), (d, BF16)])
    xa_params = [kp["norm_xa_pre"], bw["xa_w_q"], k, v, bw["xa_w_o"], kp["norm_xa_post"], kp["norm_ffn_pre"]]
    res = stage_fwd(f"xattn_{l}_fwd", fn_xattn, [(x1, d, 0)], xa_params, tb=TB_XA, outs=[(d, F32), (d, BF16)],
                    side=([next_w_in_shard], True) if next_w_in_shard is not None else None)
    x2, h3 = res[0]
    next_w_in = _merge_shards(res[3][0], BIG[0][1] - 1) if next_w_in_shard is not None else None
    f = bw["ffn_w_gu"].shape[1] // 2
    up_block = f // _pick_n(f)
    gate, up, a = matmul_fused(f"ffn_gu_{l}", h3, [(bw["ffn_w_gu"], 0), (bw["ffn_w_gu"], up_block)], "nn", f, [],
                               _glu_fwd_tiles, (BF16, BF16, BF16))
    o3 = matmul(f"ffn_down_{l}", a, bw["ffn_w_down"], "nn", F32)
    sv.update(x1=x1, k=k, v=v, x2=x2, h3=h3, gate=gate, up=up, a=a, o3=o3, merge_params=merge_params,
              merge_rows=merge_rows, mem_params=mem_params, xa_params=xa_params)
    if g_next is not None:
        (x3, hn), _, _ = stage_fwd(f"res_{l}_fwd", fn_res_norm, [(x2, d, 0), (o3, d, 0)], [kp["norm_ffn_post"], g_next],
                                   tb=TB_ROW, outs=[(d, F32), (d, BF16)])
        return x3, hn, sv, bw, next_w_in
    (dx2, do3), (loss, dg_post), _ = stage_fwd(f"loss_{l}", fn_loss_head, [(x2, d, 0), (o3, d, 0), (target, d, 0)],
                                               [kp["norm_ffn_post"]], tb=TB_ROW, outs=[(d, F32), (d, F32)],
                                               accs=[(8, LANES), (1, d)])
    sv["head"] = (dx2, do3, dg_post)
    return None, loss[0, 0], sv, bw, next_w_in


def _layer_backward(l, sv, mem, bw, kp, rot, ret_c, g_next, dx3, dhn, s5_side=None):
    d = sv["x"].shape[1]
    gk = {}
    parts = [None] * len(BIG)

    def send(i, g):
        axis = BIG[i][1] - 1
        if isinstance(g, tuple):
            return jnp.concatenate([_split_shards(p.astype(BF16), axis, N_DEV // len(g)) for p in g], axis=0)
        return _split_shards(g.astype(BF16), axis)

    def scatter(*items):
        return [send(i, g) for i, g in items], False

    def took(got, *idx):
        for i, p in zip(idx, got):
            parts[i] = p

    res_rows = [(sv["x2"], d, 0), (sv["o3"], d, 0)]
    if g_next is not None:
        (dx2, do3), (gk["norm_ffn_post"], gk["g_next"]) = stage_bwd(
            f"res_{l}_bwd", fn_res_norm, res_rows, [kp["norm_ffn_post"], g_next], [dx3, dhn], tb=TB_ROW,
            diff_rows=(0, 1), diff_params=(0, 1))
    else:
        dx2, do3, gk["norm_ffn_post"] = sv["head"]
    f = sv["a"].shape[1]
    g_down = matmul(f"ffn_down_dw_{l}", sv["a"], do3, "tn", BF16)
    d_gate, d_up = matmul_fused(f"ffn_down_dx_{l}", do3, [(bw["ffn_w_down"], 0)], "nt", f, [sv["gate"], sv["up"]],
                                _glu_bwd_tiles, (BF16, BF16))
    dh3, got = matmul(f"ffn_gate_dx_{l}", d_gate, bw["ffn_w_gu"][:, :f], "nt", F32, side=scatter((8, g_down)))
    took(got, 8)
    dh3 = matmul(f"ffn_up_dx_{l}", d_up, bw["ffn_w_gu"][:, f:], "nt", F32, add=dh3)
    g_gu = (matmul(f"ffn_gate_dw_{l}", sv["h3"], d_gate, "tn", BF16),
            matmul(f"ffn_up_dw_{l}", sv["h3"], d_up, "tn", BF16))
    (dx1,), xa_g, got = stage_bwd(f"xattn_{l}_bwd", fn_xattn, [(sv["x1"], d, 0)], sv["xa_params"], [dx2, dh3],
                                  tb=TB_XA, diff_rows=(0,), diff_params=tuple(range(7)), side=scatter((7, g_gu)))
    took(got, 7)
    gk["norm_xa_pre"], g_q, dk, dv, g_o, gk["norm_xa_post"], gk["norm_ffn_pre"] = xa_g
    _, (gk["norm_mem"], dwk, dwv) = stage_bwd(f"mem_{l}_bwd", fn_mem, [(mem, d, 0)], sv["mem_params"], [dk, dv],
                                              tb=mem.shape[0], diff_params=(0, 1, 2))
    g_kv = jnp.concatenate([dwk, dwv], axis=1)
    merge_d, merge_g, got = stage_bwd(f"merge_{l}_bwd", fn_merge, sv["merge_rows"], sv["merge_params"], [dx1],
                                      tb=TB_ROW, diff_rows=tuple(range(6)), diff_params=tuple(range(7)),
                                      row_dtypes=(F32, F32, F32, F32, BF16, F32),
                                      side=scatter((4, g_q), (6, g_o), (5, g_kv)))
    took(got, 4, 6, 5)
    dys, dgl, dx0 = merge_d[:4], merge_d[4], merge_d[5]
    g_up = jnp.stack(merge_g[:4])
    gk["b_gate"], g_out, gk["norm_mix_post"] = merge_g[4:]
    ops = _mixer_operands(l, sv["z"], kp, rot, ret_c)
    dz = None
    for key, dy in zip(("hg", "ret", "lru", "s5"), dys):
        o = ops[key]
        res = stage_bwd(o["name"] + "_bwd", o["fn"], o["rows"], o["params"], [dy], tb=o["tb"],
                        saved=sv[key + "_states"], diff_rows=(0,), diff_params=o["diff_params"], row_dtypes=(BF16,),
                        side=scatter((2, g_up), (3, g_out)) if key == "hg" else s5_side if key == "s5" else None,
                        into=(dz, sv["z"].shape[1]))
        if key == "hg":
            took(res[2], 2, 3)
        elif key == "s5" and s5_side is not None:
            gk["s5_side"] = res[2]
        dz = res[0][0]
        gk[key] = res[1]
    g_gate = matmul(f"gate_proj_dw_{l}", sv["h"], dgl, "tn", BF16)
    g_in = matmul(f"in_proj_dw_{l}", sv["h"], dz, "tn", BF16)
    dh, got = matmul(f"gate_proj_dx_{l}", dgl, bw["w_gate"], "nt", F32, side=scatter((1, g_gate)))
    took(got, 1)
    dh, got = matmul(f"in_proj_dx_{l}", dz, bw["w_in"], "nt", F32, add=dh, side=scatter((0, g_in)))
    took(got, 0)
    return dx0, dh, gk, parts


def _kernel_grads_to_prep(gk):
    hg, ret, lru, s5 = gk["hg"], gk["ret"], gk["lru"], gk["s5"]
    return dict(
        lb=hg[0], hg_norm=hg[1], ret_norm=ret[0],
        conv_w=lru[0], conv_b=lru[1], wa=lru[2], ba=lru[3], wx=lru[4], bx=lru[5], sp=lru[6],
        bt_re=s5[0], bt_im=s5[1], lb_re=s5[2], lb_im=s5[3], ct_re=s5[4], ct_im=s5[5], s5_d=s5[6], glu_w=s5[7],
        glu_b=s5[8], b_gate=gk["b_gate"], norm_mix_pre=gk["norm_mix_pre"], norm_mix_post=gk["norm_mix_post"],
        norm_xa_pre=gk["norm_xa_pre"], norm_xa_post=gk["norm_xa_post"], norm_mem=gk["norm_mem"],
        norm_ffn_pre=gk["norm_ffn_pre"], norm_ffn_post=gk["norm_ffn_post"],
    )


def _step(inp):
    x, mem, target = inp["x"][0], inp["mem"][0], inp["loss_target"][0]
    seq = x.shape[0]
    depth = inp["w_in"].shape[0]
    me = 4 * lax.axis_index("x") + 2 * lax.axis_index("y") + lax.axis_index("c")

    small_shapes = [inp[n].shape for n in _SMALL_SHARDED_NAMES]
    (small_stacked,) = _exchange("gather_small", [_pack([inp[n] for n in _SMALL_SHARDED_NAMES], F32, 8)], True)
    small_all = _unpack_stacked(small_stacked, small_shapes)
    full_small = {n: _merge_shards(s, ax) for (n, ax), s in zip(SMALL_SHARDED, small_all)}

    lbs, lbs_vjp = jax.vjp(_lower_bounds, inp["hg_lower_bounds"])
    kps, prep_vjps = [], []
    for l in range(depth):
        p = {n: (full_small[n][l] if n in full_small else inp[n][l]) for n in _PREP_INPUTS}
        p["lb"] = lbs[l]
        kp, vj = jax.vjp(_prep_layer, p)
        kps.append(kp)
        prep_vjps.append(vj)
    rot = _rotary_tables(seq)
    ret_c = _retention_constants()

    def shards(l):
        return [inp[n][l].astype(BF16) for n in _BIG_NAMES]

    (h,), _, _, (stacked,) = stage_fwd("norm_in_fwd", fn_norm, [(x, x.shape[1], 0)], [kps[0]["norm_mix_pre"]],
                                       tb=TB_ROW, outs=[(x.shape[1], BF16)], side=([shards(0)[0]], True))
    saved, bws = [], []
    xs = x
    w_in = _merge_shards(stacked, BIG[0][1] - 1)
    for l in range(depth):
        last = l + 1 == depth
        g_next = None if last else kps[l + 1]["norm_mix_pre"]
        xs, h, sv, bw, w_in = _layer_forward(l, xs, h, mem, w_in, shards(l), None if last else shards(l + 1)[0],
                                             kps[l], rot, ret_c, g_next, target)
        saved.append(sv)
        bws.append(bw)
    dy, loss_local = xs, h

    big_parts = [None] * depth
    gks = [None] * depth
    dx, dh = dy, None
    small_names = REPLICATED + _SMALL_SHARDED_NAMES
    cross_layer = ("hg_lower_bounds", "norm_mix_pre")
    groups = {}
    for n in small_names:
        if n not in cross_layer:
            groups.setdefault(full_small[n].shape[1:] if n in full_small else inp[n].shape[1:], []).append(n)

    def layer_small_grads(l):
        gk = dict(gks[l], norm_mix_pre=jnp.zeros_like(kps[l]["norm_mix_pre"]))
        (gp,) = prep_vjps[l]({k: g.astype(kps[l][k].dtype) for k, g in _kernel_grads_to_prep(gk).items()})
        return gp["lb"], [jnp.stack([gp[n].astype(BF16)[None] for n in names]) for names in groups.values()]

    d_lbs = [None] * depth
    gathered = [None] * depth
    pending = None
    for l in reversed(range(depth)):
        g_next = kps[l + 1]["norm_mix_pre"] if l + 1 < depth else None
        dx, dh, gks[l], big_parts[l] = _layer_backward(l, saved[l], mem, bws[l], kps[l], rot, ret_c, g_next, dx, dh,
                                                       s5_side=(pending, True) if pending is not None else None)
        if pending is not None:
            gathered[l + 1] = list(gks[l]["s5_side"])
        d_lbs[l], pending = layer_small_grads(l)
    (grad_x,), (g_pre0,) = stage_bwd("norm_in_bwd", fn_keep_norm, [(x, x.shape[1], 0)], [kps[0]["norm_mix_pre"]],
                                     [dx, dh], tb=TB_ROW, diff_rows=(0,), diff_params=(0,))
    (g_lower_bounds,) = lbs_vjp(jnp.stack(d_lbs))
    g_mix_pre = jnp.concatenate([g_pre0] + [gks[l]["g_next"] for l in range(depth - 1)], axis=0)
    small_send = pending + [g_lower_bounds.astype(BF16)[None], g_mix_pre.astype(BF16)[None]]

    out = {}
    kinds = ("grad_", "delta_", "new_m_", "new_v_")
    small_parts = None
    for i, n in sorted(enumerate(_BIG_NAMES), key=lambda t: -int(np.prod(inp[t[1]].shape))):
        shape = inp[n].shape
        three = (shape[0], int(np.prod(shape[1:-1])), shape[-1])
        res = adamw("adamw_" + n, *[inp[pre + n].reshape(three) for pre in ("", "m_", "v_")],
                    [big_parts[l][i].reshape((N_DEV,) + three[1:]) for l in range(depth)],
                    side=(small_send, True) if small_parts is None else None)
        if small_parts is None:
            res, small_parts = res
        for kind, a in zip(kinds, res):
            out[kind + n] = a.reshape(shape)
    ng = len(groups)
    gathered[0] = list(small_parts[:ng])
    place = {n: (gi, j) for gi, names in enumerate(groups.values()) for j, n in enumerate(names)}
    for n, ax in SMALL_SHARDED:
        gi, j = place[n]
        assert len(list(groups.values())[gi]) == 1
        width = inp[n].shape[ax]
        for l in range(depth):
            gathered[l][gi] = lax.dynamic_slice_in_dim(gathered[l][gi], me * width, width, axis=ax + 2)
    garrays = [a for l in range(depth) for a in gathered[l]] + list(small_parts[ng:])
    where = [[(depth * ng + cross_layer.index(n), 0, None)] if n in cross_layer else
             [(l * ng + place[n][0], place[n][1], l) for l in range(depth)] for n in small_names]
    res = adamw_many("adamw_small", *[[inp[pre + n] for n in small_names] for pre in ("", "m_", "v_")], garrays, where)
    for n, quad in zip(small_names, res):
        for kind, a in zip(kinds, quad):
            out[kind + n] = a

    out["loss"] = lax.psum(loss_local, ("x", "y", "c"))
    out["grad_x"] = grad_x[None]
    return out


def kernel(x, mem, hg_lower_bounds, norm_mix_pre, norm_mix_post, w_in, w_gate, b_gate, hg_norm, ret_norm, lru_conv_w, lru_conv_b, lru_wa, lru_ba, lru_wx, lru_bx, lru_lambda, s5_lam_re, s5_lam_im, s5_b_re, s5_b_im, s5_c_re, s5_c_im, s5_d, s5_log_dt, s5_glu_w, s5_glu_b, w_up, w_out, norm_xa_pre, norm_xa_post, norm_mem, xa_w_q, xa_w_kv, xa_w_o, norm_ffn_pre, norm_ffn_post, ffn_w_gu, ffn_w_down, loss_target, m_hg_lower_bounds, m_norm_mix_pre, m_norm_mix_post, m_w_in, m_w_gate, m_b_gate, m_hg_norm, m_ret_norm, m_lru_conv_w, m_lru_conv_b, m_lru_wa, m_lru_ba, m_lru_wx, m_lru_bx, m_lru_lambda, m_s5_lam_re, m_s5_lam_im, m_s5_b_re, m_s5_b_im, m_s5_c_re, m_s5_c_im, m_s5_d, m_s5_log_dt, m_s5_glu_w, m_s5_glu_b, m_w_up, m_w_out, m_norm_xa_pre, m_norm_xa_post, m_norm_mem, m_xa_w_q, m_xa_w_kv, m_xa_w_o, m_norm_ffn_pre, m_norm_ffn_post, m_ffn_w_gu, m_ffn_w_down, v_hg_lower_bounds, v_norm_mix_pre, v_norm_mix_post, v_w_in, v_w_gate, v_b_gate, v_hg_norm, v_ret_norm, v_lru_conv_w, v_lru_conv_b, v_lru_wa, v_lru_ba, v_lru_wx, v_lru_bx, v_lru_lambda, v_s5_lam_re, v_s5_lam_im, v_s5_b_re, v_s5_b_im, v_s5_c_re, v_s5_c_im, v_s5_d, v_s5_log_dt, v_s5_glu_w, v_s5_glu_b, v_w_up, v_w_out, v_norm_xa_pre, v_norm_xa_post, v_norm_mem, v_xa_w_q, v_xa_w_kv, v_xa_w_o, v_norm_ffn_pre, v_norm_ffn_post, v_ffn_w_gu, v_ffn_w_down):
    values = (x, mem, hg_lower_bounds, norm_mix_pre, norm_mix_post, w_in, w_gate, b_gate, hg_norm, ret_norm, lru_conv_w, lru_conv_b, lru_wa, lru_ba, lru_wx, lru_bx, lru_lambda, s5_lam_re, s5_lam_im, s5_b_re, s5_b_im, s5_c_re, s5_c_im, s5_d, s5_log_dt, s5_glu_w, s5_glu_b, w_up, w_out, norm_xa_pre, norm_xa_post, norm_mem, xa_w_q, xa_w_kv, xa_w_o, norm_ffn_pre, norm_ffn_post, ffn_w_gu, ffn_w_down, loss_target, m_hg_lower_bounds, m_norm_mix_pre, m_norm_mix_post, m_w_in, m_w_gate, m_b_gate, m_hg_norm, m_ret_norm, m_lru_conv_w, m_lru_conv_b, m_lru_wa, m_lru_ba, m_lru_wx, m_lru_bx, m_lru_lambda, m_s5_lam_re, m_s5_lam_im, m_s5_b_re, m_s5_b_im, m_s5_c_re, m_s5_c_im, m_s5_d, m_s5_log_dt, m_s5_glu_w, m_s5_glu_b, m_w_up, m_w_out, m_norm_xa_pre, m_norm_xa_post, m_norm_mem, m_xa_w_q, m_xa_w_kv, m_xa_w_o, m_norm_ffn_pre, m_norm_ffn_post, m_ffn_w_gu, m_ffn_w_down, v_hg_lower_bounds, v_norm_mix_pre, v_norm_mix_post, v_w_in, v_w_gate, v_b_gate, v_hg_norm, v_ret_norm, v_lru_conv_w, v_lru_conv_b, v_lru_wa, v_lru_ba, v_lru_wx, v_lru_bx, v_lru_lambda, v_s5_lam_re, v_s5_lam_im, v_s5_b_re, v_s5_b_im, v_s5_c_re, v_s5_c_im, v_s5_d, v_s5_log_dt, v_s5_glu_w, v_s5_glu_b, v_w_up, v_w_out, v_norm_xa_pre, v_norm_xa_post, v_norm_mem, v_xa_w_q, v_xa_w_kv, v_xa_w_o, v_norm_ffn_pre, v_norm_ffn_post, v_ffn_w_gu, v_ffn_w_down)
    names = ("x", "mem") + WEIGHTS + ("loss_target",) + tuple("m_" + n for n in WEIGHTS) + tuple("v_" + n for n in WEIGHTS)
    out = _step(dict(zip(names, values)))
    order = ["loss", "grad_x"] + [k + n for k in ("grad_", "delta_", "new_m_", "new_v_") for n in WEIGHTS]
    return tuple(out[k] for k in order)
```

```python
import functools

import numpy as np
import jax
import jax.numpy as jnp
from jax import lax
from jax.experimental import pallas as pl
from jax.experimental.pallas import tpu as pltpu

F32 = jnp.float32
BF16 = jnp.bfloat16
EPS = 1e-6
N_DEV = 8
LANES = 128
SUBLANES = 8
VMEM_LIMIT = 60 * 1024 * 1024

HEADS = 4
HEAD_DIM = 64
MIX_W = HEADS * HEAD_DIM
HG_CHUNK = 32
RET_CHUNK = 128
S5_GROUPS = 16
S5_GROUP = 16
S5_STATE = 64
LRU_C = 8.0
XA_HEADS = 4

ADAM_LR = 0.001
ADAM_B1 = 0.9
ADAM_B2 = 0.999
ADAM_EPS = 1e-08
ADAM_WD = 0.01
ADAM_STEP = 10

BIG = (("w_in", 2), ("w_gate", 2), ("w_up", 3), ("w_out", 1), ("xa_w_q", 1), ("xa_w_kv", 2), ("xa_w_o", 1),
       ("ffn_w_gu", 2), ("ffn_w_down", 1))
SMALL_SHARDED = (("lru_conv_w", 2), ("s5_glu_w", 1))
WEIGHTS = ("hg_lower_bounds", "norm_mix_pre", "norm_mix_post", "w_in", "w_gate", "b_gate", "hg_norm", "ret_norm",
           "lru_conv_w", "lru_conv_b", "lru_wa", "lru_ba", "lru_wx", "lru_bx", "lru_lambda", "s5_lam_re", "s5_lam_im",
           "s5_b_re", "s5_b_im", "s5_c_re", "s5_c_im", "s5_d", "s5_log_dt", "s5_glu_w", "s5_glu_b", "w_up", "w_out",
           "norm_xa_pre", "norm_xa_post", "norm_mem", "xa_w_q", "xa_w_kv", "xa_w_o", "norm_ffn_pre", "norm_ffn_post",
           "ffn_w_gu", "ffn_w_down")
GATHER_IN = {"in_proj": (1,), "gate_proj": (7,), "hg": (5,), "s5": (2, 3, 4, 6), "merge": (8,)}
_BIG_NAMES = tuple(n for n, _ in BIG)
_SMALL_SHARDED_NAMES = tuple(n for n, _ in SMALL_SHARDED)
REPLICATED = tuple(n for n in WEIGHTS if n not in _BIG_NAMES and n not in _SMALL_SHARDED_NAMES)


def _dot(a, b):
    return jnp.dot(a.astype(BF16), b.astype(BF16), preferred_element_type=F32)


def _dot_nt(a, b):
    return lax.dot_general(a.astype(BF16), b.astype(BF16), (((1,), (1,)), ((), ())), preferred_element_type=F32)


def _dot_tn(a, b):
    return lax.dot_general(a.astype(BF16), b.astype(BF16), (((0,), (0,)), ((), ())), preferred_element_type=F32)


def _dot_exact(a, b):
    return jnp.dot(a, b, precision=lax.Precision.HIGHEST, preferred_element_type=F32)


def _rms(x, g):
    return x * lax.rsqrt(jnp.mean(x * x, axis=-1, keepdims=True) + EPS) * g


def _shift_down(x, d, fill):
    return jnp.concatenate([jnp.full((d, x.shape[1]), fill, x.dtype), x[:-d]], axis=0)


def _shift_up(x, d, fill):
    return jnp.concatenate([x[d:], jnp.full((d, x.shape[1]), fill, x.dtype)], axis=0)


def _cumsum_rows(x):
    d = 1
    while d < x.shape[0]:
        x = x + _shift_down(x, d, 0.0)
        d *= 2
    return x


def _lane_head(shape, dim):
    return lax.shift_right_logical(lax.broadcasted_iota(jnp.int32, shape, dim), 6)


def _head_masks(width=MIX_W):
    head = _lane_head((1, width), 1)
    return [(head == h).astype(F32) for h in range(HEADS)]


def _block_diag_mask():
    return (_lane_head((MIX_W, MIX_W), 0) == _lane_head((MIX_W, MIX_W), 1)).astype(F32)


def _head_rms(o, g):
    ms = _dot_exact(o * o, _block_diag_mask()) * (1.0 / HEAD_DIM)
    return o * lax.rsqrt(ms + EPS) * g


def _swap_pairs(x):
    lane = lax.broadcasted_iota(jnp.int32, x.shape, 1)
    return jnp.where((lane & 1) == 0, jnp.roll(x, -1, axis=1), jnp.roll(x, 1, axis=1))


def _stack_heads(t, masks):
    return jnp.concatenate([t * m for m in masks], axis=0)


@jax.custom_vjp
def _real_scan(a, u, h0):
    return _real_scan_fwd(a, u, h0)[0]


def _real_scan_fwd(a, u, h0):
    t = a.shape[0]
    acc_a, acc_u = a, u
    d = 1
    while d < t:
        acc_u = acc_u + acc_a * _shift_down(acc_u, d, 0.0)
        acc_a = acc_a * _shift_down(acc_a, d, 1.0)
        d *= 2
    h = acc_u + acc_a * h0
    return h, (a, h, h0)


def _real_scan_bwd(res, dh):
    a, h, h0 = res
    t = a.shape[0]
    acc_a = _shift_up(a, 1, 0.0)
    g = dh
    d = 1
    while d < t:
        g = g + acc_a * _shift_up(g, d, 0.0)
        acc_a = acc_a * _shift_up(acc_a, d, 1.0)
        d *= 2
    h_prev = jnp.concatenate([h0, h[:-1]], axis=0)
    return g * h_prev, g, (a * g)[0:1]


_real_scan.defvjp(_real_scan_fwd, _real_scan_bwd)


def _cmul(ar, ai, br, bi):
    return ar * br - ai * bi, ar * bi + ai * br


def _geometric_sums(ar, ai, ur, ui, forward):
    shift = _shift_down if forward else _shift_up
    t = ur.shape[0]
    g = SUBLANES
    in_group = lax.broadcasted_iota(jnp.int32, ur.shape, 0) & (g - 1)
    pr, pi, sr, si = ar, ai, ur, ui
    d = 1
    while d < g:
        keep = in_group >= d if forward else in_group < g - d
        mr, mi = _cmul(pr, pi, jnp.where(keep, shift(sr, d, 0.0), 0.0), jnp.where(keep, shift(si, d, 0.0), 0.0))
        sr, si = sr + mr, si + mi
        pr, pi = _cmul(pr, pi, pr, pi)
        d *= 2
    row = lax.broadcasted_iota(jnp.int32, (g, ur.shape[1]), 0)
    qr, qi = ar, ai
    tr, ti = jnp.zeros((g, ur.shape[1]), F32), jnp.zeros((g, ur.shape[1]), F32)
    for r in range(g):
        here = row == (r if forward else g - 1 - r)
        tr, ti = jnp.where(here, qr, tr), jnp.where(here, qi, ti)
        qr, qi = _cmul(qr, qi, ar, ai)
    outs_r, outs_i = [], []
    cr = ci = None
    order = range(t // g) if forward else reversed(range(t // g))
    for n in order:
        br, bi = sr[n * g:(n + 1) * g], si[n * g:(n + 1) * g]
        if cr is not None:
            mr, mi = _cmul(tr, ti, cr, ci)
            br, bi = br + mr, bi + mi
        edge = slice(g - 1, g) if forward else slice(0, 1)
        cr, ci = br[edge], bi[edge]
        outs_r.append(br)
        outs_i.append(bi)
    if not forward:
        outs_r.reverse()
        outs_i.reverse()
    return jnp.concatenate(outs_r, axis=0), jnp.concatenate(outs_i, axis=0)


@jax.custom_vjp
def _complex_scan(ar, ai, ur, ui, h0r, h0i):
    return _complex_scan_fwd(ar, ai, ur, ui, h0r, h0i)[0]


def _complex_scan_fwd(ar, ai, ur, ui, h0r, h0i):
    first = lax.broadcasted_iota(jnp.int32, ur.shape, 0) == 0
    cr, ci = _cmul(ar, ai, h0r, h0i)
    hr, hi = _geometric_sums(ar, ai, ur + jnp.where(first, cr, 0.0), ui + jnp.where(first, ci, 0.0), True)
    return (hr, hi), (ar, ai, hr, hi, h0r, h0i)


def _complex_scan_bwd(res, dh):
    ar, ai, hr, hi, h0r, h0i = res
    gr, gi = _geometric_sums(ar, -ai, dh[0], dh[1], False)
    qr = jnp.concatenate([h0r, hr[:-1]], axis=0)
    qi = jnp.concatenate([h0i, hi[:-1]], axis=0)
    dar = jnp.sum(gr * qr + gi * qi, axis=0, keepdims=True)
    dai = jnp.sum(gi * qr - gr * qi, axis=0, keepdims=True)
    d0r, d0i = _cmul(ar, -ai, gr[0:1], gi[0:1])
    return dar, dai, gr, gi, d0r, d0i


_complex_scan.defvjp(_complex_scan_fwd, _complex_scan_bwd)


def fn_norm(st, rows, params):
    (x,), (g,) = rows, params
    return (), (_rms(x, g),), ()


def fn_keep_norm(st, rows, params):
    (x,), (g,) = rows, params
    return (), (x, _rms(x, g)), ()


def fn_hgrn2(st, rows, params):
    (state,) = st
    (z,) = rows
    lb, norm_g = params
    q, f_logit, v_all, g = (z[:, k * MIX_W:(k + 1) * MIX_W] for k in range(4))
    f = lb + (1.0 - lb) * jax.nn.sigmoid(f_logit)
    log_f = jnp.log(f)
    k_all = 1.0 - f
    q_all = jax.nn.silu(q)
    masks = _head_masks()
    bd = _block_diag_mask()
    c = HG_CHUNK
    col = lax.broadcasted_iota(jnp.int32, (c, HEADS * c), 1) & (c - 1)
    causal = col <= lax.broadcasted_iota(jnp.int32, (c, HEADS * c), 0)
    outs = []
    for n in range(z.shape[0] // c):
        sl = slice(n * c, (n + 1) * c)
        lf = log_f[sl]
        b = _cumsum_rows(lf)
        b_end = jnp.sum(lf, axis=0, keepdims=True)
        q_dec = q_all[sl] * jnp.exp(b)
        k_inv = k_all[sl] * jnp.exp(-b)
        k_end = k_all[sl] * jnp.exp(b_end - b)
        v = v_all[sl]
        scores = jnp.where(causal, _dot_nt(q_dec, _stack_heads(k_inv, masks)), 0.0)
        outs.append(_dot(scores, _stack_heads(v, masks)) + _dot_nt(q_dec, state))
        state = state * jnp.exp(b_end) + _dot_tn(v, k_end) * bd
    o = jnp.concatenate(outs, axis=0) if len(outs) > 1 else outs[0]
    return (state,), (_head_rms(o, norm_g) * jax.nn.silu(g),), ()


def fn_retention(st, rows, params):
    (state,) = st
    z, cos_t, sin_t = rows
    norm_g, xi, zeta, decay, g_end = params
    q, k, v_all, g = (z[:, i * MIX_W:(i + 1) * MIX_W] for i in range(4))
    q_all = q * cos_t + _swap_pairs(q) * sin_t
    k_all = (k * cos_t + _swap_pairs(k) * sin_t) * (HEAD_DIM ** -0.5)
    masks = _head_masks()
    bd = _block_diag_mask()
    c = RET_CHUNK
    outs = []
    for n in range(z.shape[0] // c):
        sl = slice(n * c, (n + 1) * c)
        qc, kc, v = q_all[sl], k_all[sl], v_all[sl]
        scores = _dot_nt(qc, _stack_heads(kc, masks)) * decay
        outs.append(_dot(scores, _stack_heads(v, masks)) + _dot_nt(qc * xi, state))
        state = state * g_end + _dot_tn(v, kc * zeta) * bd
    o = jnp.concatenate(outs, axis=0) if len(outs) > 1 else outs[0]
    return (state,), (_head_rms(o, norm_g) * jax.nn.silu(g),), ()


def fn_rglru(st, rows, params):
    tail_x, tail_h = st
    (z,) = rows
    conv_w, conv_b, wa, ba, wx, bx, sp = params
    t = z.shape[0]
    xg, xi = z[:, :MIX_W], z[:, MIX_W:]
    full = jnp.concatenate([tail_x, xi], axis=0)
    xc = conv_b
    for k in range(4):
        xc = xc + conv_w[k:k + 1] * full[5 + k:5 + k + t]
    r = jax.nn.sigmoid(_dot(xc, wa) + ba)
    ig = jax.nn.sigmoid(_dot(xc, wx) + bx)
    log_a = -LRU_C * r * sp
    a = jnp.exp(log_a)
    one_minus_a2 = -jnp.tanh(log_a) * (a * a + 1.0)
    u = jnp.sqrt(one_minus_a2) * (ig * xc)
    h = _real_scan(a, u, tail_h[7:8])
    return (xi[t - 8:], h[t - 8:]), (h * jax.nn.gelu(xg),), ()


def fn_s5(st, rows, params):
    tail_r, tail_i = st
    (u,) = rows
    bt_re, bt_im, lb_re, lb_im, ct_re, ct_im, d, glu_w, glu_b = params
    t = u.shape[0]
    bu_re = _dot(u, bt_re)
    bu_im = _dot(u, bt_im)
    h_re, h_im = _complex_scan(lb_re, lb_im, bu_re, bu_im, tail_r[7:8], tail_i[7:8])
    y = _dot(h_re, ct_re) - _dot(h_im, ct_im) + d * u
    act = jax.nn.gelu(y)
    out = act * jax.nn.sigmoid(_dot(act, glu_w) + glu_b)
    return (h_re[t - 8:], h_im[t - 8:]), (out,), ()


def fn_merge(st, rows, params):
    ya, yb, yc, yd, gl, x = rows
    w0, w1, w2, w3, b_gate, w_out, g_post = params
    d = x.shape[1]
    mix = None
    for n, (y, w) in enumerate(((ya, w0), (yb, w1), (yc, w2), (yd, w3))):
        gate = jax.nn.sigmoid(gl[:, n * d:(n + 1) * d] + b_gate[:, n * d:(n + 1) * d])
        term = gate * _dot(y, w)
        mix = term if mix is None else mix + term
    return (), (x + _rms(_dot(mix, w_out), g_post),), ()


def fn_mem(st, rows, params):
    (mem,), (g, wk, wv) = rows, params
    m = _rms(mem, g)
    return (), (_dot(m, wk), _dot(m, wv)), ()


def fn_xattn(st, rows, params):
    (x,) = rows
    g_pre, wq, k, v, wo, g_post, g_next = params
    d = x.shape[1]
    dh = d // XA_HEADS
    q = _dot(_rms(x, g_pre), wq)
    heads = []
    for h in range(XA_HEADS):
        sl = slice(h * dh, (h + 1) * dh)
        s = _dot_nt(q[:, sl], k[:, sl]) * (dh ** -0.5)
        heads.append(_dot(jax.nn.softmax(s, axis=-1), v[:, sl]))
    x2 = x + _rms(_dot(jnp.concatenate(heads, axis=1), wo), g_post)
    return (), (x2, _rms(x2, g_next)), ()


def fn_res_norm(st, rows, params):
    (x, o), (g_post, g_next) = rows, params
    xn = x + _rms(o, g_post)
    return (), (xn, _rms(xn, g_next)), ()


def fn_loss_head(st, rows, params):
    (x, o, target), (g_post,) = rows, params
    y, vjp = jax.vjp(lambda o_, g_: x + _rms(o_, g_), o, g_post)
    err = y - target
    inv_d = 1.0 / x.shape[1]
    dy = err * inv_d
    do, dg = vjp(dy)
    loss = 0.5 * inv_d * jnp.sum(err * err)
    return (), (dy, do), (jnp.full((8, LANES), loss, F32), dg)


def _params():
    return pltpu.CompilerParams(dimension_semantics=("arbitrary",), vmem_limit_bytes=VMEM_LIMIT)


def _row_spec(tb, width, colblk, nb, reverse):
    if reverse:
        return pl.BlockSpec((tb, width), lambda i: (nb - 1 - i, colblk))
    return pl.BlockSpec((tb, width), lambda i: (i, colblk))


def _full_spec(shape):
    return pl.BlockSpec(shape, lambda i: (0,) * len(shape), pipeline_mode=pl.Buffered(1))


def _saved_spec(shape, nb, reverse):
    if reverse:
        return pl.BlockSpec((1,) + shape, lambda i: (nb - 1 - i, 0, 0))
    return pl.BlockSpec((1,) + shape, lambda i: (i, 0, 0))


def _param_value(ref):
    v = ref[...]
    return v if v.dtype == BF16 else v.astype(F32)


def _side(side):
    arrays, gather = side if side is not None else ((), False)
    return list(arrays), gather, len(arrays)


_ANY = pl.BlockSpec(memory_space=pl.ANY)


def stage_fwd(name, fn, rows, params, *, tb, outs, states=(), accs=(), side=None):
    n_rows = rows[0][0].shape[0]
    nb = n_rows // tb
    nr, npar, no, na, ns = len(rows), len(params), len(outs), len(accs), len(states)
    side_arrays, gather, nx = _side(side)

    def body(*refs):
        row_refs, par_refs, side_in = refs[:nr], refs[nr:nr + npar], refs[nr + npar:nr + npar + nx]
        o = nr + npar + nx
        out_refs, acc_refs = refs[o:o + no], refs[o + no:o + no + na]
        saved_refs = refs[o + no + na:o + no + na + ns]
        o = o + no + na + ns
        side_out, st_refs, sems = refs[o:o + nx], refs[o + nx:o + nx + ns], refs[o + nx + ns:]
        i = pl.program_id(0)
        if nx:
            start, wait = _exchange_copies(side_in, side_out, *sems, gather)
            pl.when(i == 0)(start)

        @pl.when(i == 0)
        def _():
            for r in st_refs + acc_refs:
                r[...] = jnp.zeros_like(r)

        st = tuple(r[...] for r in st_refs)
        for sv, s in zip(saved_refs, st):
            sv[0] = s
        new_st, out_vals, acc_vals = fn(st, tuple(r[...].astype(F32) for r in row_refs),
                                        tuple(_param_value(r) for r in par_refs))
        for r, v in zip(out_refs, out_vals):
            r[...] = v.astype(r.dtype)
        for r, v in zip(acc_refs, acc_vals):
            r[...] += v
        for r, v in zip(st_refs, new_st):
            r[...] = v
        if nx:
            pl.when(i == nb - 1)(wait)

    res = pl.pallas_call(
        body, name=name, grid=(nb,),
        in_specs=[_row_spec(tb, w, cb, nb, False) for _, w, cb in rows] + [_full_spec(p.shape) for p in params]
        + [_ANY] * nx,
        out_specs=[_row_spec(tb, w, 0, nb, False) for w, _ in outs] + [_full_spec(s) for s in accs]
        + [_saved_spec(s, nb, False) for s in states] + [_ANY] * nx,
        out_shape=[jax.ShapeDtypeStruct((n_rows, w), dt) for w, dt in outs]
        + [jax.ShapeDtypeStruct(s, F32) for s in accs] + [jax.ShapeDtypeStruct((nb,) + s, F32) for s in states]
        + _exchange_shapes(side_arrays, gather),
        scratch_shapes=[pltpu.VMEM(s, F32) for s in states] + (_exchange_sems(nx) if nx else []),
        compiler_params=_params(),
    )(*[a for a, _, _ in rows], *params, *side_arrays)
    base = (res[:no], res[no:no + na], res[no + na:no + na + ns])
    return base + (res[no + na + ns:],) if nx else base


def stage_bwd(name, fn, rows, params, cts, *, tb, saved=(), diff_rows=(), diff_params=(), row_dtypes=None, side=None,
              into=None):
    n_rows = rows[0][0].shape[0]
    nb = n_rows // tb
    nr, npar, ns, nc = len(rows), len(params), len(saved), len(cts)
    ndr, ndp = len(diff_rows), len(diff_params)
    row_dtypes = row_dtypes or (F32,) * ndr
    state_shapes = [s.shape[1:] for s in saved]
    side_arrays, gather, nx = _side(side)
    into_buffer = [into[0]] if into is not None and into[0] is not None else []
    na = len(into_buffer)
    if into is not None:
        assert ndr == 1
        drow_specs = [_row_spec(tb, rows[diff_rows[0]][1], rows[diff_rows[0]][2], nb, True)]
        drow_shapes = [jax.ShapeDtypeStruct((n_rows, into[1]), row_dtypes[0])]
    else:
        drow_specs = [_row_spec(tb, rows[k][1], 0, nb, True) for k in diff_rows]
        drow_shapes = [jax.ShapeDtypeStruct((n_rows, rows[k][1]), dt) for k, dt in zip(diff_rows, row_dtypes)]

    def body(*refs):
        row_refs, par_refs = refs[:nr], refs[nr:nr + npar]
        o = nr + npar
        saved_refs, ct_refs, side_in = refs[o:o + ns], refs[o + ns:o + ns + nc], refs[o + ns + nc:o + ns + nc + nx]
        o = o + ns + nc + nx + na
        drow_refs, dpar_refs, side_out = refs[o:o + ndr], refs[o + ndr:o + ndr + ndp], refs[o + ndr + ndp:o + ndr + ndp + nx]
        o = o + ndr + ndp + nx
        dst_refs, sems = refs[o:o + ns], refs[o + ns:]
        i = pl.program_id(0)
        if nx:
            start, wait = _exchange_copies(side_in, side_out, *sems, gather)
            pl.when(i == 0)(start)

        @pl.when(i == 0)
        def _():
            for r in dst_refs + dpar_refs:
                r[...] = jnp.zeros_like(r)

        st = tuple(r[0] for r in saved_refs)
        row_vals = [r[...].astype(F32) for r in row_refs]
        par_vals = [_param_value(r) for r in par_refs]

        def f(st_, dr_, dp_):
            rv, pv = list(row_vals), list(par_vals)
            for k, v in zip(diff_rows, dr_):
                rv[k] = v
            for k, v in zip(diff_params, dp_):
                pv[k] = v
            new_st, out_vals, _ = fn(st_, tuple(rv), tuple(pv))
            return new_st, out_vals

        _, vjp = jax.vjp(f, st, tuple(row_vals[k] for k in diff_rows), tuple(par_vals[k] for k in diff_params))
        g_st, g_rows, g_par = vjp((tuple(r[...] for r in dst_refs), tuple(r[...].astype(F32) for r in ct_refs)))
        for r, v in zip(drow_refs, g_rows):
            r[...] = v.astype(r.dtype)
        for r, v in zip(dpar_refs, g_par):
            r[...] += v
        for r, v in zip(dst_refs, g_st):
            r[...] = v
        if nx:
            pl.when(i == nb - 1)(wait)

    res = pl.pallas_call(
        body, name=name, grid=(nb,),
        in_specs=[_row_spec(tb, w, cb, nb, True) for _, w, cb in rows] + [_full_spec(p.shape) for p in params]
        + [_saved_spec(s, nb, True) for s in state_shapes] + [_row_spec(tb, c.shape[1], 0, nb, True) for c in cts]
        + [_ANY] * (nx + na),
        out_specs=drow_specs + [_full_spec(params[k].shape) for k in diff_params] + [_ANY] * nx,
        out_shape=drow_shapes + [jax.ShapeDtypeStruct(params[k].shape, F32) for k in diff_params]
        + _exchange_shapes(side_arrays, gather),
        scratch_shapes=[pltpu.VMEM(s, F32) for s in state_shapes] + (_exchange_sems(nx) if nx else []),
        input_output_aliases={nr + npar + ns + nc + nx: 0} if na else {},
        compiler_params=_params(),
    )(*[a for a, _, _ in rows], *params, *saved, *cts, *side_arrays, *into_buffer)
    base = (res[:ndr], res[ndr:ndr + ndp])
    return base + (res[ndr + ndp:],) if nx else base


def _pick(n, target):
    if n <= target:
        return n
    best = None
    for t in range(LANES, target + 1, LANES):
        if n % t == 0:
            best = t
    assert best is not None, n
    return best


def _mesh_position():
    return lax.axis_index("x"), lax.axis_index("y"), lax.axis_index("c")


def _peer(pos, k):
    x, y, c = pos
    px = 1 - x if k & 4 else x
    py = 1 - y if k & 2 else y
    pc = 1 - c if k & 1 else c
    return (px, py, pc), 4 * px + 2 * py + pc


def _exchange_copies(x_refs, o_refs, send_sems, recv_sems, local_sems, gather):
    pos = _mesh_position()
    me = 4 * pos[0] + 2 * pos[1] + pos[2]
    pairs = list(enumerate(zip(x_refs, o_refs)))

    def remote(k, a, src, dst):
        peer, _ = _peer(pos, k)
        return pltpu.make_async_remote_copy(src_ref=src, dst_ref=dst, send_sem=send_sems.at[k - 1, a],
                                            recv_sem=recv_sems.at[k - 1, a], device_id=peer,
                                            device_id_type=pl.DeviceIdType.MESH)

    def local(a, x, o):
        return pltpu.make_async_copy(x if gather else x.at[me], o.at[me], local_sems.at[a])

    def start():
        for a, (x, o) in pairs:
            local(a, x, o).start()
        for k in range(1, N_DEV):
            peer_idx = _peer(pos, k)[1]
            for a, (x, o) in pairs:
                remote(k, a, x if gather else x.at[peer_idx], o.at[me]).start()

    def wait():
        for k in range(1, N_DEV):
            peer_idx = _peer(pos, k)[1]
            for a, (x, o) in pairs:
                arrival = remote(k, a, x if gather else x.at[me], o.at[peer_idx])
                arrival.wait_recv()
                arrival.wait_send()
        for a, (x, o) in pairs:
            local(a, x, o).wait()

    return start, wait


def _exchange_shapes(arrays, gather):
    return [jax.ShapeDtypeStruct(((N_DEV,) + x.shape) if gather else x.shape, x.dtype) for x in arrays]


def _exchange_sems(n):
    return [pltpu.SemaphoreType.DMA((N_DEV - 1, n)), pltpu.SemaphoreType.DMA((N_DEV - 1, n)),
            pltpu.SemaphoreType.DMA((n,))]


def _exchange(name, arrays, gather):
    n = len(arrays)

    def body(*refs):
        start, wait = _exchange_copies(refs[:n], refs[n:2 * n], *refs[2 * n:], gather)
        start()
        wait()

    return pl.pallas_call(
        body, name=name,
        in_specs=[pl.BlockSpec(memory_space=pl.ANY)] * n, out_specs=[pl.BlockSpec(memory_space=pl.ANY)] * n,
        out_shape=_exchange_shapes(arrays, gather), scratch_shapes=_exchange_sems(n),
    )(*arrays)


def _pick_n(n):
    return 1408 if n % 1408 == 0 else _pick(n, 512)


def matmul(name, a, b, mode, out_dtype, add=None, side=None):
    if mode == "tn":
        k, m = a.shape
    else:
        m, k = a.shape
    n = b.shape[0] if mode == "nt" else b.shape[1]
    if mode == "tn":
        tm, tn, tk = _pick(m, 1408), _pick(n, 1408), _pick(k, 1024)
    else:
        tm, tn, tk = _pick(m, 1024), _pick_n(n), _pick(k, 2816)
    nk = k // tk
    grid = (m // tm, n // tn, nk)
    a_spec = pl.BlockSpec((tk, tm), lambda i, j, kk: (kk, i)) if mode == "tn" else pl.BlockSpec((tm, tk), lambda i, j, kk: (i, kk))
    b_spec = pl.BlockSpec((tn, tk), lambda i, j, kk: (j, kk)) if mode == "nt" else pl.BlockSpec((tk, tn), lambda i, j, kk: (kk, j))
    o_spec = pl.BlockSpec((tm, tn), lambda i, j, kk: (i, j))
    dims = {"nn": (((1,), (0,)), ((), ())), "nt": (((1,), (1,)), ((), ())), "tn": (((0,), (0,)), ((), ()))}[mode]
    has_add = add is not None
    side_arrays, gather = side if side is not None else ((), False)
    ns = len(side_arrays)
    n_in = 2 + has_add

    def body(*refs):
        a_ref, b_ref = refs[0], refs[1]
        side_in, o_ref, side_out = refs[n_in:n_in + ns], refs[n_in + ns], refs[n_in + ns + 1:n_in + 2 * ns + 1]
        scratch = refs[n_in + 2 * ns + 1:]
        ids = [pl.program_id(d) for d in range(3)]
        if ns:
            start, wait = _exchange_copies(side_in, side_out, *scratch[-3:], gather)
            pl.when((ids[0] == 0) & (ids[1] == 0) & (ids[2] == 0))(start)
        part = lax.dot_general(a_ref[...].astype(BF16), b_ref[...].astype(BF16), dims, preferred_element_type=F32)
        if nk == 1:
            o_ref[...] = (part + refs[2][...].astype(F32) if has_add else part).astype(o_ref.dtype)
        else:
            acc_ref = scratch[0]

            @pl.when(ids[2] == 0)
            def _():
                acc_ref[...] = part + refs[2][...].astype(F32) if has_add else part

            @pl.when(ids[2] > 0)
            def _():
                acc_ref[...] += part

            @pl.when(ids[2] == nk - 1)
            def _():
                o_ref[...] = acc_ref[...].astype(o_ref.dtype)
        if ns:
            pl.when((ids[0] == grid[0] - 1) & (ids[1] == grid[1] - 1) & (ids[2] == nk - 1))(wait)

    any_spec = pl.BlockSpec(memory_space=pl.ANY)
    res = pl.pallas_call(
        body, name=name, grid=grid,
        in_specs=[a_spec, b_spec] + ([o_spec] if has_add else []) + [any_spec] * ns,
        out_specs=[o_spec] + [any_spec] * ns,
        out_shape=[jax.ShapeDtypeStruct((m, n), out_dtype)] + _exchange_shapes(side_arrays, gather),
        scratch_shapes=([pltpu.VMEM((tm, tn), F32)] if nk > 1 else []) + (_exchange_sems(ns) if ns else []),
        compiler_params=pltpu.CompilerParams(
            dimension_semantics=("arbitrary",) * 3 if ns else ("parallel", "parallel", "arbitrary"),
            vmem_limit_bytes=VMEM_LIMIT),
    )(a, b, *([add] if has_add else []), *side_arrays)
    return (res[0], res[1:]) if ns else res[0]


def matmul_fused(name, a, bs, mode, n, extras, epilogue, out_dtypes):
    m, k = a.shape
    tm, tn = _pick(m, 1024), _pick_n(n)
    dims = {"nn": (((1,), (0,)), ((), ())), "nt": (((1,), (1,)), ((), ()))}[mode]
    nb, nx = len(bs), len(extras)

    def b_spec(off):
        if mode == "nt":
            return pl.BlockSpec((tn, k), lambda i, j: (j + off, 0))
        return pl.BlockSpec((k, tn), lambda i, j: (0, j + off))

    def body(*refs):
        a_val = refs[0][...].astype(BF16)
        parts = tuple(lax.dot_general(a_val, r[...].astype(BF16), dims, preferred_element_type=F32)
                      for r in refs[1:1 + nb])
        tiles = tuple(r[...].astype(F32) for r in refs[1 + nb:1 + nb + nx])
        for r, v in zip(refs[1 + nb + nx:], epilogue(parts, tiles)):
            r[...] = v.astype(r.dtype)

    tile = pl.BlockSpec((tm, tn), lambda i, j: (i, j))
    return pl.pallas_call(
        body, name=name, grid=(m // tm, n // tn),
        in_specs=[pl.BlockSpec((tm, k), lambda i, j: (i, 0))] + [b_spec(off) for _, off in bs] + [tile] * nx,
        out_specs=[tile] * len(out_dtypes), out_shape=[jax.ShapeDtypeStruct((m, n), dt) for dt in out_dtypes],
        compiler_params=pltpu.CompilerParams(dimension_semantics=("parallel", "parallel"),
                                             vmem_limit_bytes=VMEM_LIMIT),
    )(a, *[b for b, _ in bs], *extras)


def _glu_fwd_tiles(parts, tiles):
    gate, up = parts
    return gate, up, jax.nn.silu(gate) * up


def _glu_bwd_tiles(parts, tiles):
    (da,), (gate, up) = parts, tiles
    _, vjp = jax.vjp(lambda g, u: jax.nn.silu(g) * u, gate, up)
    return vjp(da)


def adamw(name, w, m, v, gparts, side=None):
    layers, rows, cols = w.shape
    parts = gparts[0].shape[0]
    tr = 8
    while tr * 2 * cols <= 65536 and rows % (tr * 2) == 0:
        tr *= 2
    nblk = rows // tr
    c1 = 1.0 - ADAM_B1 ** ADAM_STEP
    c2 = 1.0 - ADAM_B2 ** ADAM_STEP
    side_arrays, gather, nx = _side(side)

    def body(*refs):
        w_ref, m_ref, v_ref = refs[:3]
        g_refs = refs[3:3 + layers]
        side_in = refs[3 + layers:3 + layers + nx]
        go_ref, d_ref, mo_ref, vo_ref = refs[3 + layers + nx:7 + layers + nx]
        side_out, sems = refs[7 + layers + nx:7 + layers + 2 * nx], refs[7 + layers + 2 * nx:]
        layer = pl.program_id(0)
        if nx:
            start, wait = _exchange_copies(side_in, side_out, *sems, gather)
            pl.when((layer == 0) & (pl.program_id(1) == 0))(start)
        g = None
        for ll, g_ref in enumerate(g_refs):
            s = g_ref[0].astype(F32)
            for p in range(1, parts):
                s = s + g_ref[p].astype(F32)
            g = s if g is None else jnp.where(layer == ll, s, g)
        m_new = ADAM_B1 * m_ref[...] + (1.0 - ADAM_B1) * g
        v_new = ADAM_B2 * v_ref[...] + (1.0 - ADAM_B2) * (g * g)
        m_hat = m_new / c1
        v_hat = v_new / c2
        go_ref[...] = g
        d_ref[...] = -ADAM_LR * (m_hat / (jnp.sqrt(v_hat) + ADAM_EPS) + ADAM_WD * w_ref[...])
        mo_ref[...] = m_new
        vo_ref[...] = v_new
        if nx:
            pl.when((layer == layers - 1) & (pl.program_id(1) == nblk - 1))(wait)

    def part_spec(ll):
        return pl.BlockSpec((parts, tr, cols),
                            lambda l, i: (0, jnp.where(l == ll, i, jnp.where(l < ll, 0, nblk - 1)), 0))

    spec = pl.BlockSpec((None, tr, cols), lambda l, i: (l, i, 0))
    res = pl.pallas_call(
        body, name=name, grid=(layers, nblk),
        in_specs=[spec, spec, spec] + [part_spec(ll) for ll in range(layers)] + [_ANY] * nx,
        out_specs=[spec] * 4 + [_ANY] * nx,
        out_shape=[jax.ShapeDtypeStruct(w.shape, F32)] * 4 + _exchange_shapes(side_arrays, gather),
        scratch_shapes=_exchange_sems(nx) if nx else [],
        compiler_params=pltpu.CompilerParams(dimension_semantics=("arbitrary", "arbitrary"),
                                             vmem_limit_bytes=VMEM_LIMIT),
    )(w, m, v, *gparts, *side_arrays)
    return (res[:4], res[4:]) if nx else res


def _adamw_math(w, m, v, g):
    m_new = ADAM_B1 * m + (1.0 - ADAM_B1) * g
    v_new = ADAM_B2 * v + (1.0 - ADAM_B2) * (g * g)
    m_hat = m_new / (1.0 - ADAM_B1 ** ADAM_STEP)
    v_hat = v_new / (1.0 - ADAM_B2 ** ADAM_STEP)
    return -ADAM_LR * (m_hat / (jnp.sqrt(v_hat) + ADAM_EPS) + ADAM_WD * w), m_new, v_new


def adamw_many(name, ws, ms, vs, garrays, where):
    n, ng = len(ws), len(garrays)

    def body(*refs):
        ins, g_refs, outs = refs[:3 * n], refs[3 * n:3 * n + ng], refs[3 * n + ng:]
        for i, sources in enumerate(where):
            for a, j, layer in sources:
                g_ref = g_refs[a]
                g = g_ref[0, j].astype(F32)
                for p in range(1, g_ref.shape[0]):
                    g = g + g_ref[p, j].astype(F32)
                sl = slice(None) if layer is None else slice(layer, layer + 1)
                delta, m_new, v_new = _adamw_math(ins[i][sl], ins[n + i][sl], ins[2 * n + i][sl], g)
                for r, val in zip(outs[4 * i:4 * i + 4], (g, delta, m_new, v_new)):
                    r[sl] = val

    vmem = pl.BlockSpec(memory_space=pltpu.VMEM)
    res = pl.pallas_call(
        body, name=name, in_specs=[vmem] * (3 * n + ng), out_specs=[vmem] * (4 * n),
        out_shape=[jax.ShapeDtypeStruct(w.shape, F32) for w in ws for _ in range(4)],
        compiler_params=pltpu.CompilerParams(vmem_limit_bytes=VMEM_LIMIT),
    )(*ws, *ms, *vs, *garrays)
    return [res[4 * i:4 * i + 4] for i in range(n)]


def _pack(arrays, dtype, row_multiple):
    flat = jnp.concatenate([a.astype(dtype).reshape(-1) for a in arrays])
    unit = row_multiple * LANES
    pad = (-flat.shape[0]) % unit
    if pad:
        flat = jnp.concatenate([flat, jnp.zeros((pad,), dtype)])
    return flat.reshape(-1, LANES)


def _unpack(flat2d, shapes):
    flat = flat2d.reshape(-1)
    out, off = [], 0
    for s in shapes:
        n = int(np.prod(s))
        out.append(flat[off:off + n].reshape(s))
        off += n
    return out


def _unpack_stacked(stacked, shapes):
    flat = stacked.reshape(N_DEV, -1)
    out, off = [], 0
    for s in shapes:
        n = int(np.prod(s))
        out.append(flat[:, off:off + n].reshape((N_DEV,) + tuple(s)))
        off += n
    return out


def _merge_shards(stacked, axis):
    t = jnp.moveaxis(stacked, 0, axis)
    s = t.shape
    return t.reshape(s[:axis] + (s[axis] * s[axis + 1],) + s[axis + 2:])


def _split_shards(full, axis, n=N_DEV):
    s = full.shape
    t = full.reshape(s[:axis] + (n, s[axis] // n) + s[axis + 1:])
    return jnp.moveaxis(t, axis, 0)


def _lower_bounds(hg_lower_bounds):
    p = jax.nn.softmax(hg_lower_bounds, axis=0)
    return jnp.cumsum(p, axis=0) - p[0:1]


def _prep_layer(p):
    def row(v):
        return v.reshape(1, -1)

    eye_b = jnp.eye(HEADS, dtype=F32)
    eye_g = jnp.eye(S5_GROUPS, dtype=F32)
    step = jnp.exp(p["s5_log_dt"])[:, None]
    lam_re, lam_im = p["s5_lam_re"], p["s5_lam_im"]
    mag = jnp.exp(lam_re * step)
    lb_re = mag * jnp.cos(lam_im * step)
    lb_im = mag * jnp.sin(lam_im * step)
    den = lam_re * lam_re + lam_im * lam_im
    f_re = ((lb_re - 1.0) * lam_re + lb_im * lam_im) / den
    f_im = (lb_im * lam_re - (lb_re - 1.0) * lam_im) / den
    bb_re = f_re[..., None] * p["s5_b_re"] - f_im[..., None] * p["s5_b_im"]
    bb_im = f_re[..., None] * p["s5_b_im"] + f_im[..., None] * p["s5_b_re"]
    width = S5_GROUPS * S5_GROUP
    n_state = S5_GROUPS * S5_STATE
    return dict(
        lb=row(p["lb"]), hg_norm=row(p["hg_norm"]), ret_norm=row(p["ret_norm"]),
        conv_w=p["lru_conv_w"], conv_b=row(p["lru_conv_b"]),
        wa=jnp.einsum("nij,nm->nimj", p["lru_wa"], eye_b).reshape(MIX_W, MIX_W).astype(BF16), ba=row(p["lru_ba"]),
        wx=jnp.einsum("nij,nm->nimj", p["lru_wx"], eye_b).reshape(MIX_W, MIX_W).astype(BF16), bx=row(p["lru_bx"]),
        sp=row(jax.nn.softplus(-p["lru_lambda"])),
        bt_re=jnp.einsum("gnp,gh->gphn", bb_re, eye_g).reshape(width, n_state).astype(BF16),
        bt_im=jnp.einsum("gnp,gh->gphn", bb_im, eye_g).reshape(width, n_state).astype(BF16),
        lb_re=row(lb_re), lb_im=row(lb_im),
        ct_re=jnp.einsum("gpn,gh->gnhp", p["s5_c_re"], eye_g).reshape(n_state, width).astype(BF16),
        ct_im=jnp.einsum("gpn,gh->gnhp", p["s5_c_im"], eye_g).reshape(n_state, width).astype(BF16),
        s5_d=row(p["s5_d"]), glu_w=p["s5_glu_w"].astype(BF16), glu_b=row(p["s5_glu_b"]),
        b_gate=row(p["b_gate"]),
        norm_mix_pre=row(p["norm_mix_pre"]), norm_mix_post=row(p["norm_mix_post"]),
        norm_xa_pre=row(p["norm_xa_pre"]), norm_xa_post=row(p["norm_xa_post"]), norm_mem=row(p["norm_mem"]),
        norm_ffn_pre=row(p["norm_ffn_pre"]), norm_ffn_post=row(p["norm_ffn_post"]),
    )


_PREP_INPUTS = ("hg_norm", "ret_norm", "lru_conv_w", "lru_conv_b", "lru_wa", "lru_ba", "lru_wx", "lru_bx", "lru_lambda",
                "s5_lam_re", "s5_lam_im", "s5_b_re", "s5_b_im", "s5_c_re", "s5_c_im", "s5_d", "s5_log_dt", "s5_glu_w",
                "s5_glu_b", "b_gate", "norm_mix_pre", "norm_mix_post", "norm_xa_pre", "norm_xa_post", "norm_mem",
                "norm_ffn_pre", "norm_ffn_post")


def _retention_constants():
    lg = np.log1p(-np.power(2.0, -5.0 - np.arange(HEADS)))
    idx = np.arange(RET_CHUNK)

    def lanes(per_head_rows):
        return np.repeat(per_head_rows.T[:, :, None], HEAD_DIM, axis=2).reshape(RET_CHUNK, MIX_W)

    xi = lanes(np.exp((idx + 1.0)[None, :] * lg[:, None]))
    zeta = lanes(np.exp((RET_CHUNK - 1.0 - idx)[None, :] * lg[:, None]))
    rel = idx[:, None] - idx[None, :]
    decay = np.where(rel[None] >= 0, np.exp(np.maximum(rel, 0)[None] * lg[:, None, None]), 0.0)
    decay = np.transpose(decay, (1, 0, 2)).reshape(RET_CHUNK, HEADS * RET_CHUNK)
    g_end = np.repeat(np.exp(RET_CHUNK * lg), HEAD_DIM)[None, :]
    return tuple(jnp.asarray(a, F32) for a in (xi, zeta, decay, g_end))


def _rotary_tables(seq):
    pos = jnp.arange(seq, dtype=F32)
    inv_freq = 10000.0 ** (-jnp.arange(0, HEAD_DIM, 2, dtype=F32) / HEAD_DIM)
    ang = pos[:, None] * inv_freq[None, :]
    cos, sin = jnp.cos(ang), jnp.sin(ang)
    cos_t = jnp.tile(jnp.repeat(cos, 2, axis=1), (1, HEADS))
    sin_t = jnp.tile(jnp.stack([-sin, sin], axis=-1).reshape(seq, HEAD_DIM), (1, HEADS))
    return cos_t, sin_t


TB_HG = 512
TB_RET = 512
TB_LRU = 256
TB_S5 = 256
TB_ROW = 256
TB_XA = 512

_STATE = (MIX_W, MIX_W)
_TAIL = (8, MIX_W)
_S5_TAIL = (8, S5_GROUPS * S5_STATE)


def _mixer_operands(l, z, kp, rot, ret_c):
    xi, zeta, decay, g_end = ret_c
    return dict(
        hg=dict(name=f"hgrn2_{l}", fn=fn_hgrn2, rows=[(z, 4 * MIX_W, 0)], params=[kp["lb"], kp["hg_norm"]],
                tb=TB_HG, states=(_STATE,), diff_params=(0, 1)),
        ret=dict(name=f"retention_{l}", fn=fn_retention, rows=[(z, 4 * MIX_W, 1), (rot[0], MIX_W, 0), (rot[1], MIX_W, 0)],
                 params=[kp["ret_norm"], xi, zeta, decay, g_end], tb=TB_RET, states=(_STATE,), diff_params=(0,)),
        lru=dict(name=f"rglru_{l}", fn=fn_rglru, rows=[(z, 2 * MIX_W, 4)],
                 params=[kp["conv_w"], kp["conv_b"], kp["wa"], kp["ba"], kp["wx"], kp["bx"], kp["sp"]],
                 tb=TB_LRU, states=(_TAIL, _TAIL), diff_params=(0, 1, 2, 3, 4, 5, 6)),
        s5=dict(name=f"s5_{l}", fn=fn_s5, rows=[(z, MIX_W, 10)],
                params=[kp["bt_re"], kp["bt_im"], kp["lb_re"], kp["lb_im"], kp["ct_re"], kp["ct_im"], kp["s5_d"],
                        kp["glu_w"], kp["glu_b"]],
                tb=TB_S5, states=(_S5_TAIL, _S5_TAIL), diff_params=tuple(range(9))),
    )


def _layer_forward(l, x, h, mem, w_in, shards, next_w_in_shard, kp, rot, ret_c, g_next, target):
    d = x.shape[1]
    sv = dict(x=x, h=h)
    bw = {"w_in": w_in}

    def gather(idx):
        return [shards[i] for i in idx], True

    def take(idx, stacked):
        for i, s in zip(idx, stacked):
            bw[_BIG_NAMES[i]] = _merge_shards(s, BIG[i][1] - 1)

    z, got = matmul(f"in_proj_{l}", h, w_in, "nn", F32, side=gather(GATHER_IN["in_proj"]))
    take(GATHER_IN["in_proj"], got)
    gl, got = matmul(f"gate_proj_{l}", h, bw["w_gate"], "nn", BF16, side=gather(GATHER_IN["gate_proj"]))
    take(GATHER_IN["gate_proj"], got)
    sv.update(z=z, gl=gl)
    ops = _mixer_operands(l, z, kp, rot, ret_c)
    ys = []
    for key in ("hg", "ret", "lru", "s5"):
        o = ops[key]
        idx = GATHER_IN.get(key)
        res = stage_fwd(o["name"] + "_fwd", o["fn"], o["rows"], o["params"], tb=o["tb"], outs=[(MIX_W, F32)],
                        states=o["states"], side=gather(idx) if idx else None)
        if idx:
            take(idx, res[3])
        ys.append(res[0][0])
        sv[key + "_states"] = res[2]
    sv["ys"] = ys
    merge_params = [bw["w_up"][n] for n in range(4)] + [kp["b_gate"], bw["w_out"], kp["norm_mix_post"]]
    merge_rows = [(y, MIX_W, 0) for y in ys] + [(gl, 4 * d, 0), (x, d, 0)]
    (x1,), _, _, got = stage_fwd(f"merge_{l}_fwd", fn_merge, merge_rows, merge_params, tb=TB_ROW, outs=[(d, F32)],
                                 side=gather(GATHER_IN["merge"]))
    take(GATHER_IN["merge"], got)
    wk, wv = bw["xa_w_kv"][:, :d], bw["xa_w_kv"][:, d:]
    mem_params = [kp["norm_mem"], wk, wv]
    (k, v), _, _ = stage_fwd(f"mem_{l}_fwd", fn_mem, [(mem, d, 0)], mem_params, tb=mem.shape[0],
                             outs=[(d, BF16), (d, BF16)])
    xa_params = [kp["norm_xa_pre"], bw["xa_w_q"], k, v, bw["xa_w_o"], kp["norm_xa_post"], kp["norm_ffn_pre"]]
    res = stage_fwd(f"xattn_{l}_fwd", fn_xattn, [(x1, d, 0)], xa_params, tb=TB_XA, outs=[(d, F32), (d, BF16)],
                    side=([next_w_in_shard], True) if next_w_in_shard is not None else None)
    x2, h3 = res[0]
    next_w_in = _merge_shards(res[3][0], BIG[0][1] - 1) if next_w_in_shard is not None else None
    f = bw["ffn_w_gu"].shape[1] // 2
    up_block = f // _pick_n(f)
    gate, up, a = matmul_fused(f"ffn_gu_{l}", h3, [(bw["ffn_w_gu"], 0), (bw["ffn_w_gu"], up_block)], "nn", f, [],
                               _glu_fwd_tiles, (BF16, BF16, BF16))
    o3 = matmul(f"ffn_down_{l}", a, bw["ffn_w_down"], "nn", F32)
    sv.update(x1=x1, k=k, v=v, x2=x2, h3=h3, gate=gate, up=up, a=a, o3=o3, merge_params=merge_params,
              merge_rows=merge_rows, mem_params=mem_params, xa_params=xa_params)
    if g_next is not None:
        (x3, hn), _, _ = stage_fwd(f"res_{l}_fwd", fn_res_norm, [(x2, d, 0), (o3, d, 0)], [kp["norm_ffn_post"], g_next],
                                   tb=TB_ROW, outs=[(d, F32), (d, BF16)])
        return x3, hn, sv, bw, next_w_in
    (dx2, do3), (loss, dg_post), _ = stage_fwd(f"loss_{l}", fn_loss_head, [(x2, d, 0), (o3, d, 0), (target, d, 0)],
                                               [kp["norm_ffn_post"]], tb=TB_ROW, outs=[(d, F32), (d, F32)],
                                               accs=[(8, LANES), (1, d)])
    sv["head"] = (dx2, do3, dg_post)
    return None, loss[0, 0], sv, bw, next_w_in


def _layer_backward(l, sv, mem, bw, kp, rot, ret_c, g_next, dx3, dhn, s5_side=None):
    d = sv["x"].shape[1]
    gk = {}
    parts = [None] * len(BIG)

    def send(i, g):
        axis = BIG[i][1] - 1
        if isinstance(g, tuple):
            return jnp.concatenate([_split_shards(p.astype(BF16), axis, N_DEV // len(g)) for p in g], axis=0)
        return _split_shards(g.astype(BF16), axis)

    def scatter(*items):
        return [send(i, g) for i, g in items], False

    def took(got, *idx):
        for i, p in zip(idx, got):
            parts[i] = p

    res_rows = [(sv["x2"], d, 0), (sv["o3"], d, 0)]
    if g_next is not None:
        (dx2, do3), (gk["norm_ffn_post"], gk["g_next"]) = stage_bwd(
            f"res_{l}_bwd", fn_res_norm, res_rows, [kp["norm_ffn_post"], g_next], [dx3, dhn], tb=TB_ROW,
            diff_rows=(0, 1), diff_params=(0, 1))
    else:
        dx2, do3, gk["norm_ffn_post"] = sv["head"]
    f = sv["a"].shape[1]
    g_down = matmul(f"ffn_down_dw_{l}", sv["a"], do3, "tn", BF16)
    d_gate, d_up = matmul_fused(f"ffn_down_dx_{l}", do3, [(bw["ffn_w_down"], 0)], "nt", f, [sv["gate"], sv["up"]],
                                _glu_bwd_tiles, (BF16, BF16))
    dh3, got = matmul(f"ffn_gate_dx_{l}", d_gate, bw["ffn_w_gu"][:, :f], "nt", F32, side=scatter((8, g_down)))
    took(got, 8)
    dh3 = matmul(f"ffn_up_dx_{l}", d_up, bw["ffn_w_gu"][:, f:], "nt", F32, add=dh3)
    g_gu = (matmul(f"ffn_gate_dw_{l}", sv["h3"], d_gate, "tn", BF16),
            matmul(f"ffn_up_dw_{l}", sv["h3"], d_up, "tn", BF16))
    (dx1,), xa_g, got = stage_bwd(f"xattn_{l}_bwd", fn_xattn, [(sv["x1"], d, 0)], sv["xa_params"], [dx2, dh3],
                                  tb=TB_XA, diff_rows=(0,), diff_params=tuple(range(7)), side=scatter((7, g_gu)))
    took(got, 7)
    gk["norm_xa_pre"], g_q, dk, dv, g_o, gk["norm_xa_post"], gk["norm_ffn_pre"] = xa_g
    _, (gk["norm_mem"], dwk, dwv) = stage_bwd(f"mem_{l}_bwd", fn_mem, [(mem, d, 0)], sv["mem_params"], [dk, dv],
                                              tb=mem.shape[0], diff_params=(0, 1, 2))
    g_kv = jnp.concatenate([dwk, dwv], axis=1)
    merge_d, merge_g, got = stage_bwd(f"merge_{l}_bwd", fn_merge, sv["merge_rows"], sv["merge_params"], [dx1],
                                      tb=TB_ROW, diff_rows=tuple(range(6)), diff_params=tuple(range(7)),
                                      row_dtypes=(F32, F32, F32, F32, BF16, F32),
                                      side=scatter((4, g_q), (6, g_o), (5, g_kv)))
    took(got, 4, 6, 5)
    dys, dgl, dx0 = merge_d[:4], merge_d[4], merge_d[5]
    g_up = jnp.stack(merge_g[:4])
    gk["b_gate"], g_out, gk["norm_mix_post"] = merge_g[4:]
    ops = _mixer_operands(l, sv["z"], kp, rot, ret_c)
    dz = None
    for key, dy in zip(("hg", "ret", "lru", "s5"), dys):
        o = ops[key]
        res = stage_bwd(o["name"] + "_bwd", o["fn"], o["rows"], o["params"], [dy], tb=o["tb"],
                        saved=sv[key + "_states"], diff_rows=(0,), diff_params=o["diff_params"], row_dtypes=(BF16,),
                        side=scatter((2, g_up), (3, g_out)) if key == "hg" else s5_side if key == "s5" else None,
                        into=(dz, sv["z"].shape[1]))
        if key == "hg":
            took(res[2], 2, 3)
        elif key == "s5" and s5_side is not None:
            gk["s5_side"] = res[2]
        dz = res[0][0]
        gk[key] = res[1]
    g_gate = matmul(f"gate_proj_dw_{l}", sv["h"], dgl, "tn", BF16)
    g_in = matmul(f"in_proj_dw_{l}", sv["h"], dz, "tn", BF16)
    dh, got = matmul(f"gate_proj_dx_{l}", dgl, bw["w_gate"], "nt", F32, side=scatter((1, g_gate)))
    took(got, 1)
    dh, got = matmul(f"in_proj_dx_{l}", dz, bw["w_in"], "nt", F32, add=dh, side=scatter((0, g_in)))
    took(got, 0)
    return dx0, dh, gk, parts


def _kernel_grads_to_prep(gk):
    hg, ret, lru, s5 = gk["hg"], gk["ret"], gk["lru"], gk["s5"]
    return dict(
        lb=hg[0], hg_norm=hg[1], ret_norm=ret[0],
        conv_w=lru[0], conv_b=lru[1], wa=lru[2], ba=lru[3], wx=lru[4], bx=lru[5], sp=lru[6],
        bt_re=s5[0], bt_im=s5[1], lb_re=s5[2], lb_im=s5[3], ct_re=s5[4], ct_im=s5[5], s5_d=s5[6], glu_w=s5[7],
        glu_b=s5[8], b_gate=gk["b_gate"], norm_mix_pre=gk["norm_mix_pre"], norm_mix_post=gk["norm_mix_post"],
        norm_xa_pre=gk["norm_xa_pre"], norm_xa_post=gk["norm_xa_post"], norm_mem=gk["norm_mem"],
        norm_ffn_pre=gk["norm_ffn_pre"], norm_ffn_post=gk["norm_ffn_post"],
    )


def _step(inp):
    x, mem, target = inp["x"][0], inp["mem"][0], inp["loss_target"][0]
    seq = x.shape[0]
    depth = inp["w_in"].shape[0]
    me = 4 * lax.axis_index("x") + 2 * lax.axis_index("y") + lax.axis_index("c")

    small_shapes = [inp[n].shape for n in _SMALL_SHARDED_NAMES]
    (small_stacked,) = _exchange("gather_small", [_pack([inp[n] for n in _SMALL_SHARDED_NAMES], F32, 8)], True)
    small_all = _unpack_stacked(small_stacked, small_shapes)
    full_small = {n: _merge_shards(s, ax) for (n, ax), s in zip(SMALL_SHARDED, small_all)}

    lbs, lbs_vjp = jax.vjp(_lower_bounds, inp["hg_lower_bounds"])
    kps, prep_vjps = [], []
    for l in range(depth):
        p = {n: (full_small[n][l] if n in full_small else inp[n][l]) for n in _PREP_INPUTS}
        p["lb"] = lbs[l]
        kp, vj = jax.vjp(_prep_layer, p)
        kps.append(kp)
        prep_vjps.append(vj)
    rot = _rotary_tables(seq)
    ret_c = _retention_constants()

    def shards(l):
        return [inp[n][l].astype(BF16) for n in _BIG_NAMES]

    (h,), _, _, (stacked,) = stage_fwd("norm_in_fwd", fn_norm, [(x, x.shape[1], 0)], [kps[0]["norm_mix_pre"]],
                                       tb=TB_ROW, outs=[(x.shape[1], BF16)], side=([shards(0)[0]], True))
    saved, bws = [], []
    xs = x
    w_in = _merge_shards(stacked, BIG[0][1] - 1)
    for l in range(depth):
        last = l + 1 == depth
        g_next = None if last else kps[l + 1]["norm_mix_pre"]
        xs, h, sv, bw, w_in = _layer_forward(l, xs, h, mem, w_in, shards(l), None if last else shards(l + 1)[0],
                                             kps[l], rot, ret_c, g_next, target)
        saved.append(sv)
        bws.append(bw)
    dy, loss_local = xs, h

    big_parts = [None] * depth
    gks = [None] * depth
    dx, dh = dy, None
    small_names = REPLICATED + _SMALL_SHARDED_NAMES
    cross_layer = ("hg_lower_bounds", "norm_mix_pre")
    groups = {}
    for n in small_names:
        if n not in cross_layer:
            groups.setdefault(full_small[n].shape[1:] if n in full_small else inp[n].shape[1:], []).append(n)

    def layer_small_grads(l):
        gk = dict(gks[l], norm_mix_pre=jnp.zeros_like(kps[l]["norm_mix_pre"]))
        (gp,) = prep_vjps[l]({k: g.astype(kps[l][k].dtype) for k, g in _kernel_grads_to_prep(gk).items()})
        return gp["lb"], [jnp.stack([gp[n].astype(BF16)[None] for n in names]) for names in groups.values()]

    d_lbs = [None] * depth
    gathered = [None] * depth
    pending = None
    for l in reversed(range(depth)):
        g_next = kps[l + 1]["norm_mix_pre"] if l + 1 < depth else None
        dx, dh, gks[l], big_parts[l] = _layer_backward(l, saved[l], mem, bws[l], kps[l], rot, ret_c, g_next, dx, dh,
                                                       s5_side=(pending, True) if pending is not None else None)
        if pending is not None:
            gathered[l + 1] = list(gks[l]["s5_side"])
        d_lbs[l], pending = layer_small_grads(l)
    (g_lower_bounds,) = lbs_vjp(jnp.stack(d_lbs))
    (grad_x,), (g_pre0,), first_parts = stage_bwd(
        "norm_in_bwd", fn_keep_norm, [(x, x.shape[1], 0)], [kps[0]["norm_mix_pre"]], [dx, dh], tb=TB_ROW,
        diff_rows=(0,), diff_params=(0,), side=(pending + [g_lower_bounds.astype(BF16)[None]], True))
    g_mix_pre = jnp.concatenate([g_pre0] + [gks[l]["g_next"] for l in range(depth - 1)], axis=0)
    small_send = [g_mix_pre.astype(BF16)[None]]

    out = {}
    kinds = ("grad_", "delta_", "new_m_", "new_v_")
    small_parts = None
    for i, n in sorted(enumerate(_BIG_NAMES), key=lambda t: -int(np.prod(inp[t[1]].shape))):
        shape = inp[n].shape
        three = (shape[0], int(np.prod(shape[1:-1])), shape[-1])
        res = adamw("adamw_" + n, *[inp[pre + n].reshape(three) for pre in ("", "m_", "v_")],
                    [big_parts[l][i].reshape((N_DEV,) + three[1:]) for l in range(depth)],
                    side=(small_send, True) if small_parts is None else None)
        if small_parts is None:
            res, small_parts = res
        for kind, a in zip(kinds, res):
            out[kind + n] = a.reshape(shape)
    ng = len(groups)
    gathered[0] = list(first_parts[:ng])
    place = {n: (gi, j) for gi, names in enumerate(groups.values()) for j, n in enumerate(names)}
    for n, ax in SMALL_SHARDED:
        gi, j = place[n]
        assert len(list(groups.values())[gi]) == 1
        width = inp[n].shape[ax]
        for l in range(depth):
            gathered[l][gi] = lax.dynamic_slice_in_dim(gathered[l][gi], me * width, width, axis=ax + 2)
    garrays = [a for l in range(depth) for a in gathered[l]] + [first_parts[ng], small_parts[0]]
    where = [[(depth * ng + cross_layer.index(n), 0, None)] if n in cross_layer else
             [(l * ng + place[n][0], place[n][1], l) for l in range(depth)] for n in small_names]
    res = adamw_many("adamw_small", *[[inp[pre + n] for n in small_names] for pre in ("", "m_", "v_")], garrays, where)
    for n, quad in zip(small_names, res):
        for kind, a in zip(kinds, quad):
            out[kind + n] = a

    out["loss"] = lax.psum(loss_local, ("x", "y", "c"))
    out["grad_x"] = grad_x[None]
    return out


def kernel(x, mem, hg_lower_bounds, norm_mix_pre, norm_mix_post, w_in, w_gate, b_gate, hg_norm, ret_norm, lru_conv_w, lru_conv_b, lru_wa, lru_ba, lru_wx, lru_bx, lru_lambda, s5_lam_re, s5_lam_im, s5_b_re, s5_b_im, s5_c_re, s5_c_im, s5_d, s5_log_dt, s5_glu_w, s5_glu_b, w_up, w_out, norm_xa_pre, norm_xa_post, norm_mem, xa_w_q, xa_w_kv, xa_w_o, norm_ffn_pre, norm_ffn_post, ffn_w_gu, ffn_w_down, loss_target, m_hg_lower_bounds, m_norm_mix_pre, m_norm_mix_post, m_w_in, m_w_gate, m_b_gate, m_hg_norm, m_ret_norm, m_lru_conv_w, m_lru_conv_b, m_lru_wa, m_lru_ba, m_lru_wx, m_lru_bx, m_lru_lambda, m_s5_lam_re, m_s5_lam_im, m_s5_b_re, m_s5_b_im, m_s5_c_re, m_s5_c_im, m_s5_d, m_s5_log_dt, m_s5_glu_w, m_s5_glu_b, m_w_up, m_w_out, m_norm_xa_pre, m_norm_xa_post, m_norm_mem, m_xa_w_q, m_xa_w_kv, m_xa_w_o, m_norm_ffn_pre, m_norm_ffn_post, m_ffn_w_gu, m_ffn_w_down, v_hg_lower_bounds, v_norm_mix_pre, v_norm_mix_post, v_w_in, v_w_gate, v_b_gate, v_hg_norm, v_ret_norm, v_lru_conv_w, v_lru_conv_b, v_lru_wa, v_lru_ba, v_lru_wx, v_lru_bx, v_lru_lambda, v_s5_lam_re, v_s5_lam_im, v_s5_b_re, v_s5_b_im, v_s5_c_re, v_s5_c_im, v_s5_d, v_s5_log_dt, v_s5_glu_w, v_s5_glu_b, v_w_up, v_w_out, v_norm_xa_pre, v_norm_xa_post, v_norm_mem, v_xa_w_q, v_xa_w_kv, v_xa_w_o, v_norm_ffn_pre, v_norm_ffn_post, v_ffn_w_gu, v_ffn_w_down):
    values = (x, mem, hg_lower_bounds, norm_mix_pre, norm_mix_post, w_in, w_gate, b_gate, hg_norm, ret_norm, lru_conv_w, lru_conv_b, lru_wa, lru_ba, lru_wx, lru_bx, lru_lambda, s5_lam_re, s5_lam_im, s5_b_re, s5_b_im, s5_c_re, s5_c_im, s5_d, s5_log_dt, s5_glu_w, s5_glu_b, w_up, w_out, norm_xa_pre, norm_xa_post, norm_mem, xa_w_q, xa_w_kv, xa_w_o, norm_ffn_pre, norm_ffn_post, ffn_w_gu, ffn_w_down, loss_target, m_hg_lower_bounds, m_norm_mix_pre, m_norm_mix_post, m_w_in, m_w_gate, m_b_gate, m_hg_norm, m_ret_norm, m_lru_conv_w, m_lru_conv_b, m_lru_wa, m_lru_ba, m_lru_wx, m_lru_bx, m_lru_lambda, m_s5_lam_re, m_s5_lam_im, m_s5_b_re, m_s5_b_im, m_s5_c_re, m_s5_c_im, m_s5_d, m_s5_log_dt, m_s5_glu_w, m_s5_glu_b, m_w_up, m_w_out, m_norm_xa_pre, m_norm_xa_post, m_norm_mem, m_xa_w_q, m_xa_w_kv, m_xa_w_o, m_norm_ffn_pre, m_norm_ffn_post, m_ffn_w_gu, m_ffn_w_down, v_hg_lower_bounds, v_norm_mix_pre, v_norm_mix_post, v_w_in, v_w_gate, v_b_gate, v_hg_norm, v_ret_norm, v_lru_conv_w, v_lru_conv_b, v_lru_wa, v_lru_ba, v_lru_wx, v_lru_bx, v_lru_lambda, v_s5_lam_re, v_s5_lam_im, v_s5_b_re, v_s5_b_im, v_s5_c_re, v_s5_c_im, v_s5_d, v_s5_log_dt, v_s5_glu_w, v_s5_glu_b, v_w_up, v_w_out, v_norm_xa_pre, v_norm_xa_post, v_norm_mem, v_xa_w_q, v_xa_w_kv, v_xa_w_o, v_norm_ffn_pre, v_norm_ffn_post, v_ffn_w_gu, v_ffn_w_down)
    names = ("x", "mem") + WEIGHTS + ("loss_target",) + tuple("m_" + n for n in WEIGHTS) + tuple("v_" + n for n in WEIGHTS)
    out = _step(dict(zip(names, values)))
    order = ["loss", "grad_x"] + [k + n for k in ("grad_", "delta_", "new_m_", "new_v_") for n in WEIGHTS]
    return tuple(out[k] for k in order)
```

```python
import functools

import numpy as np
import jax
import jax.numpy as jnp
from jax import lax
from jax.experimental import pallas as pl
from jax.experimental.pallas import tpu as pltpu

F32 = jnp.float32
BF16 = jnp.bfloat16
EPS = 1e-6
N_DEV = 8
LANES = 128
SUBLANES = 8
VMEM_LIMIT = 60 * 1024 * 1024

HEADS = 4
HEAD_DIM = 64
MIX_W = HEADS * HEAD_DIM
HG_CHUNK = 32
RET_CHUNK = 128
S5_GROUPS = 16
S5_GROUP = 16
S5_STATE = 64
LRU_C = 8.0
XA_HEADS = 4

ADAM_LR = 0.001
ADAM_B1 = 0.9
ADAM_B2 = 0.999
ADAM_EPS = 1e-08
ADAM_WD = 0.01
ADAM_STEP = 10

BIG = (("w_in", 2), ("w_gate", 2), ("w_up", 3), ("w_out", 1), ("xa_w_q", 1), ("xa_w_kv", 2), ("xa_w_o", 1),
       ("ffn_w_gu", 2), ("ffn_w_down", 1))
SMALL_SHARDED = (("lru_conv_w", 2), ("s5_glu_w", 1))
WEIGHTS = ("hg_lower_bounds", "norm_mix_pre", "norm_mix_post", "w_in", "w_gate", "b_gate", "hg_norm", "ret_norm",
           "lru_conv_w", "lru_conv_b", "lru_wa", "lru_ba", "lru_wx", "lru_bx", "lru_lambda", "s5_lam_re", "s5_lam_im",
           "s5_b_re", "s5_b_im", "s5_c_re", "s5_c_im", "s5_d", "s5_log_dt", "s5_glu_w", "s5_glu_b", "w_up", "w_out",
           "norm_xa_pre", "norm_xa_post", "norm_mem", "xa_w_q", "xa_w_kv", "xa_w_o", "norm_ffn_pre", "norm_ffn_post",
           "ffn_w_gu", "ffn_w_down")
GATHER_IN = {"in_proj": (1,), "gate_proj": (7,), "hg": (5,), "s5": (2, 3, 4, 6), "merge": (8,)}
_BIG_NAMES = tuple(n for n, _ in BIG)
_SMALL_SHARDED_NAMES = tuple(n for n, _ in SMALL_SHARDED)
REPLICATED = tuple(n for n in WEIGHTS if n not in _BIG_NAMES and n not in _SMALL_SHARDED_NAMES)


def _dot(a, b):
    return jnp.dot(a.astype(BF16), b.astype(BF16), preferred_element_type=F32)


def _dot_nt(a, b):
    return lax.dot_general(a.astype(BF16), b.astype(BF16), (((1,), (1,)), ((), ())), preferred_element_type=F32)


def _dot_tn(a, b):
    return lax.dot_general(a.astype(BF16), b.astype(BF16), (((0,), (0,)), ((), ())), preferred_element_type=F32)


def _dot_exact(a, b):
    return jnp.dot(a, b, precision=lax.Precision.HIGHEST, preferred_element_type=F32)


def _rms(x, g):
    return x * lax.rsqrt(jnp.mean(x * x, axis=-1, keepdims=True) + EPS) * g


def _shift_down(x, d, fill):
    return jnp.concatenate([jnp.full((d, x.shape[1]), fill, x.dtype), x[:-d]], axis=0)


def _shift_up(x, d, fill):
    return jnp.concatenate([x[d:], jnp.full((d, x.shape[1]), fill, x.dtype)], axis=0)


def _cumsum_rows(x):
    d = 1
    while d < x.shape[0]:
        x = x + _shift_down(x, d, 0.0)
        d *= 2
    return x


def _lane_head(shape, dim):
    return lax.shift_right_logical(lax.broadcasted_iota(jnp.int32, shape, dim), 6)


def _head_masks(width=MIX_W):
    head = _lane_head((1, width), 1)
    return [(head == h).astype(F32) for h in range(HEADS)]


def _block_diag_mask():
    return (_lane_head((MIX_W, MIX_W), 0) == _lane_head((MIX_W, MIX_W), 1)).astype(F32)


def _head_rms(o, g):
    ms = _dot_exact(o * o, _block_diag_mask()) * (1.0 / HEAD_DIM)
    return o * lax.rsqrt(ms + EPS) * g


def _swap_pairs(x):
    lane = lax.broadcasted_iota(jnp.int32, x.shape, 1)
    return jnp.where((lane & 1) == 0, jnp.roll(x, -1, axis=1), jnp.roll(x, 1, axis=1))


def _stack_heads(t, masks):
    return jnp.concatenate([t * m for m in masks], axis=0)


@jax.custom_vjp
def _real_scan(a, u, h0):
    return _real_scan_fwd(a, u, h0)[0]


def _real_scan_fwd(a, u, h0):
    t = a.shape[0]
    acc_a, acc_u = a, u
    d = 1
    while d < t:
        acc_u = acc_u + acc_a * _shift_down(acc_u, d, 0.0)
        acc_a = acc_a * _shift_down(acc_a, d, 1.0)
        d *= 2
    h = acc_u + acc_a * h0
    return h, (a, h, h0)


def _real_scan_bwd(res, dh):
    a, h, h0 = res
    t = a.shape[0]
    acc_a = _shift_up(a, 1, 0.0)
    g = dh
    d = 1
    while d < t:
        g = g + acc_a * _shift_up(g, d, 0.0)
        acc_a = acc_a * _shift_up(acc_a, d, 1.0)
        d *= 2
    h_prev = jnp.concatenate([h0, h[:-1]], axis=0)
    return g * h_prev, g, (a * g)[0:1]


_real_scan.defvjp(_real_scan_fwd, _real_scan_bwd)


def _cmul(ar, ai, br, bi):
    return ar * br - ai * bi, ar * bi + ai * br


def _geometric_sums(ar, ai, ur, ui, forward):
    shift = _shift_down if forward else _shift_up
    t = ur.shape[0]
    g = SUBLANES
    in_group = lax.broadcasted_iota(jnp.int32, ur.shape, 0) & (g - 1)
    pr, pi, sr, si = ar, ai, ur, ui
    d = 1
    while d < g:
        keep = in_group >= d if forward else in_group < g - d
        mr, mi = _cmul(pr, pi, jnp.where(keep, shift(sr, d, 0.0), 0.0), jnp.where(keep, shift(si, d, 0.0), 0.0))
        sr, si = sr + mr, si + mi
        pr, pi = _cmul(pr, pi, pr, pi)
        d *= 2
    row = lax.broadcasted_iota(jnp.int32, (g, ur.shape[1]), 0)
    qr, qi = ar, ai
    tr, ti = jnp.zeros((g, ur.shape[1]), F32), jnp.zeros((g, ur.shape[1]), F32)
    for r in range(g):
        here = row == (r if forward else g - 1 - r)
        tr, ti = jnp.where(here, qr, tr), jnp.where(here, qi, ti)
        qr, qi = _cmul(qr, qi, ar, ai)
    outs_r, outs_i = [], []
    cr = ci = None
    order = range(t // g) if forward else reversed(range(t // g))
    for n in order:
        br, bi = sr[n * g:(n + 1) * g], si[n * g:(n + 1) * g]
        if cr is not None:
            mr, mi = _cmul(tr, ti, cr, ci)
            br, bi = br + mr, bi + mi
        edge = slice(g - 1, g) if forward else slice(0, 1)
        cr, ci = br[edge], bi[edge]
        outs_r.append(br)
        outs_i.append(bi)
    if not forward:
        outs_r.reverse()
        outs_i.reverse()
    return jnp.concatenate(outs_r, axis=0), jnp.concatenate(outs_i, axis=0)


@jax.custom_vjp
def _complex_scan(ar, ai, ur, ui, h0r, h0i):
    return _complex_scan_fwd(ar, ai, ur, ui, h0r, h0i)[0]


def _complex_scan_fwd(ar, ai, ur, ui, h0r, h0i):
    first = lax.broadcasted_iota(jnp.int32, ur.shape, 0) == 0
    cr, ci = _cmul(ar, ai, h0r, h0i)
    hr, hi = _geometric_sums(ar, ai, ur + jnp.where(first, cr, 0.0), ui + jnp.where(first, ci, 0.0), True)
    return (hr, hi), (ar, ai, hr, hi, h0r, h0i)


def _complex_scan_bwd(res, dh):
    ar, ai, hr, hi, h0r, h0i = res
    gr, gi = _geometric_sums(ar, -ai, dh[0], dh[1], False)
    qr = jnp.concatenate([h0r, hr[:-1]], axis=0)
    qi = jnp.concatenate([h0i, hi[:-1]], axis=0)
    dar = jnp.sum(gr * qr + gi * qi, axis=0, keepdims=True)
    dai = jnp.sum(gi * qr - gr * qi, axis=0, keepdims=True)
    d0r, d0i = _cmul(ar, -ai, gr[0:1], gi[0:1])
    return dar, dai, gr, gi, d0r, d0i


_complex_scan.defvjp(_complex_scan_fwd, _complex_scan_bwd)


def fn_norm(st, rows, params):
    (x,), (g,) = rows, params
    return (), (_rms(x, g),), ()


def fn_keep_norm(st, rows, params):
    (x,), (g,) = rows, params
    return (), (x, _rms(x, g)), ()


def fn_hgrn2(st, rows, params):
    (state,) = st
    (z,) = rows
    lb, norm_g = params
    q, f_logit, v_all, g = (z[:, k * MIX_W:(k + 1) * MIX_W] for k in range(4))
    f = lb + (1.0 - lb) * jax.nn.sigmoid(f_logit)
    log_f = jnp.log(f)
    k_all = 1.0 - f
    q_all = jax.nn.silu(q)
    masks = _head_masks()
    bd = _block_diag_mask()
    c = HG_CHUNK
    col = lax.broadcasted_iota(jnp.int32, (c, HEADS * c), 1) & (c - 1)
    causal = col <= lax.broadcasted_iota(jnp.int32, (c, HEADS * c), 0)
    outs = []
    for n in range(z.shape[0] // c):
        sl = slice(n * c, (n + 1) * c)
        lf = log_f[sl]
        b = _cumsum_rows(lf)
        b_end = jnp.sum(lf, axis=0, keepdims=True)
        q_dec = q_all[sl] * jnp.exp(b)
        k_inv = k_all[sl] * jnp.exp(-b)
        k_end = k_all[sl] * jnp.exp(b_end - b)
        v = v_all[sl]
        scores = jnp.where(causal, _dot_nt(q_dec, _stack_heads(k_inv, masks)), 0.0)
        outs.append(_dot(scores, _stack_heads(v, masks)) + _dot_nt(q_dec, state))
        state = state * jnp.exp(b_end) + _dot_tn(v, k_end) * bd
    o = jnp.concatenate(outs, axis=0) if len(outs) > 1 else outs[0]
    return (state,), (_head_rms(o, norm_g) * jax.nn.silu(g),), ()


def fn_retention(st, rows, params):
    (state,) = st
    z, cos_t, sin_t = rows
    norm_g, xi, zeta, decay, g_end = params
    q, k, v_all, g = (z[:, i * MIX_W:(i + 1) * MIX_W] for i in range(4))
    q_all = q * cos_t + _swap_pairs(q) * sin_t
    k_all = (k * cos_t + _swap_pairs(k) * sin_t) * (HEAD_DIM ** -0.5)
    masks = _head_masks()
    bd = _block_diag_mask()
    c = RET_CHUNK
    outs = []
    for n in range(z.shape[0] // c):
        sl = slice(n * c, (n + 1) * c)
        qc, kc, v = q_all[sl], k_all[sl], v_all[sl]
        scores = _dot_nt(qc, _stack_heads(kc, masks)) * decay
        outs.append(_dot(scores, _stack_heads(v, masks)) + _dot_nt(qc * xi, state))
        state = state * g_end + _dot_tn(v, kc * zeta) * bd
    o = jnp.concatenate(outs, axis=0) if len(outs) > 1 else outs[0]
    return (state,), (_head_rms(o, norm_g) * jax.nn.silu(g),), ()


def fn_rglru(st, rows, params):
    tail_x, tail_h = st
    (z,) = rows
    conv_w, conv_b, wa, ba, wx, bx, sp = params
    t = z.shape[0]
    xg, xi = z[:, :MIX_W], z[:, MIX_W:]
    full = jnp.concatenate([tail_x, xi], axis=0)
    xc = conv_b
    for k in range(4):
        xc = xc + conv_w[k:k + 1] * full[5 + k:5 + k + t]
    r = jax.nn.sigmoid(_dot(xc, wa) + ba)
    ig = jax.nn.sigmoid(_dot(xc, wx) + bx)
    log_a = -LRU_C * r * sp
    a = jnp.exp(log_a)
    one_minus_a2 = -jnp.tanh(log_a) * (a * a + 1.0)
    u = jnp.sqrt(one_minus_a2) * (ig * xc)
    h = _real_scan(a, u, tail_h[7:8])
    return (xi[t - 8:], h[t - 8:]), (h * jax.nn.gelu(xg),), ()


def fn_s5(st, rows, params):
    tail_r, tail_i = st
    (u,) = rows
    bt_re, bt_im, lb_re, lb_im, ct_re, ct_im, d, glu_w, glu_b = params
    t = u.shape[0]
    bu_re = _dot(u, bt_re)
    bu_im = _dot(u, bt_im)
    h_re, h_im = _complex_scan(lb_re, lb_im, bu_re, bu_im, tail_r[7:8], tail_i[7:8])
    y = _dot(h_re, ct_re) - _dot(h_im, ct_im) + d * u
    act = jax.nn.gelu(y)
    out = act * jax.nn.sigmoid(_dot(act, glu_w) + glu_b)
    return (h_re[t - 8:], h_im[t - 8:]), (out,), ()


def fn_merge(st, rows, params):
    ya, yb, yc, yd, gl, x = rows
    w0, w1, w2, w3, b_gate, w_out, g_post = params
    d = x.shape[1]
    mix = None
    for n, (y, w) in enumerate(((ya, w0), (yb, w1), (yc, w2), (yd, w3))):
        gate = jax.nn.sigmoid(gl[:, n * d:(n + 1) * d] + b_gate[:, n * d:(n + 1) * d])
        term = gate * _dot(y, w)
        mix = term if mix is None else mix + term
    return (), (x + _rms(_dot(mix, w_out), g_post),), ()


def fn_mem(st, rows, params):
    (mem,), (g, wk, wv) = rows, params
    m = _rms(mem, g)
    return (), (_dot(m, wk), _dot(m, wv)), ()


def fn_xattn(st, rows, params):
    (x,) = rows
    g_pre, wq, k, v, wo, g_post, g_next = params
    d = x.shape[1]
    dh = d // XA_HEADS
    q = _dot(_rms(x, g_pre), wq)
    heads = []
    for h in range(XA_HEADS):
        sl = slice(h * dh, (h + 1) * dh)
        s = _dot_nt(q[:, sl], k[:, sl]) * (dh ** -0.5)
        heads.append(_dot(jax.nn.softmax(s, axis=-1), v[:, sl]))
    x2 = x + _rms(_dot(jnp.concatenate(heads, axis=1), wo), g_post)
    return (), (x2, _rms(x2, g_next)), ()


def fn_res_norm(st, rows, params):
    (x, o), (g_post, g_next) = rows, params
    xn = x + _rms(o, g_post)
    return (), (xn, _rms(xn, g_next)), ()


def fn_loss_head(st, rows, params):
    (x, o, target), (g_post,) = rows, params
    y, vjp = jax.vjp(lambda o_, g_: x + _rms(o_, g_), o, g_post)
    err = y - target
    inv_d = 1.0 / x.shape[1]
    dy = err * inv_d
    do, dg = vjp(dy)
    loss = 0.5 * inv_d * jnp.sum(err * err)
    return (), (dy, do), (jnp.full((8, LANES), loss, F32), dg)


def _params():
    return pltpu.CompilerParams(dimension_semantics=("arbitrary",), vmem_limit_bytes=VMEM_LIMIT)


def _row_spec(tb, width, colblk, nb, reverse):
    if reverse:
        return pl.BlockSpec((tb, width), lambda i: (nb - 1 - i, colblk))
    return pl.BlockSpec((tb, width), lambda i: (i, colblk))


def _full_spec(shape):
    return pl.BlockSpec(shape, lambda i: (0,) * len(shape), pipeline_mode=pl.Buffered(1))


def _saved_spec(shape, nb, reverse):
    if reverse:
        return pl.BlockSpec((1,) + shape, lambda i: (nb - 1 - i, 0, 0))
    return pl.BlockSpec((1,) + shape, lambda i: (i, 0, 0))


def _param_value(ref):
    v = ref[...]
    return v if v.dtype == BF16 else v.astype(F32)


def _side(side):
    arrays, gather = side if side is not None else ((), False)
    return list(arrays), gather, len(arrays)


_ANY = pl.BlockSpec(memory_space=pl.ANY)


def stage_fwd(name, fn, rows, params, *, tb, outs, states=(), accs=(), side=None):
    n_rows = rows[0][0].shape[0]
    nb = n_rows // tb
    nr, npar, no, na, ns = len(rows), len(params), len(outs), len(accs), len(states)
    side_arrays, gather, nx = _side(side)

    def body(*refs):
        row_refs, par_refs, side_in = refs[:nr], refs[nr:nr + npar], refs[nr + npar:nr + npar + nx]
        o = nr + npar + nx
        out_refs, acc_refs = refs[o:o + no], refs[o + no:o + no + na]
        saved_refs = refs[o + no + na:o + no + na + ns]
        o = o + no + na + ns
        side_out, st_refs, sems = refs[o:o + nx], refs[o + nx:o + nx + ns], refs[o + nx + ns:]
        i = pl.program_id(0)
        if nx:
            start, wait = _exchange_copies(side_in, side_out, *sems, gather)
            pl.when(i == 0)(start)

        @pl.when(i == 0)
        def _():
            for r in st_refs + acc_refs:
                r[...] = jnp.zeros_like(r)

        st = tuple(r[...] for r in st_refs)
        for sv, s in zip(saved_refs, st):
            sv[0] = s
        new_st, out_vals, acc_vals = fn(st, tuple(r[...].astype(F32) for r in row_refs),
                                        tuple(_param_value(r) for r in par_refs))
        for r, v in zip(out_refs, out_vals):
            r[...] = v.astype(r.dtype)
        for r, v in zip(acc_refs, acc_vals):
            r[...] += v
        for r, v in zip(st_refs, new_st):
            r[...] = v
        if nx:
            pl.when(i == nb - 1)(wait)

    res = pl.pallas_call(
        body, name=name, grid=(nb,),
        in_specs=[_row_spec(tb, w, cb, nb, False) for _, w, cb in rows] + [_full_spec(p.shape) for p in params]
        + [_ANY] * nx,
        out_specs=[_row_spec(tb, w, 0, nb, False) for w, _ in outs] + [_full_spec(s) for s in accs]
        + [_saved_spec(s, nb, False) for s in states] + [_ANY] * nx,
        out_shape=[jax.ShapeDtypeStruct((n_rows, w), dt) for w, dt in outs]
        + [jax.ShapeDtypeStruct(s, F32) for s in accs] + [jax.ShapeDtypeStruct((nb,) + s, F32) for s in states]
        + _exchange_shapes(side_arrays, gather),
        scratch_shapes=[pltpu.VMEM(s, F32) for s in states] + (_exchange_sems(nx) if nx else []),
        compiler_params=_params(),
    )(*[a for a, _, _ in rows], *params, *side_arrays)
    base = (res[:no], res[no:no + na], res[no + na:no + na + ns])
    return base + (res[no + na + ns:],) if nx else base


def stage_bwd(name, fn, rows, params, cts, *, tb, saved=(), diff_rows=(), diff_params=(), row_dtypes=None, side=None,
              into=None):
    n_rows = rows[0][0].shape[0]
    nb = n_rows // tb
    nr, npar, ns, nc = len(rows), len(params), len(saved), len(cts)
    ndr, ndp = len(diff_rows), len(diff_params)
    row_dtypes = row_dtypes or (F32,) * ndr
    state_shapes = [s.shape[1:] for s in saved]
    side_arrays, gather, nx = _side(side)
    into_buffer = [into[0]] if into is not None and into[0] is not None else []
    na = len(into_buffer)
    if into is not None:
        assert ndr == 1
        drow_specs = [_row_spec(tb, rows[diff_rows[0]][1], rows[diff_rows[0]][2], nb, True)]
        drow_shapes = [jax.ShapeDtypeStruct((n_rows, into[1]), row_dtypes[0])]
    else:
        drow_specs = [_row_spec(tb, rows[k][1], 0, nb, True) for k in diff_rows]
        drow_shapes = [jax.ShapeDtypeStruct((n_rows, rows[k][1]), dt) for k, dt in zip(diff_rows, row_dtypes)]

    def body(*refs):
        row_refs, par_refs = refs[:nr], refs[nr:nr + npar]
        o = nr + npar
        saved_refs, ct_refs, side_in = refs[o:o + ns], refs[o + ns:o + ns + nc], refs[o + ns + nc:o + ns + nc + nx]
        o = o + ns + nc + nx + na
        drow_refs, dpar_refs, side_out = refs[o:o + ndr], refs[o + ndr:o + ndr + ndp], refs[o + ndr + ndp:o + ndr + ndp + nx]
        o = o + ndr + ndp + nx
        dst_refs, sems = refs[o:o + ns], refs[o + ns:]
        i = pl.program_id(0)
        if nx:
            start, wait = _exchange_copies(side_in, side_out, *sems, gather)
            pl.when(i == 0)(start)

        @pl.when(i == 0)
        def _():
            for r in dst_refs + dpar_refs:
                r[...] = jnp.zeros_like(r)

        st = tuple(r[0] for r in saved_refs)
        row_vals = [r[...].astype(F32) for r in row_refs]
        par_vals = [_param_value(r) for r in par_refs]

        def f(st_, dr_, dp_):
            rv, pv = list(row_vals), list(par_vals)
            for k, v in zip(diff_rows, dr_):
                rv[k] = v
            for k, v in zip(diff_params, dp_):
                pv[k] = v
            new_st, out_vals, _ = fn(st_, tuple(rv), tuple(pv))
            return new_st, out_vals

        _, vjp = jax.vjp(f, st, tuple(row_vals[k] for k in diff_rows), tuple(par_vals[k] for k in diff_params))
        g_st, g_rows, g_par = vjp((tuple(r[...] for r in dst_refs), tuple(r[...].astype(F32) for r in ct_refs)))
        for r, v in zip(drow_refs, g_rows):
            r[...] = v.astype(r.dtype)
        for r, v in zip(dpar_refs, g_par):
            r[...] += v
        for r, v in zip(dst_refs, g_st):
            r[...] = v
        if nx:
            pl.when(i == nb - 1)(wait)

    res = pl.pallas_call(
        body, name=name, grid=(nb,),
        in_specs=[_row_spec(tb, w, cb, nb, True) for _, w, cb in rows] + [_full_spec(p.shape) for p in params]
        + [_saved_spec(s, nb, True) for s in state_shapes] + [_row_spec(tb, c.shape[1], 0, nb, True) for c in cts]
        + [_ANY] * (nx + na),
        out_specs=drow_specs + [_full_spec(params[k].shape) for k in diff_params] + [_ANY] * nx,
        out_shape=drow_shapes + [jax.ShapeDtypeStruct(params[k].shape, F32) for k in diff_params]
        + _exchange_shapes(side_arrays, gather),
        scratch_shapes=[pltpu.VMEM(s, F32) for s in state_shapes] + (_exchange_sems(nx) if nx else []),
        input_output_aliases={nr + npar + ns + nc + nx: 0} if na else {},
        compiler_params=_params(),
    )(*[a for a, _, _ in rows], *params, *saved, *cts, *side_arrays, *into_buffer)
    base = (res[:ndr], res[ndr:ndr + ndp])
    return base + (res[ndr + ndp:],) if nx else base


def _pick(n, target):
    if n <= target:
        return n
    best = None
    for t in range(LANES, target + 1, LANES):
        if n % t == 0:
            best = t
    assert best is not None, n
    return best


def _mesh_position():
    return lax.axis_index("x"), lax.axis_index("y"), lax.axis_index("c")


def _peer(pos, k):
    x, y, c = pos
    px = 1 - x if k & 4 else x
    py = 1 - y if k & 2 else y
    pc = 1 - c if k & 1 else c
    return (px, py, pc), 4 * px + 2 * py + pc


def _exchange_copies(x_refs, o_refs, send_sems, recv_sems, local_sems, gather):
    pos = _mesh_position()
    me = 4 * pos[0] + 2 * pos[1] + pos[2]
    pairs = list(enumerate(zip(x_refs, o_refs)))

    def remote(k, a, src, dst):
        peer, _ = _peer(pos, k)
        return pltpu.make_async_remote_copy(src_ref=src, dst_ref=dst, send_sem=send_sems.at[k - 1, a],
                                            recv_sem=recv_sems.at[k - 1, a], device_id=peer,
                                            device_id_type=pl.DeviceIdType.MESH)

    def local(a, x, o):
        return pltpu.make_async_copy(x if gather else x.at[me], o.at[me], local_sems.at[a])

    def start():
        for a, (x, o) in pairs:
            local(a, x, o).start()
        for k in range(1, N_DEV):
            peer_idx = _peer(pos, k)[1]
            for a, (x, o) in pairs:
                remote(k, a, x if gather else x.at[peer_idx], o.at[me]).start()

    def wait():
        for k in range(1, N_DEV):
            peer_idx = _peer(pos, k)[1]
            for a, (x, o) in pairs:
                arrival = remote(k, a, x if gather else x.at[me], o.at[peer_idx])
                arrival.wait_recv()
                arrival.wait_send()
        for a, (x, o) in pairs:
            local(a, x, o).wait()

    return start, wait


def _exchange_shapes(arrays, gather):
    return [jax.ShapeDtypeStruct(((N_DEV,) + x.shape) if gather else x.shape, x.dtype) for x in arrays]


def _exchange_sems(n):
    return [pltpu.SemaphoreType.DMA((N_DEV - 1, n)), pltpu.SemaphoreType.DMA((N_DEV - 1, n)),
            pltpu.SemaphoreType.DMA((n,))]


def _exchange(name, arrays, gather):
    n = len(arrays)

    def body(*refs):
        start, wait = _exchange_copies(refs[:n], refs[n:2 * n], *refs[2 * n:], gather)
        start()
        wait()

    return pl.pallas_call(
        body, name=name,
        in_specs=[pl.BlockSpec(memory_space=pl.ANY)] * n, out_specs=[pl.BlockSpec(memory_space=pl.ANY)] * n,
        out_shape=_exchange_shapes(arrays, gather), scratch_shapes=_exchange_sems(n),
    )(*arrays)


def _pick_n(n):
    return 1408 if n % 1408 == 0 else _pick(n, 512)


def matmul(name, a, b, mode, out_dtype, add=None, side=None):
    if mode == "tn":
        k, m = a.shape
    else:
        m, k = a.shape
    n = b.shape[0] if mode == "nt" else b.shape[1]
    if mode == "tn":
        tm, tn, tk = _pick(m, 1408), _pick(n, 1408), _pick(k, 1024)
    else:
        tm, tn, tk = _pick(m, 1024), _pick_n(n), _pick(k, 2816)
    nk = k // tk
    grid = (m // tm, n // tn, nk)
    a_spec = pl.BlockSpec((tk, tm), lambda i, j, kk: (kk, i)) if mode == "tn" else pl.BlockSpec((tm, tk), lambda i, j, kk: (i, kk))
    b_spec = pl.BlockSpec((tn, tk), lambda i, j, kk: (j, kk)) if mode == "nt" else pl.BlockSpec((tk, tn), lambda i, j, kk: (kk, j))
    o_spec = pl.BlockSpec((tm, tn), lambda i, j, kk: (i, j))
    dims = {"nn": (((1,), (0,)), ((), ())), "nt": (((1,), (1,)), ((), ())), "tn": (((0,), (0,)), ((), ()))}[mode]
    has_add = add is not None
    side_arrays, gather = side if side is not None else ((), False)
    ns = len(side_arrays)
    n_in = 2 + has_add

    def body(*refs):
        a_ref, b_ref = refs[0], refs[1]
        side_in, o_ref, side_out = refs[n_in:n_in + ns], refs[n_in + ns], refs[n_in + ns + 1:n_in + 2 * ns + 1]
        scratch = refs[n_in + 2 * ns + 1:]
        ids = [pl.program_id(d) for d in range(3)]
        if ns:
            start, wait = _exchange_copies(side_in, side_out, *scratch[-3:], gather)
            pl.when((ids[0] == 0) & (ids[1] == 0) & (ids[2] == 0))(start)
        part = lax.dot_general(a_ref[...].astype(BF16), b_ref[...].astype(BF16), dims, preferred_element_type=F32)
        if nk == 1:
            o_ref[...] = (part + refs[2][...].astype(F32) if has_add else part).astype(o_ref.dtype)
        else:
            acc_ref = scratch[0]

            @pl.when(ids[2] == 0)
            def _():
                acc_ref[...] = part + refs[2][...].astype(F32) if has_add else part

            @pl.when(ids[2] > 0)
            def _():
                acc_ref[...] += part

            @pl.when(ids[2] == nk - 1)
            def _():
                o_ref[...] = acc_ref[...].astype(o_ref.dtype)
        if ns:
            pl.when((ids[0] == grid[0] - 1) & (ids[1] == grid[1] - 1) & (ids[2] == nk - 1))(wait)

    any_spec = pl.BlockSpec(memory_space=pl.ANY)
    res = pl.pallas_call(
        body, name=name, grid=grid,
        in_specs=[a_spec, b_spec] + ([o_spec] if has_add else []) + [any_spec] * ns,
        out_specs=[o_spec] + [any_spec] * ns,
        out_shape=[jax.ShapeDtypeStruct((m, n), out_dtype)] + _exchange_shapes(side_arrays, gather),
        scratch_shapes=([pltpu.VMEM((tm, tn), F32)] if nk > 1 else []) + (_exchange_sems(ns) if ns else []),
        compiler_params=pltpu.CompilerParams(
            dimension_semantics=("arbitrary",) * 3 if ns else ("parallel", "parallel", "arbitrary"),
            vmem_limit_bytes=VMEM_LIMIT),
    )(a, b, *([add] if has_add else []), *side_arrays)
    return (res[0], res[1:]) if ns else res[0]


def matmul_fused(name, a, bs, mode, n, extras, epilogue, out_dtypes):
    m, k = a.shape
    tm, tn = _pick(m, 1024), _pick_n(n)
    dims = {"nn": (((1,), (0,)), ((), ())), "nt": (((1,), (1,)), ((), ()))}[mode]
    nb, nx = len(bs), len(extras)

    def b_spec(off):
        if mode == "nt":
            return pl.BlockSpec((tn, k), lambda i, j: (j + off, 0))
        return pl.BlockSpec((k, tn), lambda i, j: (0, j + off))

    def body(*refs):
        a_val = refs[0][...].astype(BF16)
        parts = tuple(lax.dot_general(a_val, r[...].astype(BF16), dims, preferred_element_type=F32)
                      for r in refs[1:1 + nb])
        tiles = tuple(r[...].astype(F32) for r in refs[1 + nb:1 + nb + nx])
        for r, v in zip(refs[1 + nb + nx:], epilogue(parts, tiles)):
            r[...] = v.astype(r.dtype)

    tile = pl.BlockSpec((tm, tn), lambda i, j: (i, j))
    return pl.pallas_call(
        body, name=name, grid=(m // tm, n // tn),
        in_specs=[pl.BlockSpec((tm, k), lambda i, j: (i, 0))] + [b_spec(off) for _, off in bs] + [tile] * nx,
        out_specs=[tile] * len(out_dtypes), out_shape=[jax.ShapeDtypeStruct((m, n), dt) for dt in out_dtypes],
        compiler_params=pltpu.CompilerParams(dimension_semantics=("parallel", "parallel"),
                                             vmem_limit_bytes=VMEM_LIMIT),
    )(a, *[b for b, _ in bs], *extras)


def _glu_fwd_tiles(parts, tiles):
    gate, up = parts
    return gate, up, jax.nn.silu(gate) * up


def _glu_bwd_tiles(parts, tiles):
    (da,), (gate, up) = parts, tiles
    _, vjp = jax.vjp(lambda g, u: jax.nn.silu(g) * u, gate, up)
    return vjp(da)


def adamw(name, w, m, v, gparts, side=None):
    layers, rows, cols = w.shape
    parts = gparts[0].shape[0]
    tr = 8
    while tr * 2 * cols <= 65536 and rows % (tr * 2) == 0:
        tr *= 2
    nblk = rows // tr
    c1 = 1.0 - ADAM_B1 ** ADAM_STEP
    c2 = 1.0 - ADAM_B2 ** ADAM_STEP
    side_arrays, gather, nx = _side(side)

    def body(*refs):
        w_ref, m_ref, v_ref = refs[:3]
        g_refs = refs[3:3 + layers]
        side_in = refs[3 + layers:3 + layers + nx]
        go_ref, d_ref, mo_ref, vo_ref = refs[3 + layers + nx:7 + layers + nx]
        side_out, sems = refs[7 + layers + nx:7 + layers + 2 * nx], refs[7 + layers + 2 * nx:]
        layer = pl.program_id(0)
        if nx:
            start, wait = _exchange_copies(side_in, side_out, *sems, gather)
            pl.when((layer == 0) & (pl.program_id(1) == 0))(start)
        g = None
        for ll, g_ref in enumerate(g_refs):
            s = g_ref[0].astype(F32)
            for p in range(1, parts):
                s = s + g_ref[p].astype(F32)
            g = s if g is None else jnp.where(layer == ll, s, g)
        m_new = ADAM_B1 * m_ref[...] + (1.0 - ADAM_B1) * g
        v_new = ADAM_B2 * v_ref[...] + (1.0 - ADAM_B2) * (g * g)
        m_hat = m_new / c1
        v_hat = v_new / c2
        go_ref[...] = g
        d_ref[...] = -ADAM_LR * (m_hat / (jnp.sqrt(v_hat) + ADAM_EPS) + ADAM_WD * w_ref[...])
        mo_ref[...] = m_new
        vo_ref[...] = v_new
        if nx:
            pl.when((layer == layers - 1) & (pl.program_id(1) == nblk - 1))(wait)

    def part_spec(ll):
        return pl.BlockSpec((parts, tr, cols),
                            lambda l, i: (0, jnp.where(l == ll, i, jnp.where(l < ll, 0, nblk - 1)), 0))

    spec = pl.BlockSpec((None, tr, cols), lambda l, i: (l, i, 0))
    res = pl.pallas_call(
        body, name=name, grid=(layers, nblk),
        in_specs=[spec, spec, spec] + [part_spec(ll) for ll in range(layers)] + [_ANY] * nx,
        out_specs=[spec] * 4 + [_ANY] * nx,
        out_shape=[jax.ShapeDtypeStruct(w.shape, F32)] * 4 + _exchange_shapes(side_arrays, gather),
        scratch_shapes=_exchange_sems(nx) if nx else [],
        compiler_params=pltpu.CompilerParams(dimension_semantics=("arbitrary", "arbitrary"),
                                             vmem_limit_bytes=VMEM_LIMIT),
    )(w, m, v, *gparts, *side_arrays)
    return (res[:4], res[4:]) if nx else res


def _adamw_math(w, m, v, g):
    m_new = ADAM_B1 * m + (1.0 - ADAM_B1) * g
    v_new = ADAM_B2 * v + (1.0 - ADAM_B2) * (g * g)
    m_hat = m_new / (1.0 - ADAM_B1 ** ADAM_STEP)
    v_hat = v_new / (1.0 - ADAM_B2 ** ADAM_STEP)
    return -ADAM_LR * (m_hat / (jnp.sqrt(v_hat) + ADAM_EPS) + ADAM_WD * w), m_new, v_new


def adamw_many(name, ws, ms, vs, garrays, where):
    n, ng = len(ws), len(garrays)

    def body(*refs):
        ins, g_refs, outs = refs[:3 * n], refs[3 * n:3 * n + ng], refs[3 * n + ng:]
        for i, sources in enumerate(where):
            for a, j, layer in sources:
                g_ref = g_refs[a]
                g = g_ref[0, j].astype(F32)
                for p in range(1, g_ref.shape[0]):
                    g = g + g_ref[p, j].astype(F32)
                sl = slice(None) if layer is None else slice(layer, layer + 1)
                delta, m_new, v_new = _adamw_math(ins[i][sl], ins[n + i][sl], ins[2 * n + i][sl], g)
                for r, val in zip(outs[4 * i:4 * i + 4], (g, delta, m_new, v_new)):
                    r[sl] = val

    vmem = pl.BlockSpec(memory_space=pltpu.VMEM)
    res = pl.pallas_call(
        body, name=name, in_specs=[vmem] * (3 * n + ng), out_specs=[vmem] * (4 * n),
        out_shape=[jax.ShapeDtypeStruct(w.shape, F32) for w in ws for _ in range(4)],
        compiler_params=pltpu.CompilerParams(vmem_limit_bytes=VMEM_LIMIT),
    )(*ws, *ms, *vs, *garrays)
    return [res[4 * i:4 * i + 4] for i in range(n)]


def _pack(arrays, dtype, row_multiple):
    flat = jnp.concatenate([a.astype(dtype).reshape(-1) for a in arrays])
    unit = row_multiple * LANES
    pad = (-flat.shape[0]) % unit
    if pad:
        flat = jnp.concatenate([flat, jnp.zeros((pad,), dtype)])
    return flat.reshape(-1, LANES)


def _unpack(flat2d, shapes):
    flat = flat2d.reshape(-1)
    out, off = [], 0
    for s in shapes:
        n = int(np.prod(s))
        out.append(flat[off:off + n].reshape(s))
        off += n
    return out


def _unpack_stacked(stacked, shapes):
    flat = stacked.reshape(N_DEV, -1)
    out, off = [], 0
    for s in shapes:
        n = int(np.prod(s))
        out.append(flat[:, off:off + n].reshape((N_DEV,) + tuple(s)))
        off += n
    return out


def _merge_shards(stacked, axis):
    t = jnp.moveaxis(stacked, 0, axis)
    s = t.shape
    return t.reshape(s[:axis] + (s[axis] * s[axis + 1],) + s[axis + 2:])


def _split_shards(full, axis, n=N_DEV):
    s = full.shape
    t = full.reshape(s[:axis] + (n, s[axis] // n) + s[axis + 1:])
    return jnp.moveaxis(t, axis, 0)


def _lower_bounds(hg_lower_bounds):
    p = jax.nn.softmax(hg_lower_bounds, axis=0)
    return jnp.cumsum(p, axis=0) - p[0:1]


def _prep_layer(p):
    def row(v):
        return v.reshape(1, -1)

    eye_b = jnp.eye(HEADS, dtype=F32)
    eye_g = jnp.eye(S5_GROUPS, dtype=F32)
    step = jnp.exp(p["s5_log_dt"])[:, None]
    lam_re, lam_im = p["s5_lam_re"], p["s5_lam_im"]
    mag = jnp.exp(lam_re * step)
    lb_re = mag * jnp.cos(lam_im * step)
    lb_im = mag * jnp.sin(lam_im * step)
    den = lam_re * lam_re + lam_im * lam_im
    f_re = ((lb_re - 1.0) * lam_re + lb_im * lam_im) / den
    f_im = (lb_im * lam_re - (lb_re - 1.0) * lam_im) / den
    bb_re = f_re[..., None] * p["s5_b_re"] - f_im[..., None] * p["s5_b_im"]
    bb_im = f_re[..., None] * p["s5_b_im"] + f_im[..., None] * p["s5_b_re"]
    width = S5_GROUPS * S5_GROUP
    n_state = S5_GROUPS * S5_STATE
    return dict(
        lb=row(p["lb"]), hg_norm=row(p["hg_norm"]), ret_norm=row(p["ret_norm"]),
        conv_w=p["lru_conv_w"], conv_b=row(p["lru_conv_b"]),
        wa=jnp.einsum("nij,nm->nimj", p["lru_wa"], eye_b).reshape(MIX_W, MIX_W).astype(BF16), ba=row(p["lru_ba"]),
        wx=jnp.einsum("nij,nm->nimj", p["lru_wx"], eye_b).reshape(MIX_W, MIX_W).astype(BF16), bx=row(p["lru_bx"]),
        sp=row(jax.nn.softplus(-p["lru_lambda"])),
        bt_re=jnp.einsum("gnp,gh->gphn", bb_re, eye_g).reshape(width, n_state).astype(BF16),
        bt_im=jnp.einsum("gnp,gh->gphn", bb_im, eye_g).reshape(width, n_state).astype(BF16),
        lb_re=row(lb_re), lb_im=row(lb_im),
        ct_re=jnp.einsum("gpn,gh->gnhp", p["s5_c_re"], eye_g).reshape(n_state, width).astype(BF16),
        ct_im=jnp.einsum("gpn,gh->gnhp", p["s5_c_im"], eye_g).reshape(n_state, width).astype(BF16),
        s5_d=row(p["s5_d"]), glu_w=p["s5_glu_w"].astype(BF16), glu_b=row(p["s5_glu_b"]),
        b_gate=row(p["b_gate"]),
        norm_mix_pre=row(p["norm_mix_pre"]), norm_mix_post=row(p["norm_mix_post"]),
        norm_xa_pre=row(p["norm_xa_pre"]), norm_xa_post=row(p["norm_xa_post"]), norm_mem=row(p["norm_mem"]),
        norm_ffn_pre=row(p["norm_ffn_pre"]), norm_ffn_post=row(p["norm_ffn_post"]),
    )


_PREP_INPUTS = ("hg_norm", "ret_norm", "lru_conv_w", "lru_conv_b", "lru_wa", "lru_ba", "lru_wx", "lru_bx", "lru_lambda",
                "s5_lam_re", "s5_lam_im", "s5_b_re", "s5_b_im", "s5_c_re", "s5_c_im", "s5_d", "s5_log_dt", "s5_glu_w",
                "s5_glu_b", "b_gate", "norm_mix_pre", "norm_mix_post", "norm_xa_pre", "norm_xa_post", "norm_mem",
                "norm_ffn_pre", "norm_ffn_post")


def _retention_constants():
    lg = np.log1p(-np.power(2.0, -5.0 - np.arange(HEADS)))
    idx = np.arange(RET_CHUNK)

    def lanes(per_head_rows):
        return np.repeat(per_head_rows.T[:, :, None], HEAD_DIM, axis=2).reshape(RET_CHUNK, MIX_W)

    xi = lanes(np.exp((idx + 1.0)[None, :] * lg[:, None]))
    zeta = lanes(np.exp((RET_CHUNK - 1.0 - idx)[None, :] * lg[:, None]))
    rel = idx[:, None] - idx[None, :]
    decay = np.where(rel[None] >= 0, np.exp(np.maximum(rel, 0)[None] * lg[:, None, None]), 0.0)
    decay = np.transpose(decay, (1, 0, 2)).reshape(RET_CHUNK, HEADS * RET_CHUNK)
    g_end = np.repeat(np.exp(RET_CHUNK * lg), HEAD_DIM)[None, :]
    return tuple(jnp.asarray(a, F32) for a in (xi, zeta, decay, g_end))


def _rotary_tables(seq):
    pos = jnp.arange(seq, dtype=F32)
    inv_freq = 10000.0 ** (-jnp.arange(0, HEAD_DIM, 2, dtype=F32) / HEAD_DIM)
    ang = pos[:, None] * inv_freq[None, :]
    cos, sin = jnp.cos(ang), jnp.sin(ang)
    cos_t = jnp.tile(jnp.repeat(cos, 2, axis=1), (1, HEADS))
    sin_t = jnp.tile(jnp.stack([-sin, sin], axis=-1).reshape(seq, HEAD_DIM), (1, HEADS))
    return cos_t, sin_t


TB_HG = 1024
TB_RET = 1024
TB_LRU = 256
TB_S5 = 256
TB_ROW = 256
TB_XA = 512

_STATE = (MIX_W, MIX_W)
_TAIL = (8, MIX_W)
_S5_TAIL = (8, S5_GROUPS * S5_STATE)


def _mixer_operands(l, z, kp, rot, ret_c):
    xi, zeta, decay, g_end = ret_c
    return dict(
        hg=dict(name=f"hgrn2_{l}", fn=fn_hgrn2, rows=[(z, 4 * MIX_W, 0)], params=[kp["lb"], kp["hg_norm"]],
                tb=TB_HG, states=(_STATE,), diff_params=(0, 1)),
        ret=dict(name=f"retention_{l}", fn=fn_retention, rows=[(z, 4 * MIX_W, 1), (rot[0], MIX_W, 0), (rot[1], MIX_W, 0)],
                 params=[kp["ret_norm"], xi, zeta, decay, g_end], tb=TB_RET, states=(_STATE,), diff_params=(0,)),
        lru=dict(name=f"rglru_{l}", fn=fn_rglru, rows=[(z, 2 * MIX_W, 4)],
                 params=[kp["conv_w"], kp["conv_b"], kp["wa"], kp["ba"], kp["wx"], kp["bx"], kp["sp"]],
                 tb=TB_LRU, states=(_TAIL, _TAIL), diff_params=(0, 1, 2, 3, 4, 5, 6)),
        s5=dict(name=f"s5_{l}", fn=fn_s5, rows=[(z, MIX_W, 10)],
                params=[kp["bt_re"], kp["bt_im"], kp["lb_re"], kp["lb_im"], kp["ct_re"], kp["ct_im"], kp["s5_d"],
                        kp["glu_w"], kp["glu_b"]],
                tb=TB_S5, states=(_S5_TAIL, _S5_TAIL), diff_params=tuple(range(9))),
    )


def _layer_forward(l, x, h, mem, w_in, shards, next_w_in_shard, kp, rot, ret_c, g_next, target):
    d = x.shape[1]
    sv = dict(x=x, h=h)
    bw = {"w_in": w_in}

    def gather(idx):
        return [shards[i] for i in idx], True

    def take(idx, stacked):
        for i, s in zip(idx, stacked):
            bw[_BIG_NAMES[i]] = _merge_shards(s, BIG[i][1] - 1)

    z, got = matmul(f"in_proj_{l}", h, w_in, "nn", F32, side=gather(GATHER_IN["in_proj"]))
    take(GATHER_IN["in_proj"], got)
    gl, got = matmul(f"gate_proj_{l}", h, bw["w_gate"], "nn", BF16, side=gather(GATHER_IN["gate_proj"]))
    take(GATHER_IN["gate_proj"], got)
    sv.update(z=z, gl=gl)
    ops = _mixer_operands(l, z, kp, rot, ret_c)
    ys = []
    for key in ("hg", "ret", "lru", "s5"):
        o = ops[key]
        idx = GATHER_IN.get(key)
        res = stage_fwd(o["name"] + "_fwd", o["fn"], o["rows"], o["params"], tb=o["tb"], outs=[(MIX_W, F32)],
                        states=o["states"], side=gather(idx) if idx else None)
        if idx:
            take(idx, res[3])
        ys.append(res[0][0])
        sv[key + "_states"] = res[2]
    sv["ys"] = ys
    merge_params = [bw["w_up"][n] for n in range(4)] + [kp["b_gate"], bw["w_out"], kp["norm_mix_post"]]
    merge_rows = [(y, MIX_W, 0) for y in ys] + [(gl, 4 * d, 0), (x, d, 0)]
    (x1,), _, _, got = stage_fwd(f"merge_{l}_fwd", fn_merge, merge_rows, merge_params, tb=TB_ROW, outs=[(d, F32)],
                                 side=gather(GATHER_IN["merge"]))
    take(GATHER_IN["merge"], got)
    wk, wv = bw["xa_w_kv"][:, :d], bw["xa_w_kv"][:, d:]
    mem_params = [kp["norm_mem"], wk, wv]
    (k, v), _, _ = stage_fwd(f"mem_{l}_fwd", fn_mem, [(mem, d, 0)], mem_params, tb=mem.shape[0],
                             outs=[(d, BF16), (d, BF16)])
    xa_params = [kp["norm_xa_pre"], bw["xa_w_q"], k, v, bw["xa_w_o"], kp["norm_xa_post"], kp["norm_ffn_pre"]]
    res = stage_fwd(f"xattn_{l}_fwd", fn_xattn, [(x1, d, 0)], xa_params, tb=TB_XA, outs=[(d, F32), (d, BF16)],
                    side=([next_w_in_shard], True) if next_w_in_shard is not None else None)
    x2, h3 = res[0]
    next_w_in = _merge_shards(res[3][0], BIG[0][1] - 1) if next_w_in_shard is not None else None
    f = bw["ffn_w_gu"].shape[1] // 2
    up_block = f // _pick_n(f)
    gate, up, a = matmul_fused(f"ffn_gu_{l}", h3, [(bw["ffn_w_gu"], 0), (bw["ffn_w_gu"], up_block)], "nn", f, [],
                               _glu_fwd_tiles, (BF16, BF16, BF16))
    o3 = matmul(f"ffn_down_{l}", a, bw["ffn_w_down"], "nn", F32)
    sv.update(x1=x1, k=k, v=v, x2=x2, h3=h3, gate=gate, up=up, a=a, o3=o3, merge_params=merge_params,
              merge_rows=merge_rows, mem_params=mem_params, xa_params=xa_params)
    if g_next is not None:
        (x3, hn), _, _ = stage_fwd(f"res_{l}_fwd", fn_res_norm, [(x2, d, 0), (o3, d, 0)], [kp["norm_ffn_post"], g_next],
                                   tb=TB_ROW, outs=[(d, F32), (d, BF16)])
        return x3, hn, sv, bw, next_w_in
    (dx2, do3), (loss, dg_post), _ = stage_fwd(f"loss_{l}", fn_loss_head, [(x2, d, 0), (o3, d, 0), (target, d, 0)],
                                               [kp["norm_ffn_post"]], tb=TB_ROW, outs=[(d, F32), (d, F32)],
                                               accs=[(8, LANES), (1, d)])
    sv["head"] = (dx2, do3, dg_post)
    return None, loss[0, 0], sv, bw, next_w_in


def _layer_backward(l, sv, mem, bw, kp, rot, ret_c, g_next, dx3, dhn, s5_side=None):
    d = sv["x"].shape[1]
    gk = {}
    parts = [None] * len(BIG)

    def send(i, g):
        axis = BIG[i][1] - 1
        if isinstance(g, tuple):
            return jnp.concatenate([_split_shards(p.astype(BF16), axis, N_DEV // len(g)) for p in g], axis=0)
        return _split_shards(g.astype(BF16), axis)

    def scatter(*items):
        return [send(i, g) for i, g in items], False

    def took(got, *idx):
        for i, p in zip(idx, got):
            parts[i] = p

    res_rows = [(sv["x2"], d, 0), (sv["o3"], d, 0)]
    if g_next is not None:
        (dx2, do3), (gk["norm_ffn_post"], gk["g_next"]) = stage_bwd(
            f"res_{l}_bwd", fn_res_norm, res_rows, [kp["norm_ffn_post"], g_next], [dx3, dhn], tb=TB_ROW,
            diff_rows=(0, 1), diff_params=(0, 1))
    else:
        dx2, do3, gk["norm_ffn_post"] = sv["head"]
    f = sv["a"].shape[1]
    g_down = matmul(f"ffn_down_dw_{l}", sv["a"], do3, "tn", BF16)
    d_gate, d_up = matmul_fused(f"ffn_down_dx_{l}", do3, [(bw["ffn_w_down"], 0)], "nt", f, [sv["gate"], sv["up"]],
                                _glu_bwd_tiles, (BF16, BF16))
    dh3, got = matmul(f"ffn_gate_dx_{l}", d_gate, bw["ffn_w_gu"][:, :f], "nt", F32, side=scatter((8, g_down)))
    took(got, 8)
    dh3 = matmul(f"ffn_up_dx_{l}", d_up, bw["ffn_w_gu"][:, f:], "nt", F32, add=dh3)
    g_gu = (matmul(f"ffn_gate_dw_{l}", sv["h3"], d_gate, "tn", BF16),
            matmul(f"ffn_up_dw_{l}", sv["h3"], d_up, "tn", BF16))
    (dx1,), xa_g, got = stage_bwd(f"xattn_{l}_bwd", fn_xattn, [(sv["x1"], d, 0)], sv["xa_params"], [dx2, dh3],
                                  tb=TB_XA, diff_rows=(0,), diff_params=tuple(range(7)), side=scatter((7, g_gu)))
    took(got, 7)
    gk["norm_xa_pre"], g_q, dk, dv, g_o, gk["norm_xa_post"], gk["norm_ffn_pre"] = xa_g
    _, (gk["norm_mem"], dwk, dwv) = stage_bwd(f"mem_{l}_bwd", fn_mem, [(mem, d, 0)], sv["mem_params"], [dk, dv],
                                              tb=mem.shape[0], diff_params=(0, 1, 2))
    g_kv = jnp.concatenate([dwk, dwv], axis=1)
    merge_d, merge_g, got = stage_bwd(f"merge_{l}_bwd", fn_merge, sv["merge_rows"], sv["merge_params"], [dx1],
                                      tb=TB_ROW, diff_rows=tuple(range(6)), diff_params=tuple(range(7)),
                                      row_dtypes=(F32, F32, F32, F32, BF16, F32),
                                      side=scatter((4, g_q), (6, g_o), (5, g_kv)))
    took(got, 4, 6, 5)
    dys, dgl, dx0 = merge_d[:4], merge_d[4], merge_d[5]
    g_up = jnp.stack(merge_g[:4])
    gk["b_gate"], g_out, gk["norm_mix_post"] = merge_g[4:]
    ops = _mixer_operands(l, sv["z"], kp, rot, ret_c)
    dz = None
    for key, dy in zip(("hg", "ret", "lru", "s5"), dys):
        o = ops[key]
        res = stage_bwd(o["name"] + "_bwd", o["fn"], o["rows"], o["params"], [dy], tb=o["tb"],
                        saved=sv[key + "_states"], diff_rows=(0,), diff_params=o["diff_params"], row_dtypes=(BF16,),
                        side=scatter((2, g_up), (3, g_out)) if key == "hg" else s5_side if key == "s5" else None,
                        into=(dz, sv["z"].shape[1]))
        if key == "hg":
            took(res[2], 2, 3)
        elif key == "s5" and s5_side is not None:
            gk["s5_side"] = res[2]
        dz = res[0][0]
        gk[key] = res[1]
    g_gate = matmul(f"gate_proj_dw_{l}", sv["h"], dgl, "tn", BF16)
    g_in = matmul(f"in_proj_dw_{l}", sv["h"], dz, "tn", BF16)
    dh, got = matmul(f"gate_proj_dx_{l}", dgl, bw["w_gate"], "nt", F32, side=scatter((1, g_gate)))
    took(got, 1)
    dh, got = matmul(f"in_proj_dx_{l}", dz, bw["w_in"], "nt", F32, add=dh, side=scatter((0, g_in)))
    took(got, 0)
    return dx0, dh, gk, parts


def _kernel_grads_to_prep(gk):
    hg, ret, lru, s5 = gk["hg"], gk["ret"], gk["lru"], gk["s5"]
    return dict(
        lb=hg[0], hg_norm=hg[1], ret_norm=ret[0],
        conv_w=lru[0], conv_b=lru[1], wa=lru[2], ba=lru[3], wx=lru[4], bx=lru[5], sp=lru[6],
        bt_re=s5[0], bt_im=s5[1], lb_re=s5[2], lb_im=s5[3], ct_re=s5[4], ct_im=s5[5], s5_d=s5[6], glu_w=s5[7],
        glu_b=s5[8], b_gate=gk["b_gate"], norm_mix_pre=gk["norm_mix_pre"], norm_mix_post=gk["norm_mix_post"],
        norm_xa_pre=gk["norm_xa_pre"], norm_xa_post=gk["norm_xa_post"], norm_mem=gk["norm_mem"],
        norm_ffn_pre=gk["norm_ffn_pre"], norm_ffn_post=gk["norm_ffn_post"],
    )


def _step(inp):
    x, mem, target = inp["x"][0], inp["mem"][0], inp["loss_target"][0]
    seq = x.shape[0]
    depth = inp["w_in"].shape[0]
    me = 4 * lax.axis_index("x") + 2 * lax.axis_index("y") + lax.axis_index("c")

    small_shapes = [inp[n].shape for n in _SMALL_SHARDED_NAMES]
    (small_stacked,) = _exchange("gather_small", [_pack([inp[n] for n in _SMALL_SHARDED_NAMES], F32, 8)], True)
    small_all = _unpack_stacked(small_stacked, small_shapes)
    full_small = {n: _merge_shards(s, ax) for (n, ax), s in zip(SMALL_SHARDED, small_all)}

    lbs, lbs_vjp = jax.vjp(_lower_bounds, inp["hg_lower_bounds"])
    kps, prep_vjps = [], []
    for l in range(depth):
        p = {n: (full_small[n][l] if n in full_small else inp[n][l]) for n in _PREP_INPUTS}
        p["lb"] = lbs[l]
        kp, vj = jax.vjp(_prep_layer, p)
        kps.append(kp)
        prep_vjps.append(vj)
    rot = _rotary_tables(seq)
    ret_c = _retention_constants()

    def shards(l):
        return [inp[n][l].astype(BF16) for n in _BIG_NAMES]

    (h,), _, _, (stacked,) = stage_fwd("norm_in_fwd", fn_norm, [(x, x.shape[1], 0)], [kps[0]["norm_mix_pre"]],
                                       tb=TB_ROW, outs=[(x.shape[1], BF16)], side=([shards(0)[0]], True))
    saved, bws = [], []
    xs = x
    w_in = _merge_shards(stacked, BIG[0][1] - 1)
    for l in range(depth):
        last = l + 1 == depth
        g_next = None if last else kps[l + 1]["norm_mix_pre"]
        xs, h, sv, bw, w_in = _layer_forward(l, xs, h, mem, w_in, shards(l), None if last else shards(l + 1)[0],
                                             kps[l], rot, ret_c, g_next, target)
        saved.append(sv)
        bws.append(bw)
    dy, loss_local = xs, h

    big_parts = [None] * depth
    gks = [None] * depth
    dx, dh = dy, None
    small_names = REPLICATED + _SMALL_SHARDED_NAMES
    cross_layer = ("hg_lower_bounds", "norm_mix_pre")
    groups = {}
    for n in small_names:
        if n not in cross_layer:
            groups.setdefault(full_small[n].shape[1:] if n in full_small else inp[n].shape[1:], []).append(n)

    def layer_small_grads(l):
        gk = dict(gks[l], norm_mix_pre=jnp.zeros_like(kps[l]["norm_mix_pre"]))
        (gp,) = prep_vjps[l]({k: g.astype(kps[l][k].dtype) for k, g in _kernel_grads_to_prep(gk).items()})
        return gp["lb"], [jnp.stack([gp[n].astype(BF16)[None] for n in names]) for names in groups.values()]

    d_lbs = [None] * depth
    gathered = [None] * depth
    pending = None
    for l in reversed(range(depth)):
        g_next = kps[l + 1]["norm_mix_pre"] if l + 1 < depth else None
        dx, dh, gks[l], big_parts[l] = _layer_backward(l, saved[l], mem, bws[l], kps[l], rot, ret_c, g_next, dx, dh,
                                                       s5_side=(pending, True) if pending is not None else None)
        if pending is not None:
            gathered[l + 1] = list(gks[l]["s5_side"])
        d_lbs[l], pending = layer_small_grads(l)
    (grad_x,), (g_pre0,) = stage_bwd("norm_in_bwd", fn_keep_norm, [(x, x.shape[1], 0)], [kps[0]["norm_mix_pre"]],
                                     [dx, dh], tb=TB_ROW, diff_rows=(0,), diff_params=(0,))
    (g_lower_bounds,) = lbs_vjp(jnp.stack(d_lbs))
    g_mix_pre = jnp.concatenate([g_pre0] + [gks[l]["g_next"] for l in range(depth - 1)], axis=0)
    small_send = pending + [g_lower_bounds.astype(BF16)[None], g_mix_pre.astype(BF16)[None]]

    out = {}
    kinds = ("grad_", "delta_", "new_m_", "new_v_")
    small_parts = None
    for i, n in sorted(enumerate(_BIG_NAMES), key=lambda t: -int(np.prod(inp[t[1]].shape))):
        shape = inp[n].shape
        three = (shape[0], int(np.prod(shape[1:-1])), shape[-1])
        res = adamw("adamw_" + n, *[inp[pre + n].reshape(three) for pre in ("", "m_", "v_")],
                    [big_parts[l][i].reshape((N_DEV,) + three[1:]) for l in range(depth)],
                    side=(small_send, True) if small_parts is None else None)
        if small_parts is None:
            res, small_parts = res
        for kind, a in zip(kinds, res):
            out[kind + n] = a.reshape(shape)
    ng = len(groups)
    gathered[0] = list(small_parts[:ng])
    place = {n: (gi, j) for gi, names in enumerate(groups.values()) for j, n in enumerate(names)}
    for n, ax in SMALL_SHARDED:
        gi, j = place[n]
        assert len(list(groups.values())[gi]) == 1
        width = inp[n].shape[ax]
        for l in range(depth):
            gathered[l][gi] = lax.dynamic_slice_in_dim(gathered[l][gi], me * width, width, axis=ax + 2)
    garrays = [a for l in range(depth) for a in gathered[l]] + list(small_parts[ng:])
    where = [[(depth * ng + cross_layer.index(n), 0, None)] if n in cross_layer else
             [(l * ng + place[n][0], place[n][1], l) for l in range(depth)] for n in small_names]
    res = adamw_many("adamw_small", *[[inp[pre + n] for n in small_names] for pre in ("", "m_", "v_")], garrays, where)
    for n, quad in zip(small_names, res):
        for kind, a in zip(kinds, quad):
            out[kind + n] = a

    out["loss"] = lax.psum(loss_local, ("x", "y", "c"))
    out["grad_x"] = grad_x[None]
    return out


def kernel(x, mem, hg_lower_bounds, norm_mix_pre, norm_mix_post, w_in, w_gate, b_gate, hg_norm, ret_norm, lru_conv_w, lru_conv_b, lru_wa, lru_ba, lru_wx, lru_bx, lru_lambda, s5_lam_re, s5_lam_im, s5_b_re, s5_b_im, s5_c_re, s5_c_im, s5_d, s5_log_dt, s5_glu_w, s5_glu_b, w_up, w_out, norm_xa_pre, norm_xa_post, norm_mem, xa_w_q, xa_w_kv, xa_w_o, norm_ffn_pre, norm_ffn_post, ffn_w_gu, ffn_w_down, loss_target, m_hg_lower_bounds, m_norm_mix_pre, m_norm_mix_post, m_w_in, m_w_gate, m_b_gate, m_hg_norm, m_ret_norm, m_lru_conv_w, m_lru_conv_b, m_lru_wa, m_lru_ba, m_lru_wx, m_lru_bx, m_lru_lambda, m_s5_lam_re, m_s5_lam_im, m_s5_b_re, m_s5_b_im, m_s5_c_re, m_s5_c_im, m_s5_d, m_s5_log_dt, m_s5_glu_w, m_s5_glu_b, m_w_up, m_w_out, m_norm_xa_pre, m_norm_xa_post, m_norm_mem, m_xa_w_q, m_xa_w_kv, m_xa_w_o, m_norm_ffn_pre, m_norm_ffn_post, m_ffn_w_gu, m_ffn_w_down, v_hg_lower_bounds, v_norm_mix_pre, v_norm_mix_post, v_w_in, v_w_gate, v_b_gate, v_hg_norm, v_ret_norm, v_lru_conv_w, v_lru_conv_b, v_lru_wa, v_lru_ba, v_lru_wx, v_lru_bx, v_lru_lambda, v_s5_lam_re, v_s5_lam_im, v_s5_b_re, v_s5_b_im, v_s5_c_re, v_s5_c_im, v_s5_d, v_s5_log_dt, v_s5_glu_w, v_s5_glu_b, v_w_up, v_w_out, v_norm_xa_pre, v_norm_xa_post, v_norm_mem, v_xa_w_q, v_xa_w_kv, v_xa_w_o, v_norm_ffn_pre, v_norm_ffn_post, v_ffn_w_gu, v_ffn_w_down):
    values = (x, mem, hg_lower_bounds, norm_mix_pre, norm_mix_post, w_in, w_gate, b_gate, hg_norm, ret_norm, lru_conv_w, lru_conv_b, lru_wa, lru_ba, lru_wx, lru_bx, lru_lambda, s5_lam_re, s5_lam_im, s5_b_re, s5_b_im, s5_c_re, s5_c_im, s5_d, s5_log_dt, s5_glu_w, s5_glu_b, w_up, w_out, norm_xa_pre, norm_xa_post, norm_mem, xa_w_q, xa_w_kv, xa_w_o, norm_ffn_pre, norm_ffn_post, ffn_w_gu, ffn_w_down, loss_target, m_hg_lower_bounds, m_norm_mix_pre, m_norm_mix_post, m_w_in, m_w_gate, m_b_gate, m_hg_norm, m_ret_norm, m_lru_conv_w, m_lru_conv_b, m_lru_wa, m_lru_ba, m_lru_wx, m_lru_bx, m_lru_lambda, m_s5_lam_re, m_s5_lam_im, m_s5_b_re, m_s5_b_im, m_s5_c_re, m_s5_c_im, m_s5_d, m_s5_log_dt, m_s5_glu_w, m_s5_glu_b, m_w_up, m_w_out, m_norm_xa_pre, m_norm_xa_post, m_norm_mem, m_xa_w_q, m_xa_w_kv, m_xa_w_o, m_norm_ffn_pre, m_norm_ffn_post, m_ffn_w_gu, m_ffn_w_down, v_hg_lower_bounds, v_norm_mix_pre, v_norm_mix_post, v_w_in, v_w_gate, v_b_gate, v_hg_norm, v_ret_norm, v_lru_conv_w, v_lru_conv_b, v_lru_wa, v_lru_ba, v_lru_wx, v_lru_bx, v_lru_lambda, v_s5_lam_re, v_s5_lam_im, v_s5_b_re, v_s5_b_im, v_s5_c_re, v_s5_c_im, v_s5_d, v_s5_log_dt, v_s5_glu_w, v_s5_glu_b, v_w_up, v_w_out, v_norm_xa_pre, v_norm_xa_post, v_norm_mem, v_xa_w_q, v_xa_w_kv, v_xa_w_o, v_norm_ffn_pre, v_norm_ffn_post, v_ffn_w_gu, v_ffn_w_down)
    names = ("x", "mem") + WEIGHTS + ("loss_target",) + tuple("m_" + n for n in WEIGHTS) + tuple("v_" + n for n in WEIGHTS)
    out = _step(dict(zip(names, values)))
    order = ["loss", "grad_x"] + [k + n for k in ("grad_", "delta_", "new_m_", "new_v_") for n in WEIGHTS]
    return tuple(out[k] for k in order)
```

```python
import functools

import numpy as np
import jax
import jax.numpy as jnp
from jax import lax
from jax.experimental import pallas as pl
from jax.experimental.pallas import tpu as pltpu

F32 = jnp.float32
BF16 = jnp.bfloat16
EPS = 1e-6
N_DEV = 8
LANES = 128
SUBLANES = 8
VMEM_LIMIT = 60 * 1024 * 1024

HEADS = 4
HEAD_DIM = 64
MIX_W = HEADS * HEAD_DIM
HG_CHUNK = 32
RET_CHUNK = 128
S5_GROUPS = 16
S5_GROUP = 16
S5_STATE = 64
LRU_C = 8.0
XA_HEADS = 4

ADAM_LR = 0.001
ADAM_B1 = 0.9
ADAM_B2 = 0.999
ADAM_EPS = 1e-08
ADAM_WD = 0.01
ADAM_STEP = 10

BIG = (("w_in", 2), ("w_gate", 2), ("w_up", 3), ("w_out", 1), ("xa_w_q", 1), ("xa_w_kv", 2), ("xa_w_o", 1),
       ("ffn_w_gu", 2), ("ffn_w_down", 1))
SMALL_SHARDED = (("lru_conv_w", 2), ("s5_glu_w", 1))
WEIGHTS = ("hg_lower_bounds", "norm_mix_pre", "norm_mix_post", "w_in", "w_gate", "b_gate", "hg_norm", "ret_norm",
           "lru_conv_w", "lru_conv_b", "lru_wa", "lru_ba", "lru_wx", "lru_bx", "lru_lambda", "s5_lam_re", "s5_lam_im",
           "s5_b_re", "s5_b_im", "s5_c_re", "s5_c_im", "s5_d", "s5_log_dt", "s5_glu_w", "s5_glu_b", "w_up", "w_out",
           "norm_xa_pre", "norm_xa_post", "norm_mem", "xa_w_q", "xa_w_kv", "xa_w_o", "norm_ffn_pre", "norm_ffn_post",
           "ffn_w_gu", "ffn_w_down")
GATHER_IN = {"in_proj": (1,), "gate_proj": (7,), "hg": (5,), "s5": (2, 3, 4, 6), "merge": (8,)}
_BIG_NAMES = tuple(n for n, _ in BIG)
_SMALL_SHARDED_NAMES = tuple(n for n, _ in SMALL_SHARDED)
REPLICATED = tuple(n for n in WEIGHTS if n not in _BIG_NAMES and n not in _SMALL_SHARDED_NAMES)


def _dot(a, b):
    return jnp.dot(a.astype(BF16), b.astype(BF16), preferred_element_type=F32)


def _dot_nt(a, b):
    return lax.dot_general(a.astype(BF16), b.astype(BF16), (((1,), (1,)), ((), ())), preferred_element_type=F32)


def _dot_tn(a, b):
    return lax.dot_general(a.astype(BF16), b.astype(BF16), (((0,), (0,)), ((), ())), preferred_element_type=F32)


def _dot_exact(a, b):
    return jnp.dot(a, b, precision=lax.Precision.HIGHEST, preferred_element_type=F32)


def _rms(x, g):
    return x * lax.rsqrt(jnp.mean(x * x, axis=-1, keepdims=True) + EPS) * g


def _shift_down(x, d, fill):
    return jnp.concatenate([jnp.full((d, x.shape[1]), fill, x.dtype), x[:-d]], axis=0)


def _shift_up(x, d, fill):
    return jnp.concatenate([x[d:], jnp.full((d, x.shape[1]), fill, x.dtype)], axis=0)


def _cumsum_rows(x):
    d = 1
    while d < x.shape[0]:
        x = x + _shift_down(x, d, 0.0)
        d *= 2
    return x


def _lane_head(shape, dim):
    return lax.shift_right_logical(lax.broadcasted_iota(jnp.int32, shape, dim), 6)


def _head_masks(width=MIX_W):
    head = _lane_head((1, width), 1)
    return [(head == h).astype(F32) for h in range(HEADS)]


def _block_diag_mask():
    return (_lane_head((MIX_W, MIX_W), 0) == _lane_head((MIX_W, MIX_W), 1)).astype(F32)


def _head_rms(o, g):
    ms = _dot_exact(o * o, _block_diag_mask()) * (1.0 / HEAD_DIM)
    return o * lax.rsqrt(ms + EPS) * g


def _swap_pairs(x):
    lane = lax.broadcasted_iota(jnp.int32, x.shape, 1)
    return jnp.where((lane & 1) == 0, jnp.roll(x, -1, axis=1), jnp.roll(x, 1, axis=1))


def _stack_heads(t, masks):
    return jnp.concatenate([t * m for m in masks], axis=0)


@jax.custom_vjp
def _real_scan(a, u, h0):
    return _real_scan_fwd(a, u, h0)[0]


def _real_scan_fwd(a, u, h0):
    t = a.shape[0]
    acc_a, acc_u = a, u
    d = 1
    while d < t:
        acc_u = acc_u + acc_a * _shift_down(acc_u, d, 0.0)
        acc_a = acc_a * _shift_down(acc_a, d, 1.0)
        d *= 2
    h = acc_u + acc_a * h0
    return h, (a, h, h0)


def _real_scan_bwd(res, dh):
    a, h, h0 = res
    t = a.shape[0]
    acc_a = _shift_up(a, 1, 0.0)
    g = dh
    d = 1
    while d < t:
        g = g + acc_a * _shift_up(g, d, 0.0)
        acc_a = acc_a * _shift_up(acc_a, d, 1.0)
        d *= 2
    h_prev = jnp.concatenate([h0, h[:-1]], axis=0)
    return g * h_prev, g, (a * g)[0:1]


_real_scan.defvjp(_real_scan_fwd, _real_scan_bwd)


def _cmul(ar, ai, br, bi):
    return ar * br - ai * bi, ar * bi + ai * br


def _geometric_sums(ar, ai, ur, ui, forward):
    shift = _shift_down if forward else _shift_up
    t = ur.shape[0]
    g = SUBLANES
    in_group = lax.broadcasted_iota(jnp.int32, ur.shape, 0) & (g - 1)
    pr, pi, sr, si = ar, ai, ur, ui
    d = 1
    while d < g:
        keep = in_group >= d if forward else in_group < g - d
        mr, mi = _cmul(pr, pi, jnp.where(keep, shift(sr, d, 0.0), 0.0), jnp.where(keep, shift(si, d, 0.0), 0.0))
        sr, si = sr + mr, si + mi
        pr, pi = _cmul(pr, pi, pr, pi)
        d *= 2
    row = lax.broadcasted_iota(jnp.int32, (g, ur.shape[1]), 0)
    qr, qi = ar, ai
    tr, ti = jnp.zeros((g, ur.shape[1]), F32), jnp.zeros((g, ur.shape[1]), F32)
    for r in range(g):
        here = row == (r if forward else g - 1 - r)
        tr, ti = jnp.where(here, qr, tr), jnp.where(here, qi, ti)
        qr, qi = _cmul(qr, qi, ar, ai)
    outs_r, outs_i = [], []
    cr = ci = None
    order = range(t // g) if forward else reversed(range(t // g))
    for n in order:
        br, bi = sr[n * g:(n + 1) * g], si[n * g:(n + 1) * g]
        if cr is not None:
            mr, mi = _cmul(tr, ti, cr, ci)
            br, bi = br + mr, bi + mi
        edge = slice(g - 1, g) if forward else slice(0, 1)
        cr, ci = br[edge], bi[edge]
        outs_r.append(br)
        outs_i.append(bi)
    if not forward:
        outs_r.reverse()
        outs_i.reverse()
    return jnp.concatenate(outs_r, axis=0), jnp.concatenate(outs_i, axis=0)


@jax.custom_vjp
def _complex_scan(ar, ai, ur, ui, h0r, h0i):
    return _complex_scan_fwd(ar, ai, ur, ui, h0r, h0i)[0]


def _complex_scan_fwd(ar, ai, ur, ui, h0r, h0i):
    first = lax.broadcasted_iota(jnp.int32, ur.shape, 0) == 0
    cr, ci = _cmul(ar, ai, h0r, h0i)
    hr, hi = _geometric_sums(ar, ai, ur + jnp.where(first, cr, 0.0), ui + jnp.where(first, ci, 0.0), True)
    return (hr, hi), (ar, ai, hr, hi, h0r, h0i)


def _complex_scan_bwd(res, dh):
    ar, ai, hr, hi, h0r, h0i = res
    gr, gi = _geometric_sums(ar, -ai, dh[0], dh[1], False)
    qr = jnp.concatenate([h0r, hr[:-1]], axis=0)
    qi = jnp.concatenate([h0i, hi[:-1]], axis=0)
    dar = jnp.sum(gr * qr + gi * qi, axis=0, keepdims=True)
    dai = jnp.sum(gi * qr - gr * qi, axis=0, keepdims=True)
    d0r, d0i = _cmul(ar, -ai, gr[0:1], gi[0:1])
    return dar, dai, gr, gi, d0r, d0i


_complex_scan.defvjp(_complex_scan_fwd, _complex_scan_bwd)


def fn_norm(st, rows, params):
    (x,), (g,) = rows, params
    return (), (_rms(x, g),), ()


def fn_keep_norm(st, rows, params):
    (x,), (g,) = rows, params
    return (), (x, _rms(x, g)), ()


def fn_hgrn2(st, rows, params):
    (state,) = st
    (z,) = rows
    lb, norm_g = params
    q, f_logit, v_all, g = (z[:, k * MIX_W:(k + 1) * MIX_W] for k in range(4))
    f = lb + (1.0 - lb) * jax.nn.sigmoid(f_logit)
    log_f = jnp.log(f)
    k_all = 1.0 - f
    q_all = jax.nn.silu(q)
    masks = _head_masks()
    bd = _block_diag_mask()
    c = HG_CHUNK
    col = lax.broadcasted_iota(jnp.int32, (c, HEADS * c), 1) & (c - 1)
    causal = col <= lax.broadcasted_iota(jnp.int32, (c, HEADS * c), 0)
    outs = []
    for n in range(z.shape[0] // c):
        sl = slice(n * c, (n + 1) * c)
        lf = log_f[sl]
        b = _cumsum_rows(lf)
        b_end = jnp.sum(lf, axis=0, keepdims=True)
        q_dec = q_all[sl] * jnp.exp(b)
        k_inv = k_all[sl] * jnp.exp(-b)
        k_end = k_all[sl] * jnp.exp(b_end - b)
        v = v_all[sl]
        scores = jnp.where(causal, _dot_nt(q_dec, _stack_heads(k_inv, masks)), 0.0)
        outs.append(_dot(scores, _stack_heads(v, masks)) + _dot_nt(q_dec, state))
        state = state * jnp.exp(b_end) + _dot_tn(v, k_end) * bd
    o = jnp.concatenate(outs, axis=0) if len(outs) > 1 else outs[0]
    return (state,), (_head_rms(o, norm_g) * jax.nn.silu(g),), ()


def fn_retention(st, rows, params):
    (state,) = st
    z, cos_t, sin_t = rows
    norm_g, xi, zeta, decay, g_end = params
    q, k, v_all, g = (z[:, i * MIX_W:(i + 1) * MIX_W] for i in range(4))
    q_all = q * cos_t + _swap_pairs(q) * sin_t
    k_all = (k * cos_t + _swap_pairs(k) * sin_t) * (HEAD_DIM ** -0.5)
    masks = _head_masks()
    bd = _block_diag_mask()
    c = RET_CHUNK
    outs = []
    for n in range(z.shape[0] // c):
        sl = slice(n * c, (n + 1) * c)
        qc, kc, v = q_all[sl], k_all[sl], v_all[sl]
        scores = _dot_nt(qc, _stack_heads(kc, masks)) * decay
        outs.append(_dot(scores, _stack_heads(v, masks)) + _dot_nt(qc * xi, state))
        state = state * g_end + _dot_tn(v, kc * zeta) * bd
    o = jnp.concatenate(outs, axis=0) if len(outs) > 1 else outs[0]
    return (state,), (_head_rms(o, norm_g) * jax.nn.silu(g),), ()


def fn_rglru(st, rows, params):
    tail_x, tail_h = st
    (z,) = rows
    conv_w, conv_b, wa, ba, wx, bx, sp = params
    t = z.shape[0]
    xg, xi = z[:, :MIX_W], z[:, MIX_W:]
    full = jnp.concatenate([tail_x, xi], axis=0)
    xc = conv_b
    for k in range(4):
        xc = xc + conv_w[k:k + 1] * full[5 + k:5 + k + t]
    r = jax.nn.sigmoid(_dot(xc, wa) + ba)
    ig = jax.nn.sigmoid(_dot(xc, wx) + bx)
    log_a = -LRU_C * r * sp
    a = jnp.exp(log_a)
    one_minus_a2 = -jnp.tanh(log_a) * (a * a + 1.0)
    u = jnp.sqrt(one_minus_a2) * (ig * xc)
    h = _real_scan(a, u, tail_h[7:8])
    return (xi[t - 8:], h[t - 8:]), (h * jax.nn.gelu(xg),), ()


def fn_s5(st, rows, params):
    tail_r, tail_i = st
    (u,) = rows
    bt_re, bt_im, lb_re, lb_im, ct_re, ct_im, d, glu_w, glu_b = params
    t = u.shape[0]
    bu_re = _dot(u, bt_re)
    bu_im = _dot(u, bt_im)
    h_re, h_im = _complex_scan(lb_re, lb_im, bu_re, bu_im, tail_r[7:8], tail_i[7:8])
    y = _dot(h_re, ct_re) - _dot(h_im, ct_im) + d * u
    act = jax.nn.gelu(y)
    out = act * jax.nn.sigmoid(_dot(act, glu_w) + glu_b)
    return (h_re[t - 8:], h_im[t - 8:]), (out,), ()


def fn_mix(st, rows, params):
    ya, yb, yc, yd, gl = rows
    w0, w1, w2, w3, b_gate = params
    d = w0.shape[1]
    mix = None
    for n, (y, w) in enumerate(((ya, w0), (yb, w1), (yc, w2), (yd, w3))):
        gate = jax.nn.sigmoid(gl[:, n * d:(n + 1) * d] + b_gate[:, n * d:(n + 1) * d])
        term = gate * _dot(y, w)
        mix = term if mix is None else mix + term
    return (), (mix,), ()


def fn_out(st, rows, params):
    (mix, x), (w_out, g_post) = rows, params
    return (), (x + _rms(_dot(mix, w_out), g_post),), ()


def fn_mem(st, rows, params):
    (mem,), (g, wk, wv) = rows, params
    m = _rms(mem, g)
    return (), (_dot(m, wk), _dot(m, wv)), ()


def fn_xattn(st, rows, params):
    (x,) = rows
    g_pre, wq, k, v, wo, g_post, g_next = params
    d = x.shape[1]
    dh = d // XA_HEADS
    q = _dot(_rms(x, g_pre), wq)
    heads = []
    for h in range(XA_HEADS):
        sl = slice(h * dh, (h + 1) * dh)
        s = _dot_nt(q[:, sl], k[:, sl]) * (dh ** -0.5)
        heads.append(_dot(jax.nn.softmax(s, axis=-1), v[:, sl]))
    x2 = x + _rms(_dot(jnp.concatenate(heads, axis=1), wo), g_post)
    return (), (x2, _rms(x2, g_next)), ()


def fn_res_norm(st, rows, params):
    (x, o), (g_post, g_next) = rows, params
    xn = x + _rms(o, g_post)
    return (), (xn, _rms(xn, g_next)), ()


def fn_loss_head(st, rows, params):
    (x, o, target), (g_post,) = rows, params
    y, vjp = jax.vjp(lambda o_, g_: x + _rms(o_, g_), o, g_post)
    err = y - target
    inv_d = 1.0 / x.shape[1]
    dy = err * inv_d
    do, dg = vjp(dy)
    loss = 0.5 * inv_d * jnp.sum(err * err)
    return (), (dy, do), (jnp.full((8, LANES), loss, F32), dg)


def _params():
    return pltpu.CompilerParams(dimension_semantics=("arbitrary",), vmem_limit_bytes=VMEM_LIMIT)


def _row_spec(tb, width, colblk, nb, reverse):
    if reverse:
        return pl.BlockSpec((tb, width), lambda i: (nb - 1 - i, colblk))
    return pl.BlockSpec((tb, width), lambda i: (i, colblk))


def _full_spec(shape):
    return pl.BlockSpec(shape, lambda i: (0,) * len(shape), pipeline_mode=pl.Buffered(1))


def _saved_spec(shape, nb, reverse):
    if reverse:
        return pl.BlockSpec((1,) + shape, lambda i: (nb - 1 - i, 0, 0))
    return pl.BlockSpec((1,) + shape, lambda i: (i, 0, 0))


def _param_value(ref):
    v = ref[...]
    return v if v.dtype == BF16 else v.astype(F32)


def _side(side):
    arrays, gather = side if side is not None else ((), False)
    return list(arrays), gather, len(arrays)


_ANY = pl.BlockSpec(memory_space=pl.ANY)


def stage_fwd(name, fn, rows, params, *, tb, outs, states=(), accs=(), side=None):
    n_rows = rows[0][0].shape[0]
    nb = n_rows // tb
    nr, npar, no, na, ns = len(rows), len(params), len(outs), len(accs), len(states)
    side_arrays, gather, nx = _side(side)

    def body(*refs):
        row_refs, par_refs, side_in = refs[:nr], refs[nr:nr + npar], refs[nr + npar:nr + npar + nx]
        o = nr + npar + nx
        out_refs, acc_refs = refs[o:o + no], refs[o + no:o + no + na]
        saved_refs = refs[o + no + na:o + no + na + ns]
        o = o + no + na + ns
        side_out, st_refs, sems = refs[o:o + nx], refs[o + nx:o + nx + ns], refs[o + nx + ns:]
        i = pl.program_id(0)
        if nx:
            start, wait = _exchange_copies(side_in, side_out, *sems, gather)
            pl.when(i == 0)(start)

        @pl.when(i == 0)
        def _():
            for r in st_refs + acc_refs:
                r[...] = jnp.zeros_like(r)

        st = tuple(r[...] for r in st_refs)
        for sv, s in zip(saved_refs, st):
            sv[0] = s
        new_st, out_vals, acc_vals = fn(st, tuple(r[...].astype(F32) for r in row_refs),
                                        tuple(_param_value(r) for r in par_refs))
        for r, v in zip(out_refs, out_vals):
            r[...] = v.astype(r.dtype)
        for r, v in zip(acc_refs, acc_vals):
            r[...] += v
        for r, v in zip(st_refs, new_st):
            r[...] = v
        if nx:
            pl.when(i == nb - 1)(wait)

    res = pl.pallas_call(
        body, name=name, grid=(nb,),
        in_specs=[_row_spec(tb, w, cb, nb, False) for _, w, cb in rows] + [_full_spec(p.shape) for p in params]
        + [_ANY] * nx,
        out_specs=[_row_spec(tb, w, 0, nb, False) for w, _ in outs] + [_full_spec(s) for s in accs]
        + [_saved_spec(s, nb, False) for s in states] + [_ANY] * nx,
        out_shape=[jax.ShapeDtypeStruct((n_rows, w), dt) for w, dt in outs]
        + [jax.ShapeDtypeStruct(s, F32) for s in accs] + [jax.ShapeDtypeStruct((nb,) + s, F32) for s in states]
        + _exchange_shapes(side_arrays, gather),
        scratch_shapes=[pltpu.VMEM(s, F32) for s in states] + (_exchange_sems(nx) if nx else []),
        compiler_params=_params(),
    )(*[a for a, _, _ in rows], *params, *side_arrays)
    base = (res[:no], res[no:no + na], res[no + na:no + na + ns])
    return base + (res[no + na + ns:],) if nx else base


def stage_bwd(name, fn, rows, params, cts, *, tb, saved=(), diff_rows=(), diff_params=(), row_dtypes=None, side=None,
              into=None):
    n_rows = rows[0][0].shape[0]
    nb = n_rows // tb
    nr, npar, ns, nc = len(rows), len(params), len(saved), len(cts)
    ndr, ndp = len(diff_rows), len(diff_params)
    row_dtypes = row_dtypes or (F32,) * ndr
    state_shapes = [s.shape[1:] for s in saved]
    side_arrays, gather, nx = _side(side)
    into_buffer = [into[0]] if into is not None and into[0] is not None else []
    na = len(into_buffer)
    if into is not None:
        assert ndr == 1
        drow_specs = [_row_spec(tb, rows[diff_rows[0]][1], rows[diff_rows[0]][2], nb, True)]
        drow_shapes = [jax.ShapeDtypeStruct((n_rows, into[1]), row_dtypes[0])]
    else:
        drow_specs = [_row_spec(tb, rows[k][1], 0, nb, True) for k in diff_rows]
        drow_shapes = [jax.ShapeDtypeStruct((n_rows, rows[k][1]), dt) for k, dt in zip(diff_rows, row_dtypes)]

    def body(*refs):
        row_refs, par_refs = refs[:nr], refs[nr:nr + npar]
        o = nr + npar
        saved_refs, ct_refs, side_in = refs[o:o + ns], refs[o + ns:o + ns + nc], refs[o + ns + nc:o + ns + nc + nx]
        o = o + ns + nc + nx + na
        drow_refs, dpar_refs, side_out = refs[o:o + ndr], refs[o + ndr:o + ndr + ndp], refs[o + ndr + ndp:o + ndr + ndp + nx]
        o = o + ndr + ndp + nx
        dst_refs, sems = refs[o:o + ns], refs[o + ns:]
        i = pl.program_id(0)
        if nx:
            start, wait = _exchange_copies(side_in, side_out, *sems, gather)
            pl.when(i == 0)(start)

        @pl.when(i == 0)
        def _():
            for r in dst_refs + dpar_refs:
                r[...] = jnp.zeros_like(r)

        st = tuple(r[0] for r in saved_refs)
        row_vals = [r[...].astype(F32) for r in row_refs]
        par_vals = [_param_value(r) for r in par_refs]

        def f(st_, dr_, dp_):
            rv, pv = list(row_vals), list(par_vals)
            for k, v in zip(diff_rows, dr_):
                rv[k] = v
            for k, v in zip(diff_params, dp_):
                pv[k] = v
            new_st, out_vals, _ = fn(st_, tuple(rv), tuple(pv))
            return new_st, out_vals

        _, vjp = jax.vjp(f, st, tuple(row_vals[k] for k in diff_rows), tuple(par_vals[k] for k in diff_params))
        g_st, g_rows, g_par = vjp((tuple(r[...] for r in dst_refs), tuple(r[...].astype(F32) for r in ct_refs)))
        for r, v in zip(drow_refs, g_rows):
            r[...] = v.astype(r.dtype)
        for r, v in zip(dpar_refs, g_par):
            r[...] += v
        for r, v in zip(dst_refs, g_st):
            r[...] = v
        if nx:
            pl.when(i == nb - 1)(wait)

    res = pl.pallas_call(
        body, name=name, grid=(nb,),
        in_specs=[_row_spec(tb, w, cb, nb, True) for _, w, cb in rows] + [_full_spec(p.shape) for p in params]
        + [_saved_spec(s, nb, True) for s in state_shapes] + [_row_spec(tb, c.shape[1], 0, nb, True) for c in cts]
        + [_ANY] * (nx + na),
        out_specs=drow_specs + [_full_spec(params[k].shape) for k in diff_params] + [_ANY] * nx,
        out_shape=drow_shapes + [jax.ShapeDtypeStruct(params[k].shape, F32) for k in diff_params]
        + _exchange_shapes(side_arrays, gather),
        scratch_shapes=[pltpu.VMEM(s, F32) for s in state_shapes] + (_exchange_sems(nx) if nx else []),
        input_output_aliases={nr + npar + ns + nc + nx: 0} if na else {},
        compiler_params=_params(),
    )(*[a for a, _, _ in rows], *params, *saved, *cts, *side_arrays, *into_buffer)
    base = (res[:ndr], res[ndr:ndr + ndp])
    return base + (res[ndr + ndp:],) if nx else base


def _pick(n, target):
    if n <= target:
        return n
    best = None
    for t in range(LANES, target + 1, LANES):
        if n % t == 0:
            best = t
    assert best is not None, n
    return best


def _mesh_position():
    return lax.axis_index("x"), lax.axis_index("y"), lax.axis_index("c")


def _peer(pos, k):
    x, y, c = pos
    px = 1 - x if k & 4 else x
    py = 1 - y if k & 2 else y
    pc = 1 - c if k & 1 else c
    return (px, py, pc), 4 * px + 2 * py + pc


def _exchange_copies(x_refs, o_refs, send_sems, recv_sems, local_sems, gather):
    pos = _mesh_position()
    me = 4 * pos[0] + 2 * pos[1] + pos[2]
    pairs = list(enumerate(zip(x_refs, o_refs)))

    def remote(k, a, src, dst):
        peer, _ = _peer(pos, k)
        return pltpu.make_async_remote_copy(src_ref=src, dst_ref=dst, send_sem=send_sems.at[k - 1, a],
                                            recv_sem=recv_sems.at[k - 1, a], device_id=peer,
                                            device_id_type=pl.DeviceIdType.MESH)

    def local(a, x, o):
        return pltpu.make_async_copy(x if gather else x.at[me], o.at[me], local_sems.at[a])

    def start():
        for a, (x, o) in pairs:
            local(a, x, o).start()
        for k in range(1, N_DEV):
            peer_idx = _peer(pos, k)[1]
            for a, (x, o) in pairs:
                remote(k, a, x if gather else x.at[peer_idx], o.at[me]).start()

    def wait():
        for k in range(1, N_DEV):
            peer_idx = _peer(pos, k)[1]
            for a, (x, o) in pairs:
                arrival = remote(k, a, x if gather else x.at[me], o.at[peer_idx])
                arrival.wait_recv()
                arrival.wait_send()
        for a, (x, o) in pairs:
            local(a, x, o).wait()

    return start, wait


def _exchange_shapes(arrays, gather):
    return [jax.ShapeDtypeStruct(((N_DEV,) + x.shape) if gather else x.shape, x.dtype) for x in arrays]


def _exchange_sems(n):
    return [pltpu.SemaphoreType.DMA((N_DEV - 1, n)), pltpu.SemaphoreType.DMA((N_DEV - 1, n)),
            pltpu.SemaphoreType.DMA((n,))]


def _exchange(name, arrays, gather):
    n = len(arrays)

    def body(*refs):
        start, wait = _exchange_copies(refs[:n], refs[n:2 * n], *refs[2 * n:], gather)
        start()
        wait()

    return pl.pallas_call(
        body, name=name,
        in_specs=[pl.BlockSpec(memory_space=pl.ANY)] * n, out_specs=[pl.BlockSpec(memory_space=pl.ANY)] * n,
        out_shape=_exchange_shapes(arrays, gather), scratch_shapes=_exchange_sems(n),
    )(*arrays)


def _pick_n(n):
    return 1408 if n % 1408 == 0 else _pick(n, 512)


def matmul(name, a, b, mode, out_dtype, add=None, side=None):
    if mode == "tn":
        k, m = a.shape
    else:
        m, k = a.shape
    n = b.shape[0] if mode == "nt" else b.shape[1]
    if mode == "tn":
        tm, tn, tk = _pick(m, 1408), _pick(n, 1408), _pick(k, 1024)
    else:
        tm, tn, tk = _pick(m, 1024), _pick_n(n), _pick(k, 2816)
    nk = k // tk
    grid = (m // tm, n // tn, nk)
    a_spec = pl.BlockSpec((tk, tm), lambda i, j, kk: (kk, i)) if mode == "tn" else pl.BlockSpec((tm, tk), lambda i, j, kk: (i, kk))
    b_spec = pl.BlockSpec((tn, tk), lambda i, j, kk: (j, kk)) if mode == "nt" else pl.BlockSpec((tk, tn), lambda i, j, kk: (kk, j))
    o_spec = pl.BlockSpec((tm, tn), lambda i, j, kk: (i, j))
    dims = {"nn": (((1,), (0,)), ((), ())), "nt": (((1,), (1,)), ((), ())), "tn": (((0,), (0,)), ((), ()))}[mode]
    has_add = add is not None
    side_arrays, gather = side if side is not None else ((), False)
    ns = len(side_arrays)
    n_in = 2 + has_add

    def body(*refs):
        a_ref, b_ref = refs[0], refs[1]
        side_in, o_ref, side_out = refs[n_in:n_in + ns], refs[n_in + ns], refs[n_in + ns + 1:n_in + 2 * ns + 1]
        scratch = refs[n_in + 2 * ns + 1:]
        ids = [pl.program_id(d) for d in range(3)]
        if ns:
            start, wait = _exchange_copies(side_in, side_out, *scratch[-3:], gather)
            pl.when((ids[0] == 0) & (ids[1] == 0) & (ids[2] == 0))(start)
        part = lax.dot_general(a_ref[...].astype(BF16), b_ref[...].astype(BF16), dims, preferred_element_type=F32)
        if nk == 1:
            o_ref[...] = (part + refs[2][...].astype(F32) if has_add else part).astype(o_ref.dtype)
        else:
            acc_ref = scratch[0]

            @pl.when(ids[2] == 0)
            def _():
                acc_ref[...] = part + refs[2][...].astype(F32) if has_add else part

            @pl.when(ids[2] > 0)
            def _():
                acc_ref[...] += part

            @pl.when(ids[2] == nk - 1)
            def _():
                o_ref[...] = acc_ref[...].astype(o_ref.dtype)
        if ns:
            pl.when((ids[0] == grid[0] - 1) & (ids[1] == grid[1] - 1) & (ids[2] == nk - 1))(wait)

    any_spec = pl.BlockSpec(memory_space=pl.ANY)
    res = pl.pallas_call(
        body, name=name, grid=grid,
        in_specs=[a_spec, b_spec] + ([o_spec] if has_add else []) + [any_spec] * ns,
        out_specs=[o_spec] + [any_spec] * ns,
        out_shape=[jax.ShapeDtypeStruct((m, n), out_dtype)] + _exchange_shapes(side_arrays, gather),
        scratch_shapes=([pltpu.VMEM((tm, tn), F32)] if nk > 1 else []) + (_exchange_sems(ns) if ns else []),
        compiler_params=pltpu.CompilerParams(
            dimension_semantics=("arbitrary",) * 3 if ns else ("parallel", "parallel", "arbitrary"),
            vmem_limit_bytes=VMEM_LIMIT),
    )(a, b, *([add] if has_add else []), *side_arrays)
    return (res[0], res[1:]) if ns else res[0]


def matmul_fused(name, a, bs, mode, n, extras, epilogue, out_dtypes):
    m, k = a.shape
    tm, tn = _pick(m, 1024), _pick_n(n)
    dims = {"nn": (((1,), (0,)), ((), ())), "nt": (((1,), (1,)), ((), ()))}[mode]
    nb, nx = len(bs), len(extras)

    def b_spec(off):
        if mode == "nt":
            return pl.BlockSpec((tn, k), lambda i, j: (j + off, 0))
        return pl.BlockSpec((k, tn), lambda i, j: (0, j + off))

    def body(*refs):
        a_val = refs[0][...].astype(BF16)
        parts = tuple(lax.dot_general(a_val, r[...].astype(BF16), dims, preferred_element_type=F32)
                      for r in refs[1:1 + nb])
        tiles = tuple(r[...].astype(F32) for r in refs[1 + nb:1 + nb + nx])
        for r, v in zip(refs[1 + nb + nx:], epilogue(parts, tiles)):
            r[...] = v.astype(r.dtype)

    tile = pl.BlockSpec((tm, tn), lambda i, j: (i, j))
    return pl.pallas_call(
        body, name=name, grid=(m // tm, n // tn),
        in_specs=[pl.BlockSpec((tm, k), lambda i, j: (i, 0))] + [b_spec(off) for _, off in bs] + [tile] * nx,
        out_specs=[tile] * len(out_dtypes), out_shape=[jax.ShapeDtypeStruct((m, n), dt) for dt in out_dtypes],
        compiler_params=pltpu.CompilerParams(dimension_semantics=("parallel", "parallel"),
                                             vmem_limit_bytes=VMEM_LIMIT),
    )(a, *[b for b, _ in bs], *extras)


def _glu_fwd_tiles(parts, tiles):
    gate, up = parts
    return gate, up, jax.nn.silu(gate) * up


def _glu_bwd_tiles(parts, tiles):
    (da,), (gate, up) = parts, tiles
    _, vjp = jax.vjp(lambda g, u: jax.nn.silu(g) * u, gate, up)
    return vjp(da)


def adamw(name, w, m, v, gparts, side=None):
    layers, rows, cols = w.shape
    parts = gparts[0].shape[0]
    tr = 8
    while tr * 2 * cols <= 65536 and rows % (tr * 2) == 0:
        tr *= 2
    nblk = rows // tr
    c1 = 1.0 - ADAM_B1 ** ADAM_STEP
    c2 = 1.0 - ADAM_B2 ** ADAM_STEP
    side_arrays, gather, nx = _side(side)

    def body(*refs):
        w_ref, m_ref, v_ref = refs[:3]
        g_refs = refs[3:3 + layers]
        side_in = refs[3 + layers:3 + layers + nx]
        go_ref, d_ref, mo_ref, vo_ref = refs[3 + layers + nx:7 + layers + nx]
        side_out, sems = refs[7 + layers + nx:7 + layers + 2 * nx], refs[7 + layers + 2 * nx:]
        layer = pl.program_id(0)
        if nx:
            start, wait = _exchange_copies(side_in, side_out, *sems, gather)
            pl.when((layer == 0) & (pl.program_id(1) == 0))(start)
        g = None
        for ll, g_ref in enumerate(g_refs):
            s = g_ref[0].astype(F32)
            for p in range(1, parts):
                s = s + g_ref[p].astype(F32)
            g = s if g is None else jnp.where(layer == ll, s, g)
        m_new = ADAM_B1 * m_ref[...] + (1.0 - ADAM_B1) * g
        v_new = ADAM_B2 * v_ref[...] + (1.0 - ADAM_B2) * (g * g)
        m_hat = m_new / c1
        v_hat = v_new / c2
        go_ref[...] = g
        d_ref[...] = -ADAM_LR * (m_hat / (jnp.sqrt(v_hat) + ADAM_EPS) + ADAM_WD * w_ref[...])
        mo_ref[...] = m_new
        vo_ref[...] = v_new
        if nx:
            pl.when((layer == layers - 1) & (pl.program_id(1) == nblk - 1))(wait)

    def part_spec(ll):
        return pl.BlockSpec((parts, tr, cols),
                            lambda l, i: (0, jnp.where(l == ll, i, jnp.where(l < ll, 0, nblk - 1)), 0))

    spec = pl.BlockSpec((None, tr, cols), lambda l, i: (l, i, 0))
    res = pl.pallas_call(
        body, name=name, grid=(layers, nblk),
        in_specs=[spec, spec, spec] + [part_spec(ll) for ll in range(layers)] + [_ANY] * nx,
        out_specs=[spec] * 4 + [_ANY] * nx,
        out_shape=[jax.ShapeDtypeStruct(w.shape, F32)] * 4 + _exchange_shapes(side_arrays, gather),
        scratch_shapes=_exchange_sems(nx) if nx else [],
        compiler_params=pltpu.CompilerParams(dimension_semantics=("arbitrary", "arbitrary"),
                                             vmem_limit_bytes=VMEM_LIMIT),
    )(w, m, v, *gparts, *side_arrays)
    return (res[:4], res[4:]) if nx else res


def _adamw_math(w, m, v, g):
    m_new = ADAM_B1 * m + (1.0 - ADAM_B1) * g
    v_new = ADAM_B2 * v + (1.0 - ADAM_B2) * (g * g)
    m_hat = m_new / (1.0 - ADAM_B1 ** ADAM_STEP)
    v_hat = v_new / (1.0 - ADAM_B2 ** ADAM_STEP)
    return -ADAM_LR * (m_hat / (jnp.sqrt(v_hat) + ADAM_EPS) + ADAM_WD * w), m_new, v_new


def adamw_many(name, ws, ms, vs, garrays, where):
    n, ng = len(ws), len(garrays)

    def body(*refs):
        ins, g_refs, outs = refs[:3 * n], refs[3 * n:3 * n + ng], refs[3 * n + ng:]
        for i, sources in enumerate(where):
            for a, j, layer in sources:
                g_ref = g_refs[a]
                g = g_ref[0, j].astype(F32)
                for p in range(1, g_ref.shape[0]):
                    g = g + g_ref[p, j].astype(F32)
                sl = slice(None) if layer is None else slice(layer, layer + 1)
                delta, m_new, v_new = _adamw_math(ins[i][sl], ins[n + i][sl], ins[2 * n + i][sl], g)
                for r, val in zip(outs[4 * i:4 * i + 4], (g, delta, m_new, v_new)):
                    r[sl] = val

    vmem = pl.BlockSpec(memory_space=pltpu.VMEM)
    res = pl.pallas_call(
        body, name=name, in_specs=[vmem] * (3 * n + ng), out_specs=[vmem] * (4 * n),
        out_shape=[jax.ShapeDtypeStruct(w.shape, F32) for w in ws for _ in range(4)],
        compiler_params=pltpu.CompilerParams(vmem_limit_bytes=VMEM_LIMIT),
    )(*ws, *ms, *vs, *garrays)
    return [res[4 * i:4 * i + 4] for i in range(n)]


def _pack(arrays, dtype, row_multiple):
    flat = jnp.concatenate([a.astype(dtype).reshape(-1) for a in arrays])
    unit = row_multiple * LANES
    pad = (-flat.shape[0]) % unit
    if pad:
        flat = jnp.concatenate([flat, jnp.zeros((pad,), dtype)])
    return flat.reshape(-1, LANES)


def _unpack(flat2d, shapes):
    flat = flat2d.reshape(-1)
    out, off = [], 0
    for s in shapes:
        n = int(np.prod(s))
        out.append(flat[off:off + n].reshape(s))
        off += n
    return out


def _unpack_stacked(stacked, shapes):
    flat = stacked.reshape(N_DEV, -1)
    out, off = [], 0
    for s in shapes:
        n = int(np.prod(s))
        out.append(flat[:, off:off + n].reshape((N_DEV,) + tuple(s)))
        off += n
    return out


def _merge_shards(stacked, axis):
    t = jnp.moveaxis(stacked, 0, axis)
    s = t.shape
    return t.reshape(s[:axis] + (s[axis] * s[axis + 1],) + s[axis + 2:])


def _split_shards(full, axis, n=N_DEV):
    s = full.shape
    t = full.reshape(s[:axis] + (n, s[axis] // n) + s[axis + 1:])
    return jnp.moveaxis(t, axis, 0)


def _lower_bounds(hg_lower_bounds):
    p = jax.nn.softmax(hg_lower_bounds, axis=0)
    return jnp.cumsum(p, axis=0) - p[0:1]


def _prep_layer(p):
    def row(v):
        return v.reshape(1, -1)

    eye_b = jnp.eye(HEADS, dtype=F32)
    eye_g = jnp.eye(S5_GROUPS, dtype=F32)
    step = jnp.exp(p["s5_log_dt"])[:, None]
    lam_re, lam_im = p["s5_lam_re"], p["s5_lam_im"]
    mag = jnp.exp(lam_re * step)
    lb_re = mag * jnp.cos(lam_im * step)
    lb_im = mag * jnp.sin(lam_im * step)
    den = lam_re * lam_re + lam_im * lam_im
    f_re = ((lb_re - 1.0) * lam_re + lb_im * lam_im) / den
    f_im = (lb_im * lam_re - (lb_re - 1.0) * lam_im) / den
    bb_re = f_re[..., None] * p["s5_b_re"] - f_im[..., None] * p["s5_b_im"]
    bb_im = f_re[..., None] * p["s5_b_im"] + f_im[..., None] * p["s5_b_re"]
    width = S5_GROUPS * S5_GROUP
    n_state = S5_GROUPS * S5_STATE
    return dict(
        lb=row(p["lb"]), hg_norm=row(p["hg_norm"]), ret_norm=row(p["ret_norm"]),
        conv_w=p["lru_conv_w"], conv_b=row(p["lru_conv_b"]),
        wa=jnp.einsum("nij,nm->nimj", p["lru_wa"], eye_b).reshape(MIX_W, MIX_W).astype(BF16), ba=row(p["lru_ba"]),
        wx=jnp.einsum("nij,nm->nimj", p["lru_wx"], eye_b).reshape(MIX_W, MIX_W).astype(BF16), bx=row(p["lru_bx"]),
        sp=row(jax.nn.softplus(-p["lru_lambda"])),
        bt_re=jnp.einsum("gnp,gh->gphn", bb_re, eye_g).reshape(width, n_state).astype(BF16),
        bt_im=jnp.einsum("gnp,gh->gphn", bb_im, eye_g).reshape(width, n_state).astype(BF16),
        lb_re=row(lb_re), lb_im=row(lb_im),
        ct_re=jnp.einsum("gpn,gh->gnhp", p["s5_c_re"], eye_g).reshape(n_state, width).astype(BF16),
        ct_im=jnp.einsum("gpn,gh->gnhp", p["s5_c_im"], eye_g).reshape(n_state, width).astype(BF16),
        s5_d=row(p["s5_d"]), glu_w=p["s5_glu_w"].astype(BF16), glu_b=row(p["s5_glu_b"]),
        b_gate=row(p["b_gate"]),
        norm_mix_pre=row(p["norm_mix_pre"]), norm_mix_post=row(p["norm_mix_post"]),
        norm_xa_pre=row(p["norm_xa_pre"]), norm_xa_post=row(p["norm_xa_post"]), norm_mem=row(p["norm_mem"]),
        norm_ffn_pre=row(p["norm_ffn_pre"]), norm_ffn_post=row(p["norm_ffn_post"]),
    )


_PREP_INPUTS = ("hg_norm", "ret_norm", "lru_conv_w", "lru_conv_b", "lru_wa", "lru_ba", "lru_wx", "lru_bx", "lru_lambda",
                "s5_lam_re", "s5_lam_im", "s5_b_re", "s5_b_im", "s5_c_re", "s5_c_im", "s5_d", "s5_log_dt", "s5_glu_w",
                "s5_glu_b", "b_gate", "norm_mix_pre", "norm_mix_post", "norm_xa_pre", "norm_xa_post", "norm_mem",
                "norm_ffn_pre", "norm_ffn_post")


def _retention_constants():
    lg = np.log1p(-np.power(2.0, -5.0 - np.arange(HEADS)))
    idx = np.arange(RET_CHUNK)

    def lanes(per_head_rows):
        return np.repeat(per_head_rows.T[:, :, None], HEAD_DIM, axis=2).reshape(RET_CHUNK, MIX_W)

    xi = lanes(np.exp((idx + 1.0)[None, :] * lg[:, None]))
    zeta = lanes(np.exp((RET_CHUNK - 1.0 - idx)[None, :] * lg[:, None]))
    rel = idx[:, None] - idx[None, :]
    decay = np.where(rel[None] >= 0, np.exp(np.maximum(rel, 0)[None] * lg[:, None, None]), 0.0)
    decay = np.transpose(decay, (1, 0, 2)).reshape(RET_CHUNK, HEADS * RET_CHUNK)
    g_end = np.repeat(np.exp(RET_CHUNK * lg), HEAD_DIM)[None, :]
    return tuple(jnp.asarray(a, F32) for a in (xi, zeta, decay, g_end))


def _rotary_tables(seq):
    pos = jnp.arange(seq, dtype=F32)
    inv_freq = 10000.0 ** (-jnp.arange(0, HEAD_DIM, 2, dtype=F32) / HEAD_DIM)
    ang = pos[:, None] * inv_freq[None, :]
    cos, sin = jnp.cos(ang), jnp.sin(ang)
    cos_t = jnp.tile(jnp.repeat(cos, 2, axis=1), (1, HEADS))
    sin_t = jnp.tile(jnp.stack([-sin, sin], axis=-1).reshape(seq, HEAD_DIM), (1, HEADS))
    return cos_t, sin_t


TB_HG = 1024
TB_RET = 1024
TB_LRU = 256
TB_S5 = 256
TB_ROW = 256
TB_XA = 512
TB_MIX = 512

_STATE = (MIX_W, MIX_W)
_TAIL = (8, MIX_W)
_S5_TAIL = (8, S5_GROUPS * S5_STATE)


def _mixer_operands(l, z, kp, rot, ret_c):
    xi, zeta, decay, g_end = ret_c
    return dict(
        hg=dict(name=f"hgrn2_{l}", fn=fn_hgrn2, rows=[(z, 4 * MIX_W, 0)], params=[kp["lb"], kp["hg_norm"]],
                tb=TB_HG, states=(_STATE,), diff_params=(0, 1)),
        ret=dict(name=f"retention_{l}", fn=fn_retention, rows=[(z, 4 * MIX_W, 1), (rot[0], MIX_W, 0), (rot[1], MIX_W, 0)],
                 params=[kp["ret_norm"], xi, zeta, decay, g_end], tb=TB_RET, states=(_STATE,), diff_params=(0,)),
        lru=dict(name=f"rglru_{l}", fn=fn_rglru, rows=[(z, 2 * MIX_W, 4)],
                 params=[kp["conv_w"], kp["conv_b"], kp["wa"], kp["ba"], kp["wx"], kp["bx"], kp["sp"]],
                 tb=TB_LRU, states=(_TAIL, _TAIL), diff_params=(0, 1, 2, 3, 4, 5, 6)),
        s5=dict(name=f"s5_{l}", fn=fn_s5, rows=[(z, MIX_W, 10)],
                params=[kp["bt_re"], kp["bt_im"], kp["lb_re"], kp["lb_im"], kp["ct_re"], kp["ct_im"], kp["s5_d"],
                        kp["glu_w"], kp["glu_b"]],
                tb=TB_S5, states=(_S5_TAIL, _S5_TAIL), diff_params=tuple(range(9))),
    )


def _layer_forward(l, x, h, mem, w_in, shards, next_w_in_shard, kp, rot, ret_c, g_next, target):
    d = x.shape[1]
    sv = dict(x=x, h=h)
    bw = {"w_in": w_in}

    def gather(idx):
        return [shards[i] for i in idx], True

    def take(idx, stacked):
        for i, s in zip(idx, stacked):
            bw[_BIG_NAMES[i]] = _merge_shards(s, BIG[i][1] - 1)

    z, got = matmul(f"in_proj_{l}", h, w_in, "nn", F32, side=gather(GATHER_IN["in_proj"]))
    take(GATHER_IN["in_proj"], got)
    gl, got = matmul(f"gate_proj_{l}", h, bw["w_gate"], "nn", BF16, side=gather(GATHER_IN["gate_proj"]))
    take(GATHER_IN["gate_proj"], got)
    sv.update(z=z, gl=gl)
    ops = _mixer_operands(l, z, kp, rot, ret_c)
    ys = []
    for key in ("hg", "ret", "lru", "s5"):
        o = ops[key]
        idx = GATHER_IN.get(key)
        res = stage_fwd(o["name"] + "_fwd", o["fn"], o["rows"], o["params"], tb=o["tb"], outs=[(MIX_W, F32)],
                        states=o["states"], side=gather(idx) if idx else None)
        if idx:
            take(idx, res[3])
        ys.append(res[0][0])
        sv[key + "_states"] = res[2]
    sv["ys"] = ys
    mix_params = [bw["w_up"][n] for n in range(4)] + [kp["b_gate"]]
    mix_rows = [(y, MIX_W, 0) for y in ys] + [(gl, 4 * d, 0)]
    (mix,), _, _, got = stage_fwd(f"mix_{l}_fwd", fn_mix, mix_rows, mix_params, tb=TB_MIX, outs=[(d, BF16)],
                                  side=gather(GATHER_IN["merge"]))
    take(GATHER_IN["merge"], got)
    out_params = [bw["w_out"], kp["norm_mix_post"]]
    (x1,), _, _ = stage_fwd(f"out_{l}_fwd", fn_out, [(mix, d, 0), (x, d, 0)], out_params, tb=TB_MIX, outs=[(d, F32)])
    wk, wv = bw["xa_w_kv"][:, :d], bw["xa_w_kv"][:, d:]
    mem_params = [kp["norm_mem"], wk, wv]
    (k, v), _, _ = stage_fwd(f"mem_{l}_fwd", fn_mem, [(mem, d, 0)], mem_params, tb=mem.shape[0],
                             outs=[(d, BF16), (d, BF16)])
    xa_params = [kp["norm_xa_pre"], bw["xa_w_q"], k, v, bw["xa_w_o"], kp["norm_xa_post"], kp["norm_ffn_pre"]]
    res = stage_fwd(f"xattn_{l}_fwd", fn_xattn, [(x1, d, 0)], xa_params, tb=TB_XA, outs=[(d, F32), (d, BF16)],
                    side=([next_w_in_shard], True) if next_w_in_shard is not None else None)
    x2, h3 = res[0]
    next_w_in = _merge_shards(res[3][0], BIG[0][1] - 1) if next_w_in_shard is not None else None
    f = bw["ffn_w_gu"].shape[1] // 2
    up_block = f // _pick_n(f)
    gate, up, a = matmul_fused(f"ffn_gu_{l}", h3, [(bw["ffn_w_gu"], 0), (bw["ffn_w_gu"], up_block)], "nn", f, [],
                               _glu_fwd_tiles, (BF16, BF16, BF16))
    o3 = matmul(f"ffn_down_{l}", a, bw["ffn_w_down"], "nn", F32)
    sv.update(x1=x1, k=k, v=v, x2=x2, h3=h3, gate=gate, up=up, a=a, o3=o3, mix=mix, mix_params=mix_params,
              mix_rows=mix_rows, out_params=out_params, mem_params=mem_params, xa_params=xa_params)
    if g_next is not None:
        (x3, hn), _, _ = stage_fwd(f"res_{l}_fwd", fn_res_norm, [(x2, d, 0), (o3, d, 0)], [kp["norm_ffn_post"], g_next],
                                   tb=TB_ROW, outs=[(d, F32), (d, BF16)])
        return x3, hn, sv, bw, next_w_in
    (dx2, do3), (loss, dg_post), _ = stage_fwd(f"loss_{l}", fn_loss_head, [(x2, d, 0), (o3, d, 0), (target, d, 0)],
                                               [kp["norm_ffn_post"]], tb=TB_ROW, outs=[(d, F32), (d, F32)],
                                               accs=[(8, LANES), (1, d)])
    sv["head"] = (dx2, do3, dg_post)
    return None, loss[0, 0], sv, bw, next_w_in


def _layer_backward(l, sv, mem, bw, kp, rot, ret_c, g_next, dx3, dhn, s5_side=None):
    d = sv["x"].shape[1]
    gk = {}
    parts = [None] * len(BIG)

    def send(i, g):
        axis = BIG[i][1] - 1
        if isinstance(g, tuple):
            return jnp.concatenate([_split_shards(p.astype(BF16), axis, N_DEV // len(g)) for p in g], axis=0)
        return _split_shards(g.astype(BF16), axis)

    def scatter(*items):
        return [send(i, g) for i, g in items], False

    def took(got, *idx):
        for i, p in zip(idx, got):
            parts[i] = p

    res_rows = [(sv["x2"], d, 0), (sv["o3"], d, 0)]
    if g_next is not None:
        (dx2, do3), (gk["norm_ffn_post"], gk["g_next"]) = stage_bwd(
            f"res_{l}_bwd", fn_res_norm, res_rows, [kp["norm_ffn_post"], g_next], [dx3, dhn], tb=TB_ROW,
            diff_rows=(0, 1), diff_params=(0, 1))
    else:
        dx2, do3, gk["norm_ffn_post"] = sv["head"]
    f = sv["a"].shape[1]
    g_down = matmul(f"ffn_down_dw_{l}", sv["a"], do3, "tn", BF16)
    d_gate, d_up = matmul_fused(f"ffn_down_dx_{l}", do3, [(bw["ffn_w_down"], 0)], "nt", f, [sv["gate"], sv["up"]],
                                _glu_bwd_tiles, (BF16, BF16))
    dh3, got = matmul(f"ffn_gate_dx_{l}", d_gate, bw["ffn_w_gu"][:, :f], "nt", F32, side=scatter((8, g_down)))
    took(got, 8)
    dh3 = matmul(f"ffn_up_dx_{l}", d_up, bw["ffn_w_gu"][:, f:], "nt", F32, add=dh3)
    g_gu = (matmul(f"ffn_gate_dw_{l}", sv["h3"], d_gate, "tn", BF16),
            matmul(f"ffn_up_dw_{l}", sv["h3"], d_up, "tn", BF16))
    (dx1,), xa_g, got = stage_bwd(f"xattn_{l}_bwd", fn_xattn, [(sv["x1"], d, 0)], sv["xa_params"], [dx2, dh3],
                                  tb=TB_XA, diff_rows=(0,), diff_params=tuple(range(7)), side=scatter((7, g_gu)))
    took(got, 7)
    gk["norm_xa_pre"], g_q, dk, dv, g_o, gk["norm_xa_post"], gk["norm_ffn_pre"] = xa_g
    _, (gk["norm_mem"], dwk, dwv) = stage_bwd(f"mem_{l}_bwd", fn_mem, [(mem, d, 0)], sv["mem_params"], [dk, dv],
                                              tb=mem.shape[0], diff_params=(0, 1, 2))
    g_kv = jnp.concatenate([dwk, dwv], axis=1)
    (dmix, dx0), (g_out, gk["norm_mix_post"]), got = stage_bwd(
        f"out_{l}_bwd", fn_out, [(sv["mix"], d, 0), (sv["x"], d, 0)], sv["out_params"], [dx1], tb=TB_MIX,
        diff_rows=(0, 1), diff_params=(0, 1), row_dtypes=(BF16, F32), side=scatter((4, g_q), (6, g_o), (5, g_kv)))
    took(got, 4, 6, 5)
    merge_d, merge_g = stage_bwd(f"mix_{l}_bwd", fn_mix, sv["mix_rows"], sv["mix_params"], [dmix], tb=TB_MIX,
                                 diff_rows=tuple(range(5)), diff_params=tuple(range(5)),
                                 row_dtypes=(F32, F32, F32, F32, BF16))
    dys, dgl = merge_d[:4], merge_d[4]
    g_up = jnp.stack(merge_g[:4])
    gk["b_gate"] = merge_g[4]
    ops = _mixer_operands(l, sv["z"], kp, rot, ret_c)
    dz = None
    for key, dy in zip(("hg", "ret", "lru", "s5"), dys):
        o = ops[key]
        res = stage_bwd(o["name"] + "_bwd", o["fn"], o["rows"], o["params"], [dy], tb=o["tb"],
                        saved=sv[key + "_states"], diff_rows=(0,), diff_params=o["diff_params"], row_dtypes=(BF16,),
                        side=scatter((2, g_up), (3, g_out)) if key == "hg" else s5_side if key == "s5" else None,
                        into=(dz, sv["z"].shape[1]))
        if key == "hg":
            took(res[2], 2, 3)
        elif key == "s5" and s5_side is not None:
            gk["s5_side"] = res[2]
        dz = res[0][0]
        gk[key] = res[1]
    g_gate = matmul(f"gate_proj_dw_{l}", sv["h"], dgl, "tn", BF16)
    g_in = matmul(f"in_proj_dw_{l}", sv["h"], dz, "tn", BF16)
    dh, got = matmul(f"gate_proj_dx_{l}", dgl, bw["w_gate"], "nt", F32, side=scatter((1, g_gate)))
    took(got, 1)
    dh, got = matmul(f"in_proj_dx_{l}", dz, bw["w_in"], "nt", F32, add=dh, side=scatter((0, g_in)))
    took(got, 0)
    return dx0, dh, gk, parts


def _kernel_grads_to_prep(gk):
    hg, ret, lru, s5 = gk["hg"], gk["ret"], gk["lru"], gk["s5"]
    return dict(
        lb=hg[0], hg_norm=hg[1], ret_norm=ret[0],
        conv_w=lru[0], conv_b=lru[1], wa=lru[2], ba=lru[3], wx=lru[4], bx=lru[5], sp=lru[6],
        bt_re=s5[0], bt_im=s5[1], lb_re=s5[2], lb_im=s5[3], ct_re=s5[4], ct_im=s5[5], s5_d=s5[6], glu_w=s5[7],
        glu_b=s5[8], b_gate=gk["b_gate"], norm_mix_pre=gk["norm_mix_pre"], norm_mix_post=gk["norm_mix_post"],
        norm_xa_pre=gk["norm_xa_pre"], norm_xa_post=gk["norm_xa_post"], norm_mem=gk["norm_mem"],
        norm_ffn_pre=gk["norm_ffn_pre"], norm_ffn_post=gk["norm_ffn_post"],
    )


def _step(inp):
    x, mem, target = inp["x"][0], inp["mem"][0], inp["loss_target"][0]
    seq = x.shape[0]
    depth = inp["w_in"].shape[0]
    me = 4 * lax.axis_index("x") + 2 * lax.axis_index("y") + lax.axis_index("c")

    small_shapes = [inp[n].shape for n in _SMALL_SHARDED_NAMES]
    (small_stacked,) = _exchange("gather_small", [_pack([inp[n] for n in _SMALL_SHARDED_NAMES], F32, 8)], True)
    small_all = _unpack_stacked(small_stacked, small_shapes)
    full_small = {n: _merge_shards(s, ax) for (n, ax), s in zip(SMALL_SHARDED, small_all)}

    lbs, lbs_vjp = jax.vjp(_lower_bounds, inp["hg_lower_bounds"])
    kps, prep_vjps = [], []
    for l in range(depth):
        p = {n: (full_small[n][l] if n in full_small else inp[n][l]) for n in _PREP_INPUTS}
        p["lb"] = lbs[l]
        kp, vj = jax.vjp(_prep_layer, p)
        kps.append(kp)
        prep_vjps.append(vj)
    rot = _rotary_tables(seq)
    ret_c = _retention_constants()

    def shards(l):
        return [inp[n][l].astype(BF16) for n in _BIG_NAMES]

    (h,), _, _, (stacked,) = stage_fwd("norm_in_fwd", fn_norm, [(x, x.shape[1], 0)], [kps[0]["norm_mix_pre"]],
                                       tb=TB_ROW, outs=[(x.shape[1], BF16)], side=([shards(0)[0]], True))
    saved, bws = [], []
    xs = x
    w_in = _merge_shards(stacked, BIG[0][1] - 1)
    for l in range(depth):
        last = l + 1 == depth
        g_next = None if last else kps[l + 1]["norm_mix_pre"]
        xs, h, sv, bw, w_in = _layer_forward(l, xs, h, mem, w_in, shards(l), None if last else shards(l + 1)[0],
                                             kps[l], rot, ret_c, g_next, target)
        saved.append(sv)
        bws.append(bw)
    dy, loss_local = xs, h

    big_parts = [None] * depth
    gks = [None] * depth
    dx, dh = dy, None
    small_names = REPLICATED + _SMALL_SHARDED_NAMES
    cross_layer = ("hg_lower_bounds", "norm_mix_pre")
    groups = {}
    for n in small_names:
        if n not in cross_layer:
            groups.setdefault(full_small[n].shape[1:] if n in full_small else inp[n].shape[1:], []).append(n)

    def layer_small_grads(l):
        gk = dict(gks[l], norm_mix_pre=jnp.zeros_like(kps[l]["norm_mix_pre"]))
        (gp,) = prep_vjps[l]({k: g.astype(kps[l][k].dtype) for k, g in _kernel_grads_to_prep(gk).items()})
        return gp["lb"], [jnp.stack([gp[n].astype(BF16)[None] for n in names]) for names in groups.values()]

    d_lbs = [None] * depth
    gathered = [None] * depth
    pending = None
    for l in reversed(range(depth)):
        g_next = kps[l + 1]["norm_mix_pre"] if l + 1 < depth else None
        dx, dh, gks[l], big_parts[l] = _layer_backward(l, saved[l], mem, bws[l], kps[l], rot, ret_c, g_next, dx, dh,
                                                       s5_side=(pending, True) if pending is not None else None)
        if pending is not None:
            gathered[l + 1] = list(gks[l]["s5_side"])
        d_lbs[l], pending = layer_small_grads(l)
    (grad_x,), (g_pre0,) = stage_bwd("norm_in_bwd", fn_keep_norm, [(x, x.shape[1], 0)], [kps[0]["norm_mix_pre"]],
                                     [dx, dh], tb=TB_ROW, diff_rows=(0,), diff_params=(0,))
    (g_lower_bounds,) = lbs_vjp(jnp.stack(d_lbs))
    g_mix_pre = jnp.concatenate([g_pre0] + [gks[l]["g_next"] for l in range(depth - 1)], axis=0)
    small_send = pending + [g_lower_bounds.astype(BF16)[None], g_mix_pre.astype(BF16)[None]]

    out = {}
    kinds = ("grad_", "delta_", "new_m_", "new_v_")
    small_parts = None
    for i, n in sorted(enumerate(_BIG_NAMES), key=lambda t: -int(np.prod(inp[t[1]].shape))):
        shape = inp[n].shape
        three = (shape[0], int(np.prod(shape[1:-1])), shape[-1])
        res = adamw("adamw_" + n, *[inp[pre + n].reshape(three) for pre in ("", "m_", "v_")],
                    [big_parts[l][i].reshape((N_DEV,) + three[1:]) for l in range(depth)],
                    side=(small_send, True) if small_parts is None else None)
        if small_parts is None:
            res, small_parts = res
        for kind, a in zip(kinds, res):
            out[kind + n] = a.reshape(shape)
    ng = len(groups)
    gathered[0] = list(small_parts[:ng])
    place = {n: (gi, j) for gi, names in enumerate(groups.values()) for j, n in enumerate(names)}
    for n, ax in SMALL_SHARDED:
        gi, j = place[n]
        assert len(list(groups.values())[gi]) == 1
        width = inp[n].shape[ax]
        for l in range(depth):
            gathered[l][gi] = lax.dynamic_slice_in_dim(gathered[l][gi], me * width, width, axis=ax + 2)
    garrays = [a for l in range(depth) for a in gathered[l]] + list(small_parts[ng:])
    where = [[(depth * ng + cross_layer.index(n), 0, None)] if n in cross_layer else
             [(l * ng + place[n][0], place[n][1], l) for l in range(depth)] for n in small_names]
    res = adamw_many("adamw_small", *[[inp[pre + n] for n in small_names] for pre in ("", "m_", "v_")], garrays, where)
    for n, quad in zip(small_names, res):
        for kind, a in zip(kinds, quad):
            out[kind + n] = a

    out["loss"] = lax.psum(loss_local, ("x", "y", "c"))
    out["grad_x"] = grad_x[None]
    return out


def kernel(x, mem, hg_lower_bounds, norm_mix_pre, norm_mix_post, w_in, w_gate, b_gate, hg_norm, ret_norm, lru_conv_w, lru_conv_b, lru_wa, lru_ba, lru_wx, lru_bx, lru_lambda, s5_lam_re, s5_lam_im, s5_b_re, s5_b_im, s5_c_re, s5_c_im, s5_d, s5_log_dt, s5_glu_w, s5_glu_b, w_up, w_out, norm_xa_pre, norm_xa_post, norm_mem, xa_w_q, xa_w_kv, xa_w_o, norm_ffn_pre, norm_ffn_post, ffn_w_gu, ffn_w_down, loss_target, m_hg_lower_bounds, m_norm_mix_pre, m_norm_mix_post, m_w_in, m_w_gate, m_b_gate, m_hg_norm, m_ret_norm, m_lru_conv_w, m_lru_conv_b, m_lru_wa, m_lru_ba, m_lru_wx, m_lru_bx, m_lru_lambda, m_s5_lam_re, m_s5_lam_im, m_s5_b_re, m_s5_b_im, m_s5_c_re, m_s5_c_im, m_s5_d, m_s5_log_dt, m_s5_glu_w, m_s5_glu_b, m_w_up, m_w_out, m_norm_xa_pre, m_norm_xa_post, m_norm_mem, m_xa_w_q, m_xa_w_kv, m_xa_w_o, m_norm_ffn_pre, m_norm_ffn_post, m_ffn_w_gu, m_ffn_w_down, v_hg_lower_bounds, v_norm_mix_pre, v_norm_mix_post, v_w_in, v_w_gate, v_b_gate, v_hg_norm, v_ret_norm, v_lru_conv_w, v_lru_conv_b, v_lru_wa, v_lru_ba, v_lru_wx, v_lru_bx, v_lru_lambda, v_s5_lam_re, v_s5_lam_im, v_s5_b_re, v_s5_b_im, v_s5_c_re, v_s5_c_im, v_s5_d, v_s5_log_dt, v_s5_glu_w, v_s5_glu_b, v_w_up, v_w_out, v_norm_xa_pre, v_norm_xa_post, v_norm_mem, v_xa_w_q, v_xa_w_kv, v_xa_w_o, v_norm_ffn_pre, v_norm_ffn_post, v_ffn_w_gu, v_ffn_w_down):
    values = (x, mem, hg_lower_bounds, norm_mix_pre, norm_mix_post, w_in, w_gate, b_gate, hg_norm, ret_norm, lru_conv_w, lru_conv_b, lru_wa, lru_ba, lru_wx, lru_bx, lru_lambda, s5_lam_re, s5_lam_im, s5_b_re, s5_b_im, s5_c_re, s5_c_im, s5_d, s5_log_dt, s5_glu_w, s5_glu_b, w_up, w_out, norm_xa_pre, norm_xa_post, norm_mem, xa_w_q, xa_w_kv, xa_w_o, norm_ffn_pre, norm_ffn_post, ffn_w_gu, ffn_w_down, loss_target, m_hg_lower_bounds, m_norm_mix_pre, m_norm_mix_post, m_w_in, m_w_gate, m_b_gate, m_hg_norm, m_ret_norm, m_lru_conv_w, m_lru_conv_b, m_lru_wa, m_lru_ba, m_lru_wx, m_lru_bx, m_lru_lambda, m_s5_lam_re, m_s5_lam_im, m_s5_b_re, m_s5_b_im, m_s5_c_re, m_s5_c_im, m_s5_d, m_s5_log_dt, m_s5_glu_w, m_s5_glu_b, m_w_up, m_w_out, m_norm_xa_pre, m_norm_xa_post, m_norm_mem, m_xa_w_q, m_xa_w_kv, m_xa_w_o, m_norm_ffn_pre, m_norm_ffn_post, m_ffn_w_gu, m_ffn_w_down, v_hg_lower_bounds, v_norm_mix_pre, v_norm_mix_post, v_w_in, v_w_gate, v_b_gate, v_hg_norm, v_ret_norm, v_lru_conv_w, v_lru_conv_b, v_lru_wa, v_lru_ba, v_lru_wx, v_lru_bx, v_lru_lambda, v_s5_lam_re, v_s5_lam_im, v_s5_b_re, v_s5_b_im, v_s5_c_re, v_s5_c_im, v_s5_d, v_s5_log_dt, v_s5_glu_w, v_s5_glu_b, v_w_up, v_w_out, v_norm_xa_pre, v_norm_xa_post, v_norm_mem, v_xa_w_q, v_xa_w_kv, v_xa_w_o, v_norm_ffn_pre, v_norm_ffn_post, v_ffn_w_gu, v_ffn_w_down)
    names = ("x", "mem") + WEIGHTS + ("loss_target",) + tuple("m_" + n for n in WEIGHTS) + tuple("v_" + n for n in WEIGHTS)
    out = _step(dict(zip(names, values)))
    order = ["loss", "grad_x"] + [k + n for k in ("grad_", "delta_", "new_m_", "new_v_") for n in WEIGHTS]
    return tuple(out[k] for k in order)
```
